```python
import jax, jax.numpy as jnp
from jax import lax
import numpy as np

D_MODEL = 1024
BATCH = 8
SEQ = 4096
DEPTH = 4

N_HEADS = 16
HEAD_DIM = D_MODEL // N_HEADS
D_FF = ((8 * D_MODEL // 3 + 127) // 128) * 128
CONV_WIDTH = 3
Q_BLOCK = 128
N_MIXERS = 2
N_MOD = 6
RMS_EPS = 1e-6
N_SB_LAYERS = (DEPTH + 1) // 2
N_FOX_LAYERS = DEPTH // 2

kernel_name = "hybrid_stickbreak_fox_convffn"


def rmsnorm(x, g):
    xf = x.astype(jnp.float32)
    y = xf * lax.rsqrt(jnp.mean(xf * xf, axis=-1, keepdims=True) + RMS_EPS)
    return (y * g.astype(jnp.float32)).astype(x.dtype)


def modulate(h, shift, scale):
    return h * (1.0 + scale[:, None, :]) + shift[:, None, :]


def split_heads(t):
    b, s, _ = t.shape
    return t.reshape(b, s, N_HEADS, HEAD_DIM).transpose(0, 2, 1, 3)


def merge_heads(t):
    b, h, s, d = t.shape
    return t.transpose(0, 2, 1, 3).reshape(b, s, h * d)


def stick_breaking_attention(q, k, v):
    s_len = q.shape[2]
    scale = HEAD_DIM ** -0.5
    outs = []
    for blk in range(s_len // Q_BLOCK):
        q0 = blk * Q_BLOCK
        k_end = q0 + Q_BLOCK
        qb, kb, vb = q[:, :, q0:k_end], k[:, :, :k_end], v[:, :, :k_end]
        z = jnp.einsum('bhqd,bhkd->bhqk', qb, kb).astype(jnp.float32) * scale
        t_idx = q0 + jnp.arange(Q_BLOCK)[:, None]
        s_idx = jnp.arange(k_end)[None, :]
        strict = s_idx < t_idx
        log_beta = jax.nn.log_sigmoid(z)
        log_1mb = jnp.where(strict, jax.nn.log_sigmoid(-z), 0.0)
        after = lax.cumsum(log_1mb, axis=3, reverse=True) - log_1mb
        a = jnp.where(strict, jnp.exp(log_beta + after), 0.0)
        outs.append(jnp.einsum('bhqk,bhkd->bhqd', a.astype(vb.dtype), vb))
    return jnp.concatenate(outs, axis=2)


def forgetting_attention(q, k, v, log_f):
    s_len = q.shape[2]
    scale = HEAD_DIM ** -0.5
    cum = lax.cumsum(log_f, axis=2)
    outs = []
    for blk in range(s_len // Q_BLOCK):
        q0 = blk * Q_BLOCK
        k_end = q0 + Q_BLOCK
        qb, kb, vb = q[:, :, q0:k_end], k[:, :, :k_end], v[:, :, :k_end]
        z = jnp.einsum('bhqd,bhkd->bhqk', qb, kb).astype(jnp.float32) * scale
        decay = cum[:, :, q0:k_end, None] - cum[:, :, None, :k_end]
        t_idx = q0 + jnp.arange(Q_BLOCK)[:, None]
        s_idx = jnp.arange(k_end)[None, :]
        logits = jnp.where(s_idx <= t_idx, z + decay, -jnp.inf)
        p = jax.nn.softmax(logits, axis=-1)
        outs.append(jnp.einsum('bhqk,bhkd->bhqd', p.astype(vb.dtype), vb))
    return jnp.concatenate(outs, axis=2)


def causal_depthwise_conv(h, w, b):
    s_len = h.shape[1]
    hp = jnp.pad(h, ((0, 0), (CONV_WIDTH - 1, 0), (0, 0)))
    out = b[None, None, :]
    for kk in range(CONV_WIDTH):
        out = out + w[kk][None, None, :] * hp[:, kk:kk + s_len]
    return out


def _fwd_setup_inputs(seed: int = 0) -> dict:
    key = jax.random.key(seed)
    ks = jax.random.split(key, 20)
    f32 = jnp.float32
    d, f, h = D_MODEL, D_FF, N_HEADS

    def nrm(k, shape, s):
        return jax.random.normal(k, shape, f32) * s

    return {
        "x": nrm(ks[0], (BATCH, SEQ, d), 1.0),
        "c": nrm(ks[1], (BATCH, d), 1.0),
        "w_mod": nrm(ks[2], (DEPTH, d, N_MOD * d), 0.5 * d ** -0.5),
        "b_mod": nrm(ks[3], (DEPTH, N_MOD * d), 0.02),
        "g_mix_pre": 1.0 + nrm(ks[4], (DEPTH, d), 0.05),
        "g_mix_post": 1.0 + nrm(ks[5], (DEPTH, d), 0.05),
        "w_qkv": nrm(ks[6], (DEPTH, d, 3 * d), d ** -0.5),
        "w_o": nrm(ks[7], (DEPTH, d, d), d ** -0.5),
        "w_fg": nrm(ks[8], (N_FOX_LAYERS, d, h), d ** -0.5),
        "b_fg": 3.0 + nrm(ks[9], (N_FOX_LAYERS, h), 0.5),
        "g_ffn_pre": 1.0 + nrm(ks[10], (DEPTH, d), 0.05),
        "g_ffn_post": 1.0 + nrm(ks[11], (DEPTH, d), 0.05),
        "w_ffn_gate": nrm(ks[12], (DEPTH, d, f), d ** -0.5),
        "w_ffn_up": nrm(ks[13], (DEPTH, d, f), d ** -0.5),
        "w_conv": nrm(ks[14], (DEPTH, CONV_WIDTH, f), CONV_WIDTH ** -0.5),
        "b_conv": nrm(ks[15], (DEPTH, f), 0.02),
        "w_ffn_down": nrm(ks[16], (DEPTH, f, d), f ** -0.5),
    }


def _fwd_reference(x, c, w_mod, b_mod, g_mix_pre, g_mix_post, w_qkv, w_o, w_fg, b_fg,
              g_ffn_pre, g_ffn_post, w_ffn_gate, w_ffn_up, w_conv, b_conv, w_ffn_down):
    c_act = jax.nn.silu(c)
    for i in range(DEPTH):
        mod = jnp.einsum('bd,dm->bm', c_act, w_mod[i]) + b_mod[i]
        sh_a, sc_a, gt_a, sh_f, sc_f, gt_f = jnp.split(mod, N_MOD, axis=-1)

        h = modulate(rmsnorm(x, g_mix_pre[i]), sh_a, sc_a)
        qkv = jnp.einsum('bsd,de->bse', h, w_qkv[i])
        q, k, v = (split_heads(t) for t in jnp.split(qkv, 3, axis=-1))
        if i % N_MIXERS == 0:
            o = stick_breaking_attention(q, k, v)
        else:
            j = i // N_MIXERS
            f_logit = jnp.einsum('bsd,dh->bhs', h, w_fg[j]) + b_fg[j][None, :, None]
            log_f = jax.nn.log_sigmoid(f_logit.astype(jnp.float32))
            o = forgetting_attention(q, k, v, log_f)
        o = jnp.einsum('bse,ed->bsd', merge_heads(o), w_o[i])
        x = x + gt_a[:, None, :] * rmsnorm(o, g_mix_post[i])

        h = modulate(rmsnorm(x, g_ffn_pre[i]), sh_f, sc_f)
        gate = jnp.einsum('bsd,df->bsf', h, w_ffn_gate[i])
        up = jnp.einsum('bsd,df->bsf', h, w_ffn_up[i])
        gate = causal_depthwise_conv(gate, w_conv[i], b_conv[i])
        y = jnp.einsum('bsf,fd->bsd', jax.nn.silu(gate) * up, w_ffn_down[i])
        x = x + gt_f[:, None, :] * rmsnorm(y, g_ffn_post[i])
    return x


import jax as _jax
import jax.numpy as _jnp

TWIN_FORMAT = 'train_step'
FWD_PARAMS = ['x', 'c', 'w_mod', 'b_mod', 'g_mix_pre', 'g_mix_post', 'w_qkv', 'w_o', 'w_fg', 'b_fg', 'g_ffn_pre', 'g_ffn_post', 'w_ffn_gate', 'w_ffn_up', 'w_conv', 'b_conv', 'w_ffn_down']
TWIN_WEIGHTS = ['w_mod', 'b_mod', 'g_mix_pre', 'g_mix_post', 'w_qkv', 'w_o', 'w_fg', 'b_fg', 'g_ffn_pre', 'g_ffn_post', 'w_ffn_gate', 'w_ffn_up', 'w_conv', 'b_conv', 'w_ffn_down']
TWIN_DIFF_INPUT = 'x'
TWIN_INPUTS = ['x', 'c', 'w_mod', 'b_mod', 'g_mix_pre', 'g_mix_post', 'w_qkv', 'w_o', 'w_fg', 'b_fg', 'g_ffn_pre', 'g_ffn_post', 'w_ffn_gate', 'w_ffn_up', 'w_conv', 'b_conv', 'w_ffn_down', 'loss_target', 'm_w_mod', 'm_b_mod', 'm_g_mix_pre', 'm_g_mix_post', 'm_w_qkv', 'm_w_o', 'm_w_fg', 'm_b_fg', 'm_g_ffn_pre', 'm_g_ffn_post', 'm_w_ffn_gate', 'm_w_ffn_up', 'm_w_conv', 'm_b_conv', 'm_w_ffn_down', 'v_w_mod', 'v_b_mod', 'v_g_mix_pre', 'v_g_mix_post', 'v_w_qkv', 'v_w_o', 'v_w_fg', 'v_b_fg', 'v_g_ffn_pre', 'v_g_ffn_post', 'v_w_ffn_gate', 'v_w_ffn_up', 'v_w_conv', 'v_b_conv', 'v_w_ffn_down']
TWIN_OUTPUTS = ['loss', 'grad_x', 'grad_w_mod', 'grad_b_mod', 'grad_g_mix_pre', 'grad_g_mix_post', 'grad_w_qkv', 'grad_w_o', 'grad_w_fg', 'grad_b_fg', 'grad_g_ffn_pre', 'grad_g_ffn_post', 'grad_w_ffn_gate', 'grad_w_ffn_up', 'grad_w_conv', 'grad_b_conv', 'grad_w_ffn_down', 'delta_w_mod', 'delta_b_mod', 'delta_g_mix_pre', 'delta_g_mix_post', 'delta_w_qkv', 'delta_w_o', 'delta_w_fg', 'delta_b_fg', 'delta_g_ffn_pre', 'delta_g_ffn_post', 'delta_w_ffn_gate', 'delta_w_ffn_up', 'delta_w_conv', 'delta_b_conv', 'delta_w_ffn_down', 'new_m_w_mod', 'new_m_b_mod', 'new_m_g_mix_pre', 'new_m_g_mix_post', 'new_m_w_qkv', 'new_m_w_o', 'new_m_w_fg', 'new_m_b_fg', 'new_m_g_ffn_pre', 'new_m_g_ffn_post', 'new_m_w_ffn_gate', 'new_m_w_ffn_up', 'new_m_w_conv', 'new_m_b_conv', 'new_m_w_ffn_down', 'new_v_w_mod', 'new_v_b_mod', 'new_v_g_mix_pre', 'new_v_g_mix_post', 'new_v_w_qkv', 'new_v_w_o', 'new_v_w_fg', 'new_v_b_fg', 'new_v_g_ffn_pre', 'new_v_g_ffn_post', 'new_v_w_ffn_gate', 'new_v_w_ffn_up', 'new_v_w_conv', 'new_v_b_conv', 'new_v_w_ffn_down']
TWIN_LEAF_KINDS = {'loss': 'loss', 'grad_x': 'grad_x', 'grad_w_mod': 'grad_w', 'grad_b_mod': 'grad_w', 'grad_g_mix_pre': 'grad_w', 'grad_g_mix_post': 'grad_w', 'grad_w_qkv': 'grad_w', 'grad_w_o': 'grad_w', 'grad_w_fg': 'grad_w', 'grad_b_fg': 'grad_w', 'grad_g_ffn_pre': 'grad_w', 'grad_g_ffn_post': 'grad_w', 'grad_w_ffn_gate': 'grad_w', 'grad_w_ffn_up': 'grad_w', 'grad_w_conv': 'grad_w', 'grad_b_conv': 'grad_w', 'grad_w_ffn_down': 'grad_w', 'delta_w_mod': 'delta_w', 'delta_b_mod': 'delta_w', 'delta_g_mix_pre': 'delta_w', 'delta_g_mix_post': 'delta_w', 'delta_w_qkv': 'delta_w', 'delta_w_o': 'delta_w', 'delta_w_fg': 'delta_w', 'delta_b_fg': 'delta_w', 'delta_g_ffn_pre': 'delta_w', 'delta_g_ffn_post': 'delta_w', 'delta_w_ffn_gate': 'delta_w', 'delta_w_ffn_up': 'delta_w', 'delta_w_conv': 'delta_w', 'delta_b_conv': 'delta_w', 'delta_w_ffn_down': 'delta_w', 'new_m_w_mod': 'new_m', 'new_m_b_mod': 'new_m', 'new_m_g_mix_pre': 'new_m', 'new_m_g_mix_post': 'new_m', 'new_m_w_qkv': 'new_m', 'new_m_w_o': 'new_m', 'new_m_w_fg': 'new_m', 'new_m_b_fg': 'new_m', 'new_m_g_ffn_pre': 'new_m', 'new_m_g_ffn_post': 'new_m', 'new_m_w_ffn_gate': 'new_m', 'new_m_w_ffn_up': 'new_m', 'new_m_w_conv': 'new_m', 'new_m_b_conv': 'new_m', 'new_m_w_ffn_down': 'new_m', 'new_v_w_mod': 'new_v', 'new_v_b_mod': 'new_v', 'new_v_g_mix_pre': 'new_v', 'new_v_g_mix_post': 'new_v', 'new_v_w_qkv': 'new_v', 'new_v_w_o': 'new_v', 'new_v_w_fg': 'new_v', 'new_v_b_fg': 'new_v', 'new_v_g_ffn_pre': 'new_v', 'new_v_g_ffn_post': 'new_v', 'new_v_w_ffn_gate': 'new_v', 'new_v_w_ffn_up': 'new_v', 'new_v_w_conv': 'new_v', 'new_v_b_conv': 'new_v', 'new_v_w_ffn_down': 'new_v'}


def _forward(args):
    return _fwd_reference(*[args[k] for k in FWD_PARAMS])


def _output_shape():
    out = _jax.eval_shape(lambda: _forward(_fwd_setup_inputs(0)))
    return out.shape, out.dtype

N_MICROBATCH = 1
ADAM_LR = 0.001
ADAM_B1 = 0.9
ADAM_B2 = 0.999
ADAM_EPS = 1e-08
ADAM_WD = 0.01
ADAM_STEP = 10
PER_EXAMPLE_BATCH_AXIS = {'x': 0, 'c': 0, 'loss_target': 0}
SHARED_INPUTS = []
_WEIGHT_DTYPES = {'w_mod': _jnp.float32, 'b_mod': _jnp.float32, 'g_mix_pre': _jnp.float32, 'g_mix_post': _jnp.float32, 'w_qkv': _jnp.float32, 'w_o': _jnp.float32, 'w_fg': _jnp.float32, 'b_fg': _jnp.float32, 'g_ffn_pre': _jnp.float32, 'g_ffn_post': _jnp.float32, 'w_ffn_gate': _jnp.float32, 'w_ffn_up': _jnp.float32, 'w_conv': _jnp.float32, 'b_conv': _jnp.float32, 'w_ffn_down': _jnp.float32}
MOMENT_SCALE = {'w_mod': 2.506282e+00, 'b_mod': 4.397176e+00, 'g_mix_pre': 6.381730e-01, 'g_mix_post': 4.524201e+00, 'w_qkv': 8.518018e-01, 'w_o': 1.565268e+00, 'w_fg': 2.759278e-01, 'b_fg': 6.290194e-01, 'g_ffn_pre': 2.479683e-01, 'g_ffn_post': 3.784406e+00, 'w_ffn_gate': 1.218804e-01, 'w_ffn_up': 1.969455e-01, 'w_conv': 1.711247e-01, 'b_conv': 3.317415e-01, 'w_ffn_down': 3.294783e-01}


def _to_microbatches(a, axis):
    t = _jnp.moveaxis(a, axis, 0)
    t = t.reshape((N_MICROBATCH, t.shape[0] // N_MICROBATCH) + t.shape[1:])
    return _jnp.moveaxis(t, 1, axis + 1)


def setup_inputs(seed: int = 0) -> dict:
    inp = _fwd_setup_inputs(seed)
    key = _jax.random.fold_in(_jax.random.key(seed), 7919)
    shape, _ = _output_shape()
    out = dict(inp)
    out["loss_target"] = _jax.random.normal(_jax.random.fold_in(key, 0), shape, _jnp.float32)
    for i, name in enumerate(TWIN_WEIGHTS):
        w = inp[name].astype(_jnp.float32)
        if MOMENT_SCALE is None:
            s = _jnp.sqrt(_jnp.mean(_jnp.square(w)) + 1e-30)
        else:
            s = MOMENT_SCALE[name]
        km, kv = _jax.random.split(_jax.random.fold_in(key, i + 1))
        out[name] = w
        out["m_" + name] = s * _jax.random.normal(km, w.shape, _jnp.float32)
        out["v_" + name] = (s * s) * _jax.random.uniform(kv, w.shape, _jnp.float32, 0.5, 1.5)
    if N_MICROBATCH > 1:
        for name, axis in PER_EXAMPLE_BATCH_AXIS.items():
            out[name] = _to_microbatches(out[name], axis)
    return {'x': out['x'], 'c': out['c'], 'w_mod': out['w_mod'], 'b_mod': out['b_mod'], 'g_mix_pre': out['g_mix_pre'], 'g_mix_post': out['g_mix_post'], 'w_qkv': out['w_qkv'], 'w_o': out['w_o'], 'w_fg': out['w_fg'], 'b_fg': out['b_fg'], 'g_ffn_pre': out['g_ffn_pre'], 'g_ffn_post': out['g_ffn_post'], 'w_ffn_gate': out['w_ffn_gate'], 'w_ffn_up': out['w_ffn_up'], 'w_conv': out['w_conv'], 'b_conv': out['b_conv'], 'w_ffn_down': out['w_ffn_down'], 'loss_target': out['loss_target'], 'm_w_mod': out['m_w_mod'], 'm_b_mod': out['m_b_mod'], 'm_g_mix_pre': out['m_g_mix_pre'], 'm_g_mix_post': out['m_g_mix_post'], 'm_w_qkv': out['m_w_qkv'], 'm_w_o': out['m_w_o'], 'm_w_fg': out['m_w_fg'], 'm_b_fg': out['m_b_fg'], 'm_g_ffn_pre': out['m_g_ffn_pre'], 'm_g_ffn_post': out['m_g_ffn_post'], 'm_w_ffn_gate': out['m_w_ffn_gate'], 'm_w_ffn_up': out['m_w_ffn_up'], 'm_w_conv': out['m_w_conv'], 'm_b_conv': out['m_b_conv'], 'm_w_ffn_down': out['m_w_ffn_down'], 'v_w_mod': out['v_w_mod'], 'v_b_mod': out['v_b_mod'], 'v_g_mix_pre': out['v_g_mix_pre'], 'v_g_mix_post': out['v_g_mix_post'], 'v_w_qkv': out['v_w_qkv'], 'v_w_o': out['v_w_o'], 'v_w_fg': out['v_w_fg'], 'v_b_fg': out['v_b_fg'], 'v_g_ffn_pre': out['v_g_ffn_pre'], 'v_g_ffn_post': out['v_g_ffn_post'], 'v_w_ffn_gate': out['v_w_ffn_gate'], 'v_w_ffn_up': out['v_w_ffn_up'], 'v_w_conv': out['v_w_conv'], 'v_b_conv': out['v_b_conv'], 'v_w_ffn_down': out['v_w_ffn_down']}


def _loss(weights, diff, rest, loss_target):
    with _jax.named_scope("forward"):
        args = {**rest, TWIN_DIFF_INPUT: diff, **{k: w.astype(_WEIGHT_DTYPES[k]) for k, w in weights.items()}}
        y = _forward(args)
    with _jax.named_scope("loss_head"):
        err = _jnp.square(y.astype(_jnp.float32) - loss_target)
        return 0.5 * _jnp.sum(_jnp.mean(err, axis=-1)) if err.ndim else 0.5 * err


def _adamw(w, g, m, v):
    m = ADAM_B1 * m + (1.0 - ADAM_B1) * g
    v = ADAM_B2 * v + (1.0 - ADAM_B2) * _jnp.square(g)
    m_hat = m / (1.0 - ADAM_B1 ** ADAM_STEP)
    v_hat = v / (1.0 - ADAM_B2 ** ADAM_STEP)
    delta = -ADAM_LR * (m_hat / (_jnp.sqrt(v_hat) + ADAM_EPS) + ADAM_WD * w)
    return delta, m, v


def reference(x, c, w_mod, b_mod, g_mix_pre, g_mix_post, w_qkv, w_o, w_fg, b_fg, g_ffn_pre, g_ffn_post, w_ffn_gate, w_ffn_up, w_conv, b_conv, w_ffn_down, loss_target, m_w_mod, m_b_mod, m_g_mix_pre, m_g_mix_post, m_w_qkv, m_w_o, m_w_fg, m_b_fg, m_g_ffn_pre, m_g_ffn_post, m_w_ffn_gate, m_w_ffn_up, m_w_conv, m_b_conv, m_w_ffn_down, v_w_mod, v_b_mod, v_g_mix_pre, v_g_mix_post, v_w_qkv, v_w_o, v_w_fg, v_b_fg, v_g_ffn_pre, v_g_ffn_post, v_w_ffn_gate, v_w_ffn_up, v_w_conv, v_b_conv, v_w_ffn_down):
    given = dict(x=x, c=c, w_mod=w_mod, b_mod=b_mod, g_mix_pre=g_mix_pre, g_mix_post=g_mix_post, w_qkv=w_qkv, w_o=w_o, w_fg=w_fg, b_fg=b_fg, g_ffn_pre=g_ffn_pre, g_ffn_post=g_ffn_post, w_ffn_gate=w_ffn_gate, w_ffn_up=w_ffn_up, w_conv=w_conv, b_conv=b_conv, w_ffn_down=w_ffn_down, loss_target=loss_target, m_w_mod=m_w_mod, m_b_mod=m_b_mod, m_g_mix_pre=m_g_mix_pre, m_g_mix_post=m_g_mix_post, m_w_qkv=m_w_qkv, m_w_o=m_w_o, m_w_fg=m_w_fg, m_b_fg=m_b_fg, m_g_ffn_pre=m_g_ffn_pre, m_g_ffn_post=m_g_ffn_post, m_w_ffn_gate=m_w_ffn_gate, m_w_ffn_up=m_w_ffn_up, m_w_conv=m_w_conv, m_b_conv=m_b_conv, m_w_ffn_down=m_w_ffn_down, v_w_mod=v_w_mod, v_b_mod=v_b_mod, v_g_mix_pre=v_g_mix_pre, v_g_mix_post=v_g_mix_post, v_w_qkv=v_w_qkv, v_w_o=v_w_o, v_w_fg=v_w_fg, v_b_fg=v_b_fg, v_g_ffn_pre=v_g_ffn_pre, v_g_ffn_post=v_g_ffn_post, v_w_ffn_gate=v_w_ffn_gate, v_w_ffn_up=v_w_ffn_up, v_w_conv=v_w_conv, v_b_conv=v_b_conv, v_w_ffn_down=v_w_ffn_down)
    weights = {n: given[n] for n in TWIN_WEIGHTS}
    shared = {n: given[n] for n in SHARED_INPUTS}
    per_example = {n: given[n] for n in ['x', 'c']}
    grad_fn = _jax.value_and_grad(_loss, argnums=(0, 1))

    def one_microbatch(ex, loss_target):
        ex = dict(ex)
        diff = ex.pop(TWIN_DIFF_INPUT)
        return grad_fn(weights, diff, {**shared, **ex}, loss_target)

    if N_MICROBATCH == 1:
        loss, (grad_w, grad_x) = one_microbatch(per_example, given["loss_target"])
    else:
        def body(carry, xs):
            loss_sum, grad_sum = carry
            l_k, (gw_k, gx_k) = one_microbatch(xs[0], xs[1])
            with _jax.named_scope("update"):
                return (loss_sum + l_k, _jax.tree.map(_jnp.add, grad_sum, gw_k)), gx_k

        init = (_jnp.zeros((), _jnp.float32), _jax.tree.map(_jnp.zeros_like, weights))
        (loss, grad_w), grad_x = _jax.lax.scan(body, init, (per_example, given["loss_target"]))
    with _jax.named_scope("update"):
        delta_w, new_m, new_v = {}, {}, {}
        for n in TWIN_WEIGHTS:
            delta_w[n], new_m[n], new_v[n] = _adamw(weights[n], grad_w[n], given["m_" + n], given["v_" + n])
    return (loss, grad_x, *[grad_w[n] for n in TWIN_WEIGHTS], *[delta_w[n] for n in TWIN_WEIGHTS],
            *[new_m[n] for n in TWIN_WEIGHTS], *[new_v[n] for n in TWIN_WEIGHTS])
```

```python
import functools

import jax
import jax.numpy as jnp
from jax import lax
from jax.experimental import pallas as pl
from jax.experimental.pallas import tpu as pltpu

F32 = jnp.float32
BF16 = jnp.bfloat16
MESH = pl.DeviceIdType.MESH

HEAD_DIM = 64
ATT_BLOCK = 128
LANES = 128
SUBLANES = 8
RMS_EPS = 1e-6
N_MOD = 6
N_CHIPS = 4
N_DEV = 8
ADAM_LR = 0.001
ADAM_B1 = 0.9
ADAM_B2 = 0.999
ADAM_EPS = 1e-08
ADAM_WD = 0.01
ADAM_STEP = 10
VMEM_LIMIT_BYTES = 56 * 1024 * 1024
NEG_BIG = -1e30

ANY = pl.BlockSpec(memory_space=pl.ANY)
VMEM_WHOLE = pl.BlockSpec(memory_space=pltpu.VMEM)


def _params(**kw):
    return pltpu.CompilerParams(vmem_limit_bytes=VMEM_LIMIT_BYTES, **kw)


def _dot(a, b):
    return jnp.dot(a, b, preferred_element_type=F32)


def _dot_nt(a, b):
    return lax.dot_general(a, b, (((1,), (1,)), ((), ())), preferred_element_type=F32)


def _dot_tn(a, b):
    return lax.dot_general(a, b, (((0,), (0,)), ((), ())), preferred_element_type=F32)


def _split_dot(x, t, parts):
    acc = None
    rem = x
    for _ in range(parts):
        piece = rem.astype(BF16)
        rem = rem - piece.astype(F32)
        d = _dot(piece, t)
        acc = d if acc is None else acc + d
    return acc


def _pick(n, prefs):
    for p in prefs:
        if n % p == 0:
            return p
    return n


def _mm(a, b, dims, out_dtype, name, tm=None, tn=None, tk=None):
    if dims == "tn":
        K, M = a.shape
    else:
        M, K = a.shape
    N = b.shape[0] if dims == "nt" else b.shape[1]
    tm = tm or _pick(M, (512, 256, 128))
    tn = tn or _pick(N, (1536, 1408, 1024, 768, 512, 256, 128))
    tk = tk or _pick(K, (1024, 1408, 512, 256, 128))
    nk = K // tk
    grid = (M // tm, N // tn, nk)
    if dims == "tn":
        a_spec = pl.BlockSpec((tk, tm), lambda i, j, k: (k, i))
    else:
        a_spec = pl.BlockSpec((tm, tk), lambda i, j, k: (i, k))
    if dims == "nt":
        b_spec = pl.BlockSpec((tn, tk), lambda i, j, k: (j, k))
    else:
        b_spec = pl.BlockSpec((tk, tn), lambda i, j, k: (k, j))
    o_spec = pl.BlockSpec((tm, tn), lambda i, j, k: (i, j))

    def body(a_ref, b_ref, o_ref, *scratch):
        x = a_ref[...].astype(BF16)
        y = b_ref[...].astype(BF16)
        if dims == "nn":
            r = _dot(x, y)
        elif dims == "nt":
            r = _dot_nt(x, y)
        else:
            r = _dot_tn(x, y)
        if nk == 1:
            o_ref[...] = r.astype(out_dtype)
        else:
            acc = scratch[0]
            k = pl.program_id(2)

            @pl.when(k == 0)
            def _():
                acc[...] = r

            @pl.when(k > 0)
            def _():
                acc[...] += r

            @pl.when(k == nk - 1)
            def _():
                o_ref[...] = acc[...].astype(out_dtype)

    return pl.pallas_call(
        body,
        name=name,
        grid=grid,
        in_specs=[a_spec, b_spec],
        out_specs=o_spec,
        out_shape=jax.ShapeDtypeStruct((M, N), out_dtype),
        scratch_shapes=[pltpu.VMEM((tm, tn), F32)] if nk > 1 else [],
        compiler_params=_params(dimension_semantics=("parallel", "parallel", "arbitrary")),
    )(a, b)


def _row_tile(S):
    return _pick(S, (256, 128, 64, 32, 16, 8))


def _norm_mod(x, g, sc, sh, name):
    S, D = x.shape
    T = _row_tile(S)
    row = pl.BlockSpec((T, D), lambda i: (i, 0))
    vec = pl.BlockSpec((1, D), lambda i: (0, 0))

    def body(x_ref, g_ref, sc_ref, sh_ref, h_ref):
        xv = x_ref[...]
        r = lax.rsqrt(jnp.mean(xv * xv, axis=-1, keepdims=True) + RMS_EPS)
        n = (xv * r) * g_ref[...]
        h_ref[...] = (n * (1.0 + sc_ref[...]) + sh_ref[...]).astype(BF16)

    return pl.pallas_call(
        body, name=name, grid=(S // T,), in_specs=[row, vec, vec, vec], out_specs=row,
        out_shape=jax.ShapeDtypeStruct((S, D), BF16), compiler_params=_params(),
    )(x, g, sc, sh)


def _norm_mod_bwd(x, g, sc, sh, dhs, gres, name):
    S, D = x.shape
    T = _row_tile(S)
    n_dh = len(dhs)
    row = pl.BlockSpec((T, D), lambda i: (i, 0))
    vec = pl.BlockSpec((1, D), lambda i: (0, 0))

    def body(x_ref, g_ref, sc_ref, sh_ref, *refs):
        dh_refs = refs[:n_dh]
        gres_ref, dx_ref, dg_ref, dsc_ref, dsh_ref = refs[n_dh:]
        xv = x_ref[...]
        r = lax.rsqrt(jnp.mean(xv * xv, axis=-1, keepdims=True) + RMS_EPS)
        xn = xv * r
        n = xn * g_ref[...]
        dh = dh_refs[0][...]
        for extra in dh_refs[1:]:
            dh = dh + extra[...]
        dn = dh * (1.0 + sc_ref[...])
        dxn = dn * g_ref[...]
        dx = r * (dxn - xn * jnp.mean(dxn * xn, axis=-1, keepdims=True))
        dx_ref[...] = gres_ref[...] + dx

        @pl.when(pl.program_id(0) == 0)
        def _():
            dg_ref[...] = jnp.zeros_like(dg_ref)
            dsc_ref[...] = jnp.zeros_like(dsc_ref)
            dsh_ref[...] = jnp.zeros_like(dsh_ref)

        dg_ref[...] += jnp.sum(dn * xn, axis=0, keepdims=True)
        dsc_ref[...] += jnp.sum(dh * n, axis=0, keepdims=True)
        dsh_ref[...] += jnp.sum(dh, axis=0, keepdims=True)

    vshape = jax.ShapeDtypeStruct((1, D), F32)
    return pl.pallas_call(
        body, name=name, grid=(S // T,), in_specs=[row, vec, vec, vec] + [row] * (n_dh + 1),
        out_specs=[row, vec, vec, vec],
        out_shape=[jax.ShapeDtypeStruct((S, D), F32), vshape, vshape, vshape],
        compiler_params=_params(dimension_semantics=("arbitrary",)),
    )(x, g, sc, sh, *dhs, gres)


def _post_res(x, p, gp, gt, name):
    S, D = x.shape
    T = _row_tile(S)
    row = pl.BlockSpec((T, D), lambda i: (i, 0))
    vec = pl.BlockSpec((1, D), lambda i: (0, 0))

    def body(x_ref, p_ref, gp_ref, gt_ref, o_ref):
        pv = p_ref[...]
        r = lax.rsqrt(jnp.mean(pv * pv, axis=-1, keepdims=True) + RMS_EPS)
        o_ref[...] = x_ref[...] + gt_ref[...] * ((pv * r) * gp_ref[...])

    return pl.pallas_call(
        body, name=name, grid=(S // T,), in_specs=[row, row, vec, vec], out_specs=row,
        out_shape=jax.ShapeDtypeStruct((S, D), F32), compiler_params=_params(),
    )(x, p, gp, gt)


def _post_res_bwd(p, gp, gt, g, name):
    S, D = p.shape
    T = _row_tile(S)
    row = pl.BlockSpec((T, D), lambda i: (i, 0))
    vec = pl.BlockSpec((1, D), lambda i: (0, 0))

    def body(p_ref, gp_ref, gt_ref, g_ref, dp_ref, dgp_ref, dgt_ref):
        pv = p_ref[...]
        gv = g_ref[...]
        r = lax.rsqrt(jnp.mean(pv * pv, axis=-1, keepdims=True) + RMS_EPS)
        pn = pv * r
        n2 = pn * gp_ref[...]
        dn2 = gv * gt_ref[...]
        dpn = dn2 * gp_ref[...]
        dp = r * (dpn - pn * jnp.mean(dpn * pn, axis=-1, keepdims=True))
        dp_ref[...] = dp.astype(BF16)

        @pl.when(pl.program_id(0) == 0)
        def _():
            dgp_ref[...] = jnp.zeros_like(dgp_ref)
            dgt_ref[...] = jnp.zeros_like(dgt_ref)

        dgp_ref[...] += jnp.sum(dn2 * pn, axis=0, keepdims=True)
        dgt_ref[...] += jnp.sum(gv * n2, axis=0, keepdims=True)

    vshape = jax.ShapeDtypeStruct((1, D), F32)
    return pl.pallas_call(
        body, name=name, grid=(S // T,), in_specs=[row, vec, vec, row], out_specs=[row, vec, vec],
        out_shape=[jax.ShapeDtypeStruct((S, D), BF16), vshape, vshape],
        compiler_params=_params(dimension_semantics=("arbitrary",)),
    )(p, gp, gt, g)


def _loss_head(y, target, name):
    S, D = y.shape
    T = _row_tile(S)
    row = pl.BlockSpec((T, D), lambda i: (i, 0))
    vec = pl.BlockSpec((1, D), lambda i: (0, 0))

    def body(y_ref, t_ref, dy_ref, sq_ref):
        e = y_ref[...] - t_ref[...]
        dy_ref[...] = e * (1.0 / D)

        @pl.when(pl.program_id(0) == 0)
        def _():
            sq_ref[...] = jnp.zeros_like(sq_ref)

        sq_ref[...] += jnp.sum(e * e, axis=0, keepdims=True)

    return pl.pallas_call(
        body, name=name, grid=(S // T,), in_specs=[row, row], out_specs=[row, vec],
        out_shape=[jax.ShapeDtypeStruct((S, D), F32), jax.ShapeDtypeStruct((1, D), F32)],
        compiler_params=_params(dimension_semantics=("arbitrary",)),
    )(y, target)


def _shift_down(v, k, rows):
    return jnp.where(rows >= k, pltpu.roll(v, k, 0), 0.0)


def _shift_up(v, k, rows, S):
    return jnp.where(rows < S - k, pltpu.roll(v, S - k, 0), 0.0)


def _conv_gate(gu, wc, bc, name):
    S, F2 = gu.shape
    F = F2 // 2
    C = LANES

    def body(gu_ref, w_ref, b_ref, a_ref):
        rows = lax.broadcasted_iota(jnp.int32, (S, C), 0)
        gate = gu_ref[:, :C]
        up = gu_ref[:, C:]
        w = w_ref[...]
        gc = w[2:3] * gate + w[1:2] * _shift_down(gate, 1, rows) + w[0:1] * _shift_down(gate, 2, rows) + b_ref[...]
        a_ref[...] = (gc * (1.0 / (1.0 + jnp.exp(-gc))) * up).astype(BF16)

    return pl.pallas_call(
        body, name=name, grid=(F // C,),
        in_specs=[pl.BlockSpec((S, 2 * C), lambda j: (0, j)), pl.BlockSpec((3, C), lambda j: (0, j)),
                  pl.BlockSpec((1, C), lambda j: (0, j))],
        out_specs=pl.BlockSpec((S, C), lambda j: (0, j)),
        out_shape=jax.ShapeDtypeStruct((S, F), BF16), compiler_params=_params(),
    )(gu, wc, bc)


def _conv_gate_bwd(gu, wc, bc, da, name):
    S, F2 = gu.shape
    F = F2 // 2
    C = LANES

    def body(gu_ref, w_ref, b_ref, da_ref, dgu_ref, dw_ref, db_ref):
        rows = lax.broadcasted_iota(jnp.int32, (S, C), 0)
        gate = gu_ref[:, :C]
        up = gu_ref[:, C:]
        dav = da_ref[...]
        w = w_ref[...]
        g1 = _shift_down(gate, 1, rows)
        g2 = _shift_down(gate, 2, rows)
        gc = w[2:3] * gate + w[1:2] * g1 + w[0:1] * g2 + b_ref[...]
        sg = 1.0 / (1.0 + jnp.exp(-gc))
        dgu_ref[:, C:] = (dav * (gc * sg)).astype(BF16)
        dgc = dav * up * (sg * (1.0 + gc * (1.0 - sg)))
        db_ref[...] = jnp.sum(dgc, axis=0, keepdims=True)
        dw_ref[0:1, :] = jnp.sum(dgc * g2, axis=0, keepdims=True)
        dw_ref[1:2, :] = jnp.sum(dgc * g1, axis=0, keepdims=True)
        dw_ref[2:3, :] = jnp.sum(dgc * gate, axis=0, keepdims=True)
        dgate = w[2:3] * dgc + w[1:2] * _shift_up(dgc, 1, rows, S) + w[0:1] * _shift_up(dgc, 2, rows, S)
        dgu_ref[:, :C] = dgate.astype(BF16)

    return pl.pallas_call(
        body, name=name, grid=(F // C,),
        in_specs=[pl.BlockSpec((S, 2 * C), lambda j: (0, j)), pl.BlockSpec((3, C), lambda j: (0, j)),
                  pl.BlockSpec((1, C), lambda j: (0, j)), pl.BlockSpec((S, C), lambda j: (0, j))],
        out_specs=[pl.BlockSpec((S, 2 * C), lambda j: (0, j)), pl.BlockSpec((3, C), lambda j: (0, j)),
                   pl.BlockSpec((1, C), lambda j: (0, j))],
        out_shape=[jax.ShapeDtypeStruct((S, F2), BF16), jax.ShapeDtypeStruct((3, F), F32),
                   jax.ShapeDtypeStruct((1, F), F32)],
        compiler_params=_params(),
    )(gu, wc, bc, da)


def _qkv_specs(S, D):
    B = ATT_BLOCK
    nb = D // LANES
    q_spec = pl.BlockSpec((B, LANES), lambda p, i: (i, p))
    k_spec = pl.BlockSpec((S, LANES), lambda p, i: (0, nb + p))
    v_spec = pl.BlockSpec((S, LANES), lambda p, i: (0, 2 * nb + p))
    return q_spec, k_spec, v_spec


def _tile_iotas():
    B = ATT_BLOCK
    row = lax.broadcasted_iota(jnp.int32, (B, B), 0)
    col = lax.broadcasted_iota(jnp.int32, (B, B), 1)
    return row, col


def _with_ones(mask):
    B = ATT_BLOCK
    return jnp.concatenate([jnp.where(mask, 1.0, 0.0).astype(BF16), jnp.ones((B, B), BF16)], axis=1)


def _log_sigmoids(z):
    sp = jnp.log(1.0 + jnp.exp(-jnp.abs(z)))
    lb = jnp.minimum(z, 0.0) - sp
    return lb, lb - z


def _sb_fwd(qkv, name):
    S, D3 = qkv.shape
    D = D3 // 3
    B = ATT_BLOCK
    NP = D // LANES
    NQ = S // B
    scale = HEAD_DIM ** -0.5
    q_spec, k_spec, v_spec = _qkv_specs(S, D)

    def body(q_ref, k_ref, v_ref, o_ref, lt_ref):
        qi = pl.program_id(1)
        row, col = _tile_iotas()
        strict = col < row
        t_suffix = _with_ones(row > col)
        for hh in range(2):
            lanes = slice(hh * HEAD_DIM, (hh + 1) * HEAD_DIM)
            q = q_ref[:, lanes]

            def tile(j, carry, masked):
                c, acc = carry
                r0 = pl.multiple_of(j * B, B)
                k = k_ref[pl.ds(r0, B), lanes]
                v = v_ref[pl.ds(r0, B), lanes]
                z = _dot_nt(q, k) * scale
                lb, l1 = _log_sigmoids(z)
                if masked:
                    l1 = jnp.where(strict, l1, 0.0)
                sums = _split_dot(l1, t_suffix, 2)
                a = jnp.exp(lb + c + sums[:, :B])
                if masked:
                    a = jnp.where(strict, a, 0.0)
                acc = acc + _dot(a.astype(BF16), v)
                return c + sums[:, B:], acc

            carry = (jnp.zeros((B, B), F32), jnp.zeros((B, HEAD_DIM), F32))
            carry = tile(qi, carry, True)
            c, acc = lax.fori_loop(0, qi, lambda it, cr: tile(qi - 1 - it, cr, False), carry)
            o_ref[:, lanes] = acc
            lt_ref[:, hh * B:(hh + 1) * B] = c

    return pl.pallas_call(
        body, name=name, grid=(NP, NQ), in_specs=[q_spec, k_spec, v_spec],
        out_specs=[pl.BlockSpec((B, LANES), lambda p, i: (i, p)), pl.BlockSpec((B, 2 * B), lambda p, i: (i, p))],
        out_shape=[jax.ShapeDtypeStruct((S, D), F32), jax.ShapeDtypeStruct((S, 2 * NP * B), F32)],
        compiler_params=_params(),
    )(qkv, qkv, qkv)


def _sb_bwd(qkv, do, lt, name):
    S, D3 = qkv.shape
    D = D3 // 3
    B = ATT_BLOCK
    NP = D // LANES
    NQ = S // B
    scale = HEAD_DIM ** -0.5
    q_spec, k_spec, v_spec = _qkv_specs(S, D)
    tile_spec = pl.BlockSpec((B, LANES), lambda p, i: (i, p))
    seq_spec = pl.BlockSpec((S, LANES), lambda p, i: (0, p))

    def body(q_ref, k_ref, v_ref, do_ref, lt_ref, dq_ref, dk_ref, dv_ref, dk_acc, dv_acc):
        qi = pl.program_id(1)
        row, col = _tile_iotas()
        strict = col < row
        t_prefix = _with_ones(row <= col)
        t_before = _with_ones(row < col)

        @pl.when(qi == 0)
        def _():
            dk_acc[...] = jnp.zeros_like(dk_acc)
            dv_acc[...] = jnp.zeros_like(dv_acc)

        for hh in range(2):
            lanes = slice(hh * HEAD_DIM, (hh + 1) * HEAD_DIM)
            q = q_ref[:, lanes]
            dob = do_ref[:, lanes].astype(BF16)
            ltot = lt_ref[:, hh * B:(hh + 1) * B]

            def tile(j, carry, masked):
                pre, cu, dq = carry
                r0 = pl.multiple_of(j * B, B)
                k = k_ref[pl.ds(r0, B), lanes]
                v = v_ref[pl.ds(r0, B), lanes]
                z = _dot_nt(q, k) * scale
                lb, l1 = _log_sigmoids(z)
                if masked:
                    l1 = jnp.where(strict, l1, 0.0)
                sums = _split_dot(l1, t_prefix, 2)
                a = jnp.exp(lb + (ltot - pre - sums[:, :B]))
                if masked:
                    a = jnp.where(strict, a, 0.0)
                u = a * _dot_nt(dob, v)
                usums = _split_dot(u, t_before, 2)
                beta = jnp.exp(lb)
                dz = (u - (u + cu + usums[:, :B]) * beta) * scale
                if masked:
                    dz = jnp.where(strict, dz, 0.0)
                dzb = dz.astype(BF16)
                dq = dq + _dot(dzb, k)
                dk_acc[pl.ds(r0, B), lanes] += _dot_tn(dzb, q)
                dv_acc[pl.ds(r0, B), lanes] += _dot_tn(a.astype(BF16), dob)
                return pre + sums[:, B:], cu + usums[:, B:], dq

            carry = (jnp.zeros((B, B), F32), jnp.zeros((B, B), F32), jnp.zeros((B, HEAD_DIM), F32))
            carry = lax.fori_loop(0, qi, lambda j, cr: tile(j, cr, False), carry)
            _, _, dq = tile(qi, carry, True)
            dq_ref[:, lanes] = dq.astype(BF16)

        @pl.when(qi == NQ - 1)
        def _():
            dk_ref[...] = dk_acc[...].astype(BF16)
            dv_ref[...] = dv_acc[...].astype(BF16)

    out = jax.ShapeDtypeStruct((S, D), BF16)
    return pl.pallas_call(
        body, name=name, grid=(NP, NQ),
        in_specs=[q_spec, k_spec, v_spec, tile_spec, pl.BlockSpec((B, 2 * B), lambda p, i: (i, p))],
        out_specs=[tile_spec, seq_spec, seq_spec], out_shape=[out, out, out],
        scratch_shapes=[pltpu.VMEM((S, LANES), F32), pltpu.VMEM((S, LANES), F32)],
        compiler_params=_params(dimension_semantics=("arbitrary", "arbitrary")),
    )(qkv, qkv, qkv, do, lt)


def _fox_fwd(qkv, cq, ck, name):
    S, D3 = qkv.shape
    D = D3 // 3
    B = ATT_BLOCK
    NP = D // LANES
    NQ = S // B
    scale = HEAD_DIM ** -0.5
    q_spec, k_spec, v_spec = _qkv_specs(S, D)
    stat_spec = pl.BlockSpec((B, 2 * B), lambda p, i: (i, p))
    ck_spec = pl.BlockSpec((2, NQ, SUBLANES, B), lambda p, i: (p, 0, 0, 0))

    def body(q_ref, k_ref, v_ref, cq_ref, ck_ref, o_ref, lse_ref):
        qi = pl.program_id(1)
        row, col = _tile_iotas()
        causal = col <= row
        for hh in range(2):
            lanes = slice(hh * HEAD_DIM, (hh + 1) * HEAD_DIM)
            q = q_ref[:, lanes]
            cqv = cq_ref[:, hh * B:(hh + 1) * B]

            def tile(j, carry, masked):
                m, lsum, acc = carry
                r0 = pl.multiple_of(j * B, B)
                k = k_ref[pl.ds(r0, B), lanes]
                v = v_ref[pl.ds(r0, B), lanes]
                s = _dot_nt(q, k) * scale + (cqv - ck_ref[hh, j][0:1, :])
                if masked:
                    s = jnp.where(causal, s, NEG_BIG)
                m_new = jnp.maximum(m, jnp.max(s, axis=1, keepdims=True))
                p = jnp.exp(s - m_new)
                alpha = jnp.exp(m - m_new)
                lsum = alpha * lsum + jnp.sum(p, axis=1, keepdims=True)
                acc = alpha * acc + _dot(p.astype(BF16), v)
                return m_new, lsum, acc

            carry = (jnp.full((B, 1), NEG_BIG, F32), jnp.zeros((B, 1), F32), jnp.zeros((B, HEAD_DIM), F32))
            carry = tile(qi, carry, True)
            m, lsum, acc = lax.fori_loop(0, qi, lambda it, cr: tile(qi - 1 - it, cr, False), carry)
            o_ref[:, lanes] = acc * (1.0 / lsum)
            lse_ref[:, hh * B:(hh + 1) * B] = jnp.broadcast_to(m + jnp.log(lsum), (B, B))

    return pl.pallas_call(
        body, name=name, grid=(NP, NQ), in_specs=[q_spec, k_spec, v_spec, stat_spec, ck_spec],
        out_specs=[pl.BlockSpec((B, LANES), lambda p, i: (i, p)), stat_spec],
        out_shape=[jax.ShapeDtypeStruct((S, D), F32), jax.ShapeDtypeStruct((S, 2 * NP * B), F32)],
        compiler_params=_params(),
    )(qkv, qkv, qkv, cq, ck)


def _fox_bwd(qkv, do, o, lse, cq, ck, name):
    S, D3 = qkv.shape
    D = D3 // 3
    B = ATT_BLOCK
    NP = D // LANES
    NQ = S // B
    scale = HEAD_DIM ** -0.5
    q_spec, k_spec, v_spec = _qkv_specs(S, D)
    tile_spec = pl.BlockSpec((B, LANES), lambda p, i: (i, p))
    seq_spec = pl.BlockSpec((S, LANES), lambda p, i: (0, p))
    stat_spec = pl.BlockSpec((B, 2 * B), lambda p, i: (i, p))
    ck_spec = pl.BlockSpec((2, NQ, SUBLANES, B), lambda p, i: (p, 0, 0, 0))

    def body(q_ref, k_ref, v_ref, do_ref, o_ref, lse_ref, cq_ref, ck_ref,
             dq_ref, dk_ref, dv_ref, dcq_ref, dck_ref, dk_acc, dv_acc):
        qi = pl.program_id(1)
        row, col = _tile_iotas()
        causal = col <= row

        @pl.when(qi == 0)
        def _():
            dk_acc[...] = jnp.zeros_like(dk_acc)
            dv_acc[...] = jnp.zeros_like(dv_acc)
            dck_ref[...] = jnp.zeros_like(dck_ref)

        for hh in range(2):
            lanes = slice(hh * HEAD_DIM, (hh + 1) * HEAD_DIM)
            q = q_ref[:, lanes]
            dof = do_ref[:, lanes]
            dob = dof.astype(BF16)
            delta = jnp.sum(dof * o_ref[:, lanes], axis=1, keepdims=True)
            cqv = cq_ref[:, hh * B:(hh + 1) * B]
            lsev = lse_ref[:, hh * B:(hh + 1) * B]

            def tile(j, carry, masked):
                dq, rowsum = carry
                r0 = pl.multiple_of(j * B, B)
                k = k_ref[pl.ds(r0, B), lanes]
                v = v_ref[pl.ds(r0, B), lanes]
                s = _dot_nt(q, k) * scale + (cqv - ck_ref[hh, j][0:1, :])
                p = jnp.exp(s - lsev)
                if masked:
                    p = jnp.where(causal, p, 0.0)
                ds = p * (_dot_nt(dob, v) - delta)
                dsb = (ds * scale).astype(BF16)
                dq = dq + _dot(dsb, k)
                dk_acc[pl.ds(r0, B), lanes] += _dot_tn(dsb, q)
                dv_acc[pl.ds(r0, B), lanes] += _dot_tn(p.astype(BF16), dob)
                colsum = jnp.sum(ds, axis=0, keepdims=True)
                dck_ref[hh, j] = dck_ref[hh, j] - jnp.broadcast_to(colsum, (SUBLANES, B))
                return dq, rowsum + jnp.sum(ds, axis=1, keepdims=True)

            carry = (jnp.zeros((B, HEAD_DIM), F32), jnp.zeros((B, 1), F32))
            carry = lax.fori_loop(0, qi, lambda j, cr: tile(j, cr, False), carry)
            dq, rowsum = tile(qi, carry, True)
            dq_ref[:, lanes] = dq.astype(BF16)
            dcq_ref[:, hh * B:(hh + 1) * B] = jnp.broadcast_to(rowsum, (B, B))

        @pl.when(qi == NQ - 1)
        def _():
            dk_ref[...] = dk_acc[...].astype(BF16)
            dv_ref[...] = dv_acc[...].astype(BF16)

    out = jax.ShapeDtypeStruct((S, D), BF16)
    return pl.pallas_call(
        body, name=name, grid=(NP, NQ),
        in_specs=[q_spec, k_spec, v_spec, tile_spec, tile_spec, stat_spec, stat_spec, ck_spec],
        out_specs=[tile_spec, seq_spec, seq_spec, stat_spec, ck_spec],
        out_shape=[out, out, out, jax.ShapeDtypeStruct((S, 2 * NP * B), F32),
                   jax.ShapeDtypeStruct((2 * NP, NQ, SUBLANES, B), F32)],
        scratch_shapes=[pltpu.VMEM((S, LANES), F32), pltpu.VMEM((S, LANES), F32)],
        compiler_params=_params(dimension_semantics=("arbitrary", "arbitrary")),
    )(qkv, qkv, qkv, do, o, lse, cq, ck)


def _forget_cumsum(f3, bias, name):
    NQ, NH, B = f3.shape

    def body(f_ref, b_ref, cum_ref):
        row = lax.broadcasted_iota(jnp.int32, (B, B), 0)
        col = lax.broadcasted_iota(jnp.int32, (B, B), 1)
        t_incl = jnp.where(row <= col, 1.0, 0.0).astype(BF16)

        def step(b, carry):
            lf, _ = _log_sigmoids(f_ref[b] + b_ref[...])
            cum = _split_dot(lf, t_incl, 3) + carry
            cum_ref[b] = cum
            return jnp.broadcast_to(cum[:, B - 1:B], (NH, B))

        lax.fori_loop(0, NQ, step, jnp.zeros((NH, B), F32))

    return pl.pallas_call(
        body, name=name, in_specs=[VMEM_WHOLE, VMEM_WHOLE], out_specs=VMEM_WHOLE,
        out_shape=jax.ShapeDtypeStruct((NQ, NH, B), F32), compiler_params=_params(),
    )(f3, bias)


def _forget_cumsum_bwd(dcum3, f3, bias, name):
    NQ, NH, B = f3.shape

    def body(d_ref, f_ref, b_ref, df_ref, tot_ref):
        row = lax.broadcasted_iota(jnp.int32, (B, B), 0)
        col = lax.broadcasted_iota(jnp.int32, (B, B), 1)
        t_rev = jnp.where(row >= col, 1.0, 0.0).astype(BF16)

        def step(it, carry):
            run, tot = carry
            b = NQ - 1 - it
            dlf = _split_dot(d_ref[b], t_rev, 3) + run
            f = f_ref[b] + b_ref[...]
            df = dlf * (1.0 / (1.0 + jnp.exp(f)))
            df_ref[b] = df
            return jnp.broadcast_to(dlf[:, 0:1], (NH, B)), tot + df

        _, tot = lax.fori_loop(0, NQ, step, (jnp.zeros((NH, B), F32), jnp.zeros((NH, B), F32)))
        tot_ref[...] = _split_dot(tot, jnp.ones((B, B), BF16), 3)

    return pl.pallas_call(
        body, name=name, in_specs=[VMEM_WHOLE, VMEM_WHOLE, VMEM_WHOLE], out_specs=[VMEM_WHOLE, VMEM_WHOLE],
        out_shape=[jax.ShapeDtypeStruct((NQ, NH, B), F32), jax.ShapeDtypeStruct((NH, B), F32)],
        compiler_params=_params(),
    )(dcum3, f3, bias)


def _silu(x, name):
    def body(x_ref, o_ref):
        v = x_ref[...]
        o_ref[...] = v * (1.0 / (1.0 + jnp.exp(-v)))

    return pl.pallas_call(body, name=name, in_specs=[VMEM_WHOLE], out_specs=VMEM_WHOLE,
                          out_shape=jax.ShapeDtypeStruct(x.shape, F32), compiler_params=_params())(x)


def _sum_leading(x, name):
    N, R, C = x.shape
    T = _pick(R, (256, 128, 64, 32, 16, 8))

    def body(x_ref, o_ref):
        acc = x_ref[0].astype(F32)
        for r in range(1, N):
            acc = acc + x_ref[r].astype(F32)
        o_ref[...] = acc

    return pl.pallas_call(
        body, name=name, grid=(R // T,), in_specs=[pl.BlockSpec((N, T, C), lambda i: (0, i, 0))],
        out_specs=pl.BlockSpec((T, C), lambda i: (i, 0)), out_shape=jax.ShapeDtypeStruct((R, C), F32),
        compiler_params=_params(),
    )(x)


def _adamw(w, g, m, v, name):
    shape = w.shape
    C = shape[-1]
    R = w.size // C
    T = R
    for cand in (512, 256, 128, 64, 32, 16, 8):
        if R % cand == 0 and cand * C * 4 <= (1 << 20):
            T = cand
            break
    spec = pl.BlockSpec((T, C), lambda i: (i, 0))
    c1 = 1.0 / (1.0 - ADAM_B1 ** ADAM_STEP)
    c2 = 1.0 / (1.0 - ADAM_B2 ** ADAM_STEP)

    def body(w_ref, g_ref, m_ref, v_ref, d_ref, nm_ref, nv_ref):
        gv = g_ref[...]
        nm = ADAM_B1 * m_ref[...] + (1.0 - ADAM_B1) * gv
        nv = ADAM_B2 * v_ref[...] + (1.0 - ADAM_B2) * (gv * gv)
        nm_ref[...] = nm
        nv_ref[...] = nv
        d_ref[...] = -ADAM_LR * ((nm * c1) / (jnp.sqrt(nv * c2) + ADAM_EPS) + ADAM_WD * w_ref[...])

    out = jax.ShapeDtypeStruct((R, C), F32)
    d, nm, nv = pl.pallas_call(
        body, name=name, grid=(R // T,), in_specs=[spec] * 4, out_specs=[spec] * 3, out_shape=[out] * 3,
        compiler_params=_params(),
    )(w.reshape(R, C), g.reshape(R, C), m.reshape(R, C), v.reshape(R, C))
    return d.reshape(shape), nm.reshape(shape), nv.reshape(shape)


def _mesh_pos():
    return lax.axis_index("x"), lax.axis_index("y"), lax.axis_index("c")


def _all_gather_small(x2d, name):
    m_per, n = x2d.shape

    def body(x_ref, out_ref, send_sems, recv_sems, local_sem):
        x, y, c = _mesh_pos()
        me, sibling = (x, y, c), (x, y, 1 - c)
        chips = [(1 - x, y), (x, 1 - y), (1 - x, 1 - y)]

        def rows(px, py, pc):
            return out_ref.at[pl.ds((4 * px + 2 * py + pc) * m_per, m_per), :]

        def copy(k, block, to, src=None):
            return pltpu.make_async_remote_copy(
                src_ref=rows(*block) if src is None else src, dst_ref=rows(*block),
                send_sem=send_sems.at[k], recv_sem=recv_sems.at[k], device_id=to, device_id_type=MESH)

        mine = pltpu.make_async_copy(x_ref, rows(*me), local_sem)
        mine.start()
        first = [copy(0, me, sibling, src=x_ref)]
        first += [copy(1 + j, me, (*chip, c), src=x_ref) for j, chip in enumerate(chips)]
        for cp in first:
            cp.start()
        passed = [copy(4 + j, (*chip, c), sibling) for j, chip in enumerate(chips)]
        for j, chip in enumerate(chips):
            copy(1 + j, (*chip, c), me).wait_recv()
            passed[j].start()
        copy(0, sibling, me).wait_recv()
        for j, chip in enumerate(chips):
            copy(4 + j, (*chip, 1 - c), me).wait_recv()
        for cp in first + passed:
            cp.wait_send()
        mine.wait()

    return pl.pallas_call(
        body, name=name, out_shape=jax.ShapeDtypeStruct((N_DEV * m_per, n), x2d.dtype),
        in_specs=[VMEM_WHOLE], out_specs=VMEM_WHOLE,
        scratch_shapes=[pltpu.SemaphoreType.DMA((7,)), pltpu.SemaphoreType.DMA((7,)), pltpu.SemaphoreType.DMA],
        compiler_params=_params(),
    )(x2d)


def _all_gather_weights(ws, name):
    n = len(ws)

    def body(*refs):
        ins, outs = refs[:n], refs[n:2 * n]
        send_sems, recv_sems, local_sems = refs[2 * n:]
        x, y, c = _mesh_pos()
        k_me = 2 * x + y
        chips = [(1 - x, y), (x, 1 - y), (1 - x, 1 - y)]

        def remote(i, j, slot):
            px, py = chips[j]
            return pltpu.make_async_remote_copy(
                src_ref=ins[i], dst_ref=outs[i].at[slot], send_sem=send_sems.at[i, j],
                recv_sem=recv_sems.at[i, j], device_id=(px, py, c), device_id_type=MESH)

        local = [pltpu.make_async_copy(ins[i], outs[i].at[k_me], local_sems.at[i]) for i in range(n)]
        for i in range(n):
            local[i].start()
            for j in range(3):
                remote(i, j, k_me).start()
        for i in range(n):
            for j, (px, py) in enumerate(chips):
                remote(i, j, 2 * px + py).wait_recv()
        for i in range(n):
            for j in range(3):
                remote(i, j, k_me).wait_send()
            local[i].wait()

    return pl.pallas_call(
        body, name=name,
        out_shape=[jax.ShapeDtypeStruct((N_CHIPS,) + w.shape, w.dtype) for w in ws],
        in_specs=[ANY] * n, out_specs=[ANY] * n,
        scratch_shapes=[pltpu.SemaphoreType.DMA((n, 3)), pltpu.SemaphoreType.DMA((n, 3)),
                        pltpu.SemaphoreType.DMA((n,))],
        compiler_params=_params(),
    )(*ws)


def _scatter_grad_pieces(gs, name):
    n = len(gs)
    halves = [g.shape[1] // 2 for g in gs]

    def body(*refs):
        ins, outs = refs[:n], refs[n:2 * n]
        send_sems, recv_sems, local_sems = refs[2 * n:]
        x, y, c = _mesh_pos()

        def flip(v, bit):
            return 1 - v if bit else v

        def piece(i, px, py, pc):
            return ins[i].at[2 * px + py, pl.ds(pc * halves[i], halves[i])]

        def remote(i, r):
            px, py, pc = flip(x, r & 4), flip(y, r & 2), flip(c, r & 1)
            return pltpu.make_async_remote_copy(
                src_ref=piece(i, px, py, pc), dst_ref=outs[i].at[r], send_sem=send_sems.at[i, r - 1],
                recv_sem=recv_sems.at[i, r - 1], device_id=(px, py, pc), device_id_type=MESH)

        local = [pltpu.make_async_copy(piece(i, x, y, c), outs[i].at[0], local_sems.at[i]) for i in range(n)]
        for i in range(n):
            local[i].start()
            for r in range(1, N_DEV):
                remote(i, r).start()
        for i in range(n):
            for r in range(1, N_DEV):
                remote(i, r).wait_recv()
        for i in range(n):
            for r in range(1, N_DEV):
                remote(i, r).wait_send()
            local[i].wait()

    return pl.pallas_call(
        body, name=name,
        out_shape=[jax.ShapeDtypeStruct((N_DEV, h) + g.shape[2:], g.dtype) for g, h in zip(gs, halves)],
        in_specs=[ANY] * n, out_specs=[ANY] * n,
        scratch_shapes=[pltpu.SemaphoreType.DMA((n, N_DEV - 1)), pltpu.SemaphoreType.DMA((n, N_DEV - 1)),
                        pltpu.SemaphoreType.DMA((n,))],
        compiler_params=_params(),
    )(*gs)


def _swap_halves(hs, name):
    n = len(hs)

    def body(*refs):
        ins, outs = refs[:n], refs[n:2 * n]
        send_sems, recv_sems, local_sems = refs[2 * n:]
        x, y, c = _mesh_pos()

        def remote(i, slot):
            return pltpu.make_async_remote_copy(
                src_ref=ins[i], dst_ref=outs[i].at[slot], send_sem=send_sems.at[i], recv_sem=recv_sems.at[i],
                device_id=(x, y, 1 - c), device_id_type=MESH)

        local = [pltpu.make_async_copy(ins[i], outs[i].at[c], local_sems.at[i]) for i in range(n)]
        for i in range(n):
            local[i].start()
            remote(i, c).start()
        for i in range(n):
            remote(i, 1 - c).wait_recv()
        for i in range(n):
            remote(i, c).wait_send()
            local[i].wait()

    return pl.pallas_call(
        body, name=name,
        out_shape=[jax.ShapeDtypeStruct((2,) + h.shape, h.dtype) for h in hs],
        in_specs=[ANY] * n, out_specs=[ANY] * n,
        scratch_shapes=[pltpu.SemaphoreType.DMA((n,)), pltpu.SemaphoreType.DMA((n,)), pltpu.SemaphoreType.DMA((n,))],
        compiler_params=_params(),
    )(*hs)


def _pad_rows(a, rows):
    return jnp.pad(a, ((0, rows - a.shape[0]), (0, 0)))


def kernel(x, c, w_mod, b_mod, g_mix_pre, g_mix_post, w_qkv, w_o, w_fg, b_fg, g_ffn_pre, g_ffn_post, w_ffn_gate, w_ffn_up, w_conv, b_conv, w_ffn_down, loss_target, m_w_mod, m_b_mod, m_g_mix_pre, m_g_mix_post, m_w_qkv, m_w_o, m_w_fg, m_b_fg, m_g_ffn_pre, m_g_ffn_post, m_w_ffn_gate, m_w_ffn_up, m_w_conv, m_b_conv, m_w_ffn_down, v_w_mod, v_b_mod, v_g_mix_pre, v_g_mix_post, v_w_qkv, v_w_o, v_w_fg, v_b_fg, v_g_ffn_pre, v_g_ffn_post, v_w_ffn_gate, v_w_ffn_up, v_w_conv, v_b_conv, v_w_ffn_down):
    xs = x[0]
    target = loss_target[0]
    S, D = xs.shape
    L = w_mod.shape[0]
    LF = w_fg.shape[0]
    MS = w_mod.shape[2]
    QS = w_qkv.shape[2]
    OS = w_o.shape[1]
    FS = w_ffn_gate.shape[2]
    F = N_CHIPS * FS
    NH = D // HEAD_DIM
    B = ATT_BLOCK
    NQ = S // B
    ax, ay, ac = _mesh_pos()
    k_me = 2 * ax + ay
    b_me = 4 * ax + 2 * ay + ac

    conv_rows = -(-(L * 3 * FS) // D)
    conv_rows = -(-conv_rows // SUBLANES) * SUBLANES
    conv_flat = jnp.pad(w_conv.reshape(-1), (0, conv_rows * D - L * 3 * FS)).reshape(conv_rows, D)
    first = jnp.concatenate([_pad_rows(c, SUBLANES), conv_flat], axis=0)
    first_all = _all_gather_small(first, "ag_cond").reshape(N_DEV, SUBLANES + conv_rows, D)
    c_all = first_all[:, 0, :]
    conv_all = first_all[0::2, SUBLANES:, :].reshape(N_CHIPS, -1)[:, :L * 3 * FS]
    w_conv_full = conv_all.reshape(N_CHIPS, L, 3, FS).transpose(1, 2, 0, 3).reshape(L, 3, F)
    c_act = _silu(c_all, "silu_c")

    mod_part = jnp.concatenate(
        [_mm(c_act, w_mod[l], "nn", F32, "mm_mod", tm=N_DEV, tn=MS, tk=D) for l in range(L)], axis=1)
    mod_all = _all_gather_small(mod_part, "ag_mod").reshape(N_CHIPS, 2, N_DEV, L, MS)[:, 0]
    mod_mine = lax.dynamic_index_in_dim(mod_all, b_me, axis=1, keepdims=False)
    mod = mod_mine.transpose(1, 0, 2).reshape(L, N_MOD * D) + b_mod

    gq, go, gg, gu_, gd, gf = _all_gather_weights(
        [w_qkv.astype(BF16), w_o.astype(BF16), w_ffn_gate.astype(BF16), w_ffn_up.astype(BF16),
         w_ffn_down.astype(BF16), w_fg.astype(BF16)], "ag_weights")
    W_qkv = gq.transpose(1, 2, 0, 3).reshape(L, D, 3 * D)
    W_o = go.transpose(1, 0, 2, 3).reshape(L, D, D)
    W_g = gg.transpose(1, 2, 0, 3).reshape(L, D, F // LANES, 1, LANES)
    W_u = gu_.transpose(1, 2, 0, 3).reshape(L, D, F // LANES, 1, LANES)
    W_gu = jnp.concatenate([W_g, W_u], axis=3).reshape(L, D, 2 * F)
    W_d = gd.transpose(1, 0, 2, 3).reshape(L, F, D)
    W_fg = jnp.pad(gf.transpose(1, 0, 2, 3).reshape(LF, D, NH), ((0, 0), (0, 0), (0, LANES - NH)))

    def vec(a):
        return a.reshape(1, -1)

    saved = []
    xcur = xs
    for l in range(L):
        sh_a, sc_a, gt_a, sh_f, sc_f, gt_f = [vec(mod[l, j * D:(j + 1) * D]) for j in range(N_MOD)]
        is_fox = l % 2 == 1
        jf = l // 2
        h1 = _norm_mod(xcur, vec(g_mix_pre[l]), sc_a, sh_a, "norm_mod")
        qkv = _mm(h1, W_qkv[l], "nn", BF16, "mm_qkv")
        if is_fox:
            flog = _mm(h1, W_fg[jf], "nn", F32, "mm_fg")[:, :NH]
            f3 = flog.reshape(NQ, B, NH).transpose(0, 2, 1)
            bias = b_fg[jf].reshape(NH, 1)
            cum3 = _forget_cumsum(f3, bias, "forget_cumsum")
            cq = jnp.repeat(cum3.transpose(0, 2, 1).reshape(S, NH), B, axis=1)
            ck = jnp.broadcast_to(cum3.transpose(1, 0, 2)[:, :, None, :], (NH, NQ, SUBLANES, B))
            o, stat = _fox_fwd(qkv, cq, ck, "fox_fwd")
            extra = (f3, bias, cq, ck)
        else:
            o, stat = _sb_fwd(qkv, "sb_fwd")
            extra = None
        p = _mm(o, W_o[l], "nn", F32, "mm_o")
        x1 = _post_res(xcur, p, vec(g_mix_post[l]), gt_a, "post_res")
        h2 = _norm_mod(x1, vec(g_ffn_pre[l]), sc_f, sh_f, "norm_mod")
        gu = _mm(h2, W_gu[l], "nn", F32, "mm_gu")
        wc = w_conv_full[l]
        bc = vec(b_conv[l])
        a = _conv_gate(gu, wc, bc, "conv_gate")
        yv = _mm(a, W_d[l], "nn", F32, "mm_down")
        x2 = _post_res(x1, yv, vec(g_ffn_post[l]), gt_f, "post_res")
        saved.append(dict(x0=xcur, h1=h1, qkv=qkv, o=o, stat=stat, extra=extra, p=p, x1=x1, h2=h2, gu=gu, a=a,
                          y=yv, mods=(sh_a, sc_a, gt_a, sh_f, sc_f, gt_f), wc=wc, bc=bc))
        xcur = x2

    g, sq = _loss_head(xcur, target, "loss_head")
    loss_part = 0.5 * jnp.sum(sq) / D
    loss = lax.psum(loss_part, ("x", "y", "c"))

    dW_qkv, dW_o, dW_gu, dW_d = [None] * L, [None] * L, [None] * L, [None] * L
    dW_fg, db_fg = [None] * LF, [None] * LF
    dmod, dg_mix_pre, dg_mix_post, dg_ffn_pre, dg_ffn_post = [[None] * L for _ in range(5)]
    dw_conv, db_conv = [None] * L, [None] * L
    for l in reversed(range(L)):
        sv = saved[l]
        sh_a, sc_a, gt_a, sh_f, sc_f, gt_f = sv["mods"]
        is_fox = l % 2 == 1
        jf = l // 2
        dy, dgp_f, dgt_f = _post_res_bwd(sv["y"], vec(g_ffn_post[l]), gt_f, g, "post_res_bwd")
        da = _mm(dy, W_d[l], "nt", F32, "mm_da")
        dW_d[l] = _mm(sv["a"], dy, "tn", F32, "mm_dwd")
        dgu, dwc, dbc = _conv_gate_bwd(sv["gu"], sv["wc"], sv["bc"], da, "conv_gate_bwd")
        dh2 = _mm(dgu, W_gu[l], "nt", F32, "mm_dh2")
        dW_gu[l] = _mm(sv["h2"], dgu, "tn", F32, "mm_dwgu")
        g, dg_f, dsc_f, dsh_f = _norm_mod_bwd(sv["x1"], vec(g_ffn_pre[l]), sc_f, sh_f, [dh2], g, "norm_mod_bwd")
        dp, dgp_a, dgt_a = _post_res_bwd(sv["p"], vec(g_mix_post[l]), gt_a, g, "post_res_bwd")
        do = _mm(dp, W_o[l], "nt", F32, "mm_do")
        dW_o[l] = _mm(sv["o"], dp, "tn", F32, "mm_dwo")
        if is_fox:
            f3, bias, cq, ck = sv["extra"]
            dq, dk, dv, dcq, dck = _fox_bwd(sv["qkv"], do, sv["o"], sv["stat"], cq, ck, "fox_bwd")
            dcum_q = dcq.reshape(NQ, B, NH, B)[:, :, :, 0].transpose(0, 2, 1)
            dcum3 = dcum_q + dck[:, :, 0, :].transpose(1, 0, 2)
            df3, df_tot = _forget_cumsum_bwd(dcum3, f3, bias, "forget_cumsum_bwd")
            df = df3.transpose(0, 2, 1).reshape(S, NH)
            dfp = jnp.pad(df, ((0, 0), (0, LANES - NH)))
            dW_fg[jf] = _mm(sv["h1"], dfp, "tn", F32, "mm_dwfg")[:, :NH]
            db_fg[jf] = df_tot[:, 0]
            dh_extra = [_mm(dfp, W_fg[jf], "nt", F32, "mm_dh1f")]
        else:
            dq, dk, dv = _sb_bwd(sv["qkv"], do, sv["stat"], "sb_bwd")
            dh_extra = []
        dqkv = jnp.concatenate([dq, dk, dv], axis=1)
        dh1 = _mm(dqkv, W_qkv[l], "nt", F32, "mm_dh1")
        dW_qkv[l] = _mm(sv["h1"], dqkv, "tn", F32, "mm_dwqkv")
        g, dg_a, dsc_a, dsh_a = _norm_mod_bwd(sv["x0"], vec(g_mix_pre[l]), sc_a, sh_a, [dh1] + dh_extra, g,
                                              "norm_mod_bwd")
        dmod[l] = jnp.concatenate([dsh_a, dsc_a, dgt_a, dsh_f, dsc_f, dgt_f], axis=1)[0]
        dg_mix_pre[l], dg_mix_post[l], dg_ffn_pre[l], dg_ffn_post[l] = dg_a[0], dgp_a[0], dg_f[0], dgp_f[0]
        dw_conv[l], db_conv[l] = dwc, dbc[0]
    grad_x = g[None]

    pieces = [jnp.stack(dmod), jnp.stack(dg_mix_pre), jnp.stack(dg_mix_post), jnp.stack(dg_ffn_pre),
              jnp.stack(dg_ffn_post), jnp.stack(db_fg), jnp.stack(dW_fg), jnp.stack(dw_conv), jnp.stack(db_conv)]
    sizes = [pc.size for pc in pieces]
    total = sum(sizes)
    pack_rows = -(-total // (LANES * SUBLANES)) * SUBLANES
    pack = jnp.pad(jnp.concatenate([pc.reshape(-1) for pc in pieces]), (0, pack_rows * LANES - total))
    pack_all = _all_gather_small(pack.reshape(pack_rows, LANES), "ag_small_grads").reshape(N_DEV, pack_rows, LANES)
    small = _sum_leading(pack_all, "sum_small_grads").reshape(-1)
    offs = [0]
    for sz in sizes:
        offs.append(offs[-1] + sz)
    parts = [small[offs[i]:offs[i + 1]].reshape(pieces[i].shape) for i in range(len(pieces))]
    g_b_mod, g_g_mix_pre, g_g_mix_post, g_g_ffn_pre, g_g_ffn_post, g_b_fg, g_w_fg_full, g_w_conv_full, g_b_conv = parts
    g_w_fg = lax.dynamic_slice_in_dim(g_w_fg_full, k_me * OS, OS, axis=1)
    g_w_conv = lax.dynamic_slice_in_dim(g_w_conv_full, k_me * FS, FS, axis=2)
    dmod_all = pack_all.reshape(N_DEV, -1)[:, :L * N_MOD * D].reshape(N_DEV, L, N_CHIPS, MS)
    dmod_cols = lax.dynamic_index_in_dim(dmod_all, k_me, axis=2, keepdims=False).reshape(N_DEV, L * MS)
    g_w_mod = _mm(_pad_rows(c_act, LANES), _pad_rows(dmod_cols, LANES), "tn", F32, "mm_dwmod", tm=D, tn=MS, tk=LANES)
    g_w_mod = g_w_mod.reshape(D, L, MS).transpose(1, 0, 2)

    def shard_cols(dws, width):
        st = jnp.stack(dws)
        return st.reshape(L, st.shape[1], N_CHIPS, width).transpose(2, 0, 1, 3).astype(BF16)

    def shard_rows(dws, height):
        st = jnp.stack(dws)
        return st.reshape(L, N_CHIPS, height, st.shape[2]).transpose(1, 0, 2, 3).astype(BF16)

    dgu_st = jnp.stack(dW_gu).reshape(L, D, F // LANES, 2, LANES)
    dWg = [dgu_st[l, :, :, 0, :].reshape(D, F) for l in range(L)]
    dWu = [dgu_st[l, :, :, 1, :].reshape(D, F) for l in range(L)]
    gs = [shard_cols(dW_qkv, QS), shard_rows(dW_o, OS), shard_cols(dWg, FS), shard_cols(dWu, FS),
          shard_rows(dW_d, FS)]
    recv = _scatter_grad_pieces(gs, "rs_scatter")
    halves = []
    for i, r in enumerate(recv):
        cols = r.shape[-1]
        halves.append(_sum_leading(r.reshape(N_DEV, -1, cols), f"sum_grad_pieces_{i}").reshape(r.shape[1:]))
    full = _swap_halves(halves, "rs_swap")
    g_w_qkv, g_w_o, g_w_gate, g_w_up, g_w_down = [f.reshape((L,) + f.shape[2:]) for f in full]

    grads = [g_w_mod, g_b_mod, g_g_mix_pre, g_g_mix_post, g_w_qkv, g_w_o, g_w_fg, g_b_fg, g_g_ffn_pre,
             g_g_ffn_post, g_w_gate, g_w_up, g_w_conv, g_b_conv, g_w_down]
    weights = [w_mod, b_mod, g_mix_pre, g_mix_post, w_qkv, w_o, w_fg, b_fg, g_ffn_pre, g_ffn_post, w_ffn_gate,
               w_ffn_up, w_conv, b_conv, w_ffn_down]
    ms = [m_w_mod, m_b_mod, m_g_mix_pre, m_g_mix_post, m_w_qkv, m_w_o, m_w_fg, m_b_fg, m_g_ffn_pre, m_g_ffn_post,
          m_w_ffn_gate, m_w_ffn_up, m_w_conv, m_b_conv, m_w_ffn_down]
    vs = [v_w_mod, v_b_mod, v_g_mix_pre, v_g_mix_post, v_w_qkv, v_w_o, v_w_fg, v_b_fg, v_g_ffn_pre, v_g_ffn_post,
          v_w_ffn_gate, v_w_ffn_up, v_w_conv, v_b_conv, v_w_ffn_down]
    deltas, new_ms, new_vs = [], [], []
    for wv, gv, mv, vv in zip(weights, grads, ms, vs):
        d, nm, nv = _adamw(wv, gv, mv, vv, "adamw")
        deltas.append(d)
        new_ms.append(nm)
        new_vs.append(nv)
    return (loss, grad_x, *grads, *deltas, *new_ms, *new_vs)
```

```python
import functools

import jax
import jax.numpy as jnp
from jax import lax
from jax.experimental import pallas as pl
from jax.experimental.pallas import tpu as pltpu

F32 = jnp.float32
BF16 = jnp.bfloat16
MESH = pl.DeviceIdType.MESH

HEAD_DIM = 64
ATT_BLOCK = 128
ATT_Q_TILE = 1024
LANES = 128
SUBLANES = 8
RMS_EPS = 1e-6
N_MOD = 6
N_CHIPS = 4
N_DEV = 8
ADAM_LR = 0.001
ADAM_B1 = 0.9
ADAM_B2 = 0.999
ADAM_EPS = 1e-08
ADAM_WD = 0.01
ADAM_STEP = 10
VMEM_LIMIT_BYTES = 56 * 1024 * 1024
NEG_BIG = -1e30

ANY = pl.BlockSpec(memory_space=pl.ANY)
VMEM_WHOLE = pl.BlockSpec(memory_space=pltpu.VMEM)


def _params(**kw):
    return pltpu.CompilerParams(vmem_limit_bytes=VMEM_LIMIT_BYTES, **kw)


def _dot(a, b):
    return jnp.dot(a, b, preferred_element_type=F32)


def _dot_nt(a, b):
    return lax.dot_general(a, b, (((1,), (1,)), ((), ())), preferred_element_type=F32)


def _dot_tn(a, b):
    return lax.dot_general(a, b, (((0,), (0,)), ((), ())), preferred_element_type=F32)


def _split_dot(x, t, parts):
    acc = None
    rem = x
    for _ in range(parts):
        piece = rem.astype(BF16)
        rem = rem - piece.astype(F32)
        d = _dot(piece, t)
        acc = d if acc is None else acc + d
    return acc


def _pick(n, prefs):
    for p in prefs:
        if n % p == 0:
            return p
    return n


def _mm(a, b, dims, out_dtype, name, tm=None, tn=None, tk=None):
    if dims == "tn":
        K, M = a.shape
    else:
        M, K = a.shape
    N = b.shape[0] if dims == "nt" else b.shape[1]
    tm = tm or _pick(M, (512, 256, 128))
    tn = tn or _pick(N, (1536, 1408, 1024, 768, 512, 256, 128))
    tk = tk or _pick(K, (1024, 1408, 512, 256, 128))
    nk = K // tk
    grid = (M // tm, N // tn, nk)
    if dims == "tn":
        a_spec = pl.BlockSpec((tk, tm), lambda i, j, k: (k, i))
    else:
        a_spec = pl.BlockSpec((tm, tk), lambda i, j, k: (i, k))
    if dims == "nt":
        b_spec = pl.BlockSpec((tn, tk), lambda i, j, k: (j, k))
    else:
        b_spec = pl.BlockSpec((tk, tn), lambda i, j, k: (k, j))
    o_spec = pl.BlockSpec((tm, tn), lambda i, j, k: (i, j))

    def body(a_ref, b_ref, o_ref, *scratch):
        x = a_ref[...].astype(BF16)
        y = b_ref[...].astype(BF16)
        if dims == "nn":
            r = _dot(x, y)
        elif dims == "nt":
            r = _dot_nt(x, y)
        else:
            r = _dot_tn(x, y)
        if nk == 1:
            o_ref[...] = r.astype(out_dtype)
        else:
            acc = scratch[0]
            k = pl.program_id(2)

            @pl.when(k == 0)
            def _():
                acc[...] = r

            @pl.when(k > 0)
            def _():
                acc[...] += r

            @pl.when(k == nk - 1)
            def _():
                o_ref[...] = acc[...].astype(out_dtype)

    return pl.pallas_call(
        body,
        name=name,
        grid=grid,
        in_specs=[a_spec, b_spec],
        out_specs=o_spec,
        out_shape=jax.ShapeDtypeStruct((M, N), out_dtype),
        scratch_shapes=[pltpu.VMEM((tm, tn), F32)] if nk > 1 else [],
        compiler_params=_params(dimension_semantics=("parallel", "parallel", "arbitrary")),
    )(a, b)


def _row_tile(S):
    return _pick(S, (256, 128, 64, 32, 16, 8))


def _norm_mod(x, g, sc, sh, name):
    S, D = x.shape
    T = _row_tile(S)
    row = pl.BlockSpec((T, D), lambda i: (i, 0))
    vec = pl.BlockSpec((1, D), lambda i: (0, 0))

    def body(x_ref, g_ref, sc_ref, sh_ref, h_ref):
        xv = x_ref[...]
        r = lax.rsqrt(jnp.mean(xv * xv, axis=-1, keepdims=True) + RMS_EPS)
        n = (xv * r) * g_ref[...]
        h_ref[...] = (n * (1.0 + sc_ref[...]) + sh_ref[...]).astype(BF16)

    return pl.pallas_call(
        body, name=name, grid=(S // T,), in_specs=[row, vec, vec, vec], out_specs=row,
        out_shape=jax.ShapeDtypeStruct((S, D), BF16), compiler_params=_params(),
    )(x, g, sc, sh)


def _norm_mod_bwd(x, g, sc, sh, dhs, gres, name):
    S, D = x.shape
    T = _row_tile(S)
    n_dh = len(dhs)
    row = pl.BlockSpec((T, D), lambda i: (i, 0))
    vec = pl.BlockSpec((1, D), lambda i: (0, 0))

    def body(x_ref, g_ref, sc_ref, sh_ref, *refs):
        dh_refs = refs[:n_dh]
        gres_ref, dx_ref, dg_ref, dsc_ref, dsh_ref = refs[n_dh:]
        xv = x_ref[...]
        r = lax.rsqrt(jnp.mean(xv * xv, axis=-1, keepdims=True) + RMS_EPS)
        xn = xv * r
        n = xn * g_ref[...]
        dh = dh_refs[0][...]
        for extra in dh_refs[1:]:
            dh = dh + extra[...]
        dn = dh * (1.0 + sc_ref[...])
        dxn = dn * g_ref[...]
        dx = r * (dxn - xn * jnp.mean(dxn * xn, axis=-1, keepdims=True))
        dx_ref[...] = gres_ref[...] + dx

        @pl.when(pl.program_id(0) == 0)
        def _():
            dg_ref[...] = jnp.zeros_like(dg_ref)
            dsc_ref[...] = jnp.zeros_like(dsc_ref)
            dsh_ref[...] = jnp.zeros_like(dsh_ref)

        dg_ref[...] += jnp.sum(dn * xn, axis=0, keepdims=True)
        dsc_ref[...] += jnp.sum(dh * n, axis=0, keepdims=True)
        dsh_ref[...] += jnp.sum(dh, axis=0, keepdims=True)

    vshape = jax.ShapeDtypeStruct((1, D), F32)
    return pl.pallas_call(
        body, name=name, grid=(S // T,), in_specs=[row, vec, vec, vec] + [row] * (n_dh + 1),
        out_specs=[row, vec, vec, vec],
        out_shape=[jax.ShapeDtypeStruct((S, D), F32), vshape, vshape, vshape],
        compiler_params=_params(dimension_semantics=("arbitrary",)),
    )(x, g, sc, sh, *dhs, gres)


def _post_res(x, p, gp, gt, name):
    S, D = x.shape
    T = _row_tile(S)
    row = pl.BlockSpec((T, D), lambda i: (i, 0))
    vec = pl.BlockSpec((1, D), lambda i: (0, 0))

    def body(x_ref, p_ref, gp_ref, gt_ref, o_ref):
        pv = p_ref[...]
        r = lax.rsqrt(jnp.mean(pv * pv, axis=-1, keepdims=True) + RMS_EPS)
        o_ref[...] = x_ref[...] + gt_ref[...] * ((pv * r) * gp_ref[...])

    return pl.pallas_call(
        body, name=name, grid=(S // T,), in_specs=[row, row, vec, vec], out_specs=row,
        out_shape=jax.ShapeDtypeStruct((S, D), F32), compiler_params=_params(),
    )(x, p, gp, gt)


def _post_res_bwd(p, gp, gt, g, name):
    S, D = p.shape
    T = _row_tile(S)
    row = pl.BlockSpec((T, D), lambda i: (i, 0))
    vec = pl.BlockSpec((1, D), lambda i: (0, 0))

    def body(p_ref, gp_ref, gt_ref, g_ref, dp_ref, dgp_ref, dgt_ref):
        pv = p_ref[...]
        gv = g_ref[...]
        r = lax.rsqrt(jnp.mean(pv * pv, axis=-1, keepdims=True) + RMS_EPS)
        pn = pv * r
        n2 = pn * gp_ref[...]
        dn2 = gv * gt_ref[...]
        dpn = dn2 * gp_ref[...]
        dp = r * (dpn - pn * jnp.mean(dpn * pn, axis=-1, keepdims=True))
        dp_ref[...] = dp.astype(BF16)

        @pl.when(pl.program_id(0) == 0)
        def _():
            dgp_ref[...] = jnp.zeros_like(dgp_ref)
            dgt_ref[...] = jnp.zeros_like(dgt_ref)

        dgp_ref[...] += jnp.sum(dn2 * pn, axis=0, keepdims=True)
        dgt_ref[...] += jnp.sum(gv * n2, axis=0, keepdims=True)

    vshape = jax.ShapeDtypeStruct((1, D), F32)
    return pl.pallas_call(
        body, name=name, grid=(S // T,), in_specs=[row, vec, vec, row], out_specs=[row, vec, vec],
        out_shape=[jax.ShapeDtypeStruct((S, D), BF16), vshape, vshape],
        compiler_params=_params(dimension_semantics=("arbitrary",)),
    )(p, gp, gt, g)


def _loss_head(y, target, name):
    S, D = y.shape
    T = _row_tile(S)
    row = pl.BlockSpec((T, D), lambda i: (i, 0))
    vec = pl.BlockSpec((1, D), lambda i: (0, 0))

    def body(y_ref, t_ref, dy_ref, sq_ref):
        e = y_ref[...] - t_ref[...]
        dy_ref[...] = e * (1.0 / D)

        @pl.when(pl.program_id(0) == 0)
        def _():
            sq_ref[...] = jnp.zeros_like(sq_ref)

        sq_ref[...] += jnp.sum(e * e, axis=0, keepdims=True)

    return pl.pallas_call(
        body, name=name, grid=(S // T,), in_specs=[row, row], out_specs=[row, vec],
        out_shape=[jax.ShapeDtypeStruct((S, D), F32), jax.ShapeDtypeStruct((1, D), F32)],
        compiler_params=_params(dimension_semantics=("arbitrary",)),
    )(y, target)


def _shift_down(v, k, rows):
    return jnp.where(rows >= k, pltpu.roll(v, k, 0), 0.0)


def _shift_up(v, k, rows, S):
    return jnp.where(rows < S - k, pltpu.roll(v, S - k, 0), 0.0)


def _conv_gate(gu, wc, bc, name):
    S, F2 = gu.shape
    F = F2 // 2
    C = LANES

    def body(gu_ref, w_ref, b_ref, a_ref):
        rows = lax.broadcasted_iota(jnp.int32, (S, C), 0)
        gate = gu_ref[:, :C]
        up = gu_ref[:, C:]
        w = w_ref[...]
        gc = w[2:3] * gate + w[1:2] * _shift_down(gate, 1, rows) + w[0:1] * _shift_down(gate, 2, rows) + b_ref[...]
        a_ref[...] = (gc * (1.0 / (1.0 + jnp.exp(-gc))) * up).astype(BF16)

    return pl.pallas_call(
        body, name=name, grid=(F // C,),
        in_specs=[pl.BlockSpec((S, 2 * C), lambda j: (0, j)), pl.BlockSpec((3, C), lambda j: (0, j)),
                  pl.BlockSpec((1, C), lambda j: (0, j))],
        out_specs=pl.BlockSpec((S, C), lambda j: (0, j)),
        out_shape=jax.ShapeDtypeStruct((S, F), BF16), compiler_params=_params(),
    )(gu, wc, bc)


def _conv_gate_bwd(gu, wc, bc, da, name):
    S, F2 = gu.shape
    F = F2 // 2
    C = LANES

    def body(gu_ref, w_ref, b_ref, da_ref, dgu_ref, dw_ref, db_ref):
        rows = lax.broadcasted_iota(jnp.int32, (S, C), 0)
        gate = gu_ref[:, :C]
        up = gu_ref[:, C:]
        dav = da_ref[...]
        w = w_ref[...]
        g1 = _shift_down(gate, 1, rows)
        g2 = _shift_down(gate, 2, rows)
        gc = w[2:3] * gate + w[1:2] * g1 + w[0:1] * g2 + b_ref[...]
        sg = 1.0 / (1.0 + jnp.exp(-gc))
        dgu_ref[:, C:] = (dav * (gc * sg)).astype(BF16)
        dgc = dav * up * (sg * (1.0 + gc * (1.0 - sg)))
        db_ref[...] = jnp.sum(dgc, axis=0, keepdims=True)
        dw_ref[0:1, :] = jnp.sum(dgc * g2, axis=0, keepdims=True)
        dw_ref[1:2, :] = jnp.sum(dgc * g1, axis=0, keepdims=True)
        dw_ref[2:3, :] = jnp.sum(dgc * gate, axis=0, keepdims=True)
        dgate = w[2:3] * dgc + w[1:2] * _shift_up(dgc, 1, rows, S) + w[0:1] * _shift_up(dgc, 2, rows, S)
        dgu_ref[:, :C] = dgate.astype(BF16)

    return pl.pallas_call(
        body, name=name, grid=(F // C,),
        in_specs=[pl.BlockSpec((S, 2 * C), lambda j: (0, j)), pl.BlockSpec((3, C), lambda j: (0, j)),
                  pl.BlockSpec((1, C), lambda j: (0, j)), pl.BlockSpec((S, C), lambda j: (0, j))],
        out_specs=[pl.BlockSpec((S, 2 * C), lambda j: (0, j)), pl.BlockSpec((3, C), lambda j: (0, j)),
                   pl.BlockSpec((1, C), lambda j: (0, j))],
        out_shape=[jax.ShapeDtypeStruct((S, F2), BF16), jax.ShapeDtypeStruct((3, F), F32),
                   jax.ShapeDtypeStruct((1, F), F32)],
        compiler_params=_params(),
    )(gu, wc, bc, da)


ATT_SCALE = HEAD_DIM ** -0.5


def _att_specs(S, D):
    nb = D // LANES
    q_spec = pl.BlockSpec((ATT_Q_TILE, LANES), lambda p, i: (i, p))
    k_spec = pl.BlockSpec((S, LANES), lambda p, i: (0, nb + p))
    v_spec = pl.BlockSpec((S, LANES), lambda p, i: (0, 2 * nb + p))
    stat_spec = pl.BlockSpec((ATT_Q_TILE, 2 * ATT_BLOCK), lambda p, i: (i, p))
    seq_spec = pl.BlockSpec((S, LANES), lambda p, i: (0, p))
    ck_spec = pl.BlockSpec((2, S // ATT_BLOCK, SUBLANES, ATT_BLOCK), lambda p, i: (p, 0, 0, 0))
    return q_spec, k_spec, v_spec, stat_spec, seq_spec, ck_spec


def _head_lanes(hh):
    return slice(hh * HEAD_DIM, (hh + 1) * HEAD_DIM)


def _scaled_q(q_ref, hh):
    return (q_ref[:, _head_lanes(hh)].astype(F32) * ATT_SCALE).astype(BF16)


def _tri(cmp):
    B = ATT_BLOCK
    row = lax.broadcasted_iota(jnp.int32, (B, B), 0)
    col = lax.broadcasted_iota(jnp.int32, (B, B), 1)
    half = jnp.concatenate([jnp.where(cmp(row, col), 1.0, 0.0).astype(BF16), jnp.ones((B, B), BF16)], axis=1)
    return jnp.concatenate([half, half], axis=0)


def _hi_lo_dot(x, t):
    hi = x.astype(BF16)
    lo = (x - hi.astype(F32)).astype(BF16)
    return _dot(jnp.concatenate([hi, lo], axis=1), t)


def _key_minus_query(j, qi):
    row = lax.broadcasted_iota(jnp.int32, (ATT_Q_TILE, ATT_BLOCK), 0)
    col = lax.broadcasted_iota(jnp.int32, (ATT_Q_TILE, ATT_BLOCK), 1)
    return col - row + (j * ATT_BLOCK - qi * ATT_Q_TILE)


def _log_sigmoids(z):
    sp = jnp.log(1.0 + jnp.exp(-jnp.abs(z)))
    lb = jnp.minimum(z, 0.0) - sp
    return lb, lb - z


def _log_sigmoids_fast(z):
    zc = jnp.maximum(z, -80.0)
    lb = -jnp.log(1.0 + jnp.exp(-zc))
    return lb, lb - zc


def _sb_fwd(qkv, name):
    S, D3 = qkv.shape
    D = D3 // 3
    B, TQ = ATT_BLOCK, ATT_Q_TILE
    R = TQ // B
    NP = D // LANES
    q_spec, k_spec, v_spec, stat_spec, _, _ = _att_specs(S, D)

    def body(q_ref, k_ref, v_ref, o_ref, lt_ref):
        qi = pl.program_id(1)
        t_suffix = _tri(lambda r, c: r > c)
        qs = [_scaled_q(q_ref, hh) for hh in range(2)]

        def tile(j, carry, masked):
            r0 = pl.multiple_of(j * B, B)
            if masked:
                strict = _key_minus_query(j, qi) < 0
            out = []
            for hh in range(2):
                c, acc = carry[hh]
                k = k_ref[pl.ds(r0, B), _head_lanes(hh)]
                v = v_ref[pl.ds(r0, B), _head_lanes(hh)]
                lb, l1 = _log_sigmoids_fast(_dot_nt(qs[hh], k))
                if masked:
                    l1 = jnp.where(strict, l1, 0.0)
                sums = _hi_lo_dot(l1, t_suffix)
                a = jnp.exp(lb + c + sums[:, :B])
                if masked:
                    a = jnp.where(strict, a, 0.0)
                out.append((c + sums[:, B:], acc + _dot(a.astype(BF16), v)))
            return tuple(out)

        zero = (jnp.zeros((TQ, B), F32), jnp.zeros((TQ, HEAD_DIM), F32))
        last = qi * R + R - 1
        carry = lax.fori_loop(0, R, lambda it, cr: tile(last - it, cr, True), (zero, zero))
        carry = lax.fori_loop(0, qi * R, lambda it, cr: tile(qi * R - 1 - it, cr, False), carry)
        for hh in range(2):
            c, acc = carry[hh]
            o_ref[:, _head_lanes(hh)] = acc
            lt_ref[:, hh * B:(hh + 1) * B] = c

    return pl.pallas_call(
        body, name=name, grid=(NP, S // TQ), in_specs=[q_spec, k_spec, v_spec],
        out_specs=[q_spec, stat_spec],
        out_shape=[jax.ShapeDtypeStruct((S, D), F32), jax.ShapeDtypeStruct((S, 2 * NP * B), F32)],
        compiler_params=_params(),
    )(qkv, qkv, qkv)


def _sb_bwd(qkv, do, lt, name):
    S, D3 = qkv.shape
    D = D3 // 3
    B, TQ = ATT_BLOCK, ATT_Q_TILE
    R = TQ // B
    NP = D // LANES
    NQ = S // TQ
    q_spec, k_spec, v_spec, stat_spec, seq_spec, _ = _att_specs(S, D)

    def body(q_ref, k_ref, v_ref, do_ref, lt_ref, dq_ref, dk_ref, dv_ref, dk_acc, dv_acc):
        qi = pl.program_id(1)
        t_prefix = _tri(lambda r, c: r <= c)
        t_before = _tri(lambda r, c: r < c)

        @pl.when(qi == 0)
        def _():
            dk_acc[...] = jnp.zeros_like(dk_acc)
            dv_acc[...] = jnp.zeros_like(dv_acc)

        qs = [_scaled_q(q_ref, hh) for hh in range(2)]
        dob = [do_ref[:, _head_lanes(hh)].astype(BF16) for hh in range(2)]
        ltot = [lt_ref[:, hh * B:(hh + 1) * B] for hh in range(2)]

        def tile(j, carry, masked):
            r0 = pl.multiple_of(j * B, B)
            if masked:
                strict = _key_minus_query(j, qi) < 0
            out = []
            for hh in range(2):
                pre, cu, dq = carry[hh]
                lanes = _head_lanes(hh)
                k = k_ref[pl.ds(r0, B), lanes]
                v = v_ref[pl.ds(r0, B), lanes]
                lb, l1 = _log_sigmoids_fast(_dot_nt(qs[hh], k))
                if masked:
                    l1 = jnp.where(strict, l1, 0.0)
                sums = _hi_lo_dot(l1, t_prefix)
                a = jnp.exp(lb + (ltot[hh] - pre - sums[:, :B]))
                if masked:
                    a = jnp.where(strict, a, 0.0)
                u = a * _dot_nt(dob[hh], v)
                usums = _hi_lo_dot(u, t_before)
                dz = u - (u + cu + usums[:, :B]) * jnp.exp(lb)
                if masked:
                    dz = jnp.where(strict, dz, 0.0)
                dzb = dz.astype(BF16)
                dk_acc[pl.ds(r0, B), lanes] += _dot_tn(dzb, qs[hh])
                dv_acc[pl.ds(r0, B), lanes] += _dot_tn(a.astype(BF16), dob[hh])
                out.append((pre + sums[:, B:], cu + usums[:, B:], dq + _dot(dzb, k)))
            return tuple(out)

        zero = (jnp.zeros((TQ, B), F32), jnp.zeros((TQ, B), F32), jnp.zeros((TQ, HEAD_DIM), F32))
        carry = lax.fori_loop(0, qi * R, lambda j, cr: tile(j, cr, False), (zero, zero))
        carry = lax.fori_loop(0, R, lambda it, cr: tile(qi * R + it, cr, True), carry)
        for hh in range(2):
            dq_ref[:, _head_lanes(hh)] = (carry[hh][2] * ATT_SCALE).astype(BF16)

        @pl.when(qi == NQ - 1)
        def _():
            dk_ref[...] = dk_acc[...].astype(BF16)
            dv_ref[...] = dv_acc[...].astype(BF16)

    out = jax.ShapeDtypeStruct((S, D), BF16)
    return pl.pallas_call(
        body, name=name, grid=(NP, NQ), in_specs=[q_spec, k_spec, v_spec, q_spec, stat_spec],
        out_specs=[q_spec, seq_spec, seq_spec], out_shape=[out, out, out],
        scratch_shapes=[pltpu.VMEM((S, LANES), F32), pltpu.VMEM((S, LANES), F32)],
        compiler_params=_params(dimension_semantics=("arbitrary", "arbitrary")),
    )(qkv, qkv, qkv, do, lt)


def _fox_fwd(qkv, cq, ck, name):
    S, D3 = qkv.shape
    D = D3 // 3
    B, TQ = ATT_BLOCK, ATT_Q_TILE
    R = TQ // B
    NP = D // LANES
    q_spec, k_spec, v_spec, stat_spec, _, ck_spec = _att_specs(S, D)

    def body(q_ref, k_ref, v_ref, cq_ref, ck_ref, o_ref, lse_ref):
        qi = pl.program_id(1)
        qs = [_scaled_q(q_ref, hh) for hh in range(2)]
        cqv = [cq_ref[:, hh * B:(hh + 1) * B] for hh in range(2)]

        def tile(j, carry, masked):
            r0 = pl.multiple_of(j * B, B)
            if masked:
                causal = _key_minus_query(j, qi) <= 0
            out = []
            for hh in range(2):
                m, lsum, acc = carry[hh]
                k = k_ref[pl.ds(r0, B), _head_lanes(hh)]
                v = v_ref[pl.ds(r0, B), _head_lanes(hh)]
                s = _dot_nt(qs[hh], k) + (cqv[hh] - ck_ref[hh, j][0:1, :])
                if masked:
                    s = jnp.where(causal, s, NEG_BIG)
                m_new = jnp.maximum(m, jnp.max(s, axis=1, keepdims=True))
                p = jnp.exp(s - m_new)
                alpha = jnp.exp(m - m_new)
                out.append((m_new, alpha * lsum + jnp.sum(p, axis=1, keepdims=True),
                            alpha * acc + _dot(p.astype(BF16), v)))
            return tuple(out)

        zero = (jnp.full((TQ, 1), NEG_BIG, F32), jnp.zeros((TQ, 1), F32), jnp.zeros((TQ, HEAD_DIM), F32))
        carry = lax.fori_loop(0, qi * R, lambda j, cr: tile(j, cr, False), (zero, zero))
        carry = lax.fori_loop(0, R, lambda it, cr: tile(qi * R + it, cr, True), carry)
        for hh in range(2):
            m, lsum, acc = carry[hh]
            o_ref[:, _head_lanes(hh)] = acc * (1.0 / lsum)
            lse_ref[:, hh * B:(hh + 1) * B] = jnp.broadcast_to(m + jnp.log(lsum), (TQ, B))

    return pl.pallas_call(
        body, name=name, grid=(NP, S // TQ), in_specs=[q_spec, k_spec, v_spec, stat_spec, ck_spec],
        out_specs=[q_spec, stat_spec],
        out_shape=[jax.ShapeDtypeStruct((S, D), F32), jax.ShapeDtypeStruct((S, 2 * NP * B), F32)],
        compiler_params=_params(),
    )(qkv, qkv, qkv, cq, ck)


def _fox_bwd(qkv, do, o, lse, cq, ck, name):
    S, D3 = qkv.shape
    D = D3 // 3
    B, TQ = ATT_BLOCK, ATT_Q_TILE
    R = TQ // B
    NP = D // LANES
    NQ = S // TQ
    q_spec, k_spec, v_spec, stat_spec, seq_spec, ck_spec = _att_specs(S, D)

    def body(q_ref, k_ref, v_ref, do_ref, o_ref, lse_ref, cq_ref, ck_ref,
             dq_ref, dk_ref, dv_ref, dcq_ref, dck_ref, dk_acc, dv_acc):
        qi = pl.program_id(1)

        @pl.when(qi == 0)
        def _():
            dk_acc[...] = jnp.zeros_like(dk_acc)
            dv_acc[...] = jnp.zeros_like(dv_acc)
            dck_ref[...] = jnp.zeros_like(dck_ref)

        qs = [_scaled_q(q_ref, hh) for hh in range(2)]
        dof = [do_ref[:, _head_lanes(hh)] for hh in range(2)]
        dob = [d.astype(BF16) for d in dof]
        delta = [jnp.sum(dof[hh] * o_ref[:, _head_lanes(hh)], axis=1, keepdims=True) for hh in range(2)]
        cqv = [cq_ref[:, hh * B:(hh + 1) * B] for hh in range(2)]
        lsev = [lse_ref[:, hh * B:(hh + 1) * B] for hh in range(2)]

        def tile(j, carry, masked):
            r0 = pl.multiple_of(j * B, B)
            if masked:
                causal = _key_minus_query(j, qi) <= 0
            out = []
            for hh in range(2):
                dq, rowsum = carry[hh]
                lanes = _head_lanes(hh)
                k = k_ref[pl.ds(r0, B), lanes]
                v = v_ref[pl.ds(r0, B), lanes]
                s = _dot_nt(qs[hh], k) + (cqv[hh] - ck_ref[hh, j][0:1, :])
                p = jnp.exp(s - lsev[hh])
                if masked:
                    p = jnp.where(causal, p, 0.0)
                ds = p * (_dot_nt(dob[hh], v) - delta[hh])
                dsb = ds.astype(BF16)
                dk_acc[pl.ds(r0, B), lanes] += _dot_tn(dsb, qs[hh])
                dv_acc[pl.ds(r0, B), lanes] += _dot_tn(p.astype(BF16), dob[hh])
                colsum = jnp.sum(ds, axis=0, keepdims=True)
                dck_ref[hh, j] = dck_ref[hh, j] - jnp.broadcast_to(colsum, (SUBLANES, B))
                out.append((dq + _dot(dsb, k), rowsum + jnp.sum(ds, axis=1, keepdims=True)))
            return tuple(out)

        zero = (jnp.zeros((TQ, HEAD_DIM), F32), jnp.zeros((TQ, 1), F32))
        carry = lax.fori_loop(0, qi * R, lambda j, cr: tile(j, cr, False), (zero, zero))
        carry = lax.fori_loop(0, R, lambda it, cr: tile(qi * R + it, cr, True), carry)
        for hh in range(2):
            dq, rowsum = carry[hh]
            dq_ref[:, _head_lanes(hh)] = (dq * ATT_SCALE).astype(BF16)
            dcq_ref[:, hh * B:(hh + 1) * B] = jnp.broadcast_to(rowsum, (TQ, B))

        @pl.when(qi == NQ - 1)
        def _():
            dk_ref[...] = dk_acc[...].astype(BF16)
            dv_ref[...] = dv_acc[...].astype(BF16)

    out = jax.ShapeDtypeStruct((S, D), BF16)
    return pl.pallas_call(
        body, name=name, grid=(NP, NQ),
        in_specs=[q_spec, k_spec, v_spec, q_spec, q_spec, stat_spec, stat_spec, ck_spec],
        out_specs=[q_spec, seq_spec, seq_spec, stat_spec, ck_spec],
        out_shape=[out, out, out, jax.ShapeDtypeStruct((S, 2 * NP * B), F32),
                   jax.ShapeDtypeStruct((2 * NP, S // B, SUBLANES, B), F32)],
        scratch_shapes=[pltpu.VMEM((S, LANES), F32), pltpu.VMEM((S, LANES), F32)],
        compiler_params=_params(dimension_semantics=("arbitrary", "arbitrary")),
    )(qkv, qkv, qkv, do, o, lse, cq, ck)


def _forget_cumsum(f3, bias, name):
    NQ, NH, B = f3.shape

    def body(f_ref, b_ref, cum_ref):
        row = lax.broadcasted_iota(jnp.int32, (B, B), 0)
        col = lax.broadcasted_iota(jnp.int32, (B, B), 1)
        t_incl = jnp.where(row <= col, 1.0, 0.0).astype(BF16)

        def step(b, carry):
            lf, _ = _log_sigmoids(f_ref[b] + b_ref[...])
            cum = _split_dot(lf, t_incl, 3) + carry
            cum_ref[b] = cum
            return jnp.broadcast_to(cum[:, B - 1:B], (NH, B))

        lax.fori_loop(0, NQ, step, jnp.zeros((NH, B), F32))

    return pl.pallas_call(
        body, name=name, in_specs=[VMEM_WHOLE, VMEM_WHOLE], out_specs=VMEM_WHOLE,
        out_shape=jax.ShapeDtypeStruct((NQ, NH, B), F32), compiler_params=_params(),
    )(f3, bias)


def _forget_cumsum_bwd(dcum3, f3, bias, name):
    NQ, NH, B = f3.shape

    def body(d_ref, f_ref, b_ref, df_ref, tot_ref):
        row = lax.broadcasted_iota(jnp.int32, (B, B), 0)
        col = lax.broadcasted_iota(jnp.int32, (B, B), 1)
        t_rev = jnp.where(row >= col, 1.0, 0.0).astype(BF16)

        def step(it, carry):
            run, tot = carry
            b = NQ - 1 - it
            dlf = _split_dot(d_ref[b], t_rev, 3) + run
            f = f_ref[b] + b_ref[...]
            df = dlf * (1.0 / (1.0 + jnp.exp(f)))
            df_ref[b] = df
            return jnp.broadcast_to(dlf[:, 0:1], (NH, B)), tot + df

        _, tot = lax.fori_loop(0, NQ, step, (jnp.zeros((NH, B), F32), jnp.zeros((NH, B), F32)))
        tot_ref[...] = _split_dot(tot, jnp.ones((B, B), BF16), 3)

    return pl.pallas_call(
        body, name=name, in_specs=[VMEM_WHOLE, VMEM_WHOLE, VMEM_WHOLE], out_specs=[VMEM_WHOLE, VMEM_WHOLE],
        out_shape=[jax.ShapeDtypeStruct((NQ, NH, B), F32), jax.ShapeDtypeStruct((NH, B), F32)],
        compiler_params=_params(),
    )(dcum3, f3, bias)


def _silu(x, name):
    def body(x_ref, o_ref):
        v = x_ref[...]
        o_ref[...] = v * (1.0 / (1.0 + jnp.exp(-v)))

    return pl.pallas_call(body, name=name, in_specs=[VMEM_WHOLE], out_specs=VMEM_WHOLE,
                          out_shape=jax.ShapeDtypeStruct(x.shape, F32), compiler_params=_params())(x)


def _sum_leading(x, name):
    N, R, C = x.shape
    T = _pick(R, (256, 128, 64, 32, 16, 8))

    def body(x_ref, o_ref):
        acc = x_ref[0].astype(F32)
        for r in range(1, N):
            acc = acc + x_ref[r].astype(F32)
        o_ref[...] = acc

    return pl.pallas_call(
        body, name=name, grid=(R // T,), in_specs=[pl.BlockSpec((N, T, C), lambda i: (0, i, 0))],
        out_specs=pl.BlockSpec((T, C), lambda i: (i, 0)), out_shape=jax.ShapeDtypeStruct((R, C), F32),
        compiler_params=_params(),
    )(x)


def _adamw(w, g, m, v, name):
    shape = w.shape
    C = shape[-1]
    R = w.size // C
    T = R
    for cand in (512, 256, 128, 64, 32, 16, 8):
        if R % cand == 0 and cand * C * 4 <= (1 << 20):
            T = cand
            break
    spec = pl.BlockSpec((T, C), lambda i: (i, 0))
    c1 = 1.0 / (1.0 - ADAM_B1 ** ADAM_STEP)
    c2 = 1.0 / (1.0 - ADAM_B2 ** ADAM_STEP)

    def body(w_ref, g_ref, m_ref, v_ref, d_ref, nm_ref, nv_ref):
        gv = g_ref[...]
        nm = ADAM_B1 * m_ref[...] + (1.0 - ADAM_B1) * gv
        nv = ADAM_B2 * v_ref[...] + (1.0 - ADAM_B2) * (gv * gv)
        nm_ref[...] = nm
        nv_ref[...] = nv
        d_ref[...] = -ADAM_LR * ((nm * c1) / (jnp.sqrt(nv * c2) + ADAM_EPS) + ADAM_WD * w_ref[...])

    out = jax.ShapeDtypeStruct((R, C), F32)
    d, nm, nv = pl.pallas_call(
        body, name=name, grid=(R // T,), in_specs=[spec] * 4, out_specs=[spec] * 3, out_shape=[out] * 3,
        compiler_params=_params(),
    )(w.reshape(R, C), g.reshape(R, C), m.reshape(R, C), v.reshape(R, C))
    return d.reshape(shape), nm.reshape(shape), nv.reshape(shape)


def _mesh_pos():
    return lax.axis_index("x"), lax.axis_index("y"), lax.axis_index("c")


def _all_gather_small(x2d, name):
    m_per, n = x2d.shape

    def body(x_ref, out_ref, send_sems, recv_sems, local_sem):
        x, y, c = _mesh_pos()
        me, sibling = (x, y, c), (x, y, 1 - c)
        chips = [(1 - x, y), (x, 1 - y), (1 - x, 1 - y)]

        def rows(px, py, pc):
            return out_ref.at[pl.ds((4 * px + 2 * py + pc) * m_per, m_per), :]

        def copy(k, block, to, src=None):
            return pltpu.make_async_remote_copy(
                src_ref=rows(*block) if src is None else src, dst_ref=rows(*block),
                send_sem=send_sems.at[k], recv_sem=recv_sems.at[k], device_id=to, device_id_type=MESH)

        mine = pltpu.make_async_copy(x_ref, rows(*me), local_sem)
        mine.start()
        first = [copy(0, me, sibling, src=x_ref)]
        first += [copy(1 + j, me, (*chip, c), src=x_ref) for j, chip in enumerate(chips)]
        for cp in first:
            cp.start()
        passed = [copy(4 + j, (*chip, c), sibling) for j, chip in enumerate(chips)]
        for j, chip in enumerate(chips):
            copy(1 + j, (*chip, c), me).wait_recv()
            passed[j].start()
        copy(0, sibling, me).wait_recv()
        for j, chip in enumerate(chips):
            copy(4 + j, (*chip, 1 - c), me).wait_recv()
        for cp in first + passed:
            cp.wait_send()
        mine.wait()

    return pl.pallas_call(
        body, name=name, out_shape=jax.ShapeDtypeStruct((N_DEV * m_per, n), x2d.dtype),
        in_specs=[VMEM_WHOLE], out_specs=VMEM_WHOLE,
        scratch_shapes=[pltpu.SemaphoreType.DMA((7,)), pltpu.SemaphoreType.DMA((7,)), pltpu.SemaphoreType.DMA],
        compiler_params=_params(),
    )(x2d)


def _all_gather_weights(ws, name):
    n = len(ws)

    def body(*refs):
        ins, outs = refs[:n], refs[n:2 * n]
        send_sems, recv_sems, local_sems = refs[2 * n:]
        x, y, c = _mesh_pos()
        k_me = 2 * x + y
        chips = [(1 - x, y), (x, 1 - y), (1 - x, 1 - y)]

        def remote(i, j, slot):
            px, py = chips[j]
            return pltpu.make_async_remote_copy(
                src_ref=ins[i], dst_ref=outs[i].at[slot], send_sem=send_sems.at[i, j],
                recv_sem=recv_sems.at[i, j], device_id=(px, py, c), device_id_type=MESH)

        local = [pltpu.make_async_copy(ins[i], outs[i].at[k_me], local_sems.at[i]) for i in range(n)]
        for i in range(n):
            local[i].start()
            for j in range(3):
                remote(i, j, k_me).start()
        for i in range(n):
            for j, (px, py) in enumerate(chips):
                remote(i, j, 2 * px + py).wait_recv()
        for i in range(n):
            for j in range(3):
                remote(i, j, k_me).wait_send()
            local[i].wait()

    return pl.pallas_call(
        body, name=name,
        out_shape=[jax.ShapeDtypeStruct((N_CHIPS,) + w.shape, w.dtype) for w in ws],
        in_specs=[ANY] * n, out_specs=[ANY] * n,
        scratch_shapes=[pltpu.SemaphoreType.DMA((n, 3)), pltpu.SemaphoreType.DMA((n, 3)),
                        pltpu.SemaphoreType.DMA((n,))],
        compiler_params=_params(),
    )(*ws)


def _scatter_grad_pieces(gs, name):
    n = len(gs)
    halves = [g.shape[1] // 2 for g in gs]

    def body(*refs):
        ins, outs = refs[:n], refs[n:2 * n]
        send_sems, recv_sems, local_sems = refs[2 * n:]
        x, y, c = _mesh_pos()

        def flip(v, bit):
            return 1 - v if bit else v

        def piece(i, px, py, pc):
            return ins[i].at[2 * px + py, pl.ds(pc * halves[i], halves[i])]

        def remote(i, r):
            px, py, pc = flip(x, r & 4), flip(y, r & 2), flip(c, r & 1)
            return pltpu.make_async_remote_copy(
                src_ref=piece(i, px, py, pc), dst_ref=outs[i].at[r], send_sem=send_sems.at[i, r - 1],
                recv_sem=recv_sems.at[i, r - 1], device_id=(px, py, pc), device_id_type=MESH)

        local = [pltpu.make_async_copy(piece(i, x, y, c), outs[i].at[0], local_sems.at[i]) for i in range(n)]
        for i in range(n):
            local[i].start()
            for r in range(1, N_DEV):
                remote(i, r).start()
        for i in range(n):
            for r in range(1, N_DEV):
                remote(i, r).wait_recv()
        for i in range(n):
            for r in range(1, N_DEV):
                remote(i, r).wait_send()
            local[i].wait()

    return pl.pallas_call(
        body, name=name,
        out_shape=[jax.ShapeDtypeStruct((N_DEV, h) + g.shape[2:], g.dtype) for g, h in zip(gs, halves)],
        in_specs=[ANY] * n, out_specs=[ANY] * n,
        scratch_shapes=[pltpu.SemaphoreType.DMA((n, N_DEV - 1)), pltpu.SemaphoreType.DMA((n, N_DEV - 1)),
                        pltpu.SemaphoreType.DMA((n,))],
        compiler_params=_params(),
    )(*gs)


def _swap_halves(hs, name):
    n = len(hs)

    def body(*refs):
        ins, outs = refs[:n], refs[n:2 * n]
        send_sems, recv_sems, local_sems = refs[2 * n:]
        x, y, c = _mesh_pos()

        def remote(i, slot):
            return pltpu.make_async_remote_copy(
                src_ref=ins[i], dst_ref=outs[i].at[slot], send_sem=send_sems.at[i], recv_sem=recv_sems.at[i],
                device_id=(x, y, 1 - c), device_id_type=MESH)

        local = [pltpu.make_async_copy(ins[i], outs[i].at[c], local_sems.at[i]) for i in range(n)]
        for i in range(n):
            local[i].start()
            remote(i, c).start()
        for i in range(n):
            remote(i, 1 - c).wait_recv()
        for i in range(n):
            remote(i, c).wait_send()
            local[i].wait()

    return pl.pallas_call(
        body, name=name,
        out_shape=[jax.ShapeDtypeStruct((2,) + h.shape, h.dtype) for h in hs],
        in_specs=[ANY] * n, out_specs=[ANY] * n,
        scratch_shapes=[pltpu.SemaphoreType.DMA((n,)), pltpu.SemaphoreType.DMA((n,)), pltpu.SemaphoreType.DMA((n,))],
        compiler_params=_params(),
    )(*hs)


def _pad_rows(a, rows):
    return jnp.pad(a, ((0, rows - a.shape[0]), (0, 0)))


def kernel(x, c, w_mod, b_mod, g_mix_pre, g_mix_post, w_qkv, w_o, w_fg, b_fg, g_ffn_pre, g_ffn_post, w_ffn_gate, w_ffn_up, w_conv, b_conv, w_ffn_down, loss_target, m_w_mod, m_b_mod, m_g_mix_pre, m_g_mix_post, m_w_qkv, m_w_o, m_w_fg, m_b_fg, m_g_ffn_pre, m_g_ffn_post, m_w_ffn_gate, m_w_ffn_up, m_w_conv, m_b_conv, m_w_ffn_down, v_w_mod, v_b_mod, v_g_mix_pre, v_g_mix_post, v_w_qkv, v_w_o, v_w_fg, v_b_fg, v_g_ffn_pre, v_g_ffn_post, v_w_ffn_gate, v_w_ffn_up, v_w_conv, v_b_conv, v_w_ffn_down):
    xs = x[0]
    target = loss_target[0]
    S, D = xs.shape
    L = w_mod.shape[0]
    LF = w_fg.shape[0]
    MS = w_mod.shape[2]
    QS = w_qkv.shape[2]
    OS = w_o.shape[1]
    FS = w_ffn_gate.shape[2]
    F = N_CHIPS * FS
    NH = D // HEAD_DIM
    B = ATT_BLOCK
    NQ = S // B
    ax, ay, ac = _mesh_pos()
    k_me = 2 * ax + ay
    b_me = 4 * ax + 2 * ay + ac

    conv_rows = -(-(L * 3 * FS) // D)
    conv_rows = -(-conv_rows // SUBLANES) * SUBLANES
    conv_flat = jnp.pad(w_conv.reshape(-1), (0, conv_rows * D - L * 3 * FS)).reshape(conv_rows, D)
    first = jnp.concatenate([_pad_rows(c, SUBLANES), conv_flat], axis=0)
    first_all = _all_gather_small(first, "ag_cond").reshape(N_DEV, SUBLANES + conv_rows, D)
    c_all = first_all[:, 0, :]
    conv_all = first_all[0::2, SUBLANES:, :].reshape(N_CHIPS, -1)[:, :L * 3 * FS]
    w_conv_full = conv_all.reshape(N_CHIPS, L, 3, FS).transpose(1, 2, 0, 3).reshape(L, 3, F)
    c_act = _silu(c_all, "silu_c")

    mod_part = jnp.concatenate(
        [_mm(c_act, w_mod[l], "nn", F32, "mm_mod", tm=N_DEV, tn=MS, tk=D) for l in range(L)], axis=1)
    mod_all = _all_gather_small(mod_part, "ag_mod").reshape(N_CHIPS, 2, N_DEV, L, MS)[:, 0]
    mod_mine = lax.dynamic_index_in_dim(mod_all, b_me, axis=1, keepdims=False)
    mod = mod_mine.transpose(1, 0, 2).reshape(L, N_MOD * D) + b_mod

    gq, go, gg, gu_, gd, gf = _all_gather_weights(
        [w_qkv.astype(BF16), w_o.astype(BF16), w_ffn_gate.astype(BF16), w_ffn_up.astype(BF16),
         w_ffn_down.astype(BF16), w_fg.astype(BF16)], "ag_weights")
    W_qkv = gq.transpose(1, 2, 0, 3).reshape(L, D, 3 * D)
    W_o = go.transpose(1, 0, 2, 3).reshape(L, D, D)
    W_g = gg.transpose(1, 2, 0, 3).reshape(L, D, F // LANES, 1, LANES)
    W_u = gu_.transpose(1, 2, 0, 3).reshape(L, D, F // LANES, 1, LANES)
    W_gu = jnp.concatenate([W_g, W_u], axis=3).reshape(L, D, 2 * F)
    W_d = gd.transpose(1, 0, 2, 3).reshape(L, F, D)
    W_fg = jnp.pad(gf.transpose(1, 0, 2, 3).reshape(LF, D, NH), ((0, 0), (0, 0), (0, LANES - NH)))

    def vec(a):
        return a.reshape(1, -1)

    saved = []
    xcur = xs
    for l in range(L):
        sh_a, sc_a, gt_a, sh_f, sc_f, gt_f = [vec(mod[l, j * D:(j + 1) * D]) for j in range(N_MOD)]
        is_fox = l % 2 == 1
        jf = l // 2
        h1 = _norm_mod(xcur, vec(g_mix_pre[l]), sc_a, sh_a, "norm_mod")
        qkv = _mm(h1, W_qkv[l], "nn", BF16, "mm_qkv")
        if is_fox:
            flog = _mm(h1, W_fg[jf], "nn", F32, "mm_fg")[:, :NH]
            f3 = flog.reshape(NQ, B, NH).transpose(0, 2, 1)
            bias = b_fg[jf].reshape(NH, 1)
            cum3 = _forget_cumsum(f3, bias, "forget_cumsum")
            cq = jnp.repeat(cum3.transpose(0, 2, 1).reshape(S, NH), B, axis=1)
            ck = jnp.broadcast_to(cum3.transpose(1, 0, 2)[:, :, None, :], (NH, NQ, SUBLANES, B))
            o, stat = _fox_fwd(qkv, cq, ck, "fox_fwd")
            extra = (f3, bias, cq, ck)
        else:
            o, stat = _sb_fwd(qkv, "sb_fwd")
            extra = None
        p = _mm(o, W_o[l], "nn", F32, "mm_o")
        x1 = _post_res(xcur, p, vec(g_mix_post[l]), gt_a, "post_res")
        h2 = _norm_mod(x1, vec(g_ffn_pre[l]), sc_f, sh_f, "norm_mod")
        gu = _mm(h2, W_gu[l], "nn", F32, "mm_gu")
        wc = w_conv_full[l]
        bc = vec(b_conv[l])
        a = _conv_gate(gu, wc, bc, "conv_gate")
        yv = _mm(a, W_d[l], "nn", F32, "mm_down")
        x2 = _post_res(x1, yv, vec(g_ffn_post[l]), gt_f, "post_res")
        saved.append(dict(x0=xcur, h1=h1, qkv=qkv, o=o, stat=stat, extra=extra, p=p, x1=x1, h2=h2, gu=gu, a=a,
                          y=yv, mods=(sh_a, sc_a, gt_a, sh_f, sc_f, gt_f), wc=wc, bc=bc))
        xcur = x2

    g, sq = _loss_head(xcur, target, "loss_head")
    loss_part = 0.5 * jnp.sum(sq) / D
    loss = lax.psum(loss_part, ("x", "y", "c"))

    dW_qkv, dW_o, dW_gu, dW_d = [None] * L, [None] * L, [None] * L, [None] * L
    dW_fg, db_fg = [None] * LF, [None] * LF
    dmod, dg_mix_pre, dg_mix_post, dg_ffn_pre, dg_ffn_post = [[None] * L for _ in range(5)]
    dw_conv, db_conv = [None] * L, [None] * L
    for l in reversed(range(L)):
        sv = saved[l]
        sh_a, sc_a, gt_a, sh_f, sc_f, gt_f = sv["mods"]
        is_fox = l % 2 == 1
        jf = l // 2
        dy, dgp_f, dgt_f = _post_res_bwd(sv["y"], vec(g_ffn_post[l]), gt_f, g, "post_res_bwd")
        da = _mm(dy, W_d[l], "nt", F32, "mm_da")
        dW_d[l] = _mm(sv["a"], dy, "tn", F32, "mm_dwd")
        dgu, dwc, dbc = _conv_gate_bwd(sv["gu"], sv["wc"], sv["bc"], da, "conv_gate_bwd")
        dh2 = _mm(dgu, W_gu[l], "nt", F32, "mm_dh2")
        dW_gu[l] = _mm(sv["h2"], dgu, "tn", F32, "mm_dwgu")
        g, dg_f, dsc_f, dsh_f = _norm_mod_bwd(sv["x1"], vec(g_ffn_pre[l]), sc_f, sh_f, [dh2], g, "norm_mod_bwd")
        dp, dgp_a, dgt_a = _post_res_bwd(sv["p"], vec(g_mix_post[l]), gt_a, g, "post_res_bwd")
        do = _mm(dp, W_o[l], "nt", F32, "mm_do")
        dW_o[l] = _mm(sv["o"], dp, "tn", F32, "mm_dwo")
        if is_fox:
            f3, bias, cq, ck = sv["extra"]
            dq, dk, dv, dcq, dck = _fox_bwd(sv["qkv"], do, sv["o"], sv["stat"], cq, ck, "fox_bwd")
            dcum_q = dcq.reshape(NQ, B, NH, B)[:, :, :, 0].transpose(0, 2, 1)
            dcum3 = dcum_q + dck[:, :, 0, :].transpose(1, 0, 2)
            df3, df_tot = _forget_cumsum_bwd(dcum3, f3, bias, "forget_cumsum_bwd")
            df = df3.transpose(0, 2, 1).reshape(S, NH)
            dfp = jnp.pad(df, ((0, 0), (0, LANES - NH)))
            dW_fg[jf] = _mm(sv["h1"], dfp, "tn", F32, "mm_dwfg")[:, :NH]
            db_fg[jf] = df_tot[:, 0]
            dh_extra = [_mm(dfp, W_fg[jf], "nt", F32, "mm_dh1f")]
        else:
            dq, dk, dv = _sb_bwd(sv["qkv"], do, sv["stat"], "sb_bwd")
            dh_extra = []
        dqkv = jnp.concatenate([dq, dk, dv], axis=1)
        dh1 = _mm(dqkv, W_qkv[l], "nt", F32, "mm_dh1")
        dW_qkv[l] = _mm(sv["h1"], dqkv, "tn", F32, "mm_dwqkv")
        g, dg_a, dsc_a, dsh_a = _norm_mod_bwd(sv["x0"], vec(g_mix_pre[l]), sc_a, sh_a, [dh1] + dh_extra, g,
                                              "norm_mod_bwd")
        dmod[l] = jnp.concatenate([dsh_a, dsc_a, dgt_a, dsh_f, dsc_f, dgt_f], axis=1)[0]
        dg_mix_pre[l], dg_mix_post[l], dg_ffn_pre[l], dg_ffn_post[l] = dg_a[0], dgp_a[0], dg_f[0], dgp_f[0]
        dw_conv[l], db_conv[l] = dwc, dbc[0]
    grad_x = g[None]

    pieces = [jnp.stack(dmod), jnp.stack(dg_mix_pre), jnp.stack(dg_mix_post), jnp.stack(dg_ffn_pre),
              jnp.stack(dg_ffn_post), jnp.stack(db_fg), jnp.stack(dW_fg), jnp.stack(dw_conv), jnp.stack(db_conv)]
    sizes = [pc.size for pc in pieces]
    total = sum(sizes)
    pack_rows = -(-total // (LANES * SUBLANES)) * SUBLANES
    pack = jnp.pad(jnp.concatenate([pc.reshape(-1) for pc in pieces]), (0, pack_rows * LANES - total))
    pack_all = _all_gather_small(pack.reshape(pack_rows, LANES), "ag_small_grads").reshape(N_DEV, pack_rows, LANES)
    small = _sum_leading(pack_all, "sum_small_grads").reshape(-1)
    offs = [0]
    for sz in sizes:
        offs.append(offs[-1] + sz)
    parts = [small[offs[i]:offs[i + 1]].reshape(pieces[i].shape) for i in range(len(pieces))]
    g_b_mod, g_g_mix_pre, g_g_mix_post, g_g_ffn_pre, g_g_ffn_post, g_b_fg, g_w_fg_full, g_w_conv_full, g_b_conv = parts
    g_w_fg = lax.dynamic_slice_in_dim(g_w_fg_full, k_me * OS, OS, axis=1)
    g_w_conv = lax.dynamic_slice_in_dim(g_w_conv_full, k_me * FS, FS, axis=2)
    dmod_all = pack_all.reshape(N_DEV, -1)[:, :L * N_MOD * D].reshape(N_DEV, L, N_CHIPS, MS)
    dmod_cols = lax.dynamic_index_in_dim(dmod_all, k_me, axis=2, keepdims=False).reshape(N_DEV, L * MS)
    g_w_mod = _mm(_pad_rows(c_act, LANES), _pad_rows(dmod_cols, LANES), "tn", F32, "mm_dwmod", tm=D, tn=MS, tk=LANES)
    g_w_mod = g_w_mod.reshape(D, L, MS).transpose(1, 0, 2)

    def shard_cols(dws, width):
        st = jnp.stack(dws)
        return st.reshape(L, st.shape[1], N_CHIPS, width).transpose(2, 0, 1, 3).astype(BF16)

    def shard_rows(dws, height):
        st = jnp.stack(dws)
        return st.reshape(L, N_CHIPS, height, st.shape[2]).transpose(1, 0, 2, 3).astype(BF16)

    dgu_st = jnp.stack(dW_gu).reshape(L, D, F // LANES, 2, LANES)
    dWg = [dgu_st[l, :, :, 0, :].reshape(D, F) for l in range(L)]
    dWu = [dgu_st[l, :, :, 1, :].reshape(D, F) for l in range(L)]
    gs = [shard_cols(dW_qkv, QS), shard_rows(dW_o, OS), shard_cols(dWg, FS), shard_cols(dWu, FS),
          shard_rows(dW_d, FS)]
    recv = _scatter_grad_pieces(gs, "rs_scatter")
    halves = []
    for i, r in enumerate(recv):
        cols = r.shape[-1]
        halves.append(_sum_leading(r.reshape(N_DEV, -1, cols), f"sum_grad_pieces_{i}").reshape(r.shape[1:]))
    full = _swap_halves(halves, "rs_swap")
    g_w_qkv, g_w_o, g_w_gate, g_w_up, g_w_down = [f.reshape((L,) + f.shape[2:]) for f in full]

    grads = [g_w_mod, g_b_mod, g_g_mix_pre, g_g_mix_post, g_w_qkv, g_w_o, g_w_fg, g_b_fg, g_g_ffn_pre,
             g_g_ffn_post, g_w_gate, g_w_up, g_w_conv, g_b_conv, g_w_down]
    weights = [w_mod, b_mod, g_mix_pre, g_mix_post, w_qkv, w_o, w_fg, b_fg, g_ffn_pre, g_ffn_post, w_ffn_gate,
               w_ffn_up, w_conv, b_conv, w_ffn_down]
    ms = [m_w_mod, m_b_mod, m_g_mix_pre, m_g_mix_post, m_w_qkv, m_w_o, m_w_fg, m_b_fg, m_g_ffn_pre, m_g_ffn_post,
          m_w_ffn_gate, m_w_ffn_up, m_w_conv, m_b_conv, m_w_ffn_down]
    vs = [v_w_mod, v_b_mod, v_g_mix_pre, v_g_mix_post, v_w_qkv, v_w_o, v_w_fg, v_b_fg, v_g_ffn_pre, v_g_ffn_post,
          v_w_ffn_gate, v_w_ffn_up, v_w_conv, v_b_conv, v_w_ffn_down]
    deltas, new_ms, new_vs = [], [], []
    for wv, gv, mv, vv in zip(weights, grads, ms, vs):
        d, nm, nv = _adamw(wv, gv, mv, vv, "adamw")
        deltas.append(d)
        new_ms.append(nm)
        new_vs.append(nv)
    return (loss, grad_x, *grads, *deltas, *new_ms, *new_vs)
```

```python
import functools

import jax
import jax.numpy as jnp
from jax import lax
from jax.experimental import pallas as pl
from jax.experimental.pallas import tpu as pltpu

F32 = jnp.float32
BF16 = jnp.bfloat16
MESH = pl.DeviceIdType.MESH

HEAD_DIM = 64
ATT_BLOCK = 128
SB_Q_TILE = 512
FOX_Q_TILE = 1024
SB_DEAD_LOG = -110.0
LANES = 128
SUBLANES = 8
RMS_EPS = 1e-6
N_MOD = 6
N_CHIPS = 4
N_DEV = 8
ADAM_LR = 0.001
ADAM_B1 = 0.9
ADAM_B2 = 0.999
ADAM_EPS = 1e-08
ADAM_WD = 0.01
ADAM_STEP = 10
VMEM_LIMIT_BYTES = 56 * 1024 * 1024
NEG_BIG = -1e30

ANY = pl.BlockSpec(memory_space=pl.ANY)
VMEM_WHOLE = pl.BlockSpec(memory_space=pltpu.VMEM)


def _params(**kw):
    return pltpu.CompilerParams(vmem_limit_bytes=VMEM_LIMIT_BYTES, **kw)


def _dot(a, b):
    return jnp.dot(a, b, preferred_element_type=F32)


def _dot_nt(a, b):
    return lax.dot_general(a, b, (((1,), (1,)), ((), ())), preferred_element_type=F32)


def _dot_tn(a, b):
    return lax.dot_general(a, b, (((0,), (0,)), ((), ())), preferred_element_type=F32)


def _split_dot(x, t, parts):
    acc = None
    rem = x
    for _ in range(parts):
        piece = rem.astype(BF16)
        rem = rem - piece.astype(F32)
        d = _dot(piece, t)
        acc = d if acc is None else acc + d
    return acc


def _pick(n, prefs):
    for p in prefs:
        if n % p == 0:
            return p
    return n


def _mm(a, b, dims, out_dtype, name, tm=None, tn=None, tk=None):
    if dims == "tn":
        K, M = a.shape
    else:
        M, K = a.shape
    N = b.shape[0] if dims == "nt" else b.shape[1]
    tm = tm or _pick(M, (512, 256, 128))
    tn = tn or _pick(N, (1536, 1408, 1024, 768, 512, 256, 128))
    tk = tk or _pick(K, (1024, 1408, 512, 256, 128))
    nk = K // tk
    grid = (M // tm, N // tn, nk)
    if dims == "tn":
        a_spec = pl.BlockSpec((tk, tm), lambda i, j, k: (k, i))
    else:
        a_spec = pl.BlockSpec((tm, tk), lambda i, j, k: (i, k))
    if dims == "nt":
        b_spec = pl.BlockSpec((tn, tk), lambda i, j, k: (j, k))
    else:
        b_spec = pl.BlockSpec((tk, tn), lambda i, j, k: (k, j))
    o_spec = pl.BlockSpec((tm, tn), lambda i, j, k: (i, j))

    def body(a_ref, b_ref, o_ref, *scratch):
        x = a_ref[...].astype(BF16)
        y = b_ref[...].astype(BF16)
        if dims == "nn":
            r = _dot(x, y)
        elif dims == "nt":
            r = _dot_nt(x, y)
        else:
            r = _dot_tn(x, y)
        if nk == 1:
            o_ref[...] = r.astype(out_dtype)
        else:
            acc = scratch[0]
            k = pl.program_id(2)

            @pl.when(k == 0)
            def _():
                acc[...] = r

            @pl.when(k > 0)
            def _():
                acc[...] += r

            @pl.when(k == nk - 1)
            def _():
                o_ref[...] = acc[...].astype(out_dtype)

    return pl.pallas_call(
        body,
        name=name,
        grid=grid,
        in_specs=[a_spec, b_spec],
        out_specs=o_spec,
        out_shape=jax.ShapeDtypeStruct((M, N), out_dtype),
        scratch_shapes=[pltpu.VMEM((tm, tn), F32)] if nk > 1 else [],
        compiler_params=_params(dimension_semantics=("parallel", "parallel", "arbitrary")),
    )(a, b)


def _row_tile(S):
    return _pick(S, (256, 128, 64, 32, 16, 8))


def _norm_mod(x, g, sc, sh, name):
    S, D = x.shape
    T = _row_tile(S)
    row = pl.BlockSpec((T, D), lambda i: (i, 0))
    vec = pl.BlockSpec((1, D), lambda i: (0, 0))

    def body(x_ref, g_ref, sc_ref, sh_ref, h_ref):
        xv = x_ref[...]
        r = lax.rsqrt(jnp.mean(xv * xv, axis=-1, keepdims=True) + RMS_EPS)
        n = (xv * r) * g_ref[...]
        h_ref[...] = (n * (1.0 + sc_ref[...]) + sh_ref[...]).astype(BF16)

    return pl.pallas_call(
        body, name=name, grid=(S // T,), in_specs=[row, vec, vec, vec], out_specs=row,
        out_shape=jax.ShapeDtypeStruct((S, D), BF16), compiler_params=_params(),
    )(x, g, sc, sh)


def _norm_mod_bwd(x, g, sc, sh, dhs, gres, name):
    S, D = x.shape
    T = _row_tile(S)
    n_dh = len(dhs)
    row = pl.BlockSpec((T, D), lambda i: (i, 0))
    vec = pl.BlockSpec((1, D), lambda i: (0, 0))

    def body(x_ref, g_ref, sc_ref, sh_ref, *refs):
        dh_refs = refs[:n_dh]
        gres_ref, dx_ref, dg_ref, dsc_ref, dsh_ref = refs[n_dh:]
        xv = x_ref[...]
        r = lax.rsqrt(jnp.mean(xv * xv, axis=-1, keepdims=True) + RMS_EPS)
        xn = xv * r
        n = xn * g_ref[...]
        dh = dh_refs[0][...]
        for extra in dh_refs[1:]:
            dh = dh + extra[...]
        dn = dh * (1.0 + sc_ref[...])
        dxn = dn * g_ref[...]
        dx = r * (dxn - xn * jnp.mean(dxn * xn, axis=-1, keepdims=True))
        dx_ref[...] = gres_ref[...] + dx

        @pl.when(pl.program_id(0) == 0)
        def _():
            dg_ref[...] = jnp.zeros_like(dg_ref)
            dsc_ref[...] = jnp.zeros_like(dsc_ref)
            dsh_ref[...] = jnp.zeros_like(dsh_ref)

        dg_ref[...] += jnp.sum(dn * xn, axis=0, keepdims=True)
        dsc_ref[...] += jnp.sum(dh * n, axis=0, keepdims=True)
        dsh_ref[...] += jnp.sum(dh, axis=0, keepdims=True)

    vshape = jax.ShapeDtypeStruct((1, D), F32)
    return pl.pallas_call(
        body, name=name, grid=(S // T,), in_specs=[row, vec, vec, vec] + [row] * (n_dh + 1),
        out_specs=[row, vec, vec, vec],
        out_shape=[jax.ShapeDtypeStruct((S, D), F32), vshape, vshape, vshape],
        compiler_params=_params(dimension_semantics=("arbitrary",)),
    )(x, g, sc, sh, *dhs, gres)


def _post_res(x, p, gp, gt, name):
    S, D = x.shape
    T = _row_tile(S)
    row = pl.BlockSpec((T, D), lambda i: (i, 0))
    vec = pl.BlockSpec((1, D), lambda i: (0, 0))

    def body(x_ref, p_ref, gp_ref, gt_ref, o_ref):
        pv = p_ref[...]
        r = lax.rsqrt(jnp.mean(pv * pv, axis=-1, keepdims=True) + RMS_EPS)
        o_ref[...] = x_ref[...] + gt_ref[...] * ((pv * r) * gp_ref[...])

    return pl.pallas_call(
        body, name=name, grid=(S // T,), in_specs=[row, row, vec, vec], out_specs=row,
        out_shape=jax.ShapeDtypeStruct((S, D), F32), compiler_params=_params(),
    )(x, p, gp, gt)


def _post_res_bwd(p, gp, gt, g, name):
    S, D = p.shape
    T = _row_tile(S)
    row = pl.BlockSpec((T, D), lambda i: (i, 0))
    vec = pl.BlockSpec((1, D), lambda i: (0, 0))

    def body(p_ref, gp_ref, gt_ref, g_ref, dp_ref, dgp_ref, dgt_ref):
        pv = p_ref[...]
        gv = g_ref[...]
        r = lax.rsqrt(jnp.mean(pv * pv, axis=-1, keepdims=True) + RMS_EPS)
        pn = pv * r
        n2 = pn * gp_ref[...]
        dn2 = gv * gt_ref[...]
        dpn = dn2 * gp_ref[...]
        dp = r * (dpn - pn * jnp.mean(dpn * pn, axis=-1, keepdims=True))
        dp_ref[...] = dp.astype(BF16)

        @pl.when(pl.program_id(0) == 0)
        def _():
            dgp_ref[...] = jnp.zeros_like(dgp_ref)
            dgt_ref[...] = jnp.zeros_like(dgt_ref)

        dgp_ref[...] += jnp.sum(dn2 * pn, axis=0, keepdims=True)
        dgt_ref[...] += jnp.sum(gv * n2, axis=0, keepdims=True)

    vshape = jax.ShapeDtypeStruct((1, D), F32)
    return pl.pallas_call(
        body, name=name, grid=(S // T,), in_specs=[row, vec, vec, row], out_specs=[row, vec, vec],
        out_shape=[jax.ShapeDtypeStruct((S, D), BF16), vshape, vshape],
        compiler_params=_params(dimension_semantics=("arbitrary",)),
    )(p, gp, gt, g)


def _loss_head(y, target, name):
    S, D = y.shape
    T = _row_tile(S)
    row = pl.BlockSpec((T, D), lambda i: (i, 0))
    vec = pl.BlockSpec((1, D), lambda i: (0, 0))

    def body(y_ref, t_ref, dy_ref, sq_ref):
        e = y_ref[...] - t_ref[...]
        dy_ref[...] = e * (1.0 / D)

        @pl.when(pl.program_id(0) == 0)
        def _():
            sq_ref[...] = jnp.zeros_like(sq_ref)

        sq_ref[...] += jnp.sum(e * e, axis=0, keepdims=True)

    return pl.pallas_call(
        body, name=name, grid=(S // T,), in_specs=[row, row], out_specs=[row, vec],
        out_shape=[jax.ShapeDtypeStruct((S, D), F32), jax.ShapeDtypeStruct((1, D), F32)],
        compiler_params=_params(dimension_semantics=("arbitrary",)),
    )(y, target)


def _shift_down(v, k, rows):
    return jnp.where(rows >= k, pltpu.roll(v, k, 0), 0.0)


def _shift_up(v, k, rows, S):
    return jnp.where(rows < S - k, pltpu.roll(v, S - k, 0), 0.0)


def _conv_gate(gu, wc, bc, name):
    S, F2 = gu.shape
    F = F2 // 2
    C = LANES

    def body(gu_ref, w_ref, b_ref, a_ref):
        rows = lax.broadcasted_iota(jnp.int32, (S, C), 0)
        gate = gu_ref[:, :C]
        up = gu_ref[:, C:]
        w = w_ref[...]
        gc = w[2:3] * gate + w[1:2] * _shift_down(gate, 1, rows) + w[0:1] * _shift_down(gate, 2, rows) + b_ref[...]
        a_ref[...] = (gc * (1.0 / (1.0 + jnp.exp(-gc))) * up).astype(BF16)

    return pl.pallas_call(
        body, name=name, grid=(F // C,),
        in_specs=[pl.BlockSpec((S, 2 * C), lambda j: (0, j)), pl.BlockSpec((3, C), lambda j: (0, j)),
                  pl.BlockSpec((1, C), lambda j: (0, j))],
        out_specs=pl.BlockSpec((S, C), lambda j: (0, j)),
        out_shape=jax.ShapeDtypeStruct((S, F), BF16), compiler_params=_params(),
    )(gu, wc, bc)


def _conv_gate_bwd(gu, wc, bc, da, name):
    S, F2 = gu.shape
    F = F2 // 2
    C = LANES

    def body(gu_ref, w_ref, b_ref, da_ref, dgu_ref, dw_ref, db_ref):
        rows = lax.broadcasted_iota(jnp.int32, (S, C), 0)
        gate = gu_ref[:, :C]
        up = gu_ref[:, C:]
        dav = da_ref[...]
        w = w_ref[...]
        g1 = _shift_down(gate, 1, rows)
        g2 = _shift_down(gate, 2, rows)
        gc = w[2:3] * gate + w[1:2] * g1 + w[0:1] * g2 + b_ref[...]
        sg = 1.0 / (1.0 + jnp.exp(-gc))
        dgu_ref[:, C:] = (dav * (gc * sg)).astype(BF16)
        dgc = dav * up * (sg * (1.0 + gc * (1.0 - sg)))
        db_ref[...] = jnp.sum(dgc, axis=0, keepdims=True)
        dw_ref[0:1, :] = jnp.sum(dgc * g2, axis=0, keepdims=True)
        dw_ref[1:2, :] = jnp.sum(dgc * g1, axis=0, keepdims=True)
        dw_ref[2:3, :] = jnp.sum(dgc * gate, axis=0, keepdims=True)
        dgate = w[2:3] * dgc + w[1:2] * _shift_up(dgc, 1, rows, S) + w[0:1] * _shift_up(dgc, 2, rows, S)
        dgu_ref[:, :C] = dgate.astype(BF16)

    return pl.pallas_call(
        body, name=name, grid=(F // C,),
        in_specs=[pl.BlockSpec((S, 2 * C), lambda j: (0, j)), pl.BlockSpec((3, C), lambda j: (0, j)),
                  pl.BlockSpec((1, C), lambda j: (0, j)), pl.BlockSpec((S, C), lambda j: (0, j))],
        out_specs=[pl.BlockSpec((S, 2 * C), lambda j: (0, j)), pl.BlockSpec((3, C), lambda j: (0, j)),
                   pl.BlockSpec((1, C), lambda j: (0, j))],
        out_shape=[jax.ShapeDtypeStruct((S, F2), BF16), jax.ShapeDtypeStruct((3, F), F32),
                   jax.ShapeDtypeStruct((1, F), F32)],
        compiler_params=_params(),
    )(gu, wc, bc, da)


ATT_SCALE = HEAD_DIM ** -0.5


def _att_specs(S, D, TQ):
    nb = D // LANES
    q_spec = pl.BlockSpec((TQ, LANES), lambda p, i: (i, p))
    k_spec = pl.BlockSpec((S, LANES), lambda p, i: (0, nb + p))
    v_spec = pl.BlockSpec((S, LANES), lambda p, i: (0, 2 * nb + p))
    stat_spec = pl.BlockSpec((TQ, 2 * ATT_BLOCK), lambda p, i: (i, p))
    seq_spec = pl.BlockSpec((S, LANES), lambda p, i: (0, p))
    ck_spec = pl.BlockSpec((2, S // ATT_BLOCK, SUBLANES, ATT_BLOCK), lambda p, i: (p, 0, 0, 0))
    return q_spec, k_spec, v_spec, stat_spec, seq_spec, ck_spec


def _head_lanes(hh):
    return slice(hh * HEAD_DIM, (hh + 1) * HEAD_DIM)


def _scaled_q(q_ref, hh):
    return (q_ref[:, _head_lanes(hh)].astype(F32) * ATT_SCALE).astype(BF16)


def _tri(cmp):
    B = ATT_BLOCK
    row = lax.broadcasted_iota(jnp.int32, (B, B), 0)
    col = lax.broadcasted_iota(jnp.int32, (B, B), 1)
    half = jnp.concatenate([jnp.where(cmp(row, col), 1.0, 0.0).astype(BF16), jnp.ones((B, B), BF16)], axis=1)
    return jnp.concatenate([half, half], axis=0)


def _hi_lo_dot(x, t):
    hi = x.astype(BF16)
    lo = (x - hi.astype(F32)).astype(BF16)
    return _dot(jnp.concatenate([hi, lo], axis=1), t)


def _key_minus_query(j, qi, TQ):
    row = lax.broadcasted_iota(jnp.int32, (TQ, ATT_BLOCK), 0)
    col = lax.broadcasted_iota(jnp.int32, (TQ, ATT_BLOCK), 1)
    return col - row + (j * ATT_BLOCK - qi * TQ)


def _log_sigmoids(z):
    sp = jnp.log(1.0 + jnp.exp(-jnp.abs(z)))
    lb = jnp.minimum(z, 0.0) - sp
    return lb, lb - z


def _log_sigmoids_fast(z):
    zc = jnp.maximum(z, -80.0)
    lb = -jnp.log(1.0 + jnp.exp(-zc))
    return lb, lb - zc


def _sb_fwd(qkv, name):
    S, D3 = qkv.shape
    D = D3 // 3
    B, TQ = ATT_BLOCK, SB_Q_TILE
    R = TQ // B
    NP = D // LANES
    q_spec, k_spec, v_spec, stat_spec, _, _ = _att_specs(S, D, TQ)

    def body(q_ref, k_ref, v_ref, o_ref, lt_ref, first_ref):
        qi = pl.program_id(1)
        t_suffix = _tri(lambda r, c: r > c)
        qs = [_scaled_q(q_ref, hh) for hh in range(2)]

        def tile(j, carry, masked):
            r0 = pl.multiple_of(j * B, B)
            if masked:
                strict = _key_minus_query(j, qi, TQ) < 0
            out = []
            for hh in range(2):
                c, acc = carry[hh]
                k = k_ref[pl.ds(r0, B), _head_lanes(hh)]
                v = v_ref[pl.ds(r0, B), _head_lanes(hh)]
                lb, l1 = _log_sigmoids_fast(_dot_nt(qs[hh], k))
                if masked:
                    l1 = jnp.where(strict, l1, 0.0)
                sums = _hi_lo_dot(l1, t_suffix)
                a = jnp.exp(lb + c + sums[:, :B])
                if masked:
                    a = jnp.where(strict, a, 0.0)
                out.append((c + sums[:, B:], acc + _dot(a.astype(BF16), v)))
            return tuple(out)

        zero = (jnp.zeros((TQ, B), F32), jnp.zeros((TQ, HEAD_DIM), F32))
        last = qi * R + R - 1
        carry = lax.fori_loop(0, R, lambda it, cr: tile(last - it, cr, True), (zero, zero))

        def alive(cr):
            return jnp.max(jnp.maximum(cr[0][0], cr[1][0])) > SB_DEAD_LOG

        def walk(state):
            j, cr, _ = state
            cr = tile(j, cr, False)
            return j - 1, cr, alive(cr)

        j, carry, _ = lax.while_loop(lambda st: jnp.logical_and(st[0] >= 0, st[2]), walk,
                                     (qi * R - 1, carry, alive(carry)))
        first_ref[pl.program_id(0), qi] = j + 1
        for hh in range(2):
            c, acc = carry[hh]
            o_ref[:, _head_lanes(hh)] = acc
            lt_ref[:, hh * B:(hh + 1) * B] = c

    return pl.pallas_call(
        body, name=name, grid=(NP, S // TQ), in_specs=[q_spec, k_spec, v_spec],
        out_specs=[q_spec, stat_spec, pl.BlockSpec(memory_space=pltpu.SMEM)],
        out_shape=[jax.ShapeDtypeStruct((S, D), F32), jax.ShapeDtypeStruct((S, 2 * NP * B), F32),
                   jax.ShapeDtypeStruct((NP, S // TQ), jnp.int32)],
        compiler_params=_params(dimension_semantics=("arbitrary", "arbitrary")),
    )(qkv, qkv, qkv)


def _sb_bwd(qkv, do, lt, first, name):
    S, D3 = qkv.shape
    D = D3 // 3
    B, TQ = ATT_BLOCK, SB_Q_TILE
    R = TQ // B
    NP = D // LANES
    NQ = S // TQ
    q_spec, k_spec, v_spec, stat_spec, seq_spec, _ = _att_specs(S, D, TQ)

    def body(first_ref, q_ref, k_ref, v_ref, do_ref, lt_ref, dq_ref, dk_ref, dv_ref, dk_acc, dv_acc):
        qi = pl.program_id(1)
        t_prefix = _tri(lambda r, c: r <= c)
        t_before = _tri(lambda r, c: r < c)

        @pl.when(qi == 0)
        def _():
            dk_acc[...] = jnp.zeros_like(dk_acc)
            dv_acc[...] = jnp.zeros_like(dv_acc)

        qs = [_scaled_q(q_ref, hh) for hh in range(2)]
        dob = [do_ref[:, _head_lanes(hh)].astype(BF16) for hh in range(2)]
        ltot = [lt_ref[:, hh * B:(hh + 1) * B] for hh in range(2)]

        def tile(j, carry, masked):
            r0 = pl.multiple_of(j * B, B)
            if masked:
                strict = _key_minus_query(j, qi, TQ) < 0
            out = []
            for hh in range(2):
                pre, cu, dq = carry[hh]
                lanes = _head_lanes(hh)
                k = k_ref[pl.ds(r0, B), lanes]
                v = v_ref[pl.ds(r0, B), lanes]
                lb, l1 = _log_sigmoids_fast(_dot_nt(qs[hh], k))
                if masked:
                    l1 = jnp.where(strict, l1, 0.0)
                sums = _hi_lo_dot(l1, t_prefix)
                a = jnp.exp(lb + (ltot[hh] - pre - sums[:, :B]))
                if masked:
                    a = jnp.where(strict, a, 0.0)
                u = a * _dot_nt(dob[hh], v)
                usums = _hi_lo_dot(u, t_before)
                dz = u - (u + cu + usums[:, :B]) * jnp.exp(lb)
                if masked:
                    dz = jnp.where(strict, dz, 0.0)
                dzb = dz.astype(BF16)
                dk_acc[pl.ds(r0, B), lanes] += _dot_tn(dzb, qs[hh])
                dv_acc[pl.ds(r0, B), lanes] += _dot_tn(a.astype(BF16), dob[hh])
                out.append((pre + sums[:, B:], cu + usums[:, B:], dq + _dot(dzb, k)))
            return tuple(out)

        zero = (jnp.zeros((TQ, B), F32), jnp.zeros((TQ, B), F32), jnp.zeros((TQ, HEAD_DIM), F32))
        carry = lax.fori_loop(first_ref[pl.program_id(0), qi], qi * R, lambda j, cr: tile(j, cr, False),
                              (zero, zero))
        carry = lax.fori_loop(0, R, lambda it, cr: tile(qi * R + it, cr, True), carry)
        for hh in range(2):
            dq_ref[:, _head_lanes(hh)] = (carry[hh][2] * ATT_SCALE).astype(BF16)

        @pl.when(qi == NQ - 1)
        def _():
            dk_ref[...] = dk_acc[...].astype(BF16)
            dv_ref[...] = dv_acc[...].astype(BF16)

    out = jax.ShapeDtypeStruct((S, D), BF16)
    return pl.pallas_call(
        body, name=name, grid=(NP, NQ),
        in_specs=[pl.BlockSpec(memory_space=pltpu.SMEM), q_spec, k_spec, v_spec, q_spec, stat_spec],
        out_specs=[q_spec, seq_spec, seq_spec], out_shape=[out, out, out],
        scratch_shapes=[pltpu.VMEM((S, LANES), F32), pltpu.VMEM((S, LANES), F32)],
        compiler_params=_params(dimension_semantics=("arbitrary", "arbitrary")),
    )(first, qkv, qkv, qkv, do, lt)


def _fox_fwd(qkv, cq, ck, name):
    S, D3 = qkv.shape
    D = D3 // 3
    B, TQ = ATT_BLOCK, FOX_Q_TILE
    R = TQ // B
    NP = D // LANES
    q_spec, k_spec, v_spec, stat_spec, _, ck_spec = _att_specs(S, D, TQ)

    def body(q_ref, k_ref, v_ref, cq_ref, ck_ref, o_ref, lse_ref):
        qi = pl.program_id(1)
        qs = [_scaled_q(q_ref, hh) for hh in range(2)]
        cqv = [cq_ref[:, hh * B:(hh + 1) * B] for hh in range(2)]

        def tile(j, carry, masked):
            r0 = pl.multiple_of(j * B, B)
            if masked:
                causal = _key_minus_query(j, qi, TQ) <= 0
            out = []
            for hh in range(2):
                m, lsum, acc = carry[hh]
                k = k_ref[pl.ds(r0, B), _head_lanes(hh)]
                v = v_ref[pl.ds(r0, B), _head_lanes(hh)]
                s = _dot_nt(qs[hh], k) + (cqv[hh] - ck_ref[hh, j][0:1, :])
                if masked:
                    s = jnp.where(causal, s, NEG_BIG)
                m_new = jnp.maximum(m, jnp.max(s, axis=1, keepdims=True))
                p = jnp.exp(s - m_new)
                alpha = jnp.exp(m - m_new)
                out.append((m_new, alpha * lsum + jnp.sum(p, axis=1, keepdims=True),
                            alpha * acc + _dot(p.astype(BF16), v)))
            return tuple(out)

        zero = (jnp.full((TQ, 1), NEG_BIG, F32), jnp.zeros((TQ, 1), F32), jnp.zeros((TQ, HEAD_DIM), F32))
        carry = lax.fori_loop(0, qi * R, lambda j, cr: tile(j, cr, False), (zero, zero))
        carry = lax.fori_loop(0, R, lambda it, cr: tile(qi * R + it, cr, True), carry)
        for hh in range(2):
            m, lsum, acc = carry[hh]
            o_ref[:, _head_lanes(hh)] = acc * (1.0 / lsum)
            lse_ref[:, hh * B:(hh + 1) * B] = jnp.broadcast_to(m + jnp.log(lsum), (TQ, B))

    return pl.pallas_call(
        body, name=name, grid=(NP, S // TQ), in_specs=[q_spec, k_spec, v_spec, stat_spec, ck_spec],
        out_specs=[q_spec, stat_spec],
        out_shape=[jax.ShapeDtypeStruct((S, D), F32), jax.ShapeDtypeStruct((S, 2 * NP * B), F32)],
        compiler_params=_params(),
    )(qkv, qkv, qkv, cq, ck)


def _fox_bwd(qkv, do, o, lse, cq, ck, name):
    S, D3 = qkv.shape
    D = D3 // 3
    B, TQ = ATT_BLOCK, FOX_Q_TILE
    R = TQ // B
    NP = D // LANES
    NQ = S // TQ
    q_spec, k_spec, v_spec, stat_spec, seq_spec, ck_spec = _att_specs(S, D, TQ)

    def body(q_ref, k_ref, v_ref, do_ref, o_ref, lse_ref, cq_ref, ck_ref,
             dq_ref, dk_ref, dv_ref, dcq_ref, dck_ref, dk_acc, dv_acc):
        qi = pl.program_id(1)

        @pl.when(qi == 0)
        def _():
            dk_acc[...] = jnp.zeros_like(dk_acc)
            dv_acc[...] = jnp.zeros_like(dv_acc)
            dck_ref[...] = jnp.zeros_like(dck_ref)

        qs = [_scaled_q(q_ref, hh) for hh in range(2)]
        dof = [do_ref[:, _head_lanes(hh)] for hh in range(2)]
        dob = [d.astype(BF16) for d in dof]
        delta = [jnp.sum(dof[hh] * o_ref[:, _head_lanes(hh)], axis=1, keepdims=True) for hh in range(2)]
        cqv = [cq_ref[:, hh * B:(hh + 1) * B] for hh in range(2)]
        lsev = [lse_ref[:, hh * B:(hh + 1) * B] for hh in range(2)]

        def tile(j, carry, masked):
            r0 = pl.multiple_of(j * B, B)
            if masked:
                causal = _key_minus_query(j, qi, TQ) <= 0
            out = []
            for hh in range(2):
                dq, rowsum = carry[hh]
                lanes = _head_lanes(hh)
                k = k_ref[pl.ds(r0, B), lanes]
                v = v_ref[pl.ds(r0, B), lanes]
                s = _dot_nt(qs[hh], k) + (cqv[hh] - ck_ref[hh, j][0:1, :])
                p = jnp.exp(s - lsev[hh])
                if masked:
                    p = jnp.where(causal, p, 0.0)
                ds = p * (_dot_nt(dob[hh], v) - delta[hh])
                dsb = ds.astype(BF16)
                dk_acc[pl.ds(r0, B), lanes] += _dot_tn(dsb, qs[hh])
                dv_acc[pl.ds(r0, B), lanes] += _dot_tn(p.astype(BF16), dob[hh])
                colsum = jnp.sum(ds, axis=0, keepdims=True)
                dck_ref[hh, j] = dck_ref[hh, j] - jnp.broadcast_to(colsum, (SUBLANES, B))
                out.append((dq + _dot(dsb, k), rowsum + jnp.sum(ds, axis=1, keepdims=True)))
            return tuple(out)

        zero = (jnp.zeros((TQ, HEAD_DIM), F32), jnp.zeros((TQ, 1), F32))
        carry = lax.fori_loop(0, qi * R, lambda j, cr: tile(j, cr, False), (zero, zero))
        carry = lax.fori_loop(0, R, lambda it, cr: tile(qi * R + it, cr, True), carry)
        for hh in range(2):
            dq, rowsum = carry[hh]
            dq_ref[:, _head_lanes(hh)] = (dq * ATT_SCALE).astype(BF16)
            dcq_ref[:, hh * B:(hh + 1) * B] = jnp.broadcast_to(rowsum, (TQ, B))

        @pl.when(qi == NQ - 1)
        def _():
            dk_ref[...] = dk_acc[...].astype(BF16)
            dv_ref[...] = dv_acc[...].astype(BF16)

    out = jax.ShapeDtypeStruct((S, D), BF16)
    return pl.pallas_call(
        body, name=name, grid=(NP, NQ),
        in_specs=[q_spec, k_spec, v_spec, q_spec, q_spec, stat_spec, stat_spec, ck_spec],
        out_specs=[q_spec, seq_spec, seq_spec, stat_spec, ck_spec],
        out_shape=[out, out, out, jax.ShapeDtypeStruct((S, 2 * NP * B), F32),
                   jax.ShapeDtypeStruct((2 * NP, S // B, SUBLANES, B), F32)],
        scratch_shapes=[pltpu.VMEM((S, LANES), F32), pltpu.VMEM((S, LANES), F32)],
        compiler_params=_params(dimension_semantics=("arbitrary", "arbitrary")),
    )(qkv, qkv, qkv, do, o, lse, cq, ck)


def _forget_cumsum(f3, bias, name):
    NQ, NH, B = f3.shape

    def body(f_ref, b_ref, cum_ref):
        row = lax.broadcasted_iota(jnp.int32, (B, B), 0)
        col = lax.broadcasted_iota(jnp.int32, (B, B), 1)
        t_incl = jnp.where(row <= col, 1.0, 0.0).astype(BF16)

        def step(b, carry):
            lf, _ = _log_sigmoids(f_ref[b] + b_ref[...])
            cum = _split_dot(lf, t_incl, 3) + carry
            cum_ref[b] = cum
            return jnp.broadcast_to(cum[:, B - 1:B], (NH, B))

        lax.fori_loop(0, NQ, step, jnp.zeros((NH, B), F32))

    return pl.pallas_call(
        body, name=name, in_specs=[VMEM_WHOLE, VMEM_WHOLE], out_specs=VMEM_WHOLE,
        out_shape=jax.ShapeDtypeStruct((NQ, NH, B), F32), compiler_params=_params(),
    )(f3, bias)


def _forget_cumsum_bwd(dcum3, f3, bias, name):
    NQ, NH, B = f3.shape

    def body(d_ref, f_ref, b_ref, df_ref, tot_ref):
        row = lax.broadcasted_iota(jnp.int32, (B, B), 0)
        col = lax.broadcasted_iota(jnp.int32, (B, B), 1)
        t_rev = jnp.where(row >= col, 1.0, 0.0).astype(BF16)

        def step(it, carry):
            run, tot = carry
            b = NQ - 1 - it
            dlf = _split_dot(d_ref[b], t_rev, 3) + run
            f = f_ref[b] + b_ref[...]
            df = dlf * (1.0 / (1.0 + jnp.exp(f)))
            df_ref[b] = df
            return jnp.broadcast_to(dlf[:, 0:1], (NH, B)), tot + df

        _, tot = lax.fori_loop(0, NQ, step, (jnp.zeros((NH, B), F32), jnp.zeros((NH, B), F32)))
        tot_ref[...] = _split_dot(tot, jnp.ones((B, B), BF16), 3)

    return pl.pallas_call(
        body, name=name, in_specs=[VMEM_WHOLE, VMEM_WHOLE, VMEM_WHOLE], out_specs=[VMEM_WHOLE, VMEM_WHOLE],
        out_shape=[jax.ShapeDtypeStruct((NQ, NH, B), F32), jax.ShapeDtypeStruct((NH, B), F32)],
        compiler_params=_params(),
    )(dcum3, f3, bias)


def _silu(x, name):
    def body(x_ref, o_ref):
        v = x_ref[...]
        o_ref[...] = v * (1.0 / (1.0 + jnp.exp(-v)))

    return pl.pallas_call(body, name=name, in_specs=[VMEM_WHOLE], out_specs=VMEM_WHOLE,
                          out_shape=jax.ShapeDtypeStruct(x.shape, F32), compiler_params=_params())(x)


def _sum_leading(x, name):
    N, R, C = x.shape
    T = _pick(R, (256, 128, 64, 32, 16, 8))

    def body(x_ref, o_ref):
        acc = x_ref[0].astype(F32)
        for r in range(1, N):
            acc = acc + x_ref[r].astype(F32)
        o_ref[...] = acc

    return pl.pallas_call(
        body, name=name, grid=(R // T,), in_specs=[pl.BlockSpec((N, T, C), lambda i: (0, i, 0))],
        out_specs=pl.BlockSpec((T, C), lambda i: (i, 0)), out_shape=jax.ShapeDtypeStruct((R, C), F32),
        compiler_params=_params(),
    )(x)


def _adamw(w, g, m, v, name):
    shape = w.shape
    C = shape[-1]
    R = w.size // C
    T = R
    for cand in (512, 256, 128, 64, 32, 16, 8):
        if R % cand == 0 and cand * C * 4 <= (1 << 20):
            T = cand
            break
    spec = pl.BlockSpec((T, C), lambda i: (i, 0))
    c1 = 1.0 / (1.0 - ADAM_B1 ** ADAM_STEP)
    c2 = 1.0 / (1.0 - ADAM_B2 ** ADAM_STEP)

    def body(w_ref, g_ref, m_ref, v_ref, d_ref, nm_ref, nv_ref):
        gv = g_ref[...]
        nm = ADAM_B1 * m_ref[...] + (1.0 - ADAM_B1) * gv
        nv = ADAM_B2 * v_ref[...] + (1.0 - ADAM_B2) * (gv * gv)
        nm_ref[...] = nm
        nv_ref[...] = nv
        d_ref[...] = -ADAM_LR * ((nm * c1) / (jnp.sqrt(nv * c2) + ADAM_EPS) + ADAM_WD * w_ref[...])

    out = jax.ShapeDtypeStruct((R, C), F32)
    d, nm, nv = pl.pallas_call(
        body, name=name, grid=(R // T,), in_specs=[spec] * 4, out_specs=[spec] * 3, out_shape=[out] * 3,
        compiler_params=_params(),
    )(w.reshape(R, C), g.reshape(R, C), m.reshape(R, C), v.reshape(R, C))
    return d.reshape(shape), nm.reshape(shape), nv.reshape(shape)


def _mesh_pos():
    return lax.axis_index("x"), lax.axis_index("y"), lax.axis_index("c")


def _all_gather_small(x2d, name):
    m_per, n = x2d.shape

    def body(x_ref, out_ref, send_sems, recv_sems, local_sem):
        x, y, c = _mesh_pos()
        me, sibling = (x, y, c), (x, y, 1 - c)
        chips = [(1 - x, y), (x, 1 - y), (1 - x, 1 - y)]

        def rows(px, py, pc):
            return out_ref.at[pl.ds((4 * px + 2 * py + pc) * m_per, m_per), :]

        def copy(k, block, to, src=None):
            return pltpu.make_async_remote_copy(
                src_ref=rows(*block) if src is None else src, dst_ref=rows(*block),
                send_sem=send_sems.at[k], recv_sem=recv_sems.at[k], device_id=to, device_id_type=MESH)

        mine = pltpu.make_async_copy(x_ref, rows(*me), local_sem)
        mine.start()
        first = [copy(0, me, sibling, src=x_ref)]
        first += [copy(1 + j, me, (*chip, c), src=x_ref) for j, chip in enumerate(chips)]
        for cp in first:
            cp.start()
        passed = [copy(4 + j, (*chip, c), sibling) for j, chip in enumerate(chips)]
        for j, chip in enumerate(chips):
            copy(1 + j, (*chip, c), me).wait_recv()
            passed[j].start()
        copy(0, sibling, me).wait_recv()
        for j, chip in enumerate(chips):
            copy(4 + j, (*chip, 1 - c), me).wait_recv()
        for cp in first + passed:
            cp.wait_send()
        mine.wait()

    return pl.pallas_call(
        body, name=name, out_shape=jax.ShapeDtypeStruct((N_DEV * m_per, n), x2d.dtype),
        in_specs=[VMEM_WHOLE], out_specs=VMEM_WHOLE,
        scratch_shapes=[pltpu.SemaphoreType.DMA((7,)), pltpu.SemaphoreType.DMA((7,)), pltpu.SemaphoreType.DMA],
        compiler_params=_params(),
    )(x2d)


def _all_gather_weights(ws, name):
    n = len(ws)

    def body(*refs):
        ins, outs = refs[:n], refs[n:2 * n]
        send_sems, recv_sems, local_sems = refs[2 * n:]
        x, y, c = _mesh_pos()
        k_me = 2 * x + y
        chips = [(1 - x, y), (x, 1 - y), (1 - x, 1 - y)]

        def remote(i, j, slot):
            px, py = chips[j]
            return pltpu.make_async_remote_copy(
                src_ref=ins[i], dst_ref=outs[i].at[slot], send_sem=send_sems.at[i, j],
                recv_sem=recv_sems.at[i, j], device_id=(px, py, c), device_id_type=MESH)

        local = [pltpu.make_async_copy(ins[i], outs[i].at[k_me], local_sems.at[i]) for i in range(n)]
        for i in range(n):
            local[i].start()
            for j in range(3):
                remote(i, j, k_me).start()
        for i in range(n):
            for j, (px, py) in enumerate(chips):
                remote(i, j, 2 * px + py).wait_recv()
        for i in range(n):
            for j in range(3):
                remote(i, j, k_me).wait_send()
            local[i].wait()

    return pl.pallas_call(
        body, name=name,
        out_shape=[jax.ShapeDtypeStruct((N_CHIPS,) + w.shape, w.dtype) for w in ws],
        in_specs=[ANY] * n, out_specs=[ANY] * n,
        scratch_shapes=[pltpu.SemaphoreType.DMA((n, 3)), pltpu.SemaphoreType.DMA((n, 3)),
                        pltpu.SemaphoreType.DMA((n,))],
        compiler_params=_params(),
    )(*ws)


def _scatter_grad_pieces(gs, name):
    n = len(gs)
    halves = [g.shape[1] // 2 for g in gs]

    def body(*refs):
        ins, outs = refs[:n], refs[n:2 * n]
        send_sems, recv_sems, local_sems = refs[2 * n:]
        x, y, c = _mesh_pos()

        def flip(v, bit):
            return 1 - v if bit else v

        def piece(i, px, py, pc):
            return ins[i].at[2 * px + py, pl.ds(pc * halves[i], halves[i])]

        def remote(i, r):
            px, py, pc = flip(x, r & 4), flip(y, r & 2), flip(c, r & 1)
            return pltpu.make_async_remote_copy(
                src_ref=piece(i, px, py, pc), dst_ref=outs[i].at[r], send_sem=send_sems.at[i, r - 1],
                recv_sem=recv_sems.at[i, r - 1], device_id=(px, py, pc), device_id_type=MESH)

        local = [pltpu.make_async_copy(piece(i, x, y, c), outs[i].at[0], local_sems.at[i]) for i in range(n)]
        for i in range(n):
            local[i].start()
            for r in range(1, N_DEV):
                remote(i, r).start()
        for i in range(n):
            for r in range(1, N_DEV):
                remote(i, r).wait_recv()
        for i in range(n):
            for r in range(1, N_DEV):
                remote(i, r).wait_send()
            local[i].wait()

    return pl.pallas_call(
        body, name=name,
        out_shape=[jax.ShapeDtypeStruct((N_DEV, h) + g.shape[2:], g.dtype) for g, h in zip(gs, halves)],
        in_specs=[ANY] * n, out_specs=[ANY] * n,
        scratch_shapes=[pltpu.SemaphoreType.DMA((n, N_DEV - 1)), pltpu.SemaphoreType.DMA((n, N_DEV - 1)),
                        pltpu.SemaphoreType.DMA((n,))],
        compiler_params=_params(),
    )(*gs)


def _swap_halves(hs, name):
    n = len(hs)

    def body(*refs):
        ins, outs = refs[:n], refs[n:2 * n]
        send_sems, recv_sems, local_sems = refs[2 * n:]
        x, y, c = _mesh_pos()

        def remote(i, slot):
            return pltpu.make_async_remote_copy(
                src_ref=ins[i], dst_ref=outs[i].at[slot], send_sem=send_sems.at[i], recv_sem=recv_sems.at[i],
                device_id=(x, y, 1 - c), device_id_type=MESH)

        local = [pltpu.make_async_copy(ins[i], outs[i].at[c], local_sems.at[i]) for i in range(n)]
        for i in range(n):
            local[i].start()
            remote(i, c).start()
        for i in range(n):
            remote(i, 1 - c).wait_recv()
        for i in range(n):
            remote(i, c).wait_send()
            local[i].wait()

    return pl.pallas_call(
        body, name=name,
        out_shape=[jax.ShapeDtypeStruct((2,) + h.shape, h.dtype) for h in hs],
        in_specs=[ANY] * n, out_specs=[ANY] * n,
        scratch_shapes=[pltpu.SemaphoreType.DMA((n,)), pltpu.SemaphoreType.DMA((n,)), pltpu.SemaphoreType.DMA((n,))],
        compiler_params=_params(),
    )(*hs)


def _pad_rows(a, rows):
    return jnp.pad(a, ((0, rows - a.shape[0]), (0, 0)))


def kernel(x, c, w_mod, b_mod, g_mix_pre, g_mix_post, w_qkv, w_o, w_fg, b_fg, g_ffn_pre, g_ffn_post, w_ffn_gate, w_ffn_up, w_conv, b_conv, w_ffn_down, loss_target, m_w_mod, m_b_mod, m_g_mix_pre, m_g_mix_post, m_w_qkv, m_w_o, m_w_fg, m_b_fg, m_g_ffn_pre, m_g_ffn_post, m_w_ffn_gate, m_w_ffn_up, m_w_conv, m_b_conv, m_w_ffn_down, v_w_mod, v_b_mod, v_g_mix_pre, v_g_mix_post, v_w_qkv, v_w_o, v_w_fg, v_b_fg, v_g_ffn_pre, v_g_ffn_post, v_w_ffn_gate, v_w_ffn_up, v_w_conv, v_b_conv, v_w_ffn_down):
    xs = x[0]
    target = loss_target[0]
    S, D = xs.shape
    L = w_mod.shape[0]
    LF = w_fg.shape[0]
    MS = w_mod.shape[2]
    QS = w_qkv.shape[2]
    OS = w_o.shape[1]
    FS = w_ffn_gate.shape[2]
    F = N_CHIPS * FS
    NH = D // HEAD_DIM
    B = ATT_BLOCK
    NQ = S // B
    ax, ay, ac = _mesh_pos()
    k_me = 2 * ax + ay
    b_me = 4 * ax + 2 * ay + ac

    conv_rows = -(-(L * 3 * FS) // D)
    conv_rows = -(-conv_rows // SUBLANES) * SUBLANES
    conv_flat = jnp.pad(w_conv.reshape(-1), (0, conv_rows * D - L * 3 * FS)).reshape(conv_rows, D)
    first = jnp.concatenate([_pad_rows(c, SUBLANES), conv_flat], axis=0)
    first_all = _all_gather_small(first, "ag_cond").reshape(N_DEV, SUBLANES + conv_rows, D)
    c_all = first_all[:, 0, :]
    conv_all = first_all[0::2, SUBLANES:, :].reshape(N_CHIPS, -1)[:, :L * 3 * FS]
    w_conv_full = conv_all.reshape(N_CHIPS, L, 3, FS).transpose(1, 2, 0, 3).reshape(L, 3, F)
    c_act = _silu(c_all, "silu_c")

    mod_part = jnp.concatenate(
        [_mm(c_act, w_mod[l], "nn", F32, "mm_mod", tm=N_DEV, tn=MS, tk=D) for l in range(L)], axis=1)
    mod_all = _all_gather_small(mod_part, "ag_mod").reshape(N_CHIPS, 2, N_DEV, L, MS)[:, 0]
    mod_mine = lax.dynamic_index_in_dim(mod_all, b_me, axis=1, keepdims=False)
    mod = mod_mine.transpose(1, 0, 2).reshape(L, N_MOD * D) + b_mod

    gq, go, gg, gu_, gd, gf = _all_gather_weights(
        [w_qkv.astype(BF16), w_o.astype(BF16), w_ffn_gate.astype(BF16), w_ffn_up.astype(BF16),
         w_ffn_down.astype(BF16), w_fg.astype(BF16)], "ag_weights")
    def cols(gathered, l):
        return gathered[:, l].transpose(1, 0, 2).reshape(D, -1, 1, LANES)

    W_qkv = [gq[:, l].transpose(1, 0, 2).reshape(D, 3 * D) for l in range(L)]
    W_o = [go[:, l].reshape(D, D) for l in range(L)]
    W_gu = [jnp.concatenate([cols(gg, l), cols(gu_, l)], axis=2).reshape(D, 2 * F) for l in range(L)]
    W_d = [gd[:, l].reshape(F, D) for l in range(L)]
    W_fg = [jnp.pad(gf[:, j].reshape(D, NH), ((0, 0), (0, LANES - NH))) for j in range(LF)]

    def vec(a):
        return a.reshape(1, -1)

    saved = []
    xcur = xs
    for l in range(L):
        sh_a, sc_a, gt_a, sh_f, sc_f, gt_f = [vec(mod[l, j * D:(j + 1) * D]) for j in range(N_MOD)]
        is_fox = l % 2 == 1
        jf = l // 2
        h1 = _norm_mod(xcur, vec(g_mix_pre[l]), sc_a, sh_a, "norm_mod")
        qkv = _mm(h1, W_qkv[l], "nn", BF16, "mm_qkv")
        if is_fox:
            flog = _mm(h1, W_fg[jf], "nn", F32, "mm_fg")[:, :NH]
            f3 = flog.reshape(NQ, B, NH).transpose(0, 2, 1)
            bias = b_fg[jf].reshape(NH, 1)
            cum3 = _forget_cumsum(f3, bias, "forget_cumsum")
            cq = jnp.repeat(cum3.transpose(0, 2, 1).reshape(S, NH), B, axis=1)
            ck = jnp.broadcast_to(cum3.transpose(1, 0, 2)[:, :, None, :], (NH, NQ, SUBLANES, B))
            o, stat = _fox_fwd(qkv, cq, ck, "fox_fwd")
            extra = (f3, bias, cq, ck)
        else:
            o, stat, extra = _sb_fwd(qkv, "sb_fwd")
        p = _mm(o, W_o[l], "nn", F32, "mm_o")
        x1 = _post_res(xcur, p, vec(g_mix_post[l]), gt_a, "post_res")
        h2 = _norm_mod(x1, vec(g_ffn_pre[l]), sc_f, sh_f, "norm_mod")
        gu = _mm(h2, W_gu[l], "nn", F32, "mm_gu")
        wc = w_conv_full[l]
        bc = vec(b_conv[l])
        a = _conv_gate(gu, wc, bc, "conv_gate")
        yv = _mm(a, W_d[l], "nn", F32, "mm_down")
        x2 = _post_res(x1, yv, vec(g_ffn_post[l]), gt_f, "post_res")
        saved.append(dict(x0=xcur, h1=h1, qkv=qkv, o=o, stat=stat, extra=extra, p=p, x1=x1, h2=h2, gu=gu, a=a,
                          y=yv, mods=(sh_a, sc_a, gt_a, sh_f, sc_f, gt_f), wc=wc, bc=bc))
        xcur = x2

    g, sq = _loss_head(xcur, target, "loss_head")
    loss_part = 0.5 * jnp.sum(sq) / D
    loss = lax.psum(loss_part, ("x", "y", "c"))

    dW_qkv, dW_o, dW_gu, dW_d = [None] * L, [None] * L, [None] * L, [None] * L
    dW_fg, db_fg = [None] * LF, [None] * LF
    dmod, dg_mix_pre, dg_mix_post, dg_ffn_pre, dg_ffn_post = [[None] * L for _ in range(5)]
    dw_conv, db_conv = [None] * L, [None] * L
    for l in reversed(range(L)):
        sv = saved[l]
        sh_a, sc_a, gt_a, sh_f, sc_f, gt_f = sv["mods"]
        is_fox = l % 2 == 1
        jf = l // 2
        dy, dgp_f, dgt_f = _post_res_bwd(sv["y"], vec(g_ffn_post[l]), gt_f, g, "post_res_bwd")
        da = _mm(dy, W_d[l], "nt", F32, "mm_da")
        dW_d[l] = _mm(sv["a"], dy, "tn", F32, "mm_dwd")
        dgu, dwc, dbc = _conv_gate_bwd(sv["gu"], sv["wc"], sv["bc"], da, "conv_gate_bwd")
        dh2 = _mm(dgu, W_gu[l], "nt", F32, "mm_dh2")
        dW_gu[l] = _mm(sv["h2"], dgu, "tn", F32, "mm_dwgu")
        g, dg_f, dsc_f, dsh_f = _norm_mod_bwd(sv["x1"], vec(g_ffn_pre[l]), sc_f, sh_f, [dh2], g, "norm_mod_bwd")
        dp, dgp_a, dgt_a = _post_res_bwd(sv["p"], vec(g_mix_post[l]), gt_a, g, "post_res_bwd")
        do = _mm(dp, W_o[l], "nt", F32, "mm_do")
        dW_o[l] = _mm(sv["o"], dp, "tn", F32, "mm_dwo")
        if is_fox:
            f3, bias, cq, ck = sv["extra"]
            dq, dk, dv, dcq, dck = _fox_bwd(sv["qkv"], do, sv["o"], sv["stat"], cq, ck, "fox_bwd")
            dcum_q = dcq.reshape(NQ, B, NH, B)[:, :, :, 0].transpose(0, 2, 1)
            dcum3 = dcum_q + dck[:, :, 0, :].transpose(1, 0, 2)
            df3, df_tot = _forget_cumsum_bwd(dcum3, f3, bias, "forget_cumsum_bwd")
            df = df3.transpose(0, 2, 1).reshape(S, NH)
            dfp = jnp.pad(df, ((0, 0), (0, LANES - NH)))
            dW_fg[jf] = _mm(sv["h1"], dfp, "tn", F32, "mm_dwfg")[:, :NH]
            db_fg[jf] = df_tot[:, 0]
            dh_extra = [_mm(dfp, W_fg[jf], "nt", F32, "mm_dh1f")]
        else:
            dq, dk, dv = _sb_bwd(sv["qkv"], do, sv["stat"], sv["extra"], "sb_bwd")
            dh_extra = []
        dqkv = jnp.concatenate([dq, dk, dv], axis=1)
        dh1 = _mm(dqkv, W_qkv[l], "nt", F32, "mm_dh1")
        dW_qkv[l] = _mm(sv["h1"], dqkv, "tn", F32, "mm_dwqkv")
        g, dg_a, dsc_a, dsh_a = _norm_mod_bwd(sv["x0"], vec(g_mix_pre[l]), sc_a, sh_a, [dh1] + dh_extra, g,
                                              "norm_mod_bwd")
        dmod[l] = jnp.concatenate([dsh_a, dsc_a, dgt_a, dsh_f, dsc_f, dgt_f], axis=1)[0]
        dg_mix_pre[l], dg_mix_post[l], dg_ffn_pre[l], dg_ffn_post[l] = dg_a[0], dgp_a[0], dg_f[0], dgp_f[0]
        dw_conv[l], db_conv[l] = dwc, dbc[0]
    grad_x = g[None]

    pieces = [jnp.stack(dmod), jnp.stack(dg_mix_pre), jnp.stack(dg_mix_post), jnp.stack(dg_ffn_pre),
              jnp.stack(dg_ffn_post), jnp.stack(db_fg), jnp.stack(dW_fg), jnp.stack(dw_conv), jnp.stack(db_conv)]
    sizes = [pc.size for pc in pieces]
    total = sum(sizes)
    pack_rows = -(-total // (LANES * SUBLANES)) * SUBLANES
    pack = jnp.pad(jnp.concatenate([pc.reshape(-1) for pc in pieces]), (0, pack_rows * LANES - total))
    pack_all = _all_gather_small(pack.reshape(pack_rows, LANES), "ag_small_grads").reshape(N_DEV, pack_rows, LANES)
    small = _sum_leading(pack_all, "sum_small_grads").reshape(-1)
    offs = [0]
    for sz in sizes:
        offs.append(offs[-1] + sz)
    parts = [small[offs[i]:offs[i + 1]].reshape(pieces[i].shape) for i in range(len(pieces))]
    g_b_mod, g_g_mix_pre, g_g_mix_post, g_g_ffn_pre, g_g_ffn_post, g_b_fg, g_w_fg_full, g_w_conv_full, g_b_conv = parts
    g_w_fg = lax.dynamic_slice_in_dim(g_w_fg_full, k_me * OS, OS, axis=1)
    g_w_conv = lax.dynamic_slice_in_dim(g_w_conv_full, k_me * FS, FS, axis=2)
    dmod_all = pack_all.reshape(N_DEV, -1)[:, :L * N_MOD * D].reshape(N_DEV, L, N_CHIPS, MS)
    dmod_cols = lax.dynamic_index_in_dim(dmod_all, k_me, axis=2, keepdims=False).reshape(N_DEV, L * MS)
    g_w_mod = _mm(_pad_rows(c_act, LANES), _pad_rows(dmod_cols, LANES), "tn", F32, "mm_dwmod", tm=D, tn=MS, tk=LANES)
    g_w_mod = g_w_mod.reshape(D, L, MS).transpose(1, 0, 2)

    def shard_cols(dws, width):
        st = jnp.stack(dws)
        return st.reshape(L, st.shape[1], N_CHIPS, width).transpose(2, 0, 1, 3).astype(BF16)

    def shard_rows(dws, height):
        st = jnp.stack(dws)
        return st.reshape(L, N_CHIPS, height, st.shape[2]).transpose(1, 0, 2, 3).astype(BF16)

    dgu_st = jnp.stack(dW_gu).reshape(L, D, F // LANES, 2, LANES)
    dWg = [dgu_st[l, :, :, 0, :].reshape(D, F) for l in range(L)]
    dWu = [dgu_st[l, :, :, 1, :].reshape(D, F) for l in range(L)]
    gs = [shard_cols(dW_qkv, QS), shard_rows(dW_o, OS), shard_cols(dWg, FS), shard_cols(dWu, FS),
          shard_rows(dW_d, FS)]
    recv = _scatter_grad_pieces(gs, "rs_scatter")
    halves = []
    for i, r in enumerate(recv):
        cols = r.shape[-1]
        halves.append(_sum_leading(r.reshape(N_DEV, -1, cols), f"sum_grad_pieces_{i}").reshape(r.shape[1:]))
    full = _swap_halves(halves, "rs_swap")
    g_w_qkv, g_w_o, g_w_gate, g_w_up, g_w_down = [f.reshape((L,) + f.shape[2:]) for f in full]

    grads = [g_w_mod, g_b_mod, g_g_mix_pre, g_g_mix_post, g_w_qkv, g_w_o, g_w_fg, g_b_fg, g_g_ffn_pre,
             g_g_ffn_post, g_w_gate, g_w_up, g_w_conv, g_b_conv, g_w_down]
    weights = [w_mod, b_mod, g_mix_pre, g_mix_post, w_qkv, w_o, w_fg, b_fg, g_ffn_pre, g_ffn_post, w_ffn_gate,
               w_ffn_up, w_conv, b_conv, w_ffn_down]
    ms = [m_w_mod, m_b_mod, m_g_mix_pre, m_g_mix_post, m_w_qkv, m_w_o, m_w_fg, m_b_fg, m_g_ffn_pre, m_g_ffn_post,
          m_w_ffn_gate, m_w_ffn_up, m_w_conv, m_b_conv, m_w_ffn_down]
    vs = [v_w_mod, v_b_mod, v_g_mix_pre, v_g_mix_post, v_w_qkv, v_w_o, v_w_fg, v_b_fg, v_g_ffn_pre, v_g_ffn_post,
          v_w_ffn_gate, v_w_ffn_up, v_w_conv, v_b_conv, v_w_ffn_down]
    deltas, new_ms, new_vs = [], [], []
    for wv, gv, mv, vv in zip(weights, grads, ms, vs):
        d, nm, nv = _adamw(wv, gv, mv, vv, "adamw")
        deltas.append(d)
        new_ms.append(nm)
        new_vs.append(nv)
    return (loss, grad_x, *grads, *deltas, *new_ms, *new_vs)
```

```python
import functools

import jax
import jax.numpy as jnp
from jax import lax
from jax.experimental import pallas as pl
from jax.experimental.pallas import tpu as pltpu

F32 = jnp.float32
BF16 = jnp.bfloat16
MESH = pl.DeviceIdType.MESH

HEAD_DIM = 64
ATT_BLOCK = 128
SB_Q_TILE = 512
FOX_Q_TILE = 1024
SWAP_CHUNKS = 4
SB_DEAD_LOG = -110.0
LANES = 128
SUBLANES = 8
RMS_EPS = 1e-6
N_MOD = 6
N_CHIPS = 4
N_DEV = 8
ADAM_LR = 0.001
ADAM_B1 = 0.9
ADAM_B2 = 0.999
ADAM_EPS = 1e-08
ADAM_WD = 0.01
ADAM_STEP = 10
VMEM_LIMIT_BYTES = 56 * 1024 * 1024
NEG_BIG = -1e30

ANY = pl.BlockSpec(memory_space=pl.ANY)
VMEM_WHOLE = pl.BlockSpec(memory_space=pltpu.VMEM)


def _params(**kw):
    return pltpu.CompilerParams(vmem_limit_bytes=VMEM_LIMIT_BYTES, **kw)


def _dot(a, b):
    return jnp.dot(a, b, preferred_element_type=F32)


def _dot_nt(a, b):
    return lax.dot_general(a, b, (((1,), (1,)), ((), ())), preferred_element_type=F32)


def _dot_tn(a, b):
    return lax.dot_general(a, b, (((0,), (0,)), ((), ())), preferred_element_type=F32)


def _split_dot(x, t, parts):
    acc = None
    rem = x
    for _ in range(parts):
        piece = rem.astype(BF16)
        rem = rem - piece.astype(F32)
        d = _dot(piece, t)
        acc = d if acc is None else acc + d
    return acc


def _pick(n, prefs):
    for p in prefs:
        if n % p == 0:
            return p
    return n


def _mm(a, b, dims, out_dtype, name, tm=None, tn=None, tk=None):
    if dims == "tn":
        K, M = a.shape
    else:
        M, K = a.shape
    N = b.shape[0] if dims == "nt" else b.shape[1]
    tm = tm or _pick(M, (512, 256, 128))
    tn = tn or _pick(N, (1536, 1408, 1024, 768, 512, 256, 128))
    tk = tk or _pick(K, (1024, 1408, 512, 256, 128))
    nk = K // tk
    grid = (M // tm, N // tn, nk)
    if dims == "tn":
        a_spec = pl.BlockSpec((tk, tm), lambda i, j, k: (k, i))
    else:
        a_spec = pl.BlockSpec((tm, tk), lambda i, j, k: (i, k))
    if dims == "nt":
        b_spec = pl.BlockSpec((tn, tk), lambda i, j, k: (j, k))
    else:
        b_spec = pl.BlockSpec((tk, tn), lambda i, j, k: (k, j))
    o_spec = pl.BlockSpec((tm, tn), lambda i, j, k: (i, j))

    def body(a_ref, b_ref, o_ref, *scratch):
        x = a_ref[...].astype(BF16)
        y = b_ref[...].astype(BF16)
        if dims == "nn":
            r = _dot(x, y)
        elif dims == "nt":
            r = _dot_nt(x, y)
        else:
            r = _dot_tn(x, y)
        if nk == 1:
            o_ref[...] = r.astype(out_dtype)
        else:
            acc = scratch[0]
            k = pl.program_id(2)

            @pl.when(k == 0)
            def _():
                acc[...] = r

            @pl.when(k > 0)
            def _():
                acc[...] += r

            @pl.when(k == nk - 1)
            def _():
                o_ref[...] = acc[...].astype(out_dtype)

    return pl.pallas_call(
        body,
        name=name,
        grid=grid,
        in_specs=[a_spec, b_spec],
        out_specs=o_spec,
        out_shape=jax.ShapeDtypeStruct((M, N), out_dtype),
        scratch_shapes=[pltpu.VMEM((tm, tn), F32)] if nk > 1 else [],
        compiler_params=_params(dimension_semantics=("parallel", "parallel", "arbitrary")),
    )(a, b)


def _row_tile(S):
    return _pick(S, (256, 128, 64, 32, 16, 8))


def _norm_mod(x, g, sc, sh, name):
    S, D = x.shape
    T = _row_tile(S)
    row = pl.BlockSpec((T, D), lambda i: (i, 0))
    vec = pl.BlockSpec((1, D), lambda i: (0, 0))

    def body(x_ref, g_ref, sc_ref, sh_ref, h_ref):
        xv = x_ref[...]
        r = lax.rsqrt(jnp.mean(xv * xv, axis=-1, keepdims=True) + RMS_EPS)
        n = (xv * r) * g_ref[...]
        h_ref[...] = (n * (1.0 + sc_ref[...]) + sh_ref[...]).astype(BF16)

    return pl.pallas_call(
        body, name=name, grid=(S // T,), in_specs=[row, vec, vec, vec], out_specs=row,
        out_shape=jax.ShapeDtypeStruct((S, D), BF16), compiler_params=_params(),
    )(x, g, sc, sh)


def _norm_mod_bwd(x, g, sc, sh, dhs, gres, name):
    S, D = x.shape
    T = _row_tile(S)
    n_dh = len(dhs)
    row = pl.BlockSpec((T, D), lambda i: (i, 0))
    vec = pl.BlockSpec((1, D), lambda i: (0, 0))

    def body(x_ref, g_ref, sc_ref, sh_ref, *refs):
        dh_refs = refs[:n_dh]
        gres_ref, dx_ref, dg_ref, dsc_ref, dsh_ref = refs[n_dh:]
        xv = x_ref[...]
        r = lax.rsqrt(jnp.mean(xv * xv, axis=-1, keepdims=True) + RMS_EPS)
        xn = xv * r
        n = xn * g_ref[...]
        dh = dh_refs[0][...]
        for extra in dh_refs[1:]:
            dh = dh + extra[...]
        dn = dh * (1.0 + sc_ref[...])
        dxn = dn * g_ref[...]
        dx = r * (dxn - xn * jnp.mean(dxn * xn, axis=-1, keepdims=True))
        dx_ref[...] = gres_ref[...] + dx

        @pl.when(pl.program_id(0) == 0)
        def _():
            dg_ref[...] = jnp.zeros_like(dg_ref)
            dsc_ref[...] = jnp.zeros_like(dsc_ref)
            dsh_ref[...] = jnp.zeros_like(dsh_ref)

        dg_ref[...] += jnp.sum(dn * xn, axis=0, keepdims=True)
        dsc_ref[...] += jnp.sum(dh * n, axis=0, keepdims=True)
        dsh_ref[...] += jnp.sum(dh, axis=0, keepdims=True)

    vshape = jax.ShapeDtypeStruct((1, D), F32)
    return pl.pallas_call(
        body, name=name, grid=(S // T,), in_specs=[row, vec, vec, vec] + [row] * (n_dh + 1),
        out_specs=[row, vec, vec, vec],
        out_shape=[jax.ShapeDtypeStruct((S, D), F32), vshape, vshape, vshape],
        compiler_params=_params(dimension_semantics=("arbitrary",)),
    )(x, g, sc, sh, *dhs, gres)


def _post_res(x, p, gp, gt, name):
    S, D = x.shape
    T = _row_tile(S)
    row = pl.BlockSpec((T, D), lambda i: (i, 0))
    vec = pl.BlockSpec((1, D), lambda i: (0, 0))

    def body(x_ref, p_ref, gp_ref, gt_ref, o_ref):
        pv = p_ref[...]
        r = lax.rsqrt(jnp.mean(pv * pv, axis=-1, keepdims=True) + RMS_EPS)
        o_ref[...] = x_ref[...] + gt_ref[...] * ((pv * r) * gp_ref[...])

    return pl.pallas_call(
        body, name=name, grid=(S // T,), in_specs=[row, row, vec, vec], out_specs=row,
        out_shape=jax.ShapeDtypeStruct((S, D), F32), compiler_params=_params(),
    )(x, p, gp, gt)


def _post_res_bwd(p, gp, gt, g, name):
    S, D = p.shape
    T = _row_tile(S)
    row = pl.BlockSpec((T, D), lambda i: (i, 0))
    vec = pl.BlockSpec((1, D), lambda i: (0, 0))

    def body(p_ref, gp_ref, gt_ref, g_ref, dp_ref, dgp_ref, dgt_ref):
        pv = p_ref[...]
        gv = g_ref[...]
        r = lax.rsqrt(jnp.mean(pv * pv, axis=-1, keepdims=True) + RMS_EPS)
        pn = pv * r
        n2 = pn * gp_ref[...]
        dn2 = gv * gt_ref[...]
        dpn = dn2 * gp_ref[...]
        dp = r * (dpn - pn * jnp.mean(dpn * pn, axis=-1, keepdims=True))
        dp_ref[...] = dp.astype(BF16)

        @pl.when(pl.program_id(0) == 0)
        def _():
            dgp_ref[...] = jnp.zeros_like(dgp_ref)
            dgt_ref[...] = jnp.zeros_like(dgt_ref)

        dgp_ref[...] += jnp.sum(dn2 * pn, axis=0, keepdims=True)
        dgt_ref[...] += jnp.sum(gv * n2, axis=0, keepdims=True)

    vshape = jax.ShapeDtypeStruct((1, D), F32)
    return pl.pallas_call(
        body, name=name, grid=(S // T,), in_specs=[row, vec, vec, row], out_specs=[row, vec, vec],
        out_shape=[jax.ShapeDtypeStruct((S, D), BF16), vshape, vshape],
        compiler_params=_params(dimension_semantics=("arbitrary",)),
    )(p, gp, gt, g)


def _loss_head(y, target, name):
    S, D = y.shape
    T = _row_tile(S)
    row = pl.BlockSpec((T, D), lambda i: (i, 0))
    vec = pl.BlockSpec((1, D), lambda i: (0, 0))

    def body(y_ref, t_ref, dy_ref, sq_ref):
        e = y_ref[...] - t_ref[...]
        dy_ref[...] = e * (1.0 / D)

        @pl.when(pl.program_id(0) == 0)
        def _():
            sq_ref[...] = jnp.zeros_like(sq_ref)

        sq_ref[...] += jnp.sum(e * e, axis=0, keepdims=True)

    return pl.pallas_call(
        body, name=name, grid=(S // T,), in_specs=[row, row], out_specs=[row, vec],
        out_shape=[jax.ShapeDtypeStruct((S, D), F32), jax.ShapeDtypeStruct((1, D), F32)],
        compiler_params=_params(dimension_semantics=("arbitrary",)),
    )(y, target)


def _shift_down(v, k, rows):
    return jnp.where(rows >= k, pltpu.roll(v, k, 0), 0.0)


def _shift_up(v, k, rows, S):
    return jnp.where(rows < S - k, pltpu.roll(v, S - k, 0), 0.0)


def _conv_gate(gu, wc, bc, name):
    S, F2 = gu.shape
    F = F2 // 2
    C = LANES

    def body(gu_ref, w_ref, b_ref, a_ref):
        rows = lax.broadcasted_iota(jnp.int32, (S, C), 0)
        gate = gu_ref[:, :C]
        up = gu_ref[:, C:]
        w = w_ref[...]
        gc = w[2:3] * gate + w[1:2] * _shift_down(gate, 1, rows) + w[0:1] * _shift_down(gate, 2, rows) + b_ref[...]
        a_ref[...] = (gc * (1.0 / (1.0 + jnp.exp(-gc))) * up).astype(BF16)

    return pl.pallas_call(
        body, name=name, grid=(F // C,),
        in_specs=[pl.BlockSpec((S, 2 * C), lambda j: (0, j)), pl.BlockSpec((3, C), lambda j: (0, j)),
                  pl.BlockSpec((1, C), lambda j: (0, j))],
        out_specs=pl.BlockSpec((S, C), lambda j: (0, j)),
        out_shape=jax.ShapeDtypeStruct((S, F), BF16), compiler_params=_params(),
    )(gu, wc, bc)


def _conv_gate_bwd(gu, wc, bc, da, name):
    S, F2 = gu.shape
    F = F2 // 2
    C = LANES

    def body(gu_ref, w_ref, b_ref, da_ref, dgu_ref, dw_ref, db_ref):
        rows = lax.broadcasted_iota(jnp.int32, (S, C), 0)
        gate = gu_ref[:, :C]
        up = gu_ref[:, C:]
        dav = da_ref[...]
        w = w_ref[...]
        g1 = _shift_down(gate, 1, rows)
        g2 = _shift_down(gate, 2, rows)
        gc = w[2:3] * gate + w[1:2] * g1 + w[0:1] * g2 + b_ref[...]
        sg = 1.0 / (1.0 + jnp.exp(-gc))
        dgu_ref[:, C:] = (dav * (gc * sg)).astype(BF16)
        dgc = dav * up * (sg * (1.0 + gc * (1.0 - sg)))
        db_ref[...] = jnp.sum(dgc, axis=0, keepdims=True)
        dw_ref[0:1, :] = jnp.sum(dgc * g2, axis=0, keepdims=True)
        dw_ref[1:2, :] = jnp.sum(dgc * g1, axis=0, keepdims=True)
        dw_ref[2:3, :] = jnp.sum(dgc * gate, axis=0, keepdims=True)
        dgate = w[2:3] * dgc + w[1:2] * _shift_up(dgc, 1, rows, S) + w[0:1] * _shift_up(dgc, 2, rows, S)
        dgu_ref[:, :C] = dgate.astype(BF16)

    return pl.pallas_call(
        body, name=name, grid=(F // C,),
        in_specs=[pl.BlockSpec((S, 2 * C), lambda j: (0, j)), pl.BlockSpec((3, C), lambda j: (0, j)),
                  pl.BlockSpec((1, C), lambda j: (0, j)), pl.BlockSpec((S, C), lambda j: (0, j))],
        out_specs=[pl.BlockSpec((S, 2 * C), lambda j: (0, j)), pl.BlockSpec((3, C), lambda j: (0, j)),
                   pl.BlockSpec((1, C), lambda j: (0, j))],
        out_shape=[jax.ShapeDtypeStruct((S, F2), BF16), jax.ShapeDtypeStruct((3, F), F32),
                   jax.ShapeDtypeStruct((1, F), F32)],
        compiler_params=_params(),
    )(gu, wc, bc, da)


ATT_SCALE = HEAD_DIM ** -0.5


def _att_specs(S, D, TQ):
    nb = D // LANES
    q_spec = pl.BlockSpec((TQ, LANES), lambda p, i: (i, p))
    k_spec = pl.BlockSpec((S, LANES), lambda p, i: (0, nb + p))
    v_spec = pl.BlockSpec((S, LANES), lambda p, i: (0, 2 * nb + p))
    stat_spec = pl.BlockSpec((TQ, 2 * ATT_BLOCK), lambda p, i: (i, p))
    seq_spec = pl.BlockSpec((S, LANES), lambda p, i: (0, p))
    ck_spec = pl.BlockSpec((2, S // ATT_BLOCK, SUBLANES, ATT_BLOCK), lambda p, i: (p, 0, 0, 0))
    return q_spec, k_spec, v_spec, stat_spec, seq_spec, ck_spec


def _head_lanes(hh):
    return slice(hh * HEAD_DIM, (hh + 1) * HEAD_DIM)


def _scaled_q(q_ref, hh):
    return (q_ref[:, _head_lanes(hh)].astype(F32) * ATT_SCALE).astype(BF16)


def _tri(cmp):
    B = ATT_BLOCK
    row = lax.broadcasted_iota(jnp.int32, (B, B), 0)
    col = lax.broadcasted_iota(jnp.int32, (B, B), 1)
    half = jnp.concatenate([jnp.where(cmp(row, col), 1.0, 0.0).astype(BF16), jnp.ones((B, B), BF16)], axis=1)
    return jnp.concatenate([half, half], axis=0)


def _hi_lo_dot(x, t):
    hi = x.astype(BF16)
    lo = (x - hi.astype(F32)).astype(BF16)
    return _dot(jnp.concatenate([hi, lo], axis=1), t)


def _key_minus_query(j, qi, TQ):
    row = lax.broadcasted_iota(jnp.int32, (TQ, ATT_BLOCK), 0)
    col = lax.broadcasted_iota(jnp.int32, (TQ, ATT_BLOCK), 1)
    return col - row + (j * ATT_BLOCK - qi * TQ)


def _log_sigmoids(z):
    sp = jnp.log(1.0 + jnp.exp(-jnp.abs(z)))
    lb = jnp.minimum(z, 0.0) - sp
    return lb, lb - z


def _log_sigmoids_fast(z):
    zc = jnp.maximum(z, -80.0)
    lb = -jnp.log(1.0 + jnp.exp(-zc))
    return lb, lb - zc


def _sb_fwd(qkv, name):
    S, D3 = qkv.shape
    D = D3 // 3
    B, TQ = ATT_BLOCK, SB_Q_TILE
    R = TQ // B
    NP = D // LANES
    q_spec, k_spec, v_spec, stat_spec, _, _ = _att_specs(S, D, TQ)

    def body(q_ref, k_ref, v_ref, o_ref, lt_ref, first_ref):
        qi = pl.program_id(1)
        t_suffix = _tri(lambda r, c: r > c)
        qs = [_scaled_q(q_ref, hh) for hh in range(2)]

        def tile(j, carry, masked):
            r0 = pl.multiple_of(j * B, B)
            if masked:
                strict = _key_minus_query(j, qi, TQ) < 0
            out = []
            for hh in range(2):
                c, acc = carry[hh]
                k = k_ref[pl.ds(r0, B), _head_lanes(hh)]
                v = v_ref[pl.ds(r0, B), _head_lanes(hh)]
                lb, l1 = _log_sigmoids_fast(_dot_nt(qs[hh], k))
                if masked:
                    l1 = jnp.where(strict, l1, 0.0)
                sums = _hi_lo_dot(l1, t_suffix)
                a = jnp.exp(lb + c + sums[:, :B])
                if masked:
                    a = jnp.where(strict, a, 0.0)
                out.append((c + sums[:, B:], acc + _dot(a.astype(BF16), v)))
            return tuple(out)

        zero = (jnp.zeros((TQ, B), F32), jnp.zeros((TQ, HEAD_DIM), F32))
        last = qi * R + R - 1
        carry = lax.fori_loop(0, R, lambda it, cr: tile(last - it, cr, True), (zero, zero))

        def alive(cr):
            return jnp.max(jnp.maximum(cr[0][0], cr[1][0])) > SB_DEAD_LOG

        def walk(state):
            j, cr, _ = state
            cr = tile(j, cr, False)
            return j - 1, cr, alive(cr)

        j, carry, _ = lax.while_loop(lambda st: jnp.logical_and(st[0] >= 0, st[2]), walk,
                                     (qi * R - 1, carry, alive(carry)))
        first_ref[pl.program_id(0), qi] = j + 1
        for hh in range(2):
            c, acc = carry[hh]
            o_ref[:, _head_lanes(hh)] = acc
            lt_ref[:, hh * B:(hh + 1) * B] = c

    return pl.pallas_call(
        body, name=name, grid=(NP, S // TQ), in_specs=[q_spec, k_spec, v_spec],
        out_specs=[q_spec, stat_spec, pl.BlockSpec(memory_space=pltpu.SMEM)],
        out_shape=[jax.ShapeDtypeStruct((S, D), F32), jax.ShapeDtypeStruct((S, 2 * NP * B), F32),
                   jax.ShapeDtypeStruct((NP, S // TQ), jnp.int32)],
        compiler_params=_params(dimension_semantics=("arbitrary", "arbitrary")),
    )(qkv, qkv, qkv)


def _sb_bwd(qkv, do, lt, first, name):
    S, D3 = qkv.shape
    D = D3 // 3
    B, TQ = ATT_BLOCK, SB_Q_TILE
    R = TQ // B
    NP = D // LANES
    NQ = S // TQ
    q_spec, k_spec, v_spec, stat_spec, seq_spec, _ = _att_specs(S, D, TQ)

    def body(first_ref, q_ref, k_ref, v_ref, do_ref, lt_ref, dq_ref, dk_ref, dv_ref, dk_acc, dv_acc):
        qi = pl.program_id(1)
        t_prefix = _tri(lambda r, c: r <= c)
        t_before = _tri(lambda r, c: r < c)

        @pl.when(qi == 0)
        def _():
            dk_acc[...] = jnp.zeros_like(dk_acc)
            dv_acc[...] = jnp.zeros_like(dv_acc)

        qs = [_scaled_q(q_ref, hh) for hh in range(2)]
        dob = [do_ref[:, _head_lanes(hh)].astype(BF16) for hh in range(2)]
        ltot = [lt_ref[:, hh * B:(hh + 1) * B] for hh in range(2)]

        def tile(j, carry, masked):
            r0 = pl.multiple_of(j * B, B)
            if masked:
                strict = _key_minus_query(j, qi, TQ) < 0
            out = []
            for hh in range(2):
                pre, cu, dq = carry[hh]
                lanes = _head_lanes(hh)
                k = k_ref[pl.ds(r0, B), lanes]
                v = v_ref[pl.ds(r0, B), lanes]
                lb, l1 = _log_sigmoids_fast(_dot_nt(qs[hh], k))
                if masked:
                    l1 = jnp.where(strict, l1, 0.0)
                sums = _hi_lo_dot(l1, t_prefix)
                a = jnp.exp(lb + (ltot[hh] - pre - sums[:, :B]))
                if masked:
                    a = jnp.where(strict, a, 0.0)
                u = a * _dot_nt(dob[hh], v)
                usums = _hi_lo_dot(u, t_before)
                dz = u - (u + cu + usums[:, :B]) * jnp.exp(lb)
                if masked:
                    dz = jnp.where(strict, dz, 0.0)
                dzb = dz.astype(BF16)
                dk_acc[pl.ds(r0, B), lanes] += _dot_tn(dzb, qs[hh])
                dv_acc[pl.ds(r0, B), lanes] += _dot_tn(a.astype(BF16), dob[hh])
                out.append((pre + sums[:, B:], cu + usums[:, B:], dq + _dot(dzb, k)))
            return tuple(out)

        zero = (jnp.zeros((TQ, B), F32), jnp.zeros((TQ, B), F32), jnp.zeros((TQ, HEAD_DIM), F32))
        carry = lax.fori_loop(first_ref[pl.program_id(0), qi], qi * R, lambda j, cr: tile(j, cr, False),
                              (zero, zero))
        carry = lax.fori_loop(0, R, lambda it, cr: tile(qi * R + it, cr, True), carry)
        for hh in range(2):
            dq_ref[:, _head_lanes(hh)] = (carry[hh][2] * ATT_SCALE).astype(BF16)

        @pl.when(qi == NQ - 1)
        def _():
            dk_ref[...] = dk_acc[...].astype(BF16)
            dv_ref[...] = dv_acc[...].astype(BF16)

    out = jax.ShapeDtypeStruct((S, D), BF16)
    return pl.pallas_call(
        body, name=name, grid=(NP, NQ),
        in_specs=[pl.BlockSpec(memory_space=pltpu.SMEM), q_spec, k_spec, v_spec, q_spec, stat_spec],
        out_specs=[q_spec, seq_spec, seq_spec], out_shape=[out, out, out],
        scratch_shapes=[pltpu.VMEM((S, LANES), F32), pltpu.VMEM((S, LANES), F32)],
        compiler_params=_params(dimension_semantics=("arbitrary", "arbitrary")),
    )(first, qkv, qkv, qkv, do, lt)


def _fox_specs(S, D, TQ):
    row_spec = pl.BlockSpec((2 * SUBLANES, TQ), lambda p, i: (p, i))
    keyb_spec = pl.BlockSpec((S, 2 * ATT_BLOCK), lambda p, i: (0, p))
    return row_spec, keyb_spec


def _key_gt_query_t(j, qi, TQ):
    key = lax.broadcasted_iota(jnp.int32, (ATT_BLOCK, TQ), 0)
    qry = lax.broadcasted_iota(jnp.int32, (ATT_BLOCK, TQ), 1)
    return key - qry + (j * ATT_BLOCK - qi * TQ) > 0


def _fox_fwd(qkv, cq_rows, ck_b, name):
    S, D3 = qkv.shape
    D = D3 // 3
    B, TQ = ATT_BLOCK, FOX_Q_TILE
    R = TQ // B
    NP = D // LANES
    q_spec, k_spec, v_spec, _, _, _ = _att_specs(S, D, TQ)
    row_spec, keyb_spec = _fox_specs(S, D, TQ)

    def body(q_ref, k_ref, v_ref, cq_ref, ck_ref, o_ref, lse_ref):
        qi = pl.program_id(1)
        qs = [_scaled_q(q_ref, hh) for hh in range(2)]
        cq = [cq_ref[hh * SUBLANES:hh * SUBLANES + 1, :] for hh in range(2)]

        def tile(j, carry, masked):
            r0 = pl.multiple_of(j * B, B)
            if masked:
                hidden = _key_gt_query_t(j, qi, TQ)
            out = []
            for hh in range(2):
                m, lsum, acc = carry[hh]
                k = k_ref[pl.ds(r0, B), _head_lanes(hh)]
                v = v_ref[pl.ds(r0, B), _head_lanes(hh)]
                ck = jnp.tile(ck_ref[pl.ds(r0, B), hh * B:(hh + 1) * B], (1, R))
                s = _dot_nt(k, qs[hh]) + (cq[hh] - ck)
                if masked:
                    s = jnp.where(hidden, NEG_BIG, s)
                m_new = jnp.maximum(m, jnp.max(s, axis=0, keepdims=True))
                p = jnp.exp(s - m_new)
                alpha = jnp.exp(m - m_new)
                out.append((m_new, alpha * lsum + jnp.sum(p, axis=0, keepdims=True),
                            alpha * acc + _dot_tn(v, p.astype(BF16))))
            return tuple(out)

        zero = (jnp.full((1, TQ), NEG_BIG, F32), jnp.zeros((1, TQ), F32), jnp.zeros((HEAD_DIM, TQ), F32))
        carry = lax.fori_loop(0, qi * R, lambda j, cr: tile(j, cr, False), (zero, zero))
        carry = lax.fori_loop(0, R, lambda it, cr: tile(qi * R + it, cr, True), carry)
        o_t = jnp.concatenate([carry[hh][2] * (1.0 / carry[hh][1]) for hh in range(2)], axis=0)
        o_ref[...] = o_t.T
        for hh in range(2):
            m, lsum, _ = carry[hh]
            lse_ref[hh * SUBLANES:(hh + 1) * SUBLANES, :] = jnp.broadcast_to(m + jnp.log(lsum), (SUBLANES, TQ))

    return pl.pallas_call(
        body, name=name, grid=(NP, S // TQ), in_specs=[q_spec, k_spec, v_spec, row_spec, keyb_spec],
        out_specs=[q_spec, row_spec],
        out_shape=[jax.ShapeDtypeStruct((S, D), F32), jax.ShapeDtypeStruct((2 * NP * SUBLANES, S), F32)],
        compiler_params=_params(),
    )(qkv, qkv, qkv, cq_rows, ck_b)


def _fox_bwd(qkv, do, o, lse_rows, cq_rows, ck_b, name):
    S, D3 = qkv.shape
    D = D3 // 3
    B, TQ = ATT_BLOCK, FOX_Q_TILE
    R = TQ // B
    NP = D // LANES
    NQ = S // TQ
    q_spec, k_spec, v_spec, _, seq_spec, _ = _att_specs(S, D, TQ)
    row_spec, keyb_spec = _fox_specs(S, D, TQ)

    def body(q_ref, k_ref, v_ref, do_ref, o_ref, lse_ref, cq_ref, ck_ref,
             dq_ref, dk_ref, dv_ref, dcq_ref, dck_ref, dk_acc, dv_acc):
        qi = pl.program_id(1)

        @pl.when(qi == 0)
        def _():
            dk_acc[...] = jnp.zeros_like(dk_acc)
            dv_acc[...] = jnp.zeros_like(dv_acc)
            dck_ref[...] = jnp.zeros_like(dck_ref)

        qs = [_scaled_q(q_ref, hh) for hh in range(2)]
        dob = [do_ref[:, _head_lanes(hh)].astype(BF16) for hh in range(2)]
        prod_t = (do_ref[...] * o_ref[...]).T
        delta = [jnp.sum(prod_t[hh * HEAD_DIM:(hh + 1) * HEAD_DIM], axis=0, keepdims=True) for hh in range(2)]
        cq = [cq_ref[hh * SUBLANES:hh * SUBLANES + 1, :] for hh in range(2)]
        lse = [lse_ref[hh * SUBLANES:hh * SUBLANES + 1, :] for hh in range(2)]

        def tile(j, carry, masked):
            r0 = pl.multiple_of(j * B, B)
            if masked:
                hidden = _key_gt_query_t(j, qi, TQ)
            out = []
            for hh in range(2):
                dq, keysum = carry[hh]
                lanes = _head_lanes(hh)
                k = k_ref[pl.ds(r0, B), lanes]
                v = v_ref[pl.ds(r0, B), lanes]
                ck = jnp.tile(ck_ref[pl.ds(r0, B), hh * B:(hh + 1) * B], (1, R))
                p = jnp.exp(_dot_nt(k, qs[hh]) + (cq[hh] - ck) - lse[hh])
                if masked:
                    p = jnp.where(hidden, 0.0, p)
                ds = p * (_dot_nt(v, dob[hh]) - delta[hh])
                dsb = ds.astype(BF16)
                dk_acc[pl.ds(r0, B), lanes] += _dot(dsb, qs[hh])
                dv_acc[pl.ds(r0, B), lanes] += _dot(p.astype(BF16), dob[hh])
                qsum = jnp.sum(ds, axis=1, keepdims=True)
                dck_ref[pl.ds(r0, B), hh * B:(hh + 1) * B] -= jnp.broadcast_to(qsum, (B, B))
                out.append((dq + _dot_tn(k, dsb), keysum + jnp.sum(ds, axis=0, keepdims=True)))
            return tuple(out)

        zero = (jnp.zeros((HEAD_DIM, TQ), F32), jnp.zeros((1, TQ), F32))
        carry = lax.fori_loop(0, qi * R, lambda j, cr: tile(j, cr, False), (zero, zero))
        carry = lax.fori_loop(0, R, lambda it, cr: tile(qi * R + it, cr, True), carry)
        dq_t = jnp.concatenate([carry[hh][0] for hh in range(2)], axis=0)
        dq_ref[...] = (dq_t.T * ATT_SCALE).astype(BF16)
        for hh in range(2):
            dcq_ref[hh * SUBLANES:(hh + 1) * SUBLANES, :] = jnp.broadcast_to(carry[hh][1], (SUBLANES, TQ))

        @pl.when(qi == NQ - 1)
        def _():
            dk_ref[...] = dk_acc[...].astype(BF16)
            dv_ref[...] = dv_acc[...].astype(BF16)

    out = jax.ShapeDtypeStruct((S, D), BF16)
    return pl.pallas_call(
        body, name=name, grid=(NP, NQ),
        in_specs=[q_spec, k_spec, v_spec, q_spec, q_spec, row_spec, row_spec, keyb_spec],
        out_specs=[q_spec, seq_spec, seq_spec, row_spec, keyb_spec],
        out_shape=[out, out, out, jax.ShapeDtypeStruct((2 * NP * SUBLANES, S), F32),
                   jax.ShapeDtypeStruct((S, 2 * NP * B), F32)],
        scratch_shapes=[pltpu.VMEM((S, LANES), F32), pltpu.VMEM((S, LANES), F32)],
        compiler_params=_params(dimension_semantics=("arbitrary", "arbitrary")),
    )(qkv, qkv, qkv, do, o, lse_rows, cq_rows, ck_b)


def _forget_cumsum(f3, bias, name):
    NQ, NH, B = f3.shape

    def body(f_ref, b_ref, cum_ref):
        row = lax.broadcasted_iota(jnp.int32, (B, B), 0)
        col = lax.broadcasted_iota(jnp.int32, (B, B), 1)
        t_incl = jnp.where(row <= col, 1.0, 0.0).astype(BF16)

        def step(b, carry):
            lf, _ = _log_sigmoids(f_ref[b] + b_ref[...])
            cum = _split_dot(lf, t_incl, 3) + carry
            cum_ref[b] = cum
            return jnp.broadcast_to(cum[:, B - 1:B], (NH, B))

        lax.fori_loop(0, NQ, step, jnp.zeros((NH, B), F32))

    return pl.pallas_call(
        body, name=name, in_specs=[VMEM_WHOLE, VMEM_WHOLE], out_specs=VMEM_WHOLE,
        out_shape=jax.ShapeDtypeStruct((NQ, NH, B), F32), compiler_params=_params(),
    )(f3, bias)


def _forget_cumsum_bwd(dcum3, f3, bias, name):
    NQ, NH, B = f3.shape

    def body(d_ref, f_ref, b_ref, df_ref, tot_ref):
        row = lax.broadcasted_iota(jnp.int32, (B, B), 0)
        col = lax.broadcasted_iota(jnp.int32, (B, B), 1)
        t_rev = jnp.where(row >= col, 1.0, 0.0).astype(BF16)

        def step(it, carry):
            run, tot = carry
            b = NQ - 1 - it
            dlf = _split_dot(d_ref[b], t_rev, 3) + run
            f = f_ref[b] + b_ref[...]
            df = dlf * (1.0 / (1.0 + jnp.exp(f)))
            df_ref[b] = df
            return jnp.broadcast_to(dlf[:, 0:1], (NH, B)), tot + df

        _, tot = lax.fori_loop(0, NQ, step, (jnp.zeros((NH, B), F32), jnp.zeros((NH, B), F32)))
        tot_ref[...] = _split_dot(tot, jnp.ones((B, B), BF16), 3)

    return pl.pallas_call(
        body, name=name, in_specs=[VMEM_WHOLE, VMEM_WHOLE, VMEM_WHOLE], out_specs=[VMEM_WHOLE, VMEM_WHOLE],
        out_shape=[jax.ShapeDtypeStruct((NQ, NH, B), F32), jax.ShapeDtypeStruct((NH, B), F32)],
        compiler_params=_params(),
    )(dcum3, f3, bias)


def _silu(x, name):
    def body(x_ref, o_ref):
        v = x_ref[...]
        o_ref[...] = v * (1.0 / (1.0 + jnp.exp(-v)))

    return pl.pallas_call(body, name=name, in_specs=[VMEM_WHOLE], out_specs=VMEM_WHOLE,
                          out_shape=jax.ShapeDtypeStruct(x.shape, F32), compiler_params=_params())(x)


def _sum_leading(x, name):
    N, R, C = x.shape
    T = _pick(R, (256, 128, 64, 32, 16, 8))

    def body(x_ref, o_ref):
        acc = x_ref[0].astype(F32)
        for r in range(1, N):
            acc = acc + x_ref[r].astype(F32)
        o_ref[...] = acc

    return pl.pallas_call(
        body, name=name, grid=(R // T,), in_specs=[pl.BlockSpec((N, T, C), lambda i: (0, i, 0))],
        out_specs=pl.BlockSpec((T, C), lambda i: (i, 0)), out_shape=jax.ShapeDtypeStruct((R, C), F32),
        compiler_params=_params(),
    )(x)


def _adamw(w, g, m, v, name):
    shape = w.shape
    C = shape[-1]
    R = w.size // C
    T = R
    for cand in (512, 256, 128, 64, 32, 16, 8):
        if R % cand == 0 and cand * C * 4 <= (1 << 20):
            T = cand
            break
    spec = pl.BlockSpec((T, C), lambda i: (i, 0))
    c1 = 1.0 / (1.0 - ADAM_B1 ** ADAM_STEP)
    c2 = 1.0 / (1.0 - ADAM_B2 ** ADAM_STEP)

    def body(w_ref, g_ref, m_ref, v_ref, d_ref, nm_ref, nv_ref):
        gv = g_ref[...]
        nm = ADAM_B1 * m_ref[...] + (1.0 - ADAM_B1) * gv
        nv = ADAM_B2 * v_ref[...] + (1.0 - ADAM_B2) * (gv * gv)
        nm_ref[...] = nm
        nv_ref[...] = nv
        d_ref[...] = -ADAM_LR * ((nm * c1) / (jnp.sqrt(nv * c2) + ADAM_EPS) + ADAM_WD * w_ref[...])

    out = jax.ShapeDtypeStruct((R, C), F32)
    d, nm, nv = pl.pallas_call(
        body, name=name, grid=(R // T,), in_specs=[spec] * 4, out_specs=[spec] * 3, out_shape=[out] * 3,
        compiler_params=_params(),
    )(w.reshape(R, C), g.reshape(R, C), m.reshape(R, C), v.reshape(R, C))
    return d.reshape(shape), nm.reshape(shape), nv.reshape(shape)


def _mesh_pos():
    return lax.axis_index("x"), lax.axis_index("y"), lax.axis_index("c")


def _all_gather_small(x2d, name):
    m_per, n = x2d.shape

    def body(x_ref, out_ref, send_sems, recv_sems, local_sem):
        x, y, c = _mesh_pos()
        me, sibling = (x, y, c), (x, y, 1 - c)
        chips = [(1 - x, y), (x, 1 - y), (1 - x, 1 - y)]

        def rows(px, py, pc):
            return out_ref.at[pl.ds((4 * px + 2 * py + pc) * m_per, m_per), :]

        def copy(k, block, to, src=None):
            return pltpu.make_async_remote_copy(
                src_ref=rows(*block) if src is None else src, dst_ref=rows(*block),
                send_sem=send_sems.at[k], recv_sem=recv_sems.at[k], device_id=to, device_id_type=MESH)

        mine = pltpu.make_async_copy(x_ref, rows(*me), local_sem)
        mine.start()
        first = [copy(0, me, sibling, src=x_ref)]
        first += [copy(1 + j, me, (*chip, c), src=x_ref) for j, chip in enumerate(chips)]
        for cp in first:
            cp.start()
        passed = [copy(4 + j, (*chip, c), sibling) for j, chip in enumerate(chips)]
        for j, chip in enumerate(chips):
            copy(1 + j, (*chip, c), me).wait_recv()
            passed[j].start()
        copy(0, sibling, me).wait_recv()
        for j, chip in enumerate(chips):
            copy(4 + j, (*chip, 1 - c), me).wait_recv()
        for cp in first + passed:
            cp.wait_send()
        mine.wait()

    return pl.pallas_call(
        body, name=name, out_shape=jax.ShapeDtypeStruct((N_DEV * m_per, n), x2d.dtype),
        in_specs=[VMEM_WHOLE], out_specs=VMEM_WHOLE,
        scratch_shapes=[pltpu.SemaphoreType.DMA((7,)), pltpu.SemaphoreType.DMA((7,)), pltpu.SemaphoreType.DMA],
        compiler_params=_params(),
    )(x2d)


def _all_gather_weights(ws, name):
    n = len(ws)

    def body(*refs):
        ins, outs = refs[:n], refs[n:2 * n]
        send_sems, recv_sems, local_sems = refs[2 * n:]
        x, y, c = _mesh_pos()
        k_me = 2 * x + y
        chips = [(1 - x, y), (x, 1 - y), (1 - x, 1 - y)]

        def remote(i, j, slot):
            px, py = chips[j]
            return pltpu.make_async_remote_copy(
                src_ref=ins[i], dst_ref=outs[i].at[slot], send_sem=send_sems.at[i, j],
                recv_sem=recv_sems.at[i, j], device_id=(px, py, c), device_id_type=MESH)

        local = [pltpu.make_async_copy(ins[i], outs[i].at[k_me], local_sems.at[i]) for i in range(n)]
        for i in range(n):
            local[i].start()
            for j in range(3):
                remote(i, j, k_me).start()
        for i in range(n):
            for j, (px, py) in enumerate(chips):
                remote(i, j, 2 * px + py).wait_recv()
        for i in range(n):
            for j in range(3):
                remote(i, j, k_me).wait_send()
            local[i].wait()

    return pl.pallas_call(
        body, name=name,
        out_shape=[jax.ShapeDtypeStruct((N_CHIPS,) + w.shape, w.dtype) for w in ws],
        in_specs=[ANY] * n, out_specs=[ANY] * n,
        scratch_shapes=[pltpu.SemaphoreType.DMA((n, 3)), pltpu.SemaphoreType.DMA((n, 3)),
                        pltpu.SemaphoreType.DMA((n,))],
        compiler_params=_params(),
    )(*ws)


def _scatter_grad_pieces(gs, name):
    n = len(gs)
    halves = [g.shape[1] // 2 for g in gs]

    def body(*refs):
        ins, outs = refs[:n], refs[n:2 * n]
        send_sems, recv_sems, local_sems = refs[2 * n:]
        x, y, c = _mesh_pos()

        def flip(v, bit):
            return 1 - v if bit else v

        def piece(i, px, py, pc):
            return ins[i].at[2 * px + py, pl.ds(pc * halves[i], halves[i])]

        def remote(i, r):
            px, py, pc = flip(x, r & 4), flip(y, r & 2), flip(c, r & 1)
            return pltpu.make_async_remote_copy(
                src_ref=piece(i, px, py, pc), dst_ref=outs[i].at[r], send_sem=send_sems.at[i, r - 1],
                recv_sem=recv_sems.at[i, r - 1], device_id=(px, py, pc), device_id_type=MESH)

        local = [pltpu.make_async_copy(piece(i, x, y, c), outs[i].at[0], local_sems.at[i]) for i in range(n)]
        for i in range(n):
            local[i].start()
            for r in range(1, N_DEV):
                remote(i, r).start()
        for i in range(n):
            for r in range(1, N_DEV):
                remote(i, r).wait_recv()
        for i in range(n):
            for r in range(1, N_DEV):
                remote(i, r).wait_send()
            local[i].wait()

    return pl.pallas_call(
        body, name=name,
        out_shape=[jax.ShapeDtypeStruct((N_DEV, h) + g.shape[2:], g.dtype) for g, h in zip(gs, halves)],
        in_specs=[ANY] * n, out_specs=[ANY] * n,
        scratch_shapes=[pltpu.SemaphoreType.DMA((n, N_DEV - 1)), pltpu.SemaphoreType.DMA((n, N_DEV - 1)),
                        pltpu.SemaphoreType.DMA((n,))],
        compiler_params=_params(),
    )(*gs)


def _swap_halves(hs, name):
    n = len(hs)
    NC = SWAP_CHUNKS
    piece_rows = [h.shape[1] // NC for h in hs]

    def body(*refs):
        ins, outs = refs[:n], refs[n:2 * n]
        send_sems, recv_sems, local_sems = refs[2 * n:]
        x, y, c = _mesh_pos()

        def remote(i, ch, slot):
            rows = pl.ds(ch * piece_rows[i], piece_rows[i])
            return pltpu.make_async_remote_copy(
                src_ref=ins[i].at[:, rows], dst_ref=outs[i].at[slot, :, rows], send_sem=send_sems.at[i, ch],
                recv_sem=recv_sems.at[i, ch], device_id=(x, y, 1 - c), device_id_type=MESH)

        local = [pltpu.make_async_copy(ins[i], outs[i].at[c], local_sems.at[i]) for i in range(n)]
        for i in range(n):
            local[i].start()
            for ch in range(NC):
                remote(i, ch, c).start()
        for i in range(n):
            for ch in range(NC):
                remote(i, ch, 1 - c).wait_recv()
        for i in range(n):
            for ch in range(NC):
                remote(i, ch, c).wait_send()
            local[i].wait()

    return pl.pallas_call(
        body, name=name,
        out_shape=[jax.ShapeDtypeStruct((2,) + h.shape, h.dtype) for h in hs],
        in_specs=[ANY] * n, out_specs=[ANY] * n,
        scratch_shapes=[pltpu.SemaphoreType.DMA((n, NC)), pltpu.SemaphoreType.DMA((n, NC)),
                        pltpu.SemaphoreType.DMA((n,))],
        compiler_params=_params(),
    )(*hs)


def _pad_rows(a, rows):
    return jnp.pad(a, ((0, rows - a.shape[0]), (0, 0)))


def kernel(x, c, w_mod, b_mod, g_mix_pre, g_mix_post, w_qkv, w_o, w_fg, b_fg, g_ffn_pre, g_ffn_post, w_ffn_gate, w_ffn_up, w_conv, b_conv, w_ffn_down, loss_target, m_w_mod, m_b_mod, m_g_mix_pre, m_g_mix_post, m_w_qkv, m_w_o, m_w_fg, m_b_fg, m_g_ffn_pre, m_g_ffn_post, m_w_ffn_gate, m_w_ffn_up, m_w_conv, m_b_conv, m_w_ffn_down, v_w_mod, v_b_mod, v_g_mix_pre, v_g_mix_post, v_w_qkv, v_w_o, v_w_fg, v_b_fg, v_g_ffn_pre, v_g_ffn_post, v_w_ffn_gate, v_w_ffn_up, v_w_conv, v_b_conv, v_w_ffn_down):
    xs = x[0]
    target = loss_target[0]
    S, D = xs.shape
    L = w_mod.shape[0]
    LF = w_fg.shape[0]
    MS = w_mod.shape[2]
    QS = w_qkv.shape[2]
    OS = w_o.shape[1]
    FS = w_ffn_gate.shape[2]
    F = N_CHIPS * FS
    NH = D // HEAD_DIM
    B = ATT_BLOCK
    NQ = S // B
    ax, ay, ac = _mesh_pos()
    k_me = 2 * ax + ay
    b_me = 4 * ax + 2 * ay + ac

    conv_rows = -(-(L * 3 * FS) // D)
    conv_rows = -(-conv_rows // SUBLANES) * SUBLANES
    conv_flat = jnp.pad(w_conv.reshape(-1), (0, conv_rows * D - L * 3 * FS)).reshape(conv_rows, D)
    first = jnp.concatenate([_pad_rows(c, SUBLANES), conv_flat], axis=0)
    first_all = _all_gather_small(first, "ag_cond").reshape(N_DEV, SUBLANES + conv_rows, D)
    c_all = first_all[:, 0, :]
    conv_all = first_all[0::2, SUBLANES:, :].reshape(N_CHIPS, -1)[:, :L * 3 * FS]
    w_conv_full = conv_all.reshape(N_CHIPS, L, 3, FS).transpose(1, 2, 0, 3).reshape(L, 3, F)
    c_act = _silu(c_all, "silu_c")

    mod_part = jnp.concatenate(
        [_mm(c_act, w_mod[l], "nn", F32, "mm_mod", tm=N_DEV, tn=MS, tk=D) for l in range(L)], axis=1)
    mod_all = _all_gather_small(mod_part, "ag_mod").reshape(N_CHIPS, 2, N_DEV, L, MS)[:, 0]
    mod_mine = lax.dynamic_index_in_dim(mod_all, b_me, axis=1, keepdims=False)
    mod = mod_mine.transpose(1, 0, 2).reshape(L, N_MOD * D) + b_mod

    gq, go, gg, gu_, gd, gf = _all_gather_weights(
        [w_qkv.astype(BF16), w_o.astype(BF16), w_ffn_gate.astype(BF16), w_ffn_up.astype(BF16),
         w_ffn_down.astype(BF16), w_fg.astype(BF16)], "ag_weights")
    def cols(gathered, l):
        return gathered[:, l].transpose(1, 0, 2).reshape(D, -1, 1, LANES)

    W_qkv = [gq[:, l].transpose(1, 0, 2).reshape(D, 3 * D) for l in range(L)]
    W_o = [go[:, l].reshape(D, D) for l in range(L)]
    W_gu = [jnp.concatenate([cols(gg, l), cols(gu_, l)], axis=2).reshape(D, 2 * F) for l in range(L)]
    W_d = [gd[:, l].reshape(F, D) for l in range(L)]
    W_fg = [jnp.pad(gf[:, j].reshape(D, NH), ((0, 0), (0, LANES - NH))) for j in range(LF)]

    def vec(a):
        return a.reshape(1, -1)

    saved = []
    xcur = xs
    for l in range(L):
        sh_a, sc_a, gt_a, sh_f, sc_f, gt_f = [vec(mod[l, j * D:(j + 1) * D]) for j in range(N_MOD)]
        is_fox = l % 2 == 1
        jf = l // 2
        h1 = _norm_mod(xcur, vec(g_mix_pre[l]), sc_a, sh_a, "norm_mod")
        qkv = _mm(h1, W_qkv[l], "nn", BF16, "mm_qkv")
        if is_fox:
            flog = _mm(h1, W_fg[jf], "nn", F32, "mm_fg")[:, :NH]
            f3 = flog.reshape(NQ, B, NH).transpose(0, 2, 1)
            bias = b_fg[jf].reshape(NH, 1)
            cum3 = _forget_cumsum(f3, bias, "forget_cumsum")
            cum_sn = cum3.transpose(0, 2, 1).reshape(S, NH)
            ck_b = jnp.repeat(cum_sn, B, axis=1)
            cq_rows = jnp.repeat(cum_sn.T, SUBLANES, axis=0)
            o, stat = _fox_fwd(qkv, cq_rows, ck_b, "fox_fwd")
            extra = (f3, bias, cq_rows, ck_b)
        else:
            o, stat, extra = _sb_fwd(qkv, "sb_fwd")
        p = _mm(o, W_o[l], "nn", F32, "mm_o")
        x1 = _post_res(xcur, p, vec(g_mix_post[l]), gt_a, "post_res")
        h2 = _norm_mod(x1, vec(g_ffn_pre[l]), sc_f, sh_f, "norm_mod")
        gu = _mm(h2, W_gu[l], "nn", F32, "mm_gu")
        wc = w_conv_full[l]
        bc = vec(b_conv[l])
        a = _conv_gate(gu, wc, bc, "conv_gate")
        yv = _mm(a, W_d[l], "nn", F32, "mm_down")
        x2 = _post_res(x1, yv, vec(g_ffn_post[l]), gt_f, "post_res")
        saved.append(dict(x0=xcur, h1=h1, qkv=qkv, o=o, stat=stat, extra=extra, p=p, x1=x1, h2=h2, gu=gu, a=a,
                          y=yv, mods=(sh_a, sc_a, gt_a, sh_f, sc_f, gt_f), wc=wc, bc=bc))
        xcur = x2

    g, sq = _loss_head(xcur, target, "loss_head")
    loss_part = 0.5 * jnp.sum(sq) / D
    loss = lax.psum(loss_part, ("x", "y", "c"))

    dW_qkv, dW_o, dW_gu, dW_d = [None] * L, [None] * L, [None] * L, [None] * L
    dW_fg, db_fg = [None] * LF, [None] * LF
    dmod, dg_mix_pre, dg_mix_post, dg_ffn_pre, dg_ffn_post = [[None] * L for _ in range(5)]
    dw_conv, db_conv = [None] * L, [None] * L
    for l in reversed(range(L)):
        sv = saved[l]
        sh_a, sc_a, gt_a, sh_f, sc_f, gt_f = sv["mods"]
        is_fox = l % 2 == 1
        jf = l // 2
        dy, dgp_f, dgt_f = _post_res_bwd(sv["y"], vec(g_ffn_post[l]), gt_f, g, "post_res_bwd")
        da = _mm(dy, W_d[l], "nt", F32, "mm_da")
        dW_d[l] = _mm(sv["a"], dy, "tn", F32, "mm_dwd")
        dgu, dwc, dbc = _conv_gate_bwd(sv["gu"], sv["wc"], sv["bc"], da, "conv_gate_bwd")
        dh2 = _mm(dgu, W_gu[l], "nt", F32, "mm_dh2")
        dW_gu[l] = _mm(sv["h2"], dgu, "tn", F32, "mm_dwgu")
        g, dg_f, dsc_f, dsh_f = _norm_mod_bwd(sv["x1"], vec(g_ffn_pre[l]), sc_f, sh_f, [dh2], g, "norm_mod_bwd")
        dp, dgp_a, dgt_a = _post_res_bwd(sv["p"], vec(g_mix_post[l]), gt_a, g, "post_res_bwd")
        do = _mm(dp, W_o[l], "nt", F32, "mm_do")
        dW_o[l] = _mm(sv["o"], dp, "tn", F32, "mm_dwo")
        if is_fox:
            f3, bias, cq_rows, ck_b = sv["extra"]
            dq, dk, dv, dcq_rows, dck_b = _fox_bwd(sv["qkv"], do, sv["o"], sv["stat"], cq_rows, ck_b, "fox_bwd")
            dcum_sn = dcq_rows[::SUBLANES].T + dck_b[:, ::B]
            dcum3 = dcum_sn.reshape(NQ, B, NH).transpose(0, 2, 1)
            df3, df_tot = _forget_cumsum_bwd(dcum3, f3, bias, "forget_cumsum_bwd")
            df = df3.transpose(0, 2, 1).reshape(S, NH)
            dfp = jnp.pad(df, ((0, 0), (0, LANES - NH)))
            dW_fg[jf] = _mm(sv["h1"], dfp, "tn", F32, "mm_dwfg")[:, :NH]
            db_fg[jf] = df_tot[:, 0]
            dh_extra = [_mm(dfp, W_fg[jf], "nt", F32, "mm_dh1f")]
        else:
            dq, dk, dv = _sb_bwd(sv["qkv"], do, sv["stat"], sv["extra"], "sb_bwd")
            dh_extra = []
        dqkv = jnp.concatenate([dq, dk, dv], axis=1)
        dh1 = _mm(dqkv, W_qkv[l], "nt", F32, "mm_dh1")
        dW_qkv[l] = _mm(sv["h1"], dqkv, "tn", F32, "mm_dwqkv")
        g, dg_a, dsc_a, dsh_a = _norm_mod_bwd(sv["x0"], vec(g_mix_pre[l]), sc_a, sh_a, [dh1] + dh_extra, g,
                                              "norm_mod_bwd")
        dmod[l] = jnp.concatenate([dsh_a, dsc_a, dgt_a, dsh_f, dsc_f, dgt_f], axis=1)[0]
        dg_mix_pre[l], dg_mix_post[l], dg_ffn_pre[l], dg_ffn_post[l] = dg_a[0], dgp_a[0], dg_f[0], dgp_f[0]
        dw_conv[l], db_conv[l] = dwc, dbc[0]
    grad_x = g[None]

    pieces = [jnp.stack(dmod), jnp.stack(dg_mix_pre), jnp.stack(dg_mix_post), jnp.stack(dg_ffn_pre),
              jnp.stack(dg_ffn_post), jnp.stack(db_fg), jnp.stack(dW_fg), jnp.stack(dw_conv), jnp.stack(db_conv)]
    sizes = [pc.size for pc in pieces]
    total = sum(sizes)
    pack_rows = -(-total // (LANES * SUBLANES)) * SUBLANES
    pack = jnp.pad(jnp.concatenate([pc.reshape(-1) for pc in pieces]), (0, pack_rows * LANES - total))
    pack_all = _all_gather_small(pack.reshape(pack_rows, LANES), "ag_small_grads").reshape(N_DEV, pack_rows, LANES)
    small = _sum_leading(pack_all, "sum_small_grads").reshape(-1)
    offs = [0]
    for sz in sizes:
        offs.append(offs[-1] + sz)
    parts = [small[offs[i]:offs[i + 1]].reshape(pieces[i].shape) for i in range(len(pieces))]
    g_b_mod, g_g_mix_pre, g_g_mix_post, g_g_ffn_pre, g_g_ffn_post, g_b_fg, g_w_fg_full, g_w_conv_full, g_b_conv = parts
    g_w_fg = lax.dynamic_slice_in_dim(g_w_fg_full, k_me * OS, OS, axis=1)
    g_w_conv = lax.dynamic_slice_in_dim(g_w_conv_full, k_me * FS, FS, axis=2)
    dmod_all = pack_all.reshape(N_DEV, -1)[:, :L * N_MOD * D].reshape(N_DEV, L, N_CHIPS, MS)
    dmod_cols = lax.dynamic_index_in_dim(dmod_all, k_me, axis=2, keepdims=False).reshape(N_DEV, L * MS)
    g_w_mod = _mm(_pad_rows(c_act, LANES), _pad_rows(dmod_cols, LANES), "tn", F32, "mm_dwmod", tm=D, tn=MS, tk=LANES)
    g_w_mod = g_w_mod.reshape(D, L, MS).transpose(1, 0, 2)

    def shard_cols(dws, width):
        st = jnp.stack(dws)
        return st.reshape(L, st.shape[1], N_CHIPS, width).transpose(2, 0, 1, 3).astype(BF16)

    def shard_rows(dws, height):
        st = jnp.stack(dws)
        return st.reshape(L, N_CHIPS, height, st.shape[2]).transpose(1, 0, 2, 3).astype(BF16)

    dgu_st = jnp.stack(dW_gu).reshape(L, D, F // LANES, 2, LANES)
    dWg = [dgu_st[l, :, :, 0, :].reshape(D, F) for l in range(L)]
    dWu = [dgu_st[l, :, :, 1, :].reshape(D, F) for l in range(L)]
    gs = [shard_cols(dW_qkv, QS), shard_rows(dW_o, OS), shard_cols(dWg, FS), shard_cols(dWu, FS),
          shard_rows(dW_d, FS)]
    recv = _scatter_grad_pieces(gs, "rs_scatter")
    halves = []
    for i, r in enumerate(recv):
        cols = r.shape[-1]
        halves.append(_sum_leading(r.reshape(N_DEV, -1, cols), f"sum_grad_pieces_{i}").reshape(r.shape[1:]))
    full = _swap_halves(halves, "rs_swap")
    g_w_qkv, g_w_o, g_w_gate, g_w_up, g_w_down = [f.reshape((L,) + f.shape[2:]) for f in full]

    grads = [g_w_mod, g_b_mod, g_g_mix_pre, g_g_mix_post, g_w_qkv, g_w_o, g_w_fg, g_b_fg, g_g_ffn_pre,
             g_g_ffn_post, g_w_gate, g_w_up, g_w_conv, g_b_conv, g_w_down]
    weights = [w_mod, b_mod, g_mix_pre, g_mix_post, w_qkv, w_o, w_fg, b_fg, g_ffn_pre, g_ffn_post, w_ffn_gate,
               w_ffn_up, w_conv, b_conv, w_ffn_down]
    ms = [m_w_mod, m_b_mod, m_g_mix_pre, m_g_mix_post, m_w_qkv, m_w_o, m_w_fg, m_b_fg, m_g_ffn_pre, m_g_ffn_post,
          m_w_ffn_gate, m_w_ffn_up, m_w_conv, m_b_conv, m_w_ffn_down]
    vs = [v_w_mod, v_b_mod, v_g_mix_pre, v_g_mix_post, v_w_qkv, v_w_o, v_w_fg, v_b_fg, v_g_ffn_pre, v_g_ffn_post,
          v_w_ffn_gate, v_w_ffn_up, v_w_conv, v_b_conv, v_w_ffn_down]
    deltas, new_ms, new_vs = [], [], []
    for wv, gv, mv, vv in zip(weights, grads, ms, vs):
        d, nm, nv = _adamw(wv, gv, mv, vv, "adamw")
        deltas.append(d)
        new_ms.append(nm)
        new_vs.append(nv)
    return (loss, grad_x, *grads, *deltas, *new_ms, *new_vs)
```

```python
import functools

import jax
import jax.numpy as jnp
from jax import lax
from jax.experimental import pallas as pl
from jax.experimental.pallas import tpu as pltpu

F32 = jnp.float32
BF16 = jnp.bfloat16
MESH = pl.DeviceIdType.MESH

HEAD_DIM = 64
ATT_BLOCK = 128
SB_Q_TILE = 512
FOX_Q_TILE = 1024
SB_DEAD_LOG = -110.0
LANES = 128
SUBLANES = 8
RMS_EPS = 1e-6
N_MOD = 6
N_CHIPS = 4
N_DEV = 8
ADAM_LR = 0.001
ADAM_B1 = 0.9
ADAM_B2 = 0.999
ADAM_EPS = 1e-08
ADAM_WD = 0.01
ADAM_STEP = 10
VMEM_LIMIT_BYTES = 56 * 1024 * 1024
NEG_BIG = -1e30

ANY = pl.BlockSpec(memory_space=pl.ANY)
VMEM_WHOLE = pl.BlockSpec(memory_space=pltpu.VMEM)


def _params(**kw):
    return pltpu.CompilerParams(vmem_limit_bytes=VMEM_LIMIT_BYTES, **kw)


def _dot(a, b):
    return jnp.dot(a, b, preferred_element_type=F32)


def _dot_nt(a, b):
    return lax.dot_general(a, b, (((1,), (1,)), ((), ())), preferred_element_type=F32)


def _dot_tn(a, b):
    return lax.dot_general(a, b, (((0,), (0,)), ((), ())), preferred_element_type=F32)


def _split_dot(x, t, parts):
    acc = None
    rem = x
    for _ in range(parts):
        piece = rem.astype(BF16)
        rem = rem - piece.astype(F32)
        d = _dot(piece, t)
        acc = d if acc is None else acc + d
    return acc


def _pick(n, prefs):
    for p in prefs:
        if n % p == 0:
            return p
    return n


def _mm(a, b, dims, out_dtype, name, tm=None, tn=None, tk=None):
    if dims == "tn":
        K, M = a.shape
    else:
        M, K = a.shape
    N = b.shape[0] if dims == "nt" else b.shape[1]
    tm = tm or _pick(M, (512, 256, 128))
    tn = tn or _pick(N, (1536, 1408, 1024, 768, 512, 256, 128))
    tk = tk or _pick(K, (1024, 1408, 512, 256, 128))
    nk = K // tk
    grid = (M // tm, N // tn, nk)
    if dims == "tn":
        a_spec = pl.BlockSpec((tk, tm), lambda i, j, k: (k, i))
    else:
        a_spec = pl.BlockSpec((tm, tk), lambda i, j, k: (i, k))
    if dims == "nt":
        b_spec = pl.BlockSpec((tn, tk), lambda i, j, k: (j, k))
    else:
        b_spec = pl.BlockSpec((tk, tn), lambda i, j, k: (k, j))
    o_spec = pl.BlockSpec((tm, tn), lambda i, j, k: (i, j))

    def body(a_ref, b_ref, o_ref, *scratch):
        x = a_ref[...].astype(BF16)
        y = b_ref[...].astype(BF16)
        if dims == "nn":
            r = _dot(x, y)
        elif dims == "nt":
            r = _dot_nt(x, y)
        else:
            r = _dot_tn(x, y)
        if nk == 1:
            o_ref[...] = r.astype(out_dtype)
        else:
            acc = scratch[0]
            k = pl.program_id(2)

            @pl.when(k == 0)
            def _():
                acc[...] = r

            @pl.when(k > 0)
            def _():
                acc[...] += r

            @pl.when(k == nk - 1)
            def _():
                o_ref[...] = acc[...].astype(out_dtype)

    return pl.pallas_call(
        body,
        name=name,
        grid=grid,
        in_specs=[a_spec, b_spec],
        out_specs=o_spec,
        out_shape=jax.ShapeDtypeStruct((M, N), out_dtype),
        scratch_shapes=[pltpu.VMEM((tm, tn), F32)] if nk > 1 else [],
        compiler_params=_params(dimension_semantics=("parallel", "parallel", "arbitrary")),
    )(a, b)


def _row_tile(S):
    return _pick(S, (256, 128, 64, 32, 16, 8))


def _norm_mod(x, g, sc, sh, name):
    S, D = x.shape
    T = _row_tile(S)
    row = pl.BlockSpec((T, D), lambda i: (i, 0))
    vec = pl.BlockSpec((1, D), lambda i: (0, 0))

    def body(x_ref, g_ref, sc_ref, sh_ref, h_ref):
        xv = x_ref[...]
        r = lax.rsqrt(jnp.mean(xv * xv, axis=-1, keepdims=True) + RMS_EPS)
        n = (xv * r) * g_ref[...]
        h_ref[...] = (n * (1.0 + sc_ref[...]) + sh_ref[...]).astype(BF16)

    return pl.pallas_call(
        body, name=name, grid=(S // T,), in_specs=[row, vec, vec, vec], out_specs=row,
        out_shape=jax.ShapeDtypeStruct((S, D), BF16), compiler_params=_params(),
    )(x, g, sc, sh)


def _norm_mod_bwd(x, g, sc, sh, dhs, gres, name):
    S, D = x.shape
    T = _row_tile(S)
    n_dh = len(dhs)
    row = pl.BlockSpec((T, D), lambda i: (i, 0))
    vec = pl.BlockSpec((1, D), lambda i: (0, 0))

    def body(x_ref, g_ref, sc_ref, sh_ref, *refs):
        dh_refs = refs[:n_dh]
        gres_ref, dx_ref, dg_ref, dsc_ref, dsh_ref = refs[n_dh:]
        xv = x_ref[...]
        r = lax.rsqrt(jnp.mean(xv * xv, axis=-1, keepdims=True) + RMS_EPS)
        xn = xv * r
        n = xn * g_ref[...]
        dh = dh_refs[0][...]
        for extra in dh_refs[1:]:
            dh = dh + extra[...]
        dn = dh * (1.0 + sc_ref[...])
        dxn = dn * g_ref[...]
        dx = r * (dxn - xn * jnp.mean(dxn * xn, axis=-1, keepdims=True))
        dx_ref[...] = gres_ref[...] + dx

        @pl.when(pl.program_id(0) == 0)
        def _():
            dg_ref[...] = jnp.zeros_like(dg_ref)
            dsc_ref[...] = jnp.zeros_like(dsc_ref)
            dsh_ref[...] = jnp.zeros_like(dsh_ref)

        dg_ref[...] += jnp.sum(dn * xn, axis=0, keepdims=True)
        dsc_ref[...] += jnp.sum(dh * n, axis=0, keepdims=True)
        dsh_ref[...] += jnp.sum(dh, axis=0, keepdims=True)

    vshape = jax.ShapeDtypeStruct((1, D), F32)
    return pl.pallas_call(
        body, name=name, grid=(S // T,), in_specs=[row, vec, vec, vec] + [row] * (n_dh + 1),
        out_specs=[row, vec, vec, vec],
        out_shape=[jax.ShapeDtypeStruct((S, D), F32), vshape, vshape, vshape],
        compiler_params=_params(dimension_semantics=("arbitrary",)),
    )(x, g, sc, sh, *dhs, gres)


def _post_res(x, p, gp, gt, name):
    S, D = x.shape
    T = _row_tile(S)
    row = pl.BlockSpec((T, D), lambda i: (i, 0))
    vec = pl.BlockSpec((1, D), lambda i: (0, 0))

    def body(x_ref, p_ref, gp_ref, gt_ref, o_ref):
        pv = p_ref[...]
        r = lax.rsqrt(jnp.mean(pv * pv, axis=-1, keepdims=True) + RMS_EPS)
        o_ref[...] = x_ref[...] + gt_ref[...] * ((pv * r) * gp_ref[...])

    return pl.pallas_call(
        body, name=name, grid=(S // T,), in_specs=[row, row, vec, vec], out_specs=row,
        out_shape=jax.ShapeDtypeStruct((S, D), F32), compiler_params=_params(),
    )(x, p, gp, gt)


def _post_res_bwd(p, gp, gt, g, name):
    S, D = p.shape
    T = _row_tile(S)
    row = pl.BlockSpec((T, D), lambda i: (i, 0))
    vec = pl.BlockSpec((1, D), lambda i: (0, 0))

    def body(p_ref, gp_ref, gt_ref, g_ref, dp_ref, dgp_ref, dgt_ref):
        pv = p_ref[...]
        gv = g_ref[...]
        r = lax.rsqrt(jnp.mean(pv * pv, axis=-1, keepdims=True) + RMS_EPS)
        pn = pv * r
        n2 = pn * gp_ref[...]
        dn2 = gv * gt_ref[...]
        dpn = dn2 * gp_ref[...]
        dp = r * (dpn - pn * jnp.mean(dpn * pn, axis=-1, keepdims=True))
        dp_ref[...] = dp.astype(BF16)

        @pl.when(pl.program_id(0) == 0)
        def _():
            dgp_ref[...] = jnp.zeros_like(dgp_ref)
            dgt_ref[...] = jnp.zeros_like(dgt_ref)

        dgp_ref[...] += jnp.sum(dn2 * pn, axis=0, keepdims=True)
        dgt_ref[...] += jnp.sum(gv * n2, axis=0, keepdims=True)

    vshape = jax.ShapeDtypeStruct((1, D), F32)
    return pl.pallas_call(
        body, name=name, grid=(S // T,), in_specs=[row, vec, vec, row], out_specs=[row, vec, vec],
        out_shape=[jax.ShapeDtypeStruct((S, D), BF16), vshape, vshape],
        compiler_params=_params(dimension_semantics=("arbitrary",)),
    )(p, gp, gt, g)


def _loss_head(y, target, name):
    S, D = y.shape
    T = _row_tile(S)
    row = pl.BlockSpec((T, D), lambda i: (i, 0))
    vec = pl.BlockSpec((1, D), lambda i: (0, 0))

    def body(y_ref, t_ref, dy_ref, sq_ref):
        e = y_ref[...] - t_ref[...]
        dy_ref[...] = e * (1.0 / D)

        @pl.when(pl.program_id(0) == 0)
        def _():
            sq_ref[...] = jnp.zeros_like(sq_ref)

        sq_ref[...] += jnp.sum(e * e, axis=0, keepdims=True)

    return pl.pallas_call(
        body, name=name, grid=(S // T,), in_specs=[row, row], out_specs=[row, vec],
        out_shape=[jax.ShapeDtypeStruct((S, D), F32), jax.ShapeDtypeStruct((1, D), F32)],
        compiler_params=_params(dimension_semantics=("arbitrary",)),
    )(y, target)


def _shift_down(v, k, rows):
    return jnp.where(rows >= k, pltpu.roll(v, k, 0), 0.0)


def _shift_up(v, k, rows, S):
    return jnp.where(rows < S - k, pltpu.roll(v, S - k, 0), 0.0)


def _conv_gate(gu, wc, bc, name):
    S, F2 = gu.shape
    F = F2 // 2
    C = LANES

    def body(gu_ref, w_ref, b_ref, a_ref):
        rows = lax.broadcasted_iota(jnp.int32, (S, C), 0)
        gate = gu_ref[:, :C]
        up = gu_ref[:, C:]
        w = w_ref[...]
        gc = w[2:3] * gate + w[1:2] * _shift_down(gate, 1, rows) + w[0:1] * _shift_down(gate, 2, rows) + b_ref[...]
        a_ref[...] = (gc * (1.0 / (1.0 + jnp.exp(-gc))) * up).astype(BF16)

    return pl.pallas_call(
        body, name=name, grid=(F // C,),
        in_specs=[pl.BlockSpec((S, 2 * C), lambda j: (0, j)), pl.BlockSpec((3, C), lambda j: (0, j)),
                  pl.BlockSpec((1, C), lambda j: (0, j))],
        out_specs=pl.BlockSpec((S, C), lambda j: (0, j)),
        out_shape=jax.ShapeDtypeStruct((S, F), BF16), compiler_params=_params(),
    )(gu, wc, bc)


def _conv_gate_bwd(gu, wc, bc, da, name):
    S, F2 = gu.shape
    F = F2 // 2
    C = LANES

    def body(gu_ref, w_ref, b_ref, da_ref, dgu_ref, dw_ref, db_ref):
        rows = lax.broadcasted_iota(jnp.int32, (S, C), 0)
        gate = gu_ref[:, :C]
        up = gu_ref[:, C:]
        dav = da_ref[...]
        w = w_ref[...]
        g1 = _shift_down(gate, 1, rows)
        g2 = _shift_down(gate, 2, rows)
        gc = w[2:3] * gate + w[1:2] * g1 + w[0:1] * g2 + b_ref[...]
        sg = 1.0 / (1.0 + jnp.exp(-gc))
        dgu_ref[:, C:] = (dav * (gc * sg)).astype(BF16)
        dgc = dav * up * (sg * (1.0 + gc * (1.0 - sg)))
        db_ref[...] = jnp.sum(dgc, axis=0, keepdims=True)
        dw_ref[0:1, :] = jnp.sum(dgc * g2, axis=0, keepdims=True)
        dw_ref[1:2, :] = jnp.sum(dgc * g1, axis=0, keepdims=True)
        dw_ref[2:3, :] = jnp.sum(dgc * gate, axis=0, keepdims=True)
        dgate = w[2:3] * dgc + w[1:2] * _shift_up(dgc, 1, rows, S) + w[0:1] * _shift_up(dgc, 2, rows, S)
        dgu_ref[:, :C] = dgate.astype(BF16)

    return pl.pallas_call(
        body, name=name, grid=(F // C,),
        in_specs=[pl.BlockSpec((S, 2 * C), lambda j: (0, j)), pl.BlockSpec((3, C), lambda j: (0, j)),
                  pl.BlockSpec((1, C), lambda j: (0, j)), pl.BlockSpec((S, C), lambda j: (0, j))],
        out_specs=[pl.BlockSpec((S, 2 * C), lambda j: (0, j)), pl.BlockSpec((3, C), lambda j: (0, j)),
                   pl.BlockSpec((1, C), lambda j: (0, j))],
        out_shape=[jax.ShapeDtypeStruct((S, F2), BF16), jax.ShapeDtypeStruct((3, F), F32),
                   jax.ShapeDtypeStruct((1, F), F32)],
        compiler_params=_params(),
    )(gu, wc, bc, da)


ATT_SCALE = HEAD_DIM ** -0.5


def _att_specs(S, D, TQ):
    nb = D // LANES
    q_spec = pl.BlockSpec((TQ, LANES), lambda p, i: (i, p))
    k_spec = pl.BlockSpec((S, LANES), lambda p, i: (0, nb + p))
    v_spec = pl.BlockSpec((S, LANES), lambda p, i: (0, 2 * nb + p))
    stat_spec = pl.BlockSpec((TQ, 2 * ATT_BLOCK), lambda p, i: (i, p))
    seq_spec = pl.BlockSpec((S, LANES), lambda p, i: (0, p))
    ck_spec = pl.BlockSpec((2, S // ATT_BLOCK, SUBLANES, ATT_BLOCK), lambda p, i: (p, 0, 0, 0))
    return q_spec, k_spec, v_spec, stat_spec, seq_spec, ck_spec


def _att_call(body, name, grid, in_specs, out_specs, out_shape, scratch, args, side, semantics):
    n_out = len(out_shape)
    if side is not None:
        in_specs = in_specs + [ANY] * len(side.inputs)
        out_specs = out_specs + [ANY] * len(side.out_shapes)
        out_shape = out_shape + side.out_shapes
        scratch = scratch + side.sems
        args = args + side.inputs
    res = pl.pallas_call(
        body, name=name, grid=grid, in_specs=in_specs, out_specs=out_specs, out_shape=out_shape,
        scratch_shapes=scratch, compiler_params=_params(dimension_semantics=semantics),
    )(*args)
    return res[:n_out], res[n_out:]


def _grid_ends(NP, NQ):
    p, qi = pl.program_id(0), pl.program_id(1)
    return jnp.logical_and(p == 0, qi == 0), jnp.logical_and(p == NP - 1, qi == NQ - 1)


def _head_lanes(hh):
    return slice(hh * HEAD_DIM, (hh + 1) * HEAD_DIM)


def _scaled_q(q_ref, hh):
    return (q_ref[:, _head_lanes(hh)].astype(F32) * ATT_SCALE).astype(BF16)


def _tri(cmp):
    B = ATT_BLOCK
    row = lax.broadcasted_iota(jnp.int32, (B, B), 0)
    col = lax.broadcasted_iota(jnp.int32, (B, B), 1)
    half = jnp.concatenate([jnp.where(cmp(row, col), 1.0, 0.0).astype(BF16), jnp.ones((B, B), BF16)], axis=1)
    return jnp.concatenate([half, half], axis=0)


def _hi_lo_dot(x, t):
    hi = x.astype(BF16)
    lo = (x - hi.astype(F32)).astype(BF16)
    return _dot(jnp.concatenate([hi, lo], axis=1), t)


def _key_minus_query(j, qi, TQ):
    row = lax.broadcasted_iota(jnp.int32, (TQ, ATT_BLOCK), 0)
    col = lax.broadcasted_iota(jnp.int32, (TQ, ATT_BLOCK), 1)
    return col - row + (j * ATT_BLOCK - qi * TQ)


def _log_sigmoids(z):
    sp = jnp.log(1.0 + jnp.exp(-jnp.abs(z)))
    lb = jnp.minimum(z, 0.0) - sp
    return lb, lb - z


def _log_sigmoids_fast(z):
    zc = jnp.maximum(z, -80.0)
    lb = -jnp.log(1.0 + jnp.exp(-zc))
    return lb, lb - zc


def _sb_fwd(qkv, name, side=None):
    S, D3 = qkv.shape
    D = D3 // 3
    B, TQ = ATT_BLOCK, SB_Q_TILE
    R = TQ // B
    NP = D // LANES
    NQ = S // TQ
    q_spec, k_spec, v_spec, stat_spec, _, _ = _att_specs(S, D, TQ)

    def body(*refs):
        (q_ref, k_ref, v_ref), (o_ref, lt_ref, first_ref), _, parts = _side_parts(side, refs, 3, 3)
        begin, end = _side_hooks(side, parts, *_grid_ends(NP, NQ))
        begin()
        qi = pl.program_id(1)
        t_suffix = _tri(lambda r, c: r > c)
        qs = [_scaled_q(q_ref, hh) for hh in range(2)]

        def tile(j, carry, masked):
            r0 = pl.multiple_of(j * B, B)
            if masked:
                strict = _key_minus_query(j, qi, TQ) < 0
            out = []
            for hh in range(2):
                c, acc = carry[hh]
                k = k_ref[pl.ds(r0, B), _head_lanes(hh)]
                v = v_ref[pl.ds(r0, B), _head_lanes(hh)]
                lb, l1 = _log_sigmoids_fast(_dot_nt(qs[hh], k))
                if masked:
                    l1 = jnp.where(strict, l1, 0.0)
                sums = _hi_lo_dot(l1, t_suffix)
                a = jnp.exp(lb + c + sums[:, :B])
                if masked:
                    a = jnp.where(strict, a, 0.0)
                out.append((c + sums[:, B:], acc + _dot(a.astype(BF16), v)))
            return tuple(out)

        zero = (jnp.zeros((TQ, B), F32), jnp.zeros((TQ, HEAD_DIM), F32))
        last = qi * R + R - 1
        carry = lax.fori_loop(0, R, lambda it, cr: tile(last - it, cr, True), (zero, zero))

        def alive(cr):
            return jnp.max(jnp.maximum(cr[0][0], cr[1][0])) > SB_DEAD_LOG

        def walk(state):
            j, cr, _ = state
            cr = tile(j, cr, False)
            return j - 1, cr, alive(cr)

        j, carry, _ = lax.while_loop(lambda st: jnp.logical_and(st[0] >= 0, st[2]), walk,
                                     (qi * R - 1, carry, alive(carry)))
        first_ref[pl.program_id(0), qi] = (j + 1).astype(F32)
        for hh in range(2):
            c, acc = carry[hh]
            o_ref[:, _head_lanes(hh)] = acc
            lt_ref[:, hh * B:(hh + 1) * B] = c
        end()

    return _att_call(
        body, name, (NP, NQ), [q_spec, k_spec, v_spec],
        [q_spec, stat_spec, pl.BlockSpec(memory_space=pltpu.SMEM)],
        [jax.ShapeDtypeStruct((S, D), F32), jax.ShapeDtypeStruct((S, 2 * NP * B), F32),
         jax.ShapeDtypeStruct((NP, NQ), F32)], [], [qkv, qkv, qkv], side, ("arbitrary", "arbitrary"))


def _sb_bwd(qkv, do, lt, first, name, side=None):
    S, D3 = qkv.shape
    D = D3 // 3
    B, TQ = ATT_BLOCK, SB_Q_TILE
    R = TQ // B
    NP = D // LANES
    NQ = S // TQ
    q_spec, k_spec, v_spec, stat_spec, seq_spec, _ = _att_specs(S, D, TQ)

    def body(*refs):
        ins, (dq_ref, dk_ref, dv_ref), (dk_acc, dv_acc), parts = _side_parts(side, refs, 6, 3)
        first_ref, q_ref, k_ref, v_ref, do_ref, lt_ref = ins
        begin, end = _side_hooks(side, parts, *_grid_ends(NP, NQ))
        begin()
        qi = pl.program_id(1)
        t_prefix = _tri(lambda r, c: r <= c)
        t_before = _tri(lambda r, c: r < c)

        @pl.when(qi == 0)
        def _():
            dk_acc[...] = jnp.zeros_like(dk_acc)
            dv_acc[...] = jnp.zeros_like(dv_acc)

        qs = [_scaled_q(q_ref, hh) for hh in range(2)]
        dob = [do_ref[:, _head_lanes(hh)].astype(BF16) for hh in range(2)]
        ltot = [lt_ref[:, hh * B:(hh + 1) * B] for hh in range(2)]

        def tile(j, carry, masked):
            r0 = pl.multiple_of(j * B, B)
            if masked:
                strict = _key_minus_query(j, qi, TQ) < 0
            out = []
            for hh in range(2):
                pre, cu, dq = carry[hh]
                lanes = _head_lanes(hh)
                k = k_ref[pl.ds(r0, B), lanes]
                v = v_ref[pl.ds(r0, B), lanes]
                lb, l1 = _log_sigmoids_fast(_dot_nt(qs[hh], k))
                if masked:
                    l1 = jnp.where(strict, l1, 0.0)
                sums = _hi_lo_dot(l1, t_prefix)
                a = jnp.exp(lb + (ltot[hh] - pre - sums[:, :B]))
                if masked:
                    a = jnp.where(strict, a, 0.0)
                u = a * _dot_nt(dob[hh], v)
                usums = _hi_lo_dot(u, t_before)
                dz = u - (u + cu + usums[:, :B]) * jnp.exp(lb)
                if masked:
                    dz = jnp.where(strict, dz, 0.0)
                dzb = dz.astype(BF16)
                dk_acc[pl.ds(r0, B), lanes] += _dot_tn(dzb, qs[hh])
                dv_acc[pl.ds(r0, B), lanes] += _dot_tn(a.astype(BF16), dob[hh])
                out.append((pre + sums[:, B:], cu + usums[:, B:], dq + _dot(dzb, k)))
            return tuple(out)

        zero = (jnp.zeros((TQ, B), F32), jnp.zeros((TQ, B), F32), jnp.zeros((TQ, HEAD_DIM), F32))
        first = jnp.clip(first_ref[pl.program_id(0), qi].astype(jnp.int32), 0, qi * R)
        carry = lax.fori_loop(first, qi * R, lambda j, cr: tile(j, cr, False), (zero, zero))
        carry = lax.fori_loop(0, R, lambda it, cr: tile(qi * R + it, cr, True), carry)
        for hh in range(2):
            dq_ref[:, _head_lanes(hh)] = (carry[hh][2] * ATT_SCALE).astype(BF16)

        @pl.when(qi == NQ - 1)
        def _():
            dk_ref[...] = dk_acc[...].astype(BF16)
            dv_ref[...] = dv_acc[...].astype(BF16)

        end()

    out = jax.ShapeDtypeStruct((S, D), BF16)
    return _att_call(
        body, name, (NP, NQ), [pl.BlockSpec(memory_space=pltpu.SMEM), q_spec, k_spec, v_spec, q_spec, stat_spec],
        [q_spec, seq_spec, seq_spec], [out, out, out],
        [pltpu.VMEM((S, LANES), F32), pltpu.VMEM((S, LANES), F32)], [first, qkv, qkv, qkv, do, lt], side,
        ("arbitrary", "arbitrary"))


def _fox_specs(S, D, TQ):
    row_spec = pl.BlockSpec((2 * SUBLANES, TQ), lambda p, i: (p, i))
    keyb_spec = pl.BlockSpec((S, 2 * ATT_BLOCK), lambda p, i: (0, p))
    return row_spec, keyb_spec


def _key_gt_query_t(j, qi, TQ):
    key = lax.broadcasted_iota(jnp.int32, (ATT_BLOCK, TQ), 0)
    qry = lax.broadcasted_iota(jnp.int32, (ATT_BLOCK, TQ), 1)
    return key - qry + (j * ATT_BLOCK - qi * TQ) > 0


def _fox_fwd(qkv, cq_rows, ck_b, name, side=None):
    S, D3 = qkv.shape
    D = D3 // 3
    B, TQ = ATT_BLOCK, FOX_Q_TILE
    R = TQ // B
    NP = D // LANES
    NQ = S // TQ
    q_spec, k_spec, v_spec, _, _, _ = _att_specs(S, D, TQ)
    row_spec, keyb_spec = _fox_specs(S, D, TQ)

    def body(*refs):
        (q_ref, k_ref, v_ref, cq_ref, ck_ref), (o_ref, lse_ref), _, parts = _side_parts(side, refs, 5, 2)
        begin, end = _side_hooks(side, parts, *_grid_ends(NP, NQ))
        begin()
        qi = pl.program_id(1)
        qs = [_scaled_q(q_ref, hh) for hh in range(2)]
        cq = [cq_ref[hh * SUBLANES:hh * SUBLANES + 1, :] for hh in range(2)]

        def tile(j, carry, masked):
            r0 = pl.multiple_of(j * B, B)
            if masked:
                hidden = _key_gt_query_t(j, qi, TQ)
            out = []
            for hh in range(2):
                m, lsum, acc = carry[hh]
                k = k_ref[pl.ds(r0, B), _head_lanes(hh)]
                v = v_ref[pl.ds(r0, B), _head_lanes(hh)]
                ck = jnp.tile(ck_ref[pl.ds(r0, B), hh * B:(hh + 1) * B], (1, R))
                s = _dot_nt(k, qs[hh]) + (cq[hh] - ck)
                if masked:
                    s = jnp.where(hidden, NEG_BIG, s)
                m_new = jnp.maximum(m, jnp.max(s, axis=0, keepdims=True))
                p = jnp.exp(s - m_new)
                alpha = jnp.exp(m - m_new)
                out.append((m_new, alpha * lsum + jnp.sum(p, axis=0, keepdims=True),
                            alpha * acc + _dot_tn(v, p.astype(BF16))))
            return tuple(out)

        zero = (jnp.full((1, TQ), NEG_BIG, F32), jnp.zeros((1, TQ), F32), jnp.zeros((HEAD_DIM, TQ), F32))
        carry = lax.fori_loop(0, qi * R, lambda j, cr: tile(j, cr, False), (zero, zero))
        carry = lax.fori_loop(0, R, lambda it, cr: tile(qi * R + it, cr, True), carry)
        o_t = jnp.concatenate([carry[hh][2] * (1.0 / carry[hh][1]) for hh in range(2)], axis=0)
        o_ref[...] = o_t.T
        for hh in range(2):
            m, lsum, _ = carry[hh]
            lse_ref[hh * SUBLANES:(hh + 1) * SUBLANES, :] = jnp.broadcast_to(m + jnp.log(lsum), (SUBLANES, TQ))
        end()

    return _att_call(
        body, name, (NP, NQ), [q_spec, k_spec, v_spec, row_spec, keyb_spec], [q_spec, row_spec],
        [jax.ShapeDtypeStruct((S, D), F32), jax.ShapeDtypeStruct((2 * NP * SUBLANES, S), F32)], [],
        [qkv, qkv, qkv, cq_rows, ck_b], side, ("arbitrary", "arbitrary"))


def _fox_bwd(qkv, do, o, lse_rows, cq_rows, ck_b, name, side=None):
    S, D3 = qkv.shape
    D = D3 // 3
    B, TQ = ATT_BLOCK, FOX_Q_TILE
    R = TQ // B
    NP = D // LANES
    NQ = S // TQ
    q_spec, k_spec, v_spec, _, seq_spec, _ = _att_specs(S, D, TQ)
    row_spec, keyb_spec = _fox_specs(S, D, TQ)

    def body(*refs):
        ins, outs, (dk_acc, dv_acc), parts = _side_parts(side, refs, 8, 5)
        q_ref, k_ref, v_ref, do_ref, o_ref, lse_ref, cq_ref, ck_ref = ins
        dq_ref, dk_ref, dv_ref, dcq_ref, dck_ref = outs
        begin, end = _side_hooks(side, parts, *_grid_ends(NP, NQ))
        begin()
        qi = pl.program_id(1)

        @pl.when(qi == 0)
        def _():
            dk_acc[...] = jnp.zeros_like(dk_acc)
            dv_acc[...] = jnp.zeros_like(dv_acc)
            dck_ref[...] = jnp.zeros_like(dck_ref)

        qs = [_scaled_q(q_ref, hh) for hh in range(2)]
        dob = [do_ref[:, _head_lanes(hh)].astype(BF16) for hh in range(2)]
        prod_t = (do_ref[...] * o_ref[...]).T
        delta = [jnp.sum(prod_t[hh * HEAD_DIM:(hh + 1) * HEAD_DIM], axis=0, keepdims=True) for hh in range(2)]
        cq = [cq_ref[hh * SUBLANES:hh * SUBLANES + 1, :] for hh in range(2)]
        lse = [lse_ref[hh * SUBLANES:hh * SUBLANES + 1, :] for hh in range(2)]

        def tile(j, carry, masked):
            r0 = pl.multiple_of(j * B, B)
            if masked:
                hidden = _key_gt_query_t(j, qi, TQ)
            out = []
            for hh in range(2):
                dq, keysum = carry[hh]
                lanes = _head_lanes(hh)
                k = k_ref[pl.ds(r0, B), lanes]
                v = v_ref[pl.ds(r0, B), lanes]
                ck = jnp.tile(ck_ref[pl.ds(r0, B), hh * B:(hh + 1) * B], (1, R))
                p = jnp.exp(_dot_nt(k, qs[hh]) + (cq[hh] - ck) - lse[hh])
                if masked:
                    p = jnp.where(hidden, 0.0, p)
                ds = p * (_dot_nt(v, dob[hh]) - delta[hh])
                dsb = ds.astype(BF16)
                dk_acc[pl.ds(r0, B), lanes] += _dot(dsb, qs[hh])
                dv_acc[pl.ds(r0, B), lanes] += _dot(p.astype(BF16), dob[hh])
                qsum = jnp.sum(ds, axis=1, keepdims=True)
                dck_ref[pl.ds(r0, B), hh * B:(hh + 1) * B] -= jnp.broadcast_to(qsum, (B, B))
                out.append((dq + _dot_tn(k, dsb), keysum + jnp.sum(ds, axis=0, keepdims=True)))
            return tuple(out)

        zero = (jnp.zeros((HEAD_DIM, TQ), F32), jnp.zeros((1, TQ), F32))
        carry = lax.fori_loop(0, qi * R, lambda j, cr: tile(j, cr, False), (zero, zero))
        carry = lax.fori_loop(0, R, lambda it, cr: tile(qi * R + it, cr, True), carry)
        dq_t = jnp.concatenate([carry[hh][0] for hh in range(2)], axis=0)
        dq_ref[...] = (dq_t.T * ATT_SCALE).astype(BF16)
        for hh in range(2):
            dcq_ref[hh * SUBLANES:(hh + 1) * SUBLANES, :] = jnp.broadcast_to(carry[hh][1], (SUBLANES, TQ))

        @pl.when(qi == NQ - 1)
        def _():
            dk_ref[...] = dk_acc[...].astype(BF16)
            dv_ref[...] = dv_acc[...].astype(BF16)

        end()

    out = jax.ShapeDtypeStruct((S, D), BF16)
    return _att_call(
        body, name, (NP, NQ), [q_spec, k_spec, v_spec, q_spec, q_spec, row_spec, row_spec, keyb_spec],
        [q_spec, seq_spec, seq_spec, row_spec, keyb_spec],
        [out, out, out, jax.ShapeDtypeStruct((2 * NP * SUBLANES, S), F32), jax.ShapeDtypeStruct((S, 2 * NP * B), F32)],
        [pltpu.VMEM((S, LANES), F32), pltpu.VMEM((S, LANES), F32)],
        [qkv, qkv, qkv, do, o, lse_rows, cq_rows, ck_b], side, ("arbitrary", "arbitrary"))


def _forget_cumsum(f3, bias, name):
    NQ, NH, B = f3.shape

    def body(f_ref, b_ref, cum_ref):
        row = lax.broadcasted_iota(jnp.int32, (B, B), 0)
        col = lax.broadcasted_iota(jnp.int32, (B, B), 1)
        t_incl = jnp.where(row <= col, 1.0, 0.0).astype(BF16)

        def step(b, carry):
            lf, _ = _log_sigmoids(f_ref[b] + b_ref[...])
            cum = _split_dot(lf, t_incl, 3) + carry
            cum_ref[b] = cum
            return jnp.broadcast_to(cum[:, B - 1:B], (NH, B))

        lax.fori_loop(0, NQ, step, jnp.zeros((NH, B), F32))

    return pl.pallas_call(
        body, name=name, in_specs=[VMEM_WHOLE, VMEM_WHOLE], out_specs=VMEM_WHOLE,
        out_shape=jax.ShapeDtypeStruct((NQ, NH, B), F32), compiler_params=_params(),
    )(f3, bias)


def _forget_cumsum_bwd(dcum3, f3, bias, name):
    NQ, NH, B = f3.shape

    def body(d_ref, f_ref, b_ref, df_ref, tot_ref):
        row = lax.broadcasted_iota(jnp.int32, (B, B), 0)
        col = lax.broadcasted_iota(jnp.int32, (B, B), 1)
        t_rev = jnp.where(row >= col, 1.0, 0.0).astype(BF16)

        def step(it, carry):
            run, tot = carry
            b = NQ - 1 - it
            dlf = _split_dot(d_ref[b], t_rev, 3) + run
            f = f_ref[b] + b_ref[...]
            df = dlf * (1.0 / (1.0 + jnp.exp(f)))
            df_ref[b] = df
            return jnp.broadcast_to(dlf[:, 0:1], (NH, B)), tot + df

        _, tot = lax.fori_loop(0, NQ, step, (jnp.zeros((NH, B), F32), jnp.zeros((NH, B), F32)))
        tot_ref[...] = _split_dot(tot, jnp.ones((B, B), BF16), 3)

    return pl.pallas_call(
        body, name=name, in_specs=[VMEM_WHOLE, VMEM_WHOLE, VMEM_WHOLE], out_specs=[VMEM_WHOLE, VMEM_WHOLE],
        out_shape=[jax.ShapeDtypeStruct((NQ, NH, B), F32), jax.ShapeDtypeStruct((NH, B), F32)],
        compiler_params=_params(),
    )(dcum3, f3, bias)


def _silu(x, name):
    def body(x_ref, o_ref):
        v = x_ref[...]
        o_ref[...] = v * (1.0 / (1.0 + jnp.exp(-v)))

    return pl.pallas_call(body, name=name, in_specs=[VMEM_WHOLE], out_specs=VMEM_WHOLE,
                          out_shape=jax.ShapeDtypeStruct(x.shape, F32), compiler_params=_params())(x)


def _sum_leading(x, name):
    N, R, C = x.shape
    T = _pick(R, (256, 128, 64, 32, 16, 8))

    def body(x_ref, o_ref):
        acc = x_ref[0].astype(F32)
        for r in range(1, N):
            acc = acc + x_ref[r].astype(F32)
        o_ref[...] = acc

    return pl.pallas_call(
        body, name=name, grid=(R // T,), in_specs=[pl.BlockSpec((N, T, C), lambda i: (0, i, 0))],
        out_specs=pl.BlockSpec((T, C), lambda i: (i, 0)), out_shape=jax.ShapeDtypeStruct((R, C), F32),
        compiler_params=_params(),
    )(x)


def _adamw(w, g, m, v, name):
    shape = w.shape
    C = shape[-1]
    R = w.size // C
    T = R
    for cand in (512, 256, 128, 64, 32, 16, 8):
        if R % cand == 0 and cand * C * 4 <= (1 << 20):
            T = cand
            break
    spec = pl.BlockSpec((T, C), lambda i: (i, 0))
    c1 = 1.0 / (1.0 - ADAM_B1 ** ADAM_STEP)
    c2 = 1.0 / (1.0 - ADAM_B2 ** ADAM_STEP)

    def body(w_ref, g_ref, m_ref, v_ref, d_ref, nm_ref, nv_ref):
        gv = g_ref[...]
        nm = ADAM_B1 * m_ref[...] + (1.0 - ADAM_B1) * gv
        nv = ADAM_B2 * v_ref[...] + (1.0 - ADAM_B2) * (gv * gv)
        nm_ref[...] = nm
        nv_ref[...] = nv
        d_ref[...] = -ADAM_LR * ((nm * c1) / (jnp.sqrt(nv * c2) + ADAM_EPS) + ADAM_WD * w_ref[...])

    out = jax.ShapeDtypeStruct((R, C), F32)
    d, nm, nv = pl.pallas_call(
        body, name=name, grid=(R // T,), in_specs=[spec] * 4, out_specs=[spec] * 3, out_shape=[out] * 3,
        compiler_params=_params(),
    )(w.reshape(R, C), g.reshape(R, C), m.reshape(R, C), v.reshape(R, C))
    return d.reshape(shape), nm.reshape(shape), nv.reshape(shape)


def _mesh_pos():
    return lax.axis_index("x"), lax.axis_index("y"), lax.axis_index("c")


def _all_gather_small(x2d, name):
    m_per, n = x2d.shape

    def body(x_ref, out_ref, send_sems, recv_sems, local_sem):
        x, y, c = _mesh_pos()
        me, sibling = (x, y, c), (x, y, 1 - c)
        chips = [(1 - x, y), (x, 1 - y), (1 - x, 1 - y)]

        def rows(px, py, pc):
            return out_ref.at[pl.ds((4 * px + 2 * py + pc) * m_per, m_per), :]

        def copy(k, block, to, src=None):
            return pltpu.make_async_remote_copy(
                src_ref=rows(*block) if src is None else src, dst_ref=rows(*block),
                send_sem=send_sems.at[k], recv_sem=recv_sems.at[k], device_id=to, device_id_type=MESH)

        mine = pltpu.make_async_copy(x_ref, rows(*me), local_sem)
        mine.start()
        first = [copy(0, me, sibling, src=x_ref)]
        first += [copy(1 + j, me, (*chip, c), src=x_ref) for j, chip in enumerate(chips)]
        for cp in first:
            cp.start()
        passed = [copy(4 + j, (*chip, c), sibling) for j, chip in enumerate(chips)]
        for j, chip in enumerate(chips):
            copy(1 + j, (*chip, c), me).wait_recv()
            passed[j].start()
        copy(0, sibling, me).wait_recv()
        for j, chip in enumerate(chips):
            copy(4 + j, (*chip, 1 - c), me).wait_recv()
        for cp in first + passed:
            cp.wait_send()
        mine.wait()

    return pl.pallas_call(
        body, name=name, out_shape=jax.ShapeDtypeStruct((N_DEV * m_per, n), x2d.dtype),
        in_specs=[VMEM_WHOLE], out_specs=VMEM_WHOLE,
        scratch_shapes=[pltpu.SemaphoreType.DMA((7,)), pltpu.SemaphoreType.DMA((7,)), pltpu.SemaphoreType.DMA],
        compiler_params=_params(),
    )(x2d)


class _Side:
    def __init__(self, inputs, out_shapes, sems, start, wait):
        self.inputs, self.out_shapes, self.sems, self.start, self.wait = inputs, out_shapes, sems, start, wait


def _join_sides(sides):
    sides = [s for s in sides if s is not None]
    if not sides:
        return None
    bounds_in, bounds_out, bounds_sem = [0], [0], [0]
    for s in sides:
        bounds_in.append(bounds_in[-1] + len(s.inputs))
        bounds_out.append(bounds_out[-1] + len(s.out_shapes))
        bounds_sem.append(bounds_sem[-1] + len(s.sems))

    def each(method):
        def run(ins, outs, sems):
            for t, s in enumerate(sides):
                getattr(s, method)(ins[bounds_in[t]:bounds_in[t + 1]], outs[bounds_out[t]:bounds_out[t + 1]],
                                   sems[bounds_sem[t]:bounds_sem[t + 1]])
        return run

    return _Side([a for s in sides for a in s.inputs], [o for s in sides for o in s.out_shapes],
                 [m for s in sides for m in s.sems], each("start"), each("wait"))


def _side_parts(side, refs, n_in, n_out):
    if side is None:
        return refs[:n_in], refs[n_in:n_in + n_out], refs[n_in + n_out:], None
    si, so, ss = len(side.inputs), len(side.out_shapes), len(side.sems)
    a = n_in + si
    b = a + n_out + so
    ins, side_in = refs[:n_in], refs[n_in:a]
    outs, side_out = refs[a:a + n_out], refs[a + n_out:b]
    scratch, side_sems = refs[b:len(refs) - ss], refs[len(refs) - ss:]
    return ins, outs, scratch, (side_in, side_out, side_sems)


def _side_hooks(side, parts, first, last):
    if side is None:
        return lambda: None, lambda: None

    def begin():
        @pl.when(first)
        def _():
            side.start(*parts)

    def end():
        @pl.when(last)
        def _():
            side.wait(*parts)

    return begin, end


def _run_side(side, name):
    n_in = len(side.inputs)
    n_out = len(side.out_shapes)

    def body(*refs):
        parts = (refs[:n_in], refs[n_in:n_in + n_out], refs[n_in + n_out:])
        side.start(*parts)
        side.wait(*parts)

    return pl.pallas_call(
        body, name=name, out_shape=side.out_shapes, in_specs=[ANY] * n_in, out_specs=[ANY] * n_out,
        scratch_shapes=side.sems, compiler_params=_params(),
    )(*side.inputs)


def _gather_side(ws):
    n = len(ws)

    def copies(ins, outs, sems):
        send_sems, recv_sems, local_sems = sems
        x, y, c = _mesh_pos()
        k_me = 2 * x + y
        chips = [(1 - x, y), (x, 1 - y), (1 - x, 1 - y)]

        def remote(i, j, slot):
            px, py = chips[j]
            return pltpu.make_async_remote_copy(
                src_ref=ins[i], dst_ref=outs[i].at[slot], send_sem=send_sems.at[i, j],
                recv_sem=recv_sems.at[i, j], device_id=(px, py, c), device_id_type=MESH)

        local = [pltpu.make_async_copy(ins[i], outs[i].at[k_me], local_sems.at[i]) for i in range(n)]
        return remote, local, k_me, chips

    def start(ins, outs, sems):
        remote, local, k_me, _ = copies(ins, outs, sems)
        for i in range(n):
            local[i].start()
            for j in range(3):
                remote(i, j, k_me).start()

    def wait(ins, outs, sems):
        remote, local, k_me, chips = copies(ins, outs, sems)
        for i in range(n):
            for j, (px, py) in enumerate(chips):
                remote(i, j, 2 * px + py).wait_recv()
        for i in range(n):
            for j in range(3):
                remote(i, j, k_me).wait_send()
            local[i].wait()

    return _Side(list(ws), [jax.ShapeDtypeStruct((N_CHIPS,) + w.shape, w.dtype) for w in ws],
                 [pltpu.SemaphoreType.DMA((n, 3)), pltpu.SemaphoreType.DMA((n, 3)), pltpu.SemaphoreType.DMA((n,))],
                 start, wait)


def _scatter_side(gs):
    n = len(gs)
    halves = [g.shape[1] // 2 for g in gs]

    def copies(ins, outs, sems):
        send_sems, recv_sems, local_sems = sems
        x, y, c = _mesh_pos()

        def flip(v, bit):
            return 1 - v if bit else v

        def piece(i, px, py, pc):
            return ins[i].at[2 * px + py, pl.ds(pc * halves[i], halves[i])]

        def remote(i, r):
            px, py, pc = flip(x, r & 4), flip(y, r & 2), flip(c, r & 1)
            return pltpu.make_async_remote_copy(
                src_ref=piece(i, px, py, pc), dst_ref=outs[i].at[r], send_sem=send_sems.at[i, r - 1],
                recv_sem=recv_sems.at[i, r - 1], device_id=(px, py, pc), device_id_type=MESH)

        local = [pltpu.make_async_copy(piece(i, x, y, c), outs[i].at[0], local_sems.at[i]) for i in range(n)]
        return remote, local

    def start(ins, outs, sems):
        remote, local = copies(ins, outs, sems)
        for i in range(n):
            local[i].start()
            for r in range(1, N_DEV):
                remote(i, r).start()

    def wait(ins, outs, sems):
        remote, local = copies(ins, outs, sems)
        for i in range(n):
            for r in range(1, N_DEV):
                remote(i, r).wait_recv()
        for i in range(n):
            for r in range(1, N_DEV):
                remote(i, r).wait_send()
            local[i].wait()

    return _Side(list(gs), [jax.ShapeDtypeStruct((N_DEV, h) + g.shape[2:], g.dtype) for g, h in zip(gs, halves)],
                 [pltpu.SemaphoreType.DMA((n, N_DEV - 1)), pltpu.SemaphoreType.DMA((n, N_DEV - 1)),
                  pltpu.SemaphoreType.DMA((n,))], start, wait)


def _swap_side(hs):
    n = len(hs)

    def copies(ins, outs, sems):
        send_sems, recv_sems, local_sems = sems
        x, y, c = _mesh_pos()

        def remote(i, slot):
            return pltpu.make_async_remote_copy(
                src_ref=ins[i], dst_ref=outs[i].at[slot], send_sem=send_sems.at[i], recv_sem=recv_sems.at[i],
                device_id=(x, y, 1 - c), device_id_type=MESH)

        local = [pltpu.make_async_copy(ins[i], outs[i].at[c], local_sems.at[i]) for i in range(n)]
        return remote, local, c

    def start(ins, outs, sems):
        remote, local, c = copies(ins, outs, sems)
        for i in range(n):
            local[i].start()
            remote(i, c).start()

    def wait(ins, outs, sems):
        remote, local, c = copies(ins, outs, sems)
        for i in range(n):
            remote(i, 1 - c).wait_recv()
        for i in range(n):
            remote(i, c).wait_send()
            local[i].wait()

    return _Side(list(hs), [jax.ShapeDtypeStruct((2,) + h.shape, h.dtype) for h in hs],
                 [pltpu.SemaphoreType.DMA((n,)), pltpu.SemaphoreType.DMA((n,)), pltpu.SemaphoreType.DMA((n,))],
                 start, wait)


def _pad_rows(a, rows):
    return jnp.pad(a, ((0, rows - a.shape[0]), (0, 0)))


def kernel(x, c, w_mod, b_mod, g_mix_pre, g_mix_post, w_qkv, w_o, w_fg, b_fg, g_ffn_pre, g_ffn_post, w_ffn_gate, w_ffn_up, w_conv, b_conv, w_ffn_down, loss_target, m_w_mod, m_b_mod, m_g_mix_pre, m_g_mix_post, m_w_qkv, m_w_o, m_w_fg, m_b_fg, m_g_ffn_pre, m_g_ffn_post, m_w_ffn_gate, m_w_ffn_up, m_w_conv, m_b_conv, m_w_ffn_down, v_w_mod, v_b_mod, v_g_mix_pre, v_g_mix_post, v_w_qkv, v_w_o, v_w_fg, v_b_fg, v_g_ffn_pre, v_g_ffn_post, v_w_ffn_gate, v_w_ffn_up, v_w_conv, v_b_conv, v_w_ffn_down):
    xs = x[0]
    target = loss_target[0]
    S, D = xs.shape
    L = w_mod.shape[0]
    LF = w_fg.shape[0]
    MS = w_mod.shape[2]
    QS = w_qkv.shape[2]
    OS = w_o.shape[1]
    FS = w_ffn_gate.shape[2]
    F = N_CHIPS * FS
    NH = D // HEAD_DIM
    B = ATT_BLOCK
    NQ = S // B
    ax, ay, ac = _mesh_pos()
    k_me = 2 * ax + ay
    b_me = 4 * ax + 2 * ay + ac

    conv_rows = -(-(L * 3 * FS) // D)
    conv_rows = -(-conv_rows // SUBLANES) * SUBLANES
    conv_flat = jnp.pad(w_conv.reshape(-1), (0, conv_rows * D - L * 3 * FS)).reshape(conv_rows, D)
    first = jnp.concatenate([_pad_rows(c, SUBLANES), conv_flat], axis=0)
    first_all = _all_gather_small(first, "ag_cond").reshape(N_DEV, SUBLANES + conv_rows, D)
    c_all = first_all[:, 0, :]
    conv_all = first_all[0::2, SUBLANES:, :].reshape(N_CHIPS, -1)[:, :L * 3 * FS]
    w_conv_full = conv_all.reshape(N_CHIPS, L, 3, FS).transpose(1, 2, 0, 3).reshape(L, 3, F)
    c_act = _silu(c_all, "silu_c")

    mod_part = jnp.concatenate(
        [_mm(c_act, w_mod[l], "nn", F32, "mm_mod", tm=N_DEV, tn=MS, tk=D) for l in range(L)], axis=1)
    mod_all = _all_gather_small(mod_part, "ag_mod").reshape(N_CHIPS, 2, N_DEV, L, MS)[:, 0]
    mod_mine = lax.dynamic_index_in_dim(mod_all, b_me, axis=1, keepdims=False)
    mod = mod_mine.transpose(1, 0, 2).reshape(L, N_MOD * D) + b_mod

    shards_b = [w.astype(BF16) for w in (w_qkv, w_o, w_ffn_gate, w_ffn_up, w_ffn_down)]
    w_fg_b = w_fg.astype(BF16)

    def layer_shards(l):
        ws = [w[l] for w in shards_b]
        return ws + [w_fg_b[l // 2]] if l % 2 == 1 else ws

    def relayout(gathered):
        def cols(gth):
            return gth.transpose(1, 0, 2).reshape(D, -1, 1, LANES)

        gq, go, gg, gu_, gd = gathered[:5]
        wts = dict(qkv=gq.transpose(1, 0, 2).reshape(D, 3 * D), o=go.reshape(D, D),
                   gu=jnp.concatenate([cols(gg), cols(gu_)], axis=2).reshape(D, 2 * F),
                   d=gd.reshape(F, D))
        if len(gathered) > 5:
            wts["fg"] = jnp.pad(gathered[5].reshape(D, NH), ((0, 0), (0, LANES - NH)))
        return wts

    W = [None] * L
    W[0] = relayout(_run_side(_gather_side(layer_shards(0)), "ag_weights_first"))

    def vec(a):
        return a.reshape(1, -1)

    saved = []
    xcur = xs
    for l in range(L):
        sh_a, sc_a, gt_a, sh_f, sc_f, gt_f = [vec(mod[l, j * D:(j + 1) * D]) for j in range(N_MOD)]
        is_fox = l % 2 == 1
        jf = l // 2
        h1 = _norm_mod(xcur, vec(g_mix_pre[l]), sc_a, sh_a, "norm_mod")
        qkv = _mm(h1, W[l]["qkv"], "nn", BF16, "mm_qkv")
        next_weights = _gather_side(layer_shards(l + 1)) if l + 1 < L else None
        if is_fox:
            flog = _mm(h1, W[l]["fg"], "nn", F32, "mm_fg")[:, :NH]
            f3 = flog.reshape(NQ, B, NH).transpose(0, 2, 1)
            bias = b_fg[jf].reshape(NH, 1)
            cum3 = _forget_cumsum(f3, bias, "forget_cumsum")
            cum_sn = cum3.transpose(0, 2, 1).reshape(S, NH)
            ck_b = jnp.repeat(cum_sn, B, axis=1)
            cq_rows = jnp.repeat(cum_sn.T, SUBLANES, axis=0)
            (o, stat), gathered = _fox_fwd(qkv, cq_rows, ck_b, "fox_fwd", next_weights)
            extra = (f3, bias, cq_rows, ck_b)
        else:
            (o, stat, extra), gathered = _sb_fwd(qkv, "sb_fwd", next_weights)
        if next_weights is not None:
            W[l + 1] = relayout(gathered)
        p = _mm(o, W[l]["o"], "nn", F32, "mm_o")
        x1 = _post_res(xcur, p, vec(g_mix_post[l]), gt_a, "post_res")
        h2 = _norm_mod(x1, vec(g_ffn_pre[l]), sc_f, sh_f, "norm_mod")
        gu = _mm(h2, W[l]["gu"], "nn", F32, "mm_gu")
        wc = w_conv_full[l]
        bc = vec(b_conv[l])
        a = _conv_gate(gu, wc, bc, "conv_gate")
        yv = _mm(a, W[l]["d"], "nn", F32, "mm_down")
        x2 = _post_res(x1, yv, vec(g_ffn_post[l]), gt_f, "post_res")
        saved.append(dict(x0=xcur, h1=h1, qkv=qkv, o=o, stat=stat, extra=extra, p=p, x1=x1, h2=h2, gu=gu, a=a,
                          y=yv, mods=(sh_a, sc_a, gt_a, sh_f, sc_f, gt_f), wc=wc, bc=bc))
        xcur = x2

    g, sq = _loss_head(xcur, target, "loss_head")
    loss_part = 0.5 * jnp.sum(sq) / D
    loss = lax.psum(loss_part, ("x", "y", "c"))

    dW_qkv, dW_o, dW_gu, dW_d = [None] * L, [None] * L, [None] * L, [None] * L
    dW_fg, db_fg = [None] * LF, [None] * LF
    dmod, dg_mix_pre, dg_mix_post, dg_ffn_pre, dg_ffn_post = [[None] * L for _ in range(5)]
    dw_conv, db_conv = [None] * L, [None] * L

    def grad_pieces(l):
        def cols(dw, width):
            return dw.reshape(dw.shape[0], N_CHIPS, width).transpose(1, 0, 2).astype(BF16)

        def rows(dw, height):
            return dw.reshape(N_CHIPS, height, dw.shape[1]).astype(BF16)

        dgu = dW_gu[l].reshape(D, F // LANES, 2, LANES)
        return [cols(dW_qkv[l], QS), rows(dW_o[l], OS), cols(dgu[:, :, 0].reshape(D, F), FS),
                cols(dgu[:, :, 1].reshape(D, F), FS), rows(dW_d[l], FS)]

    def reduce_pieces(recv):
        return [_sum_leading(r, f"sum_grad_pieces_{i}") for i, r in enumerate(recv)]

    def whole(swapped):
        return [o.reshape((-1,) + o.shape[2:]) for o in swapped]

    n_big = 5
    halves, grad_shards = [None] * L, [None] * L

    for l in reversed(range(L)):
        sv = saved[l]
        sh_a, sc_a, gt_a, sh_f, sc_f, gt_f = sv["mods"]
        is_fox = l % 2 == 1
        jf = l // 2
        dy, dgp_f, dgt_f = _post_res_bwd(sv["y"], vec(g_ffn_post[l]), gt_f, g, "post_res_bwd")
        da = _mm(dy, W[l]["d"], "nt", F32, "mm_da")
        dW_d[l] = _mm(sv["a"], dy, "tn", F32, "mm_dwd")
        dgu, dwc, dbc = _conv_gate_bwd(sv["gu"], sv["wc"], sv["bc"], da, "conv_gate_bwd")
        dh2 = _mm(dgu, W[l]["gu"], "nt", F32, "mm_dh2")
        dW_gu[l] = _mm(sv["h2"], dgu, "tn", F32, "mm_dwgu")
        g, dg_f, dsc_f, dsh_f = _norm_mod_bwd(sv["x1"], vec(g_ffn_pre[l]), sc_f, sh_f, [dh2], g, "norm_mod_bwd")
        dp, dgp_a, dgt_a = _post_res_bwd(sv["p"], vec(g_mix_post[l]), gt_a, g, "post_res_bwd")
        do = _mm(dp, W[l]["o"], "nt", F32, "mm_do")
        dW_o[l] = _mm(sv["o"], dp, "tn", F32, "mm_dwo")
        to_scatter = l + 1 if l + 1 < L else None
        to_swap = l + 2 if l + 2 < L else None
        side = _join_sides([_scatter_side(grad_pieces(to_scatter)) if to_scatter is not None else None,
                            _swap_side(halves[to_swap]) if to_swap is not None else None])
        if is_fox:
            f3, bias, cq_rows, ck_b = sv["extra"]
            (dq, dk, dv, dcq_rows, dck_b), side_outs = _fox_bwd(sv["qkv"], do, sv["o"], sv["stat"], cq_rows, ck_b,
                                                                "fox_bwd", side)
            dcum_sn = dcq_rows[::SUBLANES].T + dck_b[:, ::B]
            dcum3 = dcum_sn.reshape(NQ, B, NH).transpose(0, 2, 1)
            df3, df_tot = _forget_cumsum_bwd(dcum3, f3, bias, "forget_cumsum_bwd")
            df = df3.transpose(0, 2, 1).reshape(S, NH)
            dfp = jnp.pad(df, ((0, 0), (0, LANES - NH)))
            dW_fg[jf] = _mm(sv["h1"], dfp, "tn", F32, "mm_dwfg")[:, :NH]
            db_fg[jf] = df_tot[:, 0]
            dh_extra = [_mm(dfp, W[l]["fg"], "nt", F32, "mm_dh1f")]
        else:
            (dq, dk, dv), side_outs = _sb_bwd(sv["qkv"], do, sv["stat"], sv["extra"], "sb_bwd", side)
            dh_extra = []
        if to_scatter is not None:
            halves[to_scatter] = reduce_pieces(side_outs[:n_big])
        if to_swap is not None:
            grad_shards[to_swap] = whole(side_outs[-n_big:])
        dqkv = jnp.concatenate([dq, dk, dv], axis=1)
        dh1 = _mm(dqkv, W[l]["qkv"], "nt", F32, "mm_dh1")
        dW_qkv[l] = _mm(sv["h1"], dqkv, "tn", F32, "mm_dwqkv")
        g, dg_a, dsc_a, dsh_a = _norm_mod_bwd(sv["x0"], vec(g_mix_pre[l]), sc_a, sh_a, [dh1] + dh_extra, g,
                                              "norm_mod_bwd")
        dmod[l] = jnp.concatenate([dsh_a, dsc_a, dgt_a, dsh_f, dsc_f, dgt_f], axis=1)[0]
        dg_mix_pre[l], dg_mix_post[l], dg_ffn_pre[l], dg_ffn_post[l] = dg_a[0], dgp_a[0], dg_f[0], dgp_f[0]
        dw_conv[l], db_conv[l] = dwc, dbc[0]
    grad_x = g[None]

    pieces = [jnp.stack(dmod), jnp.stack(dg_mix_pre), jnp.stack(dg_mix_post), jnp.stack(dg_ffn_pre),
              jnp.stack(dg_ffn_post), jnp.stack(db_fg), jnp.stack(dW_fg), jnp.stack(dw_conv), jnp.stack(db_conv)]
    sizes = [pc.size for pc in pieces]
    total = sum(sizes)
    pack_rows = -(-total // (LANES * SUBLANES)) * SUBLANES
    pack = jnp.pad(jnp.concatenate([pc.reshape(-1) for pc in pieces]), (0, pack_rows * LANES - total))
    pack_all = _all_gather_small(pack.reshape(pack_rows, LANES), "ag_small_grads").reshape(N_DEV, pack_rows, LANES)
    small = _sum_leading(pack_all, "sum_small_grads").reshape(-1)
    offs = [0]
    for sz in sizes:
        offs.append(offs[-1] + sz)
    parts = [small[offs[i]:offs[i + 1]].reshape(pieces[i].shape) for i in range(len(pieces))]
    g_b_mod, g_g_mix_pre, g_g_mix_post, g_g_ffn_pre, g_g_ffn_post, g_b_fg, g_w_fg_full, g_w_conv_full, g_b_conv = parts
    g_w_fg = lax.dynamic_slice_in_dim(g_w_fg_full, k_me * OS, OS, axis=1)
    g_w_conv = lax.dynamic_slice_in_dim(g_w_conv_full, k_me * FS, FS, axis=2)
    dmod_all = pack_all.reshape(N_DEV, -1)[:, :L * N_MOD * D].reshape(N_DEV, L, N_CHIPS, MS)
    dmod_cols = lax.dynamic_index_in_dim(dmod_all, k_me, axis=2, keepdims=False).reshape(N_DEV, L * MS)
    g_w_mod = _mm(_pad_rows(c_act, LANES), _pad_rows(dmod_cols, LANES), "tn", F32, "mm_dwmod", tm=D, tn=MS, tk=LANES)
    g_w_mod = g_w_mod.reshape(D, L, MS).transpose(1, 0, 2)

    halves[0] = reduce_pieces(_run_side(_scatter_side(grad_pieces(0)), "rs_scatter_last"))
    unswapped = [l for l in range(L) if grad_shards[l] is None]
    swapped = _run_side(_join_sides([_swap_side(halves[l]) for l in unswapped]), "rs_swap_last")
    for t, l in enumerate(unswapped):
        grad_shards[l] = whole(swapped[t * n_big:(t + 1) * n_big])
    g_w_qkv, g_w_o, g_w_gate, g_w_up, g_w_down = [jnp.stack([grad_shards[l][i] for l in range(L)])
                                                  for i in range(n_big)]

    grads = [g_w_mod, g_b_mod, g_g_mix_pre, g_g_mix_post, g_w_qkv, g_w_o, g_w_fg, g_b_fg, g_g_ffn_pre,
             g_g_ffn_post, g_w_gate, g_w_up, g_w_conv, g_b_conv, g_w_down]
    weights = [w_mod, b_mod, g_mix_pre, g_mix_post, w_qkv, w_o, w_fg, b_fg, g_ffn_pre, g_ffn_post, w_ffn_gate,
               w_ffn_up, w_conv, b_conv, w_ffn_down]
    ms = [m_w_mod, m_b_mod, m_g_mix_pre, m_g_mix_post, m_w_qkv, m_w_o, m_w_fg, m_b_fg, m_g_ffn_pre, m_g_ffn_post,
          m_w_ffn_gate, m_w_ffn_up, m_w_conv, m_b_conv, m_w_ffn_down]
    vs = [v_w_mod, v_b_mod, v_g_mix_pre, v_g_mix_post, v_w_qkv, v_w_o, v_w_fg, v_b_fg, v_g_ffn_pre, v_g_ffn_post,
          v_w_ffn_gate, v_w_ffn_up, v_w_conv, v_b_conv, v_w_ffn_down]
    deltas, new_ms, new_vs = [], [], []
    for wv, gv, mv, vv in zip(weights, grads, ms, vs):
        d, nm, nv = _adamw(wv, gv, mv, vv, "adamw")
        deltas.append(d)
        new_ms.append(nm)
        new_vs.append(nv)
    return (loss, grad_x, *grads, *deltas, *new_ms, *new_vs)
```

```python
import functools

import jax
import jax.numpy as jnp
from jax import lax
from jax.experimental import pallas as pl
from jax.experimental.pallas import tpu as pltpu

F32 = jnp.float32
BF16 = jnp.bfloat16
MESH = pl.DeviceIdType.MESH

HEAD_DIM = 64
ATT_BLOCK = 128
SB_Q_TILE = 512
FOX_Q_TILE = 1024
SB_DEAD_LOG = -110.0
LANES = 128
SUBLANES = 8
RMS_EPS = 1e-6
N_MOD = 6
N_CHIPS = 4
N_DEV = 8
ADAM_LR = 0.001
ADAM_B1 = 0.9
ADAM_B2 = 0.999
ADAM_EPS = 1e-08
ADAM_WD = 0.01
ADAM_STEP = 10
VMEM_LIMIT_BYTES = 56 * 1024 * 1024
NEG_BIG = -1e30

ANY = pl.BlockSpec(memory_space=pl.ANY)
VMEM_WHOLE = pl.BlockSpec(memory_space=pltpu.VMEM)


def _params(**kw):
    return pltpu.CompilerParams(vmem_limit_bytes=VMEM_LIMIT_BYTES, **kw)


def _dot(a, b):
    return jnp.dot(a, b, preferred_element_type=F32)


def _dot_nt(a, b):
    return lax.dot_general(a, b, (((1,), (1,)), ((), ())), preferred_element_type=F32)


def _dot_tn(a, b):
    return lax.dot_general(a, b, (((0,), (0,)), ((), ())), preferred_element_type=F32)


def _split_dot(x, t, parts):
    acc = None
    rem = x
    for _ in range(parts):
        piece = rem.astype(BF16)
        rem = rem - piece.astype(F32)
        d = _dot(piece, t)
        acc = d if acc is None else acc + d
    return acc


def _pick(n, prefs):
    for p in prefs:
        if n % p == 0:
            return p
    return n


def _mm(a, b, dims, out_dtype, name, tm=None, tn=None, tk=None):
    if dims == "tn":
        K, M = a.shape
    else:
        M, K = a.shape
    N = b.shape[0] if dims == "nt" else b.shape[1]
    tm = tm or _pick(M, (512, 256, 128))
    tn = tn or _pick(N, (1536, 1408, 1024, 768, 512, 256, 128))
    tk = tk or _pick(K, (1024, 1408, 512, 256, 128))
    nk = K // tk
    grid = (M // tm, N // tn, nk)
    if dims == "tn":
        a_spec = pl.BlockSpec((tk, tm), lambda i, j, k: (k, i))
    else:
        a_spec = pl.BlockSpec((tm, tk), lambda i, j, k: (i, k))
    if dims == "nt":
        b_spec = pl.BlockSpec((tn, tk), lambda i, j, k: (j, k))
    else:
        b_spec = pl.BlockSpec((tk, tn), lambda i, j, k: (k, j))
    o_spec = pl.BlockSpec((tm, tn), lambda i, j, k: (i, j))

    def body(a_ref, b_ref, o_ref, *scratch):
        x = a_ref[...].astype(BF16)
        y = b_ref[...].astype(BF16)
        if dims == "nn":
            r = _dot(x, y)
        elif dims == "nt":
            r = _dot_nt(x, y)
        else:
            r = _dot_tn(x, y)
        if nk == 1:
            o_ref[...] = r.astype(out_dtype)
        else:
            acc = scratch[0]
            k = pl.program_id(2)

            @pl.when(k == 0)
            def _():
                acc[...] = r

            @pl.when(k > 0)
            def _():
                acc[...] += r

            @pl.when(k == nk - 1)
            def _():
                o_ref[...] = acc[...].astype(out_dtype)

    return pl.pallas_call(
        body,
        name=name,
        grid=grid,
        in_specs=[a_spec, b_spec],
        out_specs=o_spec,
        out_shape=jax.ShapeDtypeStruct((M, N), out_dtype),
        scratch_shapes=[pltpu.VMEM((tm, tn), F32)] if nk > 1 else [],
        compiler_params=_params(dimension_semantics=("parallel", "parallel", "arbitrary")),
    )(a, b)


def _row_tile(S):
    return _pick(S, (256, 128, 64, 32, 16, 8))


def _norm_mod(x, g, sc, sh, name):
    S, D = x.shape
    T = _row_tile(S)
    row = pl.BlockSpec((T, D), lambda i: (i, 0))
    vec = pl.BlockSpec((1, D), lambda i: (0, 0))

    def body(x_ref, g_ref, sc_ref, sh_ref, h_ref):
        xv = x_ref[...]
        r = lax.rsqrt(jnp.mean(xv * xv, axis=-1, keepdims=True) + RMS_EPS)
        n = (xv * r) * g_ref[...]
        h_ref[...] = (n * (1.0 + sc_ref[...]) + sh_ref[...]).astype(BF16)

    return pl.pallas_call(
        body, name=name, grid=(S // T,), in_specs=[row, vec, vec, vec], out_specs=row,
        out_shape=jax.ShapeDtypeStruct((S, D), BF16), compiler_params=_params(),
    )(x, g, sc, sh)


def _norm_mod_bwd(x, g, sc, sh, dhs, gres, name):
    S, D = x.shape
    T = _row_tile(S)
    n_dh = len(dhs)
    row = pl.BlockSpec((T, D), lambda i: (i, 0))
    vec = pl.BlockSpec((1, D), lambda i: (0, 0))

    def body(x_ref, g_ref, sc_ref, sh_ref, *refs):
        dh_refs = refs[:n_dh]
        gres_ref, dx_ref, dg_ref, dsc_ref, dsh_ref = refs[n_dh:]
        xv = x_ref[...]
        r = lax.rsqrt(jnp.mean(xv * xv, axis=-1, keepdims=True) + RMS_EPS)
        xn = xv * r
        n = xn * g_ref[...]
        dh = dh_refs[0][...]
        for extra in dh_refs[1:]:
            dh = dh + extra[...]
        dn = dh * (1.0 + sc_ref[...])
        dxn = dn * g_ref[...]
        dx = r * (dxn - xn * jnp.mean(dxn * xn, axis=-1, keepdims=True))
        dx_ref[...] = gres_ref[...] + dx

        @pl.when(pl.program_id(0) == 0)
        def _():
            dg_ref[...] = jnp.zeros_like(dg_ref)
            dsc_ref[...] = jnp.zeros_like(dsc_ref)
            dsh_ref[...] = jnp.zeros_like(dsh_ref)

        dg_ref[...] += jnp.sum(dn * xn, axis=0, keepdims=True)
        dsc_ref[...] += jnp.sum(dh * n, axis=0, keepdims=True)
        dsh_ref[...] += jnp.sum(dh, axis=0, keepdims=True)

    vshape = jax.ShapeDtypeStruct((1, D), F32)
    return pl.pallas_call(
        body, name=name, grid=(S // T,), in_specs=[row, vec, vec, vec] + [row] * (n_dh + 1),
        out_specs=[row, vec, vec, vec],
        out_shape=[jax.ShapeDtypeStruct((S, D), F32), vshape, vshape, vshape],
        compiler_params=_params(dimension_semantics=("arbitrary",)),
    )(x, g, sc, sh, *dhs, gres)


def _post_res(x, p, gp, gt, name):
    S, D = x.shape
    T = _row_tile(S)
    row = pl.BlockSpec((T, D), lambda i: (i, 0))
    vec = pl.BlockSpec((1, D), lambda i: (0, 0))

    def body(x_ref, p_ref, gp_ref, gt_ref, o_ref):
        pv = p_ref[...]
        r = lax.rsqrt(jnp.mean(pv * pv, axis=-1, keepdims=True) + RMS_EPS)
        o_ref[...] = x_ref[...] + gt_ref[...] * ((pv * r) * gp_ref[...])

    return pl.pallas_call(
        body, name=name, grid=(S // T,), in_specs=[row, row, vec, vec], out_specs=row,
        out_shape=jax.ShapeDtypeStruct((S, D), F32), compiler_params=_params(),
    )(x, p, gp, gt)


def _post_res_bwd(p, gp, gt, g, name):
    S, D = p.shape
    T = _row_tile(S)
    row = pl.BlockSpec((T, D), lambda i: (i, 0))
    vec = pl.BlockSpec((1, D), lambda i: (0, 0))

    def body(p_ref, gp_ref, gt_ref, g_ref, dp_ref, dgp_ref, dgt_ref):
        pv = p_ref[...]
        gv = g_ref[...]
        r = lax.rsqrt(jnp.mean(pv * pv, axis=-1, keepdims=True) + RMS_EPS)
        pn = pv * r
        n2 = pn * gp_ref[...]
        dn2 = gv * gt_ref[...]
        dpn = dn2 * gp_ref[...]
        dp = r * (dpn - pn * jnp.mean(dpn * pn, axis=-1, keepdims=True))
        dp_ref[...] = dp.astype(BF16)

        @pl.when(pl.program_id(0) == 0)
        def _():
            dgp_ref[...] = jnp.zeros_like(dgp_ref)
            dgt_ref[...] = jnp.zeros_like(dgt_ref)

        dgp_ref[...] += jnp.sum(dn2 * pn, axis=0, keepdims=True)
        dgt_ref[...] += jnp.sum(gv * n2, axis=0, keepdims=True)

    vshape = jax.ShapeDtypeStruct((1, D), F32)
    return pl.pallas_call(
        body, name=name, grid=(S // T,), in_specs=[row, vec, vec, row], out_specs=[row, vec, vec],
        out_shape=[jax.ShapeDtypeStruct((S, D), BF16), vshape, vshape],
        compiler_params=_params(dimension_semantics=("arbitrary",)),
    )(p, gp, gt, g)


def _loss_head(y, target, name):
    S, D = y.shape
    T = _row_tile(S)
    row = pl.BlockSpec((T, D), lambda i: (i, 0))
    vec = pl.BlockSpec((1, D), lambda i: (0, 0))

    def body(y_ref, t_ref, dy_ref, sq_ref):
        e = y_ref[...] - t_ref[...]
        dy_ref[...] = e * (1.0 / D)

        @pl.when(pl.program_id(0) == 0)
        def _():
            sq_ref[...] = jnp.zeros_like(sq_ref)

        sq_ref[...] += jnp.sum(e * e, axis=0, keepdims=True)

    return pl.pallas_call(
        body, name=name, grid=(S // T,), in_specs=[row, row], out_specs=[row, vec],
        out_shape=[jax.ShapeDtypeStruct((S, D), F32), jax.ShapeDtypeStruct((1, D), F32)],
        compiler_params=_params(dimension_semantics=("arbitrary",)),
    )(y, target)


def _shift_down(v, k, rows):
    return jnp.where(rows >= k, pltpu.roll(v, k, 0), 0.0)


def _shift_up(v, k, rows, S):
    return jnp.where(rows < S - k, pltpu.roll(v, S - k, 0), 0.0)


def _conv_gate(gu, wc, bc, name):
    S, F2 = gu.shape
    F = F2 // 2
    C = LANES

    def body(gu_ref, w_ref, b_ref, a_ref):
        rows = lax.broadcasted_iota(jnp.int32, (S, C), 0)
        gate = gu_ref[:, :C]
        up = gu_ref[:, C:]
        w = w_ref[...]
        gc = w[2:3] * gate + w[1:2] * _shift_down(gate, 1, rows) + w[0:1] * _shift_down(gate, 2, rows) + b_ref[...]
        a_ref[...] = (gc * (1.0 / (1.0 + jnp.exp(-gc))) * up).astype(BF16)

    return pl.pallas_call(
        body, name=name, grid=(F // C,),
        in_specs=[pl.BlockSpec((S, 2 * C), lambda j: (0, j)), pl.BlockSpec((3, C), lambda j: (0, j)),
                  pl.BlockSpec((1, C), lambda j: (0, j))],
        out_specs=pl.BlockSpec((S, C), lambda j: (0, j)),
        out_shape=jax.ShapeDtypeStruct((S, F), BF16), compiler_params=_params(),
    )(gu, wc, bc)


def _conv_gate_bwd(gu, wc, bc, da, name):
    S, F2 = gu.shape
    F = F2 // 2
    C = LANES

    def body(gu_ref, w_ref, b_ref, da_ref, dgu_ref, dw_ref, db_ref):
        rows = lax.broadcasted_iota(jnp.int32, (S, C), 0)
        gate = gu_ref[:, :C]
        up = gu_ref[:, C:]
        dav = da_ref[...]
        w = w_ref[...]
        g1 = _shift_down(gate, 1, rows)
        g2 = _shift_down(gate, 2, rows)
        gc = w[2:3] * gate + w[1:2] * g1 + w[0:1] * g2 + b_ref[...]
        sg = 1.0 / (1.0 + jnp.exp(-gc))
        dgu_ref[:, C:] = (dav * (gc * sg)).astype(BF16)
        dgc = dav * up * (sg * (1.0 + gc * (1.0 - sg)))
        db_ref[...] = jnp.sum(dgc, axis=0, keepdims=True)
        dw_ref[0:1, :] = jnp.sum(dgc * g2, axis=0, keepdims=True)
        dw_ref[1:2, :] = jnp.sum(dgc * g1, axis=0, keepdims=True)
        dw_ref[2:3, :] = jnp.sum(dgc * gate, axis=0, keepdims=True)
        dgate = w[2:3] * dgc + w[1:2] * _shift_up(dgc, 1, rows, S) + w[0:1] * _shift_up(dgc, 2, rows, S)
        dgu_ref[:, :C] = dgate.astype(BF16)

    return pl.pallas_call(
        body, name=name, grid=(F // C,),
        in_specs=[pl.BlockSpec((S, 2 * C), lambda j: (0, j)), pl.BlockSpec((3, C), lambda j: (0, j)),
                  pl.BlockSpec((1, C), lambda j: (0, j)), pl.BlockSpec((S, C), lambda j: (0, j))],
        out_specs=[pl.BlockSpec((S, 2 * C), lambda j: (0, j)), pl.BlockSpec((3, C), lambda j: (0, j)),
                   pl.BlockSpec((1, C), lambda j: (0, j))],
        out_shape=[jax.ShapeDtypeStruct((S, F2), BF16), jax.ShapeDtypeStruct((3, F), F32),
                   jax.ShapeDtypeStruct((1, F), F32)],
        compiler_params=_params(),
    )(gu, wc, bc, da)


ATT_SCALE = HEAD_DIM ** -0.5


def _att_specs(S, D, TQ):
    nb = D // LANES
    q_spec = pl.BlockSpec((TQ, LANES), lambda p, i: (i, p))
    k_spec = pl.BlockSpec((S, LANES), lambda p, i: (0, nb + p))
    v_spec = pl.BlockSpec((S, LANES), lambda p, i: (0, 2 * nb + p))
    stat_spec = pl.BlockSpec((TQ, 2 * ATT_BLOCK), lambda p, i: (i, p))
    seq_spec = pl.BlockSpec((S, LANES), lambda p, i: (0, p))
    ck_spec = pl.BlockSpec((2, S // ATT_BLOCK, SUBLANES, ATT_BLOCK), lambda p, i: (p, 0, 0, 0))
    return q_spec, k_spec, v_spec, stat_spec, seq_spec, ck_spec


def _att_call(body, name, grid, in_specs, out_specs, out_shape, scratch, args, side, semantics):
    n_out = len(out_shape)
    if side is not None:
        in_specs = in_specs + [ANY] * len(side.inputs)
        out_specs = out_specs + [ANY] * len(side.out_shapes)
        out_shape = out_shape + side.out_shapes
        scratch = scratch + side.sems
        args = args + side.inputs
    res = pl.pallas_call(
        body, name=name, grid=grid, in_specs=in_specs, out_specs=out_specs, out_shape=out_shape,
        scratch_shapes=scratch, compiler_params=_params(dimension_semantics=semantics),
    )(*args)
    return res[:n_out], res[n_out:]


def _grid_ends(NP, NQ):
    p, qi = pl.program_id(0), pl.program_id(1)
    return jnp.logical_and(p == 0, qi == 0), jnp.logical_and(p == NP - 1, qi == NQ - 1)


def _head_lanes(hh):
    return slice(hh * HEAD_DIM, (hh + 1) * HEAD_DIM)


def _below(old, lo, new, axis=0):
    if lo == 0:
        return new
    keep = old[:lo] if axis == 0 else old[:, :lo]
    return jnp.concatenate([keep, new], axis=axis)


def _scaled_q(q_ref, hh):
    return (q_ref[:, _head_lanes(hh)].astype(F32) * ATT_SCALE).astype(BF16)


def _tri(cmp):
    B = ATT_BLOCK
    row = lax.broadcasted_iota(jnp.int32, (B, B), 0)
    col = lax.broadcasted_iota(jnp.int32, (B, B), 1)
    half = jnp.concatenate([jnp.where(cmp(row, col), 1.0, 0.0).astype(BF16), jnp.ones((B, B), BF16)], axis=1)
    return jnp.concatenate([half, half], axis=0)


def _hi_lo_dot(x, t):
    hi = x.astype(BF16)
    lo = (x - hi.astype(F32)).astype(BF16)
    return _dot(jnp.concatenate([hi, lo], axis=1), t)


def _key_minus_query(j, qi, TQ):
    row = lax.broadcasted_iota(jnp.int32, (TQ, ATT_BLOCK), 0)
    col = lax.broadcasted_iota(jnp.int32, (TQ, ATT_BLOCK), 1)
    return col - row + (j * ATT_BLOCK - qi * TQ)


def _log_sigmoids(z):
    sp = jnp.log(1.0 + jnp.exp(-jnp.abs(z)))
    lb = jnp.minimum(z, 0.0) - sp
    return lb, lb - z


def _log_sigmoids_fast(z):
    zc = jnp.maximum(z, -80.0)
    lb = -jnp.log(1.0 + jnp.exp(-zc))
    return lb, lb - zc


def _sb_fwd(qkv, name, side=None):
    S, D3 = qkv.shape
    D = D3 // 3
    B, TQ = ATT_BLOCK, SB_Q_TILE
    R = TQ // B
    NP = D // LANES
    NQ = S // TQ
    q_spec, k_spec, v_spec, stat_spec, _, _ = _att_specs(S, D, TQ)

    def body(*refs):
        (q_ref, k_ref, v_ref), (o_ref, lt_ref, first_ref), _, parts = _side_parts(side, refs, 3, 3)
        begin, end = _side_hooks(side, parts, *_grid_ends(NP, NQ))
        begin()
        qi = pl.program_id(1)
        t_suffix = _tri(lambda r, c: r > c)
        qs = [_scaled_q(q_ref, hh) for hh in range(2)]

        def tile(j, carry, masked, lo=0):
            r0 = pl.multiple_of(j * B, B)
            if masked:
                strict = _key_minus_query(j, qi, TQ)[lo:] < 0
            out = []
            for hh in range(2):
                c, acc = carry[hh]
                k = k_ref[pl.ds(r0, B), _head_lanes(hh)]
                v = v_ref[pl.ds(r0, B), _head_lanes(hh)]
                lb, l1 = _log_sigmoids_fast(_dot_nt(qs[hh][lo:], k))
                if masked:
                    l1 = jnp.where(strict, l1, 0.0)
                sums = _hi_lo_dot(l1, t_suffix)
                a = jnp.exp(lb + c[lo:] + sums[:, :B])
                if masked:
                    a = jnp.where(strict, a, 0.0)
                out.append((_below(c, lo, c[lo:] + sums[:, B:]), _below(acc, lo, acc[lo:] + _dot(a.astype(BF16), v))))
            return tuple(out)

        carry = (jnp.zeros((TQ, B), F32), jnp.zeros((TQ, HEAD_DIM), F32))
        carry = (carry, carry)
        for jj in reversed(range(R)):
            carry = tile(qi * R + jj, carry, True, jj * B)

        def alive(cr):
            return jnp.max(jnp.maximum(cr[0][0], cr[1][0])) > SB_DEAD_LOG

        def walk(state):
            j, cr, _ = state
            cr = tile(j, cr, False)
            return j - 1, cr, alive(cr)

        j, carry, _ = lax.while_loop(lambda st: jnp.logical_and(st[0] >= 0, st[2]), walk,
                                     (qi * R - 1, carry, alive(carry)))
        first_ref[pl.program_id(0), qi] = (j + 1).astype(F32)
        for hh in range(2):
            c, acc = carry[hh]
            o_ref[:, _head_lanes(hh)] = acc
            lt_ref[:, hh * B:(hh + 1) * B] = c
        end()

    return _att_call(
        body, name, (NP, NQ), [q_spec, k_spec, v_spec],
        [q_spec, stat_spec, pl.BlockSpec(memory_space=pltpu.SMEM)],
        [jax.ShapeDtypeStruct((S, D), F32), jax.ShapeDtypeStruct((S, 2 * NP * B), F32),
         jax.ShapeDtypeStruct((NP, NQ), F32)], [], [qkv, qkv, qkv], side, ("arbitrary", "arbitrary"))


def _sb_bwd(qkv, do, lt, first, name, side=None):
    S, D3 = qkv.shape
    D = D3 // 3
    B, TQ = ATT_BLOCK, SB_Q_TILE
    R = TQ // B
    NP = D // LANES
    NQ = S // TQ
    q_spec, k_spec, v_spec, stat_spec, seq_spec, _ = _att_specs(S, D, TQ)

    def body(*refs):
        ins, (dq_ref, dk_ref, dv_ref), (dk_acc, dv_acc), parts = _side_parts(side, refs, 6, 3)
        first_ref, q_ref, k_ref, v_ref, do_ref, lt_ref = ins
        begin, end = _side_hooks(side, parts, *_grid_ends(NP, NQ))
        begin()
        qi = pl.program_id(1)
        t_prefix = _tri(lambda r, c: r <= c)
        t_before = _tri(lambda r, c: r < c)

        @pl.when(qi == 0)
        def _():
            dk_acc[...] = jnp.zeros_like(dk_acc)
            dv_acc[...] = jnp.zeros_like(dv_acc)

        qs = [_scaled_q(q_ref, hh) for hh in range(2)]
        dob = [do_ref[:, _head_lanes(hh)].astype(BF16) for hh in range(2)]
        ltot = [lt_ref[:, hh * B:(hh + 1) * B] for hh in range(2)]

        def tile(j, carry, masked, lo=0):
            r0 = pl.multiple_of(j * B, B)
            if masked:
                strict = _key_minus_query(j, qi, TQ)[lo:] < 0
            out = []
            for hh in range(2):
                pre, cu, dq = carry[hh]
                lanes = _head_lanes(hh)
                k = k_ref[pl.ds(r0, B), lanes]
                v = v_ref[pl.ds(r0, B), lanes]
                lb, l1 = _log_sigmoids_fast(_dot_nt(qs[hh][lo:], k))
                if masked:
                    l1 = jnp.where(strict, l1, 0.0)
                sums = _hi_lo_dot(l1, t_prefix)
                a = jnp.exp(lb + (ltot[hh][lo:] - pre[lo:] - sums[:, :B]))
                if masked:
                    a = jnp.where(strict, a, 0.0)
                u = a * _dot_nt(dob[hh][lo:], v)
                usums = _hi_lo_dot(u, t_before)
                dz = u - (u + cu[lo:] + usums[:, :B]) * jnp.exp(lb)
                if masked:
                    dz = jnp.where(strict, dz, 0.0)
                dzb = dz.astype(BF16)
                dk_acc[pl.ds(r0, B), lanes] += _dot_tn(dzb, qs[hh][lo:])
                dv_acc[pl.ds(r0, B), lanes] += _dot_tn(a.astype(BF16), dob[hh][lo:])
                out.append((_below(pre, lo, pre[lo:] + sums[:, B:]), _below(cu, lo, cu[lo:] + usums[:, B:]),
                            _below(dq, lo, dq[lo:] + _dot(dzb, k))))
            return tuple(out)

        zero = (jnp.zeros((TQ, B), F32), jnp.zeros((TQ, B), F32), jnp.zeros((TQ, HEAD_DIM), F32))
        first = jnp.clip(first_ref[pl.program_id(0), qi].astype(jnp.int32), 0, qi * R)
        carry = lax.fori_loop(first, qi * R, lambda j, cr: tile(j, cr, False), (zero, zero))
        for jj in range(R):
            carry = tile(qi * R + jj, carry, True, jj * B)
        for hh in range(2):
            dq_ref[:, _head_lanes(hh)] = (carry[hh][2] * ATT_SCALE).astype(BF16)

        @pl.when(qi == NQ - 1)
        def _():
            dk_ref[...] = dk_acc[...].astype(BF16)
            dv_ref[...] = dv_acc[...].astype(BF16)

        end()

    out = jax.ShapeDtypeStruct((S, D), BF16)
    return _att_call(
        body, name, (NP, NQ), [pl.BlockSpec(memory_space=pltpu.SMEM), q_spec, k_spec, v_spec, q_spec, stat_spec],
        [q_spec, seq_spec, seq_spec], [out, out, out],
        [pltpu.VMEM((S, LANES), F32), pltpu.VMEM((S, LANES), F32)], [first, qkv, qkv, qkv, do, lt], side,
        ("arbitrary", "arbitrary"))


def _fox_specs(S, D, TQ):
    row_spec = pl.BlockSpec((2 * SUBLANES, TQ), lambda p, i: (p, i))
    keyb_spec = pl.BlockSpec((S, 2 * ATT_BLOCK), lambda p, i: (0, p))
    return row_spec, keyb_spec


def _key_gt_query_t(j, qi, TQ):
    key = lax.broadcasted_iota(jnp.int32, (ATT_BLOCK, TQ), 0)
    qry = lax.broadcasted_iota(jnp.int32, (ATT_BLOCK, TQ), 1)
    return key - qry + (j * ATT_BLOCK - qi * TQ) > 0


def _fox_fwd(qkv, cq_rows, ck_b, name, side=None):
    S, D3 = qkv.shape
    D = D3 // 3
    B, TQ = ATT_BLOCK, FOX_Q_TILE
    R = TQ // B
    NP = D // LANES
    NQ = S // TQ
    q_spec, k_spec, v_spec, _, _, _ = _att_specs(S, D, TQ)
    row_spec, keyb_spec = _fox_specs(S, D, TQ)

    def body(*refs):
        (q_ref, k_ref, v_ref, cq_ref, ck_ref), (o_ref, lse_ref), _, parts = _side_parts(side, refs, 5, 2)
        begin, end = _side_hooks(side, parts, *_grid_ends(NP, NQ))
        begin()
        qi = pl.program_id(1)
        qs = [_scaled_q(q_ref, hh) for hh in range(2)]
        cq = [cq_ref[hh * SUBLANES:hh * SUBLANES + 1, :] for hh in range(2)]

        def tile(j, carry, masked, lo=0):
            r0 = pl.multiple_of(j * B, B)
            if masked:
                hidden = _key_gt_query_t(j, qi, TQ)[:, lo:]
            out = []
            for hh in range(2):
                m, lsum, acc = carry[hh]
                k = k_ref[pl.ds(r0, B), _head_lanes(hh)]
                v = v_ref[pl.ds(r0, B), _head_lanes(hh)]
                ck = jnp.tile(ck_ref[pl.ds(r0, B), hh * B:(hh + 1) * B], (1, (TQ - lo) // B))
                s = _dot_nt(k, qs[hh][lo:]) + (cq[hh][:, lo:] - ck)
                if masked:
                    s = jnp.where(hidden, NEG_BIG, s)
                m_new = jnp.maximum(m[:, lo:], jnp.max(s, axis=0, keepdims=True))
                p = jnp.exp(s - m_new)
                alpha = jnp.exp(m[:, lo:] - m_new)
                out.append((_below(m, lo, m_new, 1),
                            _below(lsum, lo, alpha * lsum[:, lo:] + jnp.sum(p, axis=0, keepdims=True), 1),
                            _below(acc, lo, alpha * acc[:, lo:] + _dot_tn(v, p.astype(BF16)), 1)))
            return tuple(out)

        zero = (jnp.full((1, TQ), NEG_BIG, F32), jnp.zeros((1, TQ), F32), jnp.zeros((HEAD_DIM, TQ), F32))
        carry = lax.fori_loop(0, qi * R, lambda j, cr: tile(j, cr, False), (zero, zero))
        for jj in range(R):
            carry = tile(qi * R + jj, carry, True, jj * B)
        o_t = jnp.concatenate([carry[hh][2] * (1.0 / carry[hh][1]) for hh in range(2)], axis=0)
        o_ref[...] = o_t.T
        for hh in range(2):
            m, lsum, _ = carry[hh]
            lse_ref[hh * SUBLANES:(hh + 1) * SUBLANES, :] = jnp.broadcast_to(m + jnp.log(lsum), (SUBLANES, TQ))
        end()

    return _att_call(
        body, name, (NP, NQ), [q_spec, k_spec, v_spec, row_spec, keyb_spec], [q_spec, row_spec],
        [jax.ShapeDtypeStruct((S, D), F32), jax.ShapeDtypeStruct((2 * NP * SUBLANES, S), F32)], [],
        [qkv, qkv, qkv, cq_rows, ck_b], side, ("arbitrary", "arbitrary"))


def _fox_bwd(qkv, do, o, lse_rows, cq_rows, ck_b, name, side=None):
    S, D3 = qkv.shape
    D = D3 // 3
    B, TQ = ATT_BLOCK, FOX_Q_TILE
    R = TQ // B
    NP = D // LANES
    NQ = S // TQ
    q_spec, k_spec, v_spec, _, seq_spec, _ = _att_specs(S, D, TQ)
    row_spec, keyb_spec = _fox_specs(S, D, TQ)

    def body(*refs):
        ins, outs, (dk_acc, dv_acc), parts = _side_parts(side, refs, 8, 5)
        q_ref, k_ref, v_ref, do_ref, o_ref, lse_ref, cq_ref, ck_ref = ins
        dq_ref, dk_ref, dv_ref, dcq_ref, dck_ref = outs
        begin, end = _side_hooks(side, parts, *_grid_ends(NP, NQ))
        begin()
        qi = pl.program_id(1)

        @pl.when(qi == 0)
        def _():
            dk_acc[...] = jnp.zeros_like(dk_acc)
            dv_acc[...] = jnp.zeros_like(dv_acc)
            dck_ref[...] = jnp.zeros_like(dck_ref)

        qs = [_scaled_q(q_ref, hh) for hh in range(2)]
        dob = [do_ref[:, _head_lanes(hh)].astype(BF16) for hh in range(2)]
        prod_t = (do_ref[...] * o_ref[...]).T
        delta = [jnp.sum(prod_t[hh * HEAD_DIM:(hh + 1) * HEAD_DIM], axis=0, keepdims=True) for hh in range(2)]
        cq = [cq_ref[hh * SUBLANES:hh * SUBLANES + 1, :] for hh in range(2)]
        lse = [lse_ref[hh * SUBLANES:hh * SUBLANES + 1, :] for hh in range(2)]

        def tile(j, carry, masked, lo=0):
            r0 = pl.multiple_of(j * B, B)
            if masked:
                hidden = _key_gt_query_t(j, qi, TQ)[:, lo:]
            out = []
            for hh in range(2):
                dq, keysum = carry[hh]
                lanes = _head_lanes(hh)
                k = k_ref[pl.ds(r0, B), lanes]
                v = v_ref[pl.ds(r0, B), lanes]
                ck = jnp.tile(ck_ref[pl.ds(r0, B), hh * B:(hh + 1) * B], (1, (TQ - lo) // B))
                p = jnp.exp(_dot_nt(k, qs[hh][lo:]) + (cq[hh][:, lo:] - ck) - lse[hh][:, lo:])
                if masked:
                    p = jnp.where(hidden, 0.0, p)
                ds = p * (_dot_nt(v, dob[hh][lo:]) - delta[hh][:, lo:])
                dsb = ds.astype(BF16)
                dk_acc[pl.ds(r0, B), lanes] += _dot(dsb, qs[hh][lo:])
                dv_acc[pl.ds(r0, B), lanes] += _dot(p.astype(BF16), dob[hh][lo:])
                qsum = jnp.sum(ds, axis=1, keepdims=True)
                dck_ref[pl.ds(r0, B), hh * B:(hh + 1) * B] -= jnp.broadcast_to(qsum, (B, B))
                out.append((_below(dq, lo, dq[:, lo:] + _dot_tn(k, dsb), 1),
                            _below(keysum, lo, keysum[:, lo:] + jnp.sum(ds, axis=0, keepdims=True), 1)))
            return tuple(out)

        zero = (jnp.zeros((HEAD_DIM, TQ), F32), jnp.zeros((1, TQ), F32))
        carry = lax.fori_loop(0, qi * R, lambda j, cr: tile(j, cr, False), (zero, zero))
        for jj in range(R):
            carry = tile(qi * R + jj, carry, True, jj * B)
        dq_t = jnp.concatenate([carry[hh][0] for hh in range(2)], axis=0)
        dq_ref[...] = (dq_t.T * ATT_SCALE).astype(BF16)
        for hh in range(2):
            dcq_ref[hh * SUBLANES:(hh + 1) * SUBLANES, :] = jnp.broadcast_to(carry[hh][1], (SUBLANES, TQ))

        @pl.when(qi == NQ - 1)
        def _():
            dk_ref[...] = dk_acc[...].astype(BF16)
            dv_ref[...] = dv_acc[...].astype(BF16)

        end()

    out = jax.ShapeDtypeStruct((S, D), BF16)
    return _att_call(
        body, name, (NP, NQ), [q_spec, k_spec, v_spec, q_spec, q_spec, row_spec, row_spec, keyb_spec],
        [q_spec, seq_spec, seq_spec, row_spec, keyb_spec],
        [out, out, out, jax.ShapeDtypeStruct((2 * NP * SUBLANES, S), F32), jax.ShapeDtypeStruct((S, 2 * NP * B), F32)],
        [pltpu.VMEM((S, LANES), F32), pltpu.VMEM((S, LANES), F32)],
        [qkv, qkv, qkv, do, o, lse_rows, cq_rows, ck_b], side, ("arbitrary", "arbitrary"))


def _forget_cumsum(f3, bias, name):
    NQ, NH, B = f3.shape

    def body(f_ref, b_ref, cum_ref):
        row = lax.broadcasted_iota(jnp.int32, (B, B), 0)
        col = lax.broadcasted_iota(jnp.int32, (B, B), 1)
        t_incl = jnp.where(row <= col, 1.0, 0.0).astype(BF16)

        def step(b, carry):
            lf, _ = _log_sigmoids(f_ref[b] + b_ref[...])
            cum = _split_dot(lf, t_incl, 3) + carry
            cum_ref[b] = cum
            return jnp.broadcast_to(cum[:, B - 1:B], (NH, B))

        lax.fori_loop(0, NQ, step, jnp.zeros((NH, B), F32))

    return pl.pallas_call(
        body, name=name, in_specs=[VMEM_WHOLE, VMEM_WHOLE], out_specs=VMEM_WHOLE,
        out_shape=jax.ShapeDtypeStruct((NQ, NH, B), F32), compiler_params=_params(),
    )(f3, bias)


def _forget_cumsum_bwd(dcum3, f3, bias, name):
    NQ, NH, B = f3.shape

    def body(d_ref, f_ref, b_ref, df_ref, tot_ref):
        row = lax.broadcasted_iota(jnp.int32, (B, B), 0)
        col = lax.broadcasted_iota(jnp.int32, (B, B), 1)
        t_rev = jnp.where(row >= col, 1.0, 0.0).astype(BF16)

        def step(it, carry):
            run, tot = carry
            b = NQ - 1 - it
            dlf = _split_dot(d_ref[b], t_rev, 3) + run
            f = f_ref[b] + b_ref[...]
            df = dlf * (1.0 / (1.0 + jnp.exp(f)))
            df_ref[b] = df
            return jnp.broadcast_to(dlf[:, 0:1], (NH, B)), tot + df

        _, tot = lax.fori_loop(0, NQ, step, (jnp.zeros((NH, B), F32), jnp.zeros((NH, B), F32)))
        tot_ref[...] = _split_dot(tot, jnp.ones((B, B), BF16), 3)

    return pl.pallas_call(
        body, name=name, in_specs=[VMEM_WHOLE, VMEM_WHOLE, VMEM_WHOLE], out_specs=[VMEM_WHOLE, VMEM_WHOLE],
        out_shape=[jax.ShapeDtypeStruct((NQ, NH, B), F32), jax.ShapeDtypeStruct((NH, B), F32)],
        compiler_params=_params(),
    )(dcum3, f3, bias)


def _silu(x, name):
    def body(x_ref, o_ref):
        v = x_ref[...]
        o_ref[...] = v * (1.0 / (1.0 + jnp.exp(-v)))

    return pl.pallas_call(body, name=name, in_specs=[VMEM_WHOLE], out_specs=VMEM_WHOLE,
                          out_shape=jax.ShapeDtypeStruct(x.shape, F32), compiler_params=_params())(x)


def _sum_leading(x, name):
    N, R, C = x.shape
    T = _pick(R, (256, 128, 64, 32, 16, 8))

    def body(x_ref, o_ref):
        acc = x_ref[0].astype(F32)
        for r in range(1, N):
            acc = acc + x_ref[r].astype(F32)
        o_ref[...] = acc

    return pl.pallas_call(
        body, name=name, grid=(R // T,), in_specs=[pl.BlockSpec((N, T, C), lambda i: (0, i, 0))],
        out_specs=pl.BlockSpec((T, C), lambda i: (i, 0)), out_shape=jax.ShapeDtypeStruct((R, C), F32),
        compiler_params=_params(),
    )(x)


def _adamw(w, g, m, v, name):
    shape = w.shape
    C = shape[-1]
    R = w.size // C
    T = R
    for cand in (512, 256, 128, 64, 32, 16, 8):
        if R % cand == 0 and cand * C * 4 <= (1 << 20):
            T = cand
            break
    spec = pl.BlockSpec((T, C), lambda i: (i, 0))
    c1 = 1.0 / (1.0 - ADAM_B1 ** ADAM_STEP)
    c2 = 1.0 / (1.0 - ADAM_B2 ** ADAM_STEP)

    def body(w_ref, g_ref, m_ref, v_ref, d_ref, nm_ref, nv_ref):
        gv = g_ref[...]
        nm = ADAM_B1 * m_ref[...] + (1.0 - ADAM_B1) * gv
        nv = ADAM_B2 * v_ref[...] + (1.0 - ADAM_B2) * (gv * gv)
        nm_ref[...] = nm
        nv_ref[...] = nv
        d_ref[...] = -ADAM_LR * ((nm * c1) / (jnp.sqrt(nv * c2) + ADAM_EPS) + ADAM_WD * w_ref[...])

    out = jax.ShapeDtypeStruct((R, C), F32)
    d, nm, nv = pl.pallas_call(
        body, name=name, grid=(R // T,), in_specs=[spec] * 4, out_specs=[spec] * 3, out_shape=[out] * 3,
        compiler_params=_params(),
    )(w.reshape(R, C), g.reshape(R, C), m.reshape(R, C), v.reshape(R, C))
    return d.reshape(shape), nm.reshape(shape), nv.reshape(shape)


def _mesh_pos():
    return lax.axis_index("x"), lax.axis_index("y"), lax.axis_index("c")


def _all_gather_small(x2d, name):
    m_per, n = x2d.shape

    def body(x_ref, out_ref, send_sems, recv_sems, local_sem):
        x, y, c = _mesh_pos()
        me, sibling = (x, y, c), (x, y, 1 - c)
        chips = [(1 - x, y), (x, 1 - y), (1 - x, 1 - y)]

        def rows(px, py, pc):
            return out_ref.at[pl.ds((4 * px + 2 * py + pc) * m_per, m_per), :]

        def copy(k, block, to, src=None):
            return pltpu.make_async_remote_copy(
                src_ref=rows(*block) if src is None else src, dst_ref=rows(*block),
                send_sem=send_sems.at[k], recv_sem=recv_sems.at[k], device_id=to, device_id_type=MESH)

        mine = pltpu.make_async_copy(x_ref, rows(*me), local_sem)
        mine.start()
        first = [copy(0, me, sibling, src=x_ref)]
        first += [copy(1 + j, me, (*chip, c), src=x_ref) for j, chip in enumerate(chips)]
        for cp in first:
            cp.start()
        passed = [copy(4 + j, (*chip, c), sibling) for j, chip in enumerate(chips)]
        for j, chip in enumerate(chips):
            copy(1 + j, (*chip, c), me).wait_recv()
            passed[j].start()
        copy(0, sibling, me).wait_recv()
        for j, chip in enumerate(chips):
            copy(4 + j, (*chip, 1 - c), me).wait_recv()
        for cp in first + passed:
            cp.wait_send()
        mine.wait()

    return pl.pallas_call(
        body, name=name, out_shape=jax.ShapeDtypeStruct((N_DEV * m_per, n), x2d.dtype),
        in_specs=[VMEM_WHOLE], out_specs=VMEM_WHOLE,
        scratch_shapes=[pltpu.SemaphoreType.DMA((7,)), pltpu.SemaphoreType.DMA((7,)), pltpu.SemaphoreType.DMA],
        compiler_params=_params(),
    )(x2d)


class _Side:
    def __init__(self, inputs, out_shapes, sems, start, wait):
        self.inputs, self.out_shapes, self.sems, self.start, self.wait = inputs, out_shapes, sems, start, wait


def _join_sides(sides):
    sides = [s for s in sides if s is not None]
    if not sides:
        return None
    bounds_in, bounds_out, bounds_sem = [0], [0], [0]
    for s in sides:
        bounds_in.append(bounds_in[-1] + len(s.inputs))
        bounds_out.append(bounds_out[-1] + len(s.out_shapes))
        bounds_sem.append(bounds_sem[-1] + len(s.sems))

    def each(method):
        def run(ins, outs, sems):
            for t, s in enumerate(sides):
                getattr(s, method)(ins[bounds_in[t]:bounds_in[t + 1]], outs[bounds_out[t]:bounds_out[t + 1]],
                                   sems[bounds_sem[t]:bounds_sem[t + 1]])
        return run

    return _Side([a for s in sides for a in s.inputs], [o for s in sides for o in s.out_shapes],
                 [m for s in sides for m in s.sems], each("start"), each("wait"))


def _side_parts(side, refs, n_in, n_out):
    if side is None:
        return refs[:n_in], refs[n_in:n_in + n_out], refs[n_in + n_out:], None
    si, so, ss = len(side.inputs), len(side.out_shapes), len(side.sems)
    a = n_in + si
    b = a + n_out + so
    ins, side_in = refs[:n_in], refs[n_in:a]
    outs, side_out = refs[a:a + n_out], refs[a + n_out:b]
    scratch, side_sems = refs[b:len(refs) - ss], refs[len(refs) - ss:]
    return ins, outs, scratch, (side_in, side_out, side_sems)


def _side_hooks(side, parts, first, last):
    if side is None:
        return lambda: None, lambda: None

    def begin():
        @pl.when(first)
        def _():
            side.start(*parts)

    def end():
        @pl.when(last)
        def _():
            side.wait(*parts)

    return begin, end


def _run_side(side, name):
    n_in = len(side.inputs)
    n_out = len(side.out_shapes)

    def body(*refs):
        parts = (refs[:n_in], refs[n_in:n_in + n_out], refs[n_in + n_out:])
        side.start(*parts)
        side.wait(*parts)

    return pl.pallas_call(
        body, name=name, out_shape=side.out_shapes, in_specs=[ANY] * n_in, out_specs=[ANY] * n_out,
        scratch_shapes=side.sems, compiler_params=_params(),
    )(*side.inputs)


def _gather_side(ws):
    n = len(ws)

    def copies(ins, outs, sems):
        send_sems, recv_sems, local_sems = sems
        x, y, c = _mesh_pos()
        k_me = 2 * x + y
        chips = [(1 - x, y), (x, 1 - y), (1 - x, 1 - y)]

        def remote(i, j, slot):
            px, py = chips[j]
            return pltpu.make_async_remote_copy(
                src_ref=ins[i], dst_ref=outs[i].at[slot], send_sem=send_sems.at[i, j],
                recv_sem=recv_sems.at[i, j], device_id=(px, py, c), device_id_type=MESH)

        local = [pltpu.make_async_copy(ins[i], outs[i].at[k_me], local_sems.at[i]) for i in range(n)]
        return remote, local, k_me, chips

    def start(ins, outs, sems):
        remote, local, k_me, _ = copies(ins, outs, sems)
        for i in range(n):
            local[i].start()
            for j in range(3):
                remote(i, j, k_me).start()

    def wait(ins, outs, sems):
        remote, local, k_me, chips = copies(ins, outs, sems)
        for i in range(n):
            for j, (px, py) in enumerate(chips):
                remote(i, j, 2 * px + py).wait_recv()
        for i in range(n):
            for j in range(3):
                remote(i, j, k_me).wait_send()
            local[i].wait()

    return _Side(list(ws), [jax.ShapeDtypeStruct((N_CHIPS,) + w.shape, w.dtype) for w in ws],
                 [pltpu.SemaphoreType.DMA((n, 3)), pltpu.SemaphoreType.DMA((n, 3)), pltpu.SemaphoreType.DMA((n,))],
                 start, wait)


def _scatter_side(gs):
    n = len(gs)
    halves = [g.shape[1] // 2 for g in gs]

    def copies(ins, outs, sems):
        send_sems, recv_sems, local_sems = sems
        x, y, c = _mesh_pos()

        def flip(v, bit):
            return 1 - v if bit else v

        def piece(i, px, py, pc):
            return ins[i].at[2 * px + py, pl.ds(pc * halves[i], halves[i])]

        def remote(i, r):
            px, py, pc = flip(x, r & 4), flip(y, r & 2), flip(c, r & 1)
            return pltpu.make_async_remote_copy(
                src_ref=piece(i, px, py, pc), dst_ref=outs[i].at[r], send_sem=send_sems.at[i, r - 1],
                recv_sem=recv_sems.at[i, r - 1], device_id=(px, py, pc), device_id_type=MESH)

        local = [pltpu.make_async_copy(piece(i, x, y, c), outs[i].at[0], local_sems.at[i]) for i in range(n)]
        return remote, local

    def start(ins, outs, sems):
        remote, local = copies(ins, outs, sems)
        for i in range(n):
            local[i].start()
            for r in range(1, N_DEV):
                remote(i, r).start()

    def wait(ins, outs, sems):
        remote, local = copies(ins, outs, sems)
        for i in range(n):
            for r in range(1, N_DEV):
                remote(i, r).wait_recv()
        for i in range(n):
            for r in range(1, N_DEV):
                remote(i, r).wait_send()
            local[i].wait()

    return _Side(list(gs), [jax.ShapeDtypeStruct((N_DEV, h) + g.shape[2:], g.dtype) for g, h in zip(gs, halves)],
                 [pltpu.SemaphoreType.DMA((n, N_DEV - 1)), pltpu.SemaphoreType.DMA((n, N_DEV - 1)),
                  pltpu.SemaphoreType.DMA((n,))], start, wait)


def _swap_side(hs):
    n = len(hs)

    def copies(ins, outs, sems):
        send_sems, recv_sems, local_sems = sems
        x, y, c = _mesh_pos()

        def remote(i, slot):
            return pltpu.make_async_remote_copy(
                src_ref=ins[i], dst_ref=outs[i].at[slot], send_sem=send_sems.at[i], recv_sem=recv_sems.at[i],
                device_id=(x, y, 1 - c), device_id_type=MESH)

        local = [pltpu.make_async_copy(ins[i], outs[i].at[c], local_sems.at[i]) for i in range(n)]
        return remote, local, c

    def start(ins, outs, sems):
        remote, local, c = copies(ins, outs, sems)
        for i in range(n):
            local[i].start()
            remote(i, c).start()

    def wait(ins, outs, sems):
        remote, local, c = copies(ins, outs, sems)
        for i in range(n):
            remote(i, 1 - c).wait_recv()
        for i in range(n):
            remote(i, c).wait_send()
            local[i].wait()

    return _Side(list(hs), [jax.ShapeDtypeStruct((2,) + h.shape, h.dtype) for h in hs],
                 [pltpu.SemaphoreType.DMA((n,)), pltpu.SemaphoreType.DMA((n,)), pltpu.SemaphoreType.DMA((n,))],
                 start, wait)


def _pad_rows(a, rows):
    return jnp.pad(a, ((0, rows - a.shape[0]), (0, 0)))


def kernel(x, c, w_mod, b_mod, g_mix_pre, g_mix_post, w_qkv, w_o, w_fg, b_fg, g_ffn_pre, g_ffn_post, w_ffn_gate, w_ffn_up, w_conv, b_conv, w_ffn_down, loss_target, m_w_mod, m_b_mod, m_g_mix_pre, m_g_mix_post, m_w_qkv, m_w_o, m_w_fg, m_b_fg, m_g_ffn_pre, m_g_ffn_post, m_w_ffn_gate, m_w_ffn_up, m_w_conv, m_b_conv, m_w_ffn_down, v_w_mod, v_b_mod, v_g_mix_pre, v_g_mix_post, v_w_qkv, v_w_o, v_w_fg, v_b_fg, v_g_ffn_pre, v_g_ffn_post, v_w_ffn_gate, v_w_ffn_up, v_w_conv, v_b_conv, v_w_ffn_down):
    xs = x[0]
    target = loss_target[0]
    S, D = xs.shape
    L = w_mod.shape[0]
    LF = w_fg.shape[0]
    MS = w_mod.shape[2]
    QS = w_qkv.shape[2]
    OS = w_o.shape[1]
    FS = w_ffn_gate.shape[2]
    F = N_CHIPS * FS
    NH = D // HEAD_DIM
    B = ATT_BLOCK
    NQ = S // B
    ax, ay, ac = _mesh_pos()
    k_me = 2 * ax + ay
    b_me = 4 * ax + 2 * ay + ac

    conv_rows = -(-(L * 3 * FS) // D)
    conv_rows = -(-conv_rows // SUBLANES) * SUBLANES
    conv_flat = jnp.pad(w_conv.reshape(-1), (0, conv_rows * D - L * 3 * FS)).reshape(conv_rows, D)
    first = jnp.concatenate([_pad_rows(c, SUBLANES), conv_flat], axis=0)
    first_all = _all_gather_small(first, "ag_cond").reshape(N_DEV, SUBLANES + conv_rows, D)
    c_all = first_all[:, 0, :]
    conv_all = first_all[0::2, SUBLANES:, :].reshape(N_CHIPS, -1)[:, :L * 3 * FS]
    w_conv_full = conv_all.reshape(N_CHIPS, L, 3, FS).transpose(1, 2, 0, 3).reshape(L, 3, F)
    c_act = _silu(c_all, "silu_c")

    mod_part = jnp.concatenate(
        [_mm(c_act, w_mod[l], "nn", F32, "mm_mod", tm=N_DEV, tn=MS, tk=D) for l in range(L)], axis=1)
    mod_all = _all_gather_small(mod_part, "ag_mod").reshape(N_CHIPS, 2, N_DEV, L, MS)[:, 0]
    mod_mine = lax.dynamic_index_in_dim(mod_all, b_me, axis=1, keepdims=False)
    mod = mod_mine.transpose(1, 0, 2).reshape(L, N_MOD * D) + b_mod

    shards_b = [w.astype(BF16) for w in (w_qkv, w_o, w_ffn_gate, w_ffn_up, w_ffn_down)]
    w_fg_b = w_fg.astype(BF16)

    def layer_shards(l):
        ws = [w[l] for w in shards_b]
        return ws + [w_fg_b[l // 2]] if l % 2 == 1 else ws

    def relayout(gathered):
        def cols(gth):
            return gth.transpose(1, 0, 2).reshape(D, -1, 1, LANES)

        gq, go, gg, gu_, gd = gathered[:5]
        wts = dict(qkv=gq.transpose(1, 0, 2).reshape(D, 3 * D), o=go.reshape(D, D),
                   gu=jnp.concatenate([cols(gg), cols(gu_)], axis=2).reshape(D, 2 * F),
                   d=gd.reshape(F, D))
        if len(gathered) > 5:
            wts["fg"] = jnp.pad(gathered[5].reshape(D, NH), ((0, 0), (0, LANES - NH)))
        return wts

    W = [None] * L
    W[0] = relayout(_run_side(_gather_side(layer_shards(0)), "ag_weights_first"))

    def vec(a):
        return a.reshape(1, -1)

    saved = []
    xcur = xs
    for l in range(L):
        sh_a, sc_a, gt_a, sh_f, sc_f, gt_f = [vec(mod[l, j * D:(j + 1) * D]) for j in range(N_MOD)]
        is_fox = l % 2 == 1
        jf = l // 2
        h1 = _norm_mod(xcur, vec(g_mix_pre[l]), sc_a, sh_a, "norm_mod")
        qkv = _mm(h1, W[l]["qkv"], "nn", BF16, "mm_qkv")
        next_weights = _gather_side(layer_shards(l + 1)) if l + 1 < L else None
        if is_fox:
            flog = _mm(h1, W[l]["fg"], "nn", F32, "mm_fg")[:, :NH]
            f3 = flog.reshape(NQ, B, NH).transpose(0, 2, 1)
            bias = b_fg[jf].reshape(NH, 1)
            cum3 = _forget_cumsum(f3, bias, "forget_cumsum")
            cum_sn = cum3.transpose(0, 2, 1).reshape(S, NH)
            ck_b = jnp.repeat(cum_sn, B, axis=1)
            cq_rows = jnp.repeat(cum_sn.T, SUBLANES, axis=0)
            (o, stat), gathered = _fox_fwd(qkv, cq_rows, ck_b, "fox_fwd", next_weights)
            extra = (f3, bias, cq_rows, ck_b)
        else:
            (o, stat, extra), gathered = _sb_fwd(qkv, "sb_fwd", next_weights)
        if next_weights is not None:
            W[l + 1] = relayout(gathered)
        p = _mm(o, W[l]["o"], "nn", F32, "mm_o")
        x1 = _post_res(xcur, p, vec(g_mix_post[l]), gt_a, "post_res")
        h2 = _norm_mod(x1, vec(g_ffn_pre[l]), sc_f, sh_f, "norm_mod")
        gu = _mm(h2, W[l]["gu"], "nn", F32, "mm_gu")
        wc = w_conv_full[l]
        bc = vec(b_conv[l])
        a = _conv_gate(gu, wc, bc, "conv_gate")
        yv = _mm(a, W[l]["d"], "nn", F32, "mm_down")
        x2 = _post_res(x1, yv, vec(g_ffn_post[l]), gt_f, "post_res")
        saved.append(dict(x0=xcur, h1=h1, qkv=qkv, o=o, stat=stat, extra=extra, p=p, x1=x1, h2=h2, gu=gu, a=a,
                          y=yv, mods=(sh_a, sc_a, gt_a, sh_f, sc_f, gt_f), wc=wc, bc=bc))
        xcur = x2

    g, sq = _loss_head(xcur, target, "loss_head")
    loss_part = 0.5 * jnp.sum(sq) / D
    loss = lax.psum(loss_part, ("x", "y", "c"))

    dW_qkv, dW_o, dW_gu, dW_d = [None] * L, [None] * L, [None] * L, [None] * L
    dW_fg, db_fg = [None] * LF, [None] * LF
    dmod, dg_mix_pre, dg_mix_post, dg_ffn_pre, dg_ffn_post = [[None] * L for _ in range(5)]
    dw_conv, db_conv = [None] * L, [None] * L

    def cols(dw, width):
        return dw.reshape(dw.shape[0], N_CHIPS, width).transpose(1, 0, 2).astype(BF16)

    def rows(dw, height):
        return dw.reshape(N_CHIPS, height, dw.shape[1]).astype(BF16)

    n_big = 5
    grad_shards = [[None] * n_big for _ in range(L)]
    to_scatter, to_swap = [], []

    def carried():
        sc, sw = list(to_scatter), list(to_swap)
        del to_scatter[:], to_swap[:]
        side = _join_sides([_scatter_side([a for _, _, a in sc]) if sc else None,
                            _swap_side([a for _, _, a in sw]) if sw else None])

        def taken(outs):
            for (lay, i, _), recv in zip(sc, outs[:len(sc)]):
                to_swap.append((lay, i, _sum_leading(recv, f"sum_grad_pieces_{i}")))
            for (lay, i, _), both in zip(sw, outs[len(sc):]):
                grad_shards[lay][i] = both.reshape((-1,) + both.shape[2:])

        return side, taken

    for l in reversed(range(L)):
        sv = saved[l]
        sh_a, sc_a, gt_a, sh_f, sc_f, gt_f = sv["mods"]
        is_fox = l % 2 == 1
        jf = l // 2
        dy, dgp_f, dgt_f = _post_res_bwd(sv["y"], vec(g_ffn_post[l]), gt_f, g, "post_res_bwd")
        da = _mm(dy, W[l]["d"], "nt", F32, "mm_da")
        dW_d[l] = _mm(sv["a"], dy, "tn", F32, "mm_dwd")
        dgu, dwc, dbc = _conv_gate_bwd(sv["gu"], sv["wc"], sv["bc"], da, "conv_gate_bwd")
        dh2 = _mm(dgu, W[l]["gu"], "nt", F32, "mm_dh2")
        dW_gu[l] = _mm(sv["h2"], dgu, "tn", F32, "mm_dwgu")
        g, dg_f, dsc_f, dsh_f = _norm_mod_bwd(sv["x1"], vec(g_ffn_pre[l]), sc_f, sh_f, [dh2], g, "norm_mod_bwd")
        dp, dgp_a, dgt_a = _post_res_bwd(sv["p"], vec(g_mix_post[l]), gt_a, g, "post_res_bwd")
        do = _mm(dp, W[l]["o"], "nt", F32, "mm_do")
        dW_o[l] = _mm(sv["o"], dp, "tn", F32, "mm_dwo")
        dgu_w = dW_gu[l].reshape(D, F // LANES, 2, LANES)
        to_scatter += [(l, 1, rows(dW_o[l], OS)), (l, 2, cols(dgu_w[:, :, 0].reshape(D, F), FS)),
                       (l, 3, cols(dgu_w[:, :, 1].reshape(D, F), FS)), (l, 4, rows(dW_d[l], FS))]
        side, taken = carried()
        if is_fox:
            f3, bias, cq_rows, ck_b = sv["extra"]
            (dq, dk, dv, dcq_rows, dck_b), side_outs = _fox_bwd(sv["qkv"], do, sv["o"], sv["stat"], cq_rows, ck_b,
                                                                "fox_bwd", side)
            dcum_sn = dcq_rows[::SUBLANES].T + dck_b[:, ::B]
            dcum3 = dcum_sn.reshape(NQ, B, NH).transpose(0, 2, 1)
            df3, df_tot = _forget_cumsum_bwd(dcum3, f3, bias, "forget_cumsum_bwd")
            df = df3.transpose(0, 2, 1).reshape(S, NH)
            dfp = jnp.pad(df, ((0, 0), (0, LANES - NH)))
            dW_fg[jf] = _mm(sv["h1"], dfp, "tn", F32, "mm_dwfg")[:, :NH]
            db_fg[jf] = df_tot[:, 0]
            dh_extra = [_mm(dfp, W[l]["fg"], "nt", F32, "mm_dh1f")]
        else:
            (dq, dk, dv), side_outs = _sb_bwd(sv["qkv"], do, sv["stat"], sv["extra"], "sb_bwd", side)
            dh_extra = []
        taken(side_outs)
        dqkv = jnp.concatenate([dq, dk, dv], axis=1)
        dh1 = _mm(dqkv, W[l]["qkv"], "nt", F32, "mm_dh1")
        dW_qkv[l] = _mm(sv["h1"], dqkv, "tn", F32, "mm_dwqkv")
        to_scatter.append((l, 0, cols(dW_qkv[l], QS)))
        g, dg_a, dsc_a, dsh_a = _norm_mod_bwd(sv["x0"], vec(g_mix_pre[l]), sc_a, sh_a, [dh1] + dh_extra, g,
                                              "norm_mod_bwd")
        dmod[l] = jnp.concatenate([dsh_a, dsc_a, dgt_a, dsh_f, dsc_f, dgt_f], axis=1)[0]
        dg_mix_pre[l], dg_mix_post[l], dg_ffn_pre[l], dg_ffn_post[l] = dg_a[0], dgp_a[0], dg_f[0], dgp_f[0]
        dw_conv[l], db_conv[l] = dwc, dbc[0]
    grad_x = g[None]

    pieces = [jnp.stack(dmod), jnp.stack(dg_mix_pre), jnp.stack(dg_mix_post), jnp.stack(dg_ffn_pre),
              jnp.stack(dg_ffn_post), jnp.stack(db_fg), jnp.stack(dW_fg), jnp.stack(dw_conv), jnp.stack(db_conv)]
    sizes = [pc.size for pc in pieces]
    total = sum(sizes)
    pack_rows = -(-total // (LANES * SUBLANES)) * SUBLANES
    pack = jnp.pad(jnp.concatenate([pc.reshape(-1) for pc in pieces]), (0, pack_rows * LANES - total))
    pack_all = _all_gather_small(pack.reshape(pack_rows, LANES), "ag_small_grads").reshape(N_DEV, pack_rows, LANES)
    small = _sum_leading(pack_all, "sum_small_grads").reshape(-1)
    offs = [0]
    for sz in sizes:
        offs.append(offs[-1] + sz)
    parts = [small[offs[i]:offs[i + 1]].reshape(pieces[i].shape) for i in range(len(pieces))]
    g_b_mod, g_g_mix_pre, g_g_mix_post, g_g_ffn_pre, g_g_ffn_post, g_b_fg, g_w_fg_full, g_w_conv_full, g_b_conv = parts
    g_w_fg = lax.dynamic_slice_in_dim(g_w_fg_full, k_me * OS, OS, axis=1)
    g_w_conv = lax.dynamic_slice_in_dim(g_w_conv_full, k_me * FS, FS, axis=2)
    dmod_all = pack_all.reshape(N_DEV, -1)[:, :L * N_MOD * D].reshape(N_DEV, L, N_CHIPS, MS)
    dmod_cols = lax.dynamic_index_in_dim(dmod_all, k_me, axis=2, keepdims=False).reshape(N_DEV, L * MS)
    g_w_mod = _mm(_pad_rows(c_act, LANES), _pad_rows(dmod_cols, LANES), "tn", F32, "mm_dwmod", tm=D, tn=MS, tk=LANES)
    g_w_mod = g_w_mod.reshape(D, L, MS).transpose(1, 0, 2)

    side, taken = carried()
    taken(_run_side(side, "rs_tail_scatter"))
    side, taken = carried()
    taken(_run_side(side, "rs_tail_swap"))
    g_w_qkv, g_w_o, g_w_gate, g_w_up, g_w_down = [jnp.stack([grad_shards[l][i] for l in range(L)])
                                                  for i in range(n_big)]

    grads = [g_w_mod, g_b_mod, g_g_mix_pre, g_g_mix_post, g_w_qkv, g_w_o, g_w_fg, g_b_fg, g_g_ffn_pre,
             g_g_ffn_post, g_w_gate, g_w_up, g_w_conv, g_b_conv, g_w_down]
    weights = [w_mod, b_mod, g_mix_pre, g_mix_post, w_qkv, w_o, w_fg, b_fg, g_ffn_pre, g_ffn_post, w_ffn_gate,
               w_ffn_up, w_conv, b_conv, w_ffn_down]
    ms = [m_w_mod, m_b_mod, m_g_mix_pre, m_g_mix_post, m_w_qkv, m_w_o, m_w_fg, m_b_fg, m_g_ffn_pre, m_g_ffn_post,
          m_w_ffn_gate, m_w_ffn_up, m_w_conv, m_b_conv, m_w_ffn_down]
    vs = [v_w_mod, v_b_mod, v_g_mix_pre, v_g_mix_post, v_w_qkv, v_w_o, v_w_fg, v_b_fg, v_g_ffn_pre, v_g_ffn_post,
          v_w_ffn_gate, v_w_ffn_up, v_w_conv, v_b_conv, v_w_ffn_down]
    deltas, new_ms, new_vs = [], [], []
    for wv, gv, mv, vv in zip(weights, grads, ms, vs):
        d, nm, nv = _adamw(wv, gv, mv, vv, "adamw")
        deltas.append(d)
        new_ms.append(nm)
        new_vs.append(nv)
    return (loss, grad_x, *grads, *deltas, *new_ms, *new_vs)
```

```python
import functools

import jax
import jax.numpy as jnp
from jax import lax
from jax.experimental import pallas as pl
from jax.experimental.pallas import tpu as pltpu

F32 = jnp.float32
BF16 = jnp.bfloat16
MESH = pl.DeviceIdType.MESH

HEAD_DIM = 64
ATT_BLOCK = 128
SB_Q_TILE = 512
FOX_Q_TILE = 1024
SB_DEAD_LOG = -110.0
LANES = 128
SUBLANES = 8
RMS_EPS = 1e-6
N_MOD = 6
N_CHIPS = 4
N_DEV = 8
ADAM_LR = 0.001
ADAM_B1 = 0.9
ADAM_B2 = 0.999
ADAM_EPS = 1e-08
ADAM_WD = 0.01
ADAM_STEP = 10
VMEM_LIMIT_BYTES = 56 * 1024 * 1024
NEG_BIG = -1e30

ANY = pl.BlockSpec(memory_space=pl.ANY)
VMEM_WHOLE = pl.BlockSpec(memory_space=pltpu.VMEM)


def _params(**kw):
    return pltpu.CompilerParams(vmem_limit_bytes=VMEM_LIMIT_BYTES, **kw)


def _dot(a, b):
    return jnp.dot(a, b, preferred_element_type=F32)


def _dot_nt(a, b):
    return lax.dot_general(a, b, (((1,), (1,)), ((), ())), preferred_element_type=F32)


def _dot_tn(a, b):
    return lax.dot_general(a, b, (((0,), (0,)), ((), ())), preferred_element_type=F32)


def _split_dot(x, t, parts):
    acc = None
    rem = x
    for _ in range(parts):
        piece = rem.astype(BF16)
        rem = rem - piece.astype(F32)
        d = _dot(piece, t)
        acc = d if acc is None else acc + d
    return acc


def _pick(n, prefs):
    for p in prefs:
        if n % p == 0:
            return p
    return n


def _mm(a, b, dims, out_dtype, name, tm=None, tn=None, tk=None):
    if dims == "tn":
        K, M = a.shape
    else:
        M, K = a.shape
    N = b.shape[0] if dims == "nt" else b.shape[1]
    tm = tm or _pick(M, (512, 256, 128))
    tn = tn or _pick(N, (1536, 1408, 1024, 768, 512, 256, 128))
    tk = tk or _pick(K, (1024, 1408, 512, 256, 128))
    nk = K // tk
    grid = (M // tm, N // tn, nk)
    if dims == "tn":
        a_spec = pl.BlockSpec((tk, tm), lambda i, j, k: (k, i))
    else:
        a_spec = pl.BlockSpec((tm, tk), lambda i, j, k: (i, k))
    if dims == "nt":
        b_spec = pl.BlockSpec((tn, tk), lambda i, j, k: (j, k))
    else:
        b_spec = pl.BlockSpec((tk, tn), lambda i, j, k: (k, j))
    o_spec = pl.BlockSpec((tm, tn), lambda i, j, k: (i, j))

    def body(a_ref, b_ref, o_ref, *scratch):
        x = a_ref[...].astype(BF16)
        y = b_ref[...].astype(BF16)
        if dims == "nn":
            r = _dot(x, y)
        elif dims == "nt":
            r = _dot_nt(x, y)
        else:
            r = _dot_tn(x, y)
        if nk == 1:
            o_ref[...] = r.astype(out_dtype)
        else:
            acc = scratch[0]
            k = pl.program_id(2)

            @pl.when(k == 0)
            def _():
                acc[...] = r

            @pl.when(k > 0)
            def _():
                acc[...] += r

            @pl.when(k == nk - 1)
            def _():
                o_ref[...] = acc[...].astype(out_dtype)

    return pl.pallas_call(
        body,
        name=name,
        grid=grid,
        in_specs=[a_spec, b_spec],
        out_specs=o_spec,
        out_shape=jax.ShapeDtypeStruct((M, N), out_dtype),
        scratch_shapes=[pltpu.VMEM((tm, tn), F32)] if nk > 1 else [],
        compiler_params=_params(dimension_semantics=("parallel", "parallel", "arbitrary")),
    )(a, b)


def _row_tile(S):
    return _pick(S, (256, 128, 64, 32, 16, 8))


def _norm_mod(x, g, sc, sh, name):
    S, D = x.shape
    T = _row_tile(S)
    row = pl.BlockSpec((T, D), lambda i: (i, 0))
    vec = pl.BlockSpec((1, D), lambda i: (0, 0))

    def body(x_ref, g_ref, sc_ref, sh_ref, h_ref):
        xv = x_ref[...]
        r = lax.rsqrt(jnp.mean(xv * xv, axis=-1, keepdims=True) + RMS_EPS)
        n = (xv * r) * g_ref[...]
        h_ref[...] = (n * (1.0 + sc_ref[...]) + sh_ref[...]).astype(BF16)

    return pl.pallas_call(
        body, name=name, grid=(S // T,), in_specs=[row, vec, vec, vec], out_specs=row,
        out_shape=jax.ShapeDtypeStruct((S, D), BF16), compiler_params=_params(),
    )(x, g, sc, sh)


def _norm_mod_bwd(x, g, sc, sh, dhs, gres, name):
    S, D = x.shape
    T = _row_tile(S)
    n_dh = len(dhs)
    row = pl.BlockSpec((T, D), lambda i: (i, 0))
    vec = pl.BlockSpec((1, D), lambda i: (0, 0))

    def body(x_ref, g_ref, sc_ref, sh_ref, *refs):
        dh_refs = refs[:n_dh]
        gres_ref, dx_ref, dg_ref, dsc_ref, dsh_ref = refs[n_dh:]
        xv = x_ref[...]
        r = lax.rsqrt(jnp.mean(xv * xv, axis=-1, keepdims=True) + RMS_EPS)
        xn = xv * r
        n = xn * g_ref[...]
        dh = dh_refs[0][...]
        for extra in dh_refs[1:]:
            dh = dh + extra[...]
        dn = dh * (1.0 + sc_ref[...])
        dxn = dn * g_ref[...]
        dx = r * (dxn - xn * jnp.mean(dxn * xn, axis=-1, keepdims=True))
        dx_ref[...] = gres_ref[...] + dx

        @pl.when(pl.program_id(0) == 0)
        def _():
            dg_ref[...] = jnp.zeros_like(dg_ref)
            dsc_ref[...] = jnp.zeros_like(dsc_ref)
            dsh_ref[...] = jnp.zeros_like(dsh_ref)

        dg_ref[...] += jnp.sum(dn * xn, axis=0, keepdims=True)
        dsc_ref[...] += jnp.sum(dh * n, axis=0, keepdims=True)
        dsh_ref[...] += jnp.sum(dh, axis=0, keepdims=True)

    vshape = jax.ShapeDtypeStruct((1, D), F32)
    return pl.pallas_call(
        body, name=name, grid=(S // T,), in_specs=[row, vec, vec, vec] + [row] * (n_dh + 1),
        out_specs=[row, vec, vec, vec],
        out_shape=[jax.ShapeDtypeStruct((S, D), F32), vshape, vshape, vshape],
        compiler_params=_params(dimension_semantics=("arbitrary",)),
    )(x, g, sc, sh, *dhs, gres)


def _post_res(x, p, gp, gt, name):
    S, D = x.shape
    T = _row_tile(S)
    row = pl.BlockSpec((T, D), lambda i: (i, 0))
    vec = pl.BlockSpec((1, D), lambda i: (0, 0))

    def body(x_ref, p_ref, gp_ref, gt_ref, o_ref):
        pv = p_ref[...]
        r = lax.rsqrt(jnp.mean(pv * pv, axis=-1, keepdims=True) + RMS_EPS)
        o_ref[...] = x_ref[...] + gt_ref[...] * ((pv * r) * gp_ref[...])

    return pl.pallas_call(
        body, name=name, grid=(S // T,), in_specs=[row, row, vec, vec], out_specs=row,
        out_shape=jax.ShapeDtypeStruct((S, D), F32), compiler_params=_params(),
    )(x, p, gp, gt)


def _post_res_bwd(p, gp, gt, g, name):
    S, D = p.shape
    T = _row_tile(S)
    row = pl.BlockSpec((T, D), lambda i: (i, 0))
    vec = pl.BlockSpec((1, D), lambda i: (0, 0))

    def body(p_ref, gp_ref, gt_ref, g_ref, dp_ref, dgp_ref, dgt_ref):
        pv = p_ref[...]
        gv = g_ref[...]
        r = lax.rsqrt(jnp.mean(pv * pv, axis=-1, keepdims=True) + RMS_EPS)
        pn = pv * r
        n2 = pn * gp_ref[...]
        dn2 = gv * gt_ref[...]
        dpn = dn2 * gp_ref[...]
        dp = r * (dpn - pn * jnp.mean(dpn * pn, axis=-1, keepdims=True))
        dp_ref[...] = dp.astype(BF16)

        @pl.when(pl.program_id(0) == 0)
        def _():
            dgp_ref[...] = jnp.zeros_like(dgp_ref)
            dgt_ref[...] = jnp.zeros_like(dgt_ref)

        dgp_ref[...] += jnp.sum(dn2 * pn, axis=0, keepdims=True)
        dgt_ref[...] += jnp.sum(gv * n2, axis=0, keepdims=True)

    vshape = jax.ShapeDtypeStruct((1, D), F32)
    return pl.pallas_call(
        body, name=name, grid=(S // T,), in_specs=[row, vec, vec, row], out_specs=[row, vec, vec],
        out_shape=[jax.ShapeDtypeStruct((S, D), BF16), vshape, vshape],
        compiler_params=_params(dimension_semantics=("arbitrary",)),
    )(p, gp, gt, g)


def _loss_head(y, target, name):
    S, D = y.shape
    T = _row_tile(S)
    row = pl.BlockSpec((T, D), lambda i: (i, 0))
    vec = pl.BlockSpec((1, D), lambda i: (0, 0))

    def body(y_ref, t_ref, dy_ref, sq_ref):
        e = y_ref[...] - t_ref[...]
        dy_ref[...] = e * (1.0 / D)

        @pl.when(pl.program_id(0) == 0)
        def _():
            sq_ref[...] = jnp.zeros_like(sq_ref)

        sq_ref[...] += jnp.sum(e * e, axis=0, keepdims=True)

    return pl.pallas_call(
        body, name=name, grid=(S // T,), in_specs=[row, row], out_specs=[row, vec],
        out_shape=[jax.ShapeDtypeStruct((S, D), F32), jax.ShapeDtypeStruct((1, D), F32)],
        compiler_params=_params(dimension_semantics=("arbitrary",)),
    )(y, target)


def _shift_down(v, k, rows):
    return jnp.where(rows >= k, pltpu.roll(v, k, 0), 0.0)


def _shift_up(v, k, rows, S):
    return jnp.where(rows < S - k, pltpu.roll(v, S - k, 0), 0.0)


def _conv_gate(gate_all, up_all, wc, bc, name):
    S, F = gate_all.shape
    C = LANES
    seq = pl.BlockSpec((S, C), lambda j: (0, j))

    def body(g_ref, u_ref, w_ref, b_ref, a_ref):
        rows = lax.broadcasted_iota(jnp.int32, (S, C), 0)
        gate = g_ref[...]
        w = w_ref[...]
        gc = w[2:3] * gate + w[1:2] * _shift_down(gate, 1, rows) + w[0:1] * _shift_down(gate, 2, rows) + b_ref[...]
        a_ref[...] = (gc * (1.0 / (1.0 + jnp.exp(-gc))) * u_ref[...]).astype(BF16)

    return pl.pallas_call(
        body, name=name, grid=(F // C,),
        in_specs=[seq, seq, pl.BlockSpec((3, C), lambda j: (0, j)), pl.BlockSpec((1, C), lambda j: (0, j))],
        out_specs=seq, out_shape=jax.ShapeDtypeStruct((S, F), BF16), compiler_params=_params(),
    )(gate_all, up_all, wc, bc)


def _conv_gate_bwd(gate_all, up_all, wc, bc, da, name):
    S, F = gate_all.shape
    C = LANES
    seq = pl.BlockSpec((S, C), lambda j: (0, j))

    def body(g_ref, u_ref, w_ref, b_ref, da_ref, dg_ref, du_ref, dw_ref, db_ref):
        rows = lax.broadcasted_iota(jnp.int32, (S, C), 0)
        gate = g_ref[...]
        up = u_ref[...]
        dav = da_ref[...]
        w = w_ref[...]
        g1 = _shift_down(gate, 1, rows)
        g2 = _shift_down(gate, 2, rows)
        gc = w[2:3] * gate + w[1:2] * g1 + w[0:1] * g2 + b_ref[...]
        sg = 1.0 / (1.0 + jnp.exp(-gc))
        du_ref[...] = (dav * (gc * sg)).astype(BF16)
        dgc = dav * up * (sg * (1.0 + gc * (1.0 - sg)))
        db_ref[...] = jnp.sum(dgc, axis=0, keepdims=True)
        dw_ref[0:1, :] = jnp.sum(dgc * g2, axis=0, keepdims=True)
        dw_ref[1:2, :] = jnp.sum(dgc * g1, axis=0, keepdims=True)
        dw_ref[2:3, :] = jnp.sum(dgc * gate, axis=0, keepdims=True)
        dgate = w[2:3] * dgc + w[1:2] * _shift_up(dgc, 1, rows, S) + w[0:1] * _shift_up(dgc, 2, rows, S)
        dg_ref[...] = dgate.astype(BF16)

    return pl.pallas_call(
        body, name=name, grid=(F // C,),
        in_specs=[seq, seq, pl.BlockSpec((3, C), lambda j: (0, j)), pl.BlockSpec((1, C), lambda j: (0, j)), seq],
        out_specs=[seq, seq, pl.BlockSpec((3, C), lambda j: (0, j)), pl.BlockSpec((1, C), lambda j: (0, j))],
        out_shape=[jax.ShapeDtypeStruct((S, F), BF16), jax.ShapeDtypeStruct((S, F), BF16),
                   jax.ShapeDtypeStruct((3, F), F32), jax.ShapeDtypeStruct((1, F), F32)],
        compiler_params=_params(),
    )(gate_all, up_all, wc, bc, da)


ATT_SCALE = HEAD_DIM ** -0.5


def _att_specs(S, D, TQ):
    nb = D // LANES
    q_spec = pl.BlockSpec((TQ, LANES), lambda p, i: (i, p))
    k_spec = pl.BlockSpec((S, LANES), lambda p, i: (0, nb + p))
    v_spec = pl.BlockSpec((S, LANES), lambda p, i: (0, 2 * nb + p))
    stat_spec = pl.BlockSpec((TQ, 2 * ATT_BLOCK), lambda p, i: (i, p))
    seq_spec = pl.BlockSpec((S, LANES), lambda p, i: (0, p))
    ck_spec = pl.BlockSpec((2, S // ATT_BLOCK, SUBLANES, ATT_BLOCK), lambda p, i: (p, 0, 0, 0))
    return q_spec, k_spec, v_spec, stat_spec, seq_spec, ck_spec


def _att_call(body, name, grid, in_specs, out_specs, out_shape, scratch, args, side, semantics):
    n_out = len(out_shape)
    if side is not None:
        in_specs = in_specs + [ANY] * len(side.inputs)
        out_specs = out_specs + [ANY] * len(side.out_shapes)
        out_shape = out_shape + side.out_shapes
        scratch = scratch + side.sems
        args = args + side.inputs
    res = pl.pallas_call(
        body, name=name, grid=grid, in_specs=in_specs, out_specs=out_specs, out_shape=out_shape,
        scratch_shapes=scratch, compiler_params=_params(dimension_semantics=semantics),
    )(*args)
    return res[:n_out], res[n_out:]


def _grid_ends(NP, NQ):
    p, qi = pl.program_id(0), pl.program_id(1)
    return jnp.logical_and(p == 0, qi == 0), jnp.logical_and(p == NP - 1, qi == NQ - 1)


def _head_lanes(hh):
    return slice(hh * HEAD_DIM, (hh + 1) * HEAD_DIM)


def _below(old, lo, new, axis=0):
    if lo == 0:
        return new
    keep = old[:lo] if axis == 0 else old[:, :lo]
    return jnp.concatenate([keep, new], axis=axis)


def _scaled_q(q_ref, hh):
    return (q_ref[:, _head_lanes(hh)].astype(F32) * ATT_SCALE).astype(BF16)


def _tri(cmp):
    B = ATT_BLOCK
    row = lax.broadcasted_iota(jnp.int32, (B, B), 0)
    col = lax.broadcasted_iota(jnp.int32, (B, B), 1)
    half = jnp.concatenate([jnp.where(cmp(row, col), 1.0, 0.0).astype(BF16), jnp.ones((B, B), BF16)], axis=1)
    return jnp.concatenate([half, half], axis=0)


def _hi_lo_dot(x, t):
    hi = x.astype(BF16)
    lo = (x - hi.astype(F32)).astype(BF16)
    return _dot(jnp.concatenate([hi, lo], axis=1), t)


def _key_minus_query(j, qi, TQ):
    row = lax.broadcasted_iota(jnp.int32, (TQ, ATT_BLOCK), 0)
    col = lax.broadcasted_iota(jnp.int32, (TQ, ATT_BLOCK), 1)
    return col - row + (j * ATT_BLOCK - qi * TQ)


def _log_sigmoids(z):
    sp = jnp.log(1.0 + jnp.exp(-jnp.abs(z)))
    lb = jnp.minimum(z, 0.0) - sp
    return lb, lb - z


def _log_sigmoids_fast(z):
    zc = jnp.maximum(z, -80.0)
    lb = -jnp.log(1.0 + jnp.exp(-zc))
    return lb, lb - zc


def _sb_fwd(qkv, name, side=None):
    S, D3 = qkv.shape
    D = D3 // 3
    B, TQ = ATT_BLOCK, SB_Q_TILE
    R = TQ // B
    NP = D // LANES
    NQ = S // TQ
    q_spec, k_spec, v_spec, stat_spec, _, _ = _att_specs(S, D, TQ)

    def body(*refs):
        (q_ref, k_ref, v_ref), (o_ref, lt_ref, first_ref), _, parts = _side_parts(side, refs, 3, 3)
        begin, end = _side_hooks(side, parts, *_grid_ends(NP, NQ))
        begin()
        qi = pl.program_id(1)
        t_suffix = _tri(lambda r, c: r > c)
        qs = [_scaled_q(q_ref, hh) for hh in range(2)]

        def tile(j, carry, masked, lo=0):
            r0 = pl.multiple_of(j * B, B)
            if masked:
                strict = _key_minus_query(j, qi, TQ)[lo:] < 0
            out = []
            for hh in range(2):
                c, acc = carry[hh]
                k = k_ref[pl.ds(r0, B), _head_lanes(hh)]
                v = v_ref[pl.ds(r0, B), _head_lanes(hh)]
                lb, l1 = _log_sigmoids_fast(_dot_nt(qs[hh][lo:], k))
                if masked:
                    l1 = jnp.where(strict, l1, 0.0)
                sums = _hi_lo_dot(l1, t_suffix)
                a = jnp.exp(lb + c[lo:] + sums[:, :B])
                if masked:
                    a = jnp.where(strict, a, 0.0)
                out.append((_below(c, lo, c[lo:] + sums[:, B:]), _below(acc, lo, acc[lo:] + _dot(a.astype(BF16), v))))
            return tuple(out)

        carry = (jnp.zeros((TQ, B), F32), jnp.zeros((TQ, HEAD_DIM), F32))
        carry = (carry, carry)
        for jj in reversed(range(R)):
            carry = tile(qi * R + jj, carry, True, jj * B)

        def alive(cr):
            return jnp.max(jnp.maximum(cr[0][0], cr[1][0])) > SB_DEAD_LOG

        def walk(state):
            j, cr, _ = state
            cr = tile(j, cr, False)
            return j - 1, cr, alive(cr)

        j, carry, _ = lax.while_loop(lambda st: jnp.logical_and(st[0] >= 0, st[2]), walk,
                                     (qi * R - 1, carry, alive(carry)))
        first_ref[pl.program_id(0), qi] = (j + 1).astype(F32)
        for hh in range(2):
            c, acc = carry[hh]
            o_ref[:, _head_lanes(hh)] = acc
            lt_ref[:, hh * B:(hh + 1) * B] = c
        end()

    return _att_call(
        body, name, (NP, NQ), [q_spec, k_spec, v_spec],
        [q_spec, stat_spec, pl.BlockSpec(memory_space=pltpu.SMEM)],
        [jax.ShapeDtypeStruct((S, D), F32), jax.ShapeDtypeStruct((S, 2 * NP * B), F32),
         jax.ShapeDtypeStruct((NP, NQ), F32)], [], [qkv, qkv, qkv], side, ("arbitrary", "arbitrary"))


def _sb_bwd(qkv, do, lt, first, name, side=None):
    S, D3 = qkv.shape
    D = D3 // 3
    B, TQ = ATT_BLOCK, SB_Q_TILE
    R = TQ // B
    NP = D // LANES
    NQ = S // TQ
    q_spec, k_spec, v_spec, stat_spec, seq_spec, _ = _att_specs(S, D, TQ)

    def body(*refs):
        ins, (dq_ref, dk_ref, dv_ref), (dk_acc, dv_acc), parts = _side_parts(side, refs, 6, 3)
        first_ref, q_ref, k_ref, v_ref, do_ref, lt_ref = ins
        begin, end = _side_hooks(side, parts, *_grid_ends(NP, NQ))
        begin()
        qi = pl.program_id(1)
        t_prefix = _tri(lambda r, c: r <= c)
        t_before = _tri(lambda r, c: r < c)

        @pl.when(qi == 0)
        def _():
            dk_acc[...] = jnp.zeros_like(dk_acc)
            dv_acc[...] = jnp.zeros_like(dv_acc)

        qs = [_scaled_q(q_ref, hh) for hh in range(2)]
        dob = [do_ref[:, _head_lanes(hh)].astype(BF16) for hh in range(2)]
        ltot = [lt_ref[:, hh * B:(hh + 1) * B] for hh in range(2)]

        def tile(j, carry, masked, lo=0):
            r0 = pl.multiple_of(j * B, B)
            if masked:
                strict = _key_minus_query(j, qi, TQ)[lo:] < 0
            out = []
            for hh in range(2):
                pre, cu, dq = carry[hh]
                lanes = _head_lanes(hh)
                k = k_ref[pl.ds(r0, B), lanes]
                v = v_ref[pl.ds(r0, B), lanes]
                lb, l1 = _log_sigmoids_fast(_dot_nt(qs[hh][lo:], k))
                if masked:
                    l1 = jnp.where(strict, l1, 0.0)
                sums = _hi_lo_dot(l1, t_prefix)
                a = jnp.exp(lb + (ltot[hh][lo:] - pre[lo:] - sums[:, :B]))
                if masked:
                    a = jnp.where(strict, a, 0.0)
                u = a * _dot_nt(dob[hh][lo:], v)
                usums = _hi_lo_dot(u, t_before)
                dz = u - (u + cu[lo:] + usums[:, :B]) * jnp.exp(lb)
                if masked:
                    dz = jnp.where(strict, dz, 0.0)
                dzb = dz.astype(BF16)
                dk_acc[pl.ds(r0, B), lanes] += _dot_tn(dzb, qs[hh][lo:])
                dv_acc[pl.ds(r0, B), lanes] += _dot_tn(a.astype(BF16), dob[hh][lo:])
                out.append((_below(pre, lo, pre[lo:] + sums[:, B:]), _below(cu, lo, cu[lo:] + usums[:, B:]),
                            _below(dq, lo, dq[lo:] + _dot(dzb, k))))
            return tuple(out)

        zero = (jnp.zeros((TQ, B), F32), jnp.zeros((TQ, B), F32), jnp.zeros((TQ, HEAD_DIM), F32))
        first = jnp.clip(first_ref[pl.program_id(0), qi].astype(jnp.int32), 0, qi * R)
        carry = lax.fori_loop(first, qi * R, lambda j, cr: tile(j, cr, False), (zero, zero))
        for jj in range(R):
            carry = tile(qi * R + jj, carry, True, jj * B)
        for hh in range(2):
            dq_ref[:, _head_lanes(hh)] = (carry[hh][2] * ATT_SCALE).astype(BF16)

        @pl.when(qi == NQ - 1)
        def _():
            dk_ref[...] = dk_acc[...].astype(BF16)
            dv_ref[...] = dv_acc[...].astype(BF16)

        end()

    out = jax.ShapeDtypeStruct((S, D), BF16)
    return _att_call(
        body, name, (NP, NQ), [pl.BlockSpec(memory_space=pltpu.SMEM), q_spec, k_spec, v_spec, q_spec, stat_spec],
        [q_spec, seq_spec, seq_spec], [out, out, out],
        [pltpu.VMEM((S, LANES), F32), pltpu.VMEM((S, LANES), F32)], [first, qkv, qkv, qkv, do, lt], side,
        ("arbitrary", "arbitrary"))


def _fox_specs(S, D, TQ):
    row_spec = pl.BlockSpec((2 * SUBLANES, TQ), lambda p, i: (p, i))
    keyb_spec = pl.BlockSpec((S, 2 * ATT_BLOCK), lambda p, i: (0, p))
    return row_spec, keyb_spec


def _key_gt_query_t(j, qi, TQ):
    key = lax.broadcasted_iota(jnp.int32, (ATT_BLOCK, TQ), 0)
    qry = lax.broadcasted_iota(jnp.int32, (ATT_BLOCK, TQ), 1)
    return key - qry + (j * ATT_BLOCK - qi * TQ) > 0


def _fox_fwd(qkv, cq_rows, ck_b, name, side=None):
    S, D3 = qkv.shape
    D = D3 // 3
    B, TQ = ATT_BLOCK, FOX_Q_TILE
    R = TQ // B
    NP = D // LANES
    NQ = S // TQ
    q_spec, k_spec, v_spec, _, _, _ = _att_specs(S, D, TQ)
    row_spec, keyb_spec = _fox_specs(S, D, TQ)

    def body(*refs):
        (q_ref, k_ref, v_ref, cq_ref, ck_ref), (o_ref, lse_ref), _, parts = _side_parts(side, refs, 5, 2)
        begin, end = _side_hooks(side, parts, *_grid_ends(NP, NQ))
        begin()
        qi = pl.program_id(1)
        qs = [_scaled_q(q_ref, hh) for hh in range(2)]
        cq = [cq_ref[hh * SUBLANES:hh * SUBLANES + 1, :] for hh in range(2)]

        def tile(j, carry, masked, lo=0):
            r0 = pl.multiple_of(j * B, B)
            if masked:
                hidden = _key_gt_query_t(j, qi, TQ)[:, lo:]
            out = []
            for hh in range(2):
                m, lsum, acc = carry[hh]
                k = k_ref[pl.ds(r0, B), _head_lanes(hh)]
                v = v_ref[pl.ds(r0, B), _head_lanes(hh)]
                ck = jnp.tile(ck_ref[pl.ds(r0, B), hh * B:(hh + 1) * B], (1, (TQ - lo) // B))
                s = _dot_nt(k, qs[hh][lo:]) + (cq[hh][:, lo:] - ck)
                if masked:
                    s = jnp.where(hidden, NEG_BIG, s)
                m_new = jnp.maximum(m[:, lo:], jnp.max(s, axis=0, keepdims=True))
                p = jnp.exp(s - m_new)
                alpha = jnp.exp(m[:, lo:] - m_new)
                out.append((_below(m, lo, m_new, 1),
                            _below(lsum, lo, alpha * lsum[:, lo:] + jnp.sum(p, axis=0, keepdims=True), 1),
                            _below(acc, lo, alpha * acc[:, lo:] + _dot_tn(v, p.astype(BF16)), 1)))
            return tuple(out)

        zero = (jnp.full((1, TQ), NEG_BIG, F32), jnp.zeros((1, TQ), F32), jnp.zeros((HEAD_DIM, TQ), F32))
        carry = lax.fori_loop(0, qi * R, lambda j, cr: tile(j, cr, False), (zero, zero))
        for jj in range(R):
            carry = tile(qi * R + jj, carry, True, jj * B)
        o_t = jnp.concatenate([carry[hh][2] * (1.0 / carry[hh][1]) for hh in range(2)], axis=0)
        o_ref[...] = o_t.T
        for hh in range(2):
            m, lsum, _ = carry[hh]
            lse_ref[hh * SUBLANES:(hh + 1) * SUBLANES, :] = jnp.broadcast_to(m + jnp.log(lsum), (SUBLANES, TQ))
        end()

    return _att_call(
        body, name, (NP, NQ), [q_spec, k_spec, v_spec, row_spec, keyb_spec], [q_spec, row_spec],
        [jax.ShapeDtypeStruct((S, D), F32), jax.ShapeDtypeStruct((2 * NP * SUBLANES, S), F32)], [],
        [qkv, qkv, qkv, cq_rows, ck_b], side, ("arbitrary", "arbitrary"))


def _fox_bwd(qkv, do, o, lse_rows, cq_rows, ck_b, name, side=None):
    S, D3 = qkv.shape
    D = D3 // 3
    B, TQ = ATT_BLOCK, FOX_Q_TILE
    R = TQ // B
    NP = D // LANES
    NQ = S // TQ
    q_spec, k_spec, v_spec, _, seq_spec, _ = _att_specs(S, D, TQ)
    row_spec, keyb_spec = _fox_specs(S, D, TQ)

    def body(*refs):
        ins, outs, (dk_acc, dv_acc), parts = _side_parts(side, refs, 8, 5)
        q_ref, k_ref, v_ref, do_ref, o_ref, lse_ref, cq_ref, ck_ref = ins
        dq_ref, dk_ref, dv_ref, dcq_ref, dck_ref = outs
        begin, end = _side_hooks(side, parts, *_grid_ends(NP, NQ))
        begin()
        qi = pl.program_id(1)

        @pl.when(qi == 0)
        def _():
            dk_acc[...] = jnp.zeros_like(dk_acc)
            dv_acc[...] = jnp.zeros_like(dv_acc)
            dck_ref[...] = jnp.zeros_like(dck_ref)

        qs = [_scaled_q(q_ref, hh) for hh in range(2)]
        dob = [do_ref[:, _head_lanes(hh)].astype(BF16) for hh in range(2)]
        prod_t = (do_ref[...] * o_ref[...]).T
        delta = [jnp.sum(prod_t[hh * HEAD_DIM:(hh + 1) * HEAD_DIM], axis=0, keepdims=True) for hh in range(2)]
        cq = [cq_ref[hh * SUBLANES:hh * SUBLANES + 1, :] for hh in range(2)]
        lse = [lse_ref[hh * SUBLANES:hh * SUBLANES + 1, :] for hh in range(2)]

        def tile(j, carry, masked, lo=0):
            r0 = pl.multiple_of(j * B, B)
            if masked:
                hidden = _key_gt_query_t(j, qi, TQ)[:, lo:]
            out = []
            for hh in range(2):
                dq, keysum = carry[hh]
                lanes = _head_lanes(hh)
                k = k_ref[pl.ds(r0, B), lanes]
                v = v_ref[pl.ds(r0, B), lanes]
                ck = jnp.tile(ck_ref[pl.ds(r0, B), hh * B:(hh + 1) * B], (1, (TQ - lo) // B))
                p = jnp.exp(_dot_nt(k, qs[hh][lo:]) + (cq[hh][:, lo:] - ck) - lse[hh][:, lo:])
                if masked:
                    p = jnp.where(hidden, 0.0, p)
                ds = p * (_dot_nt(v, dob[hh][lo:]) - delta[hh][:, lo:])
                dsb = ds.astype(BF16)
                dk_acc[pl.ds(r0, B), lanes] += _dot(dsb, qs[hh][lo:])
                dv_acc[pl.ds(r0, B), lanes] += _dot(p.astype(BF16), dob[hh][lo:])
                qsum = jnp.sum(ds, axis=1, keepdims=True)
                dck_ref[pl.ds(r0, B), hh * B:(hh + 1) * B] -= jnp.broadcast_to(qsum, (B, B))
                out.append((_below(dq, lo, dq[:, lo:] + _dot_tn(k, dsb), 1),
                            _below(keysum, lo, keysum[:, lo:] + jnp.sum(ds, axis=0, keepdims=True), 1)))
            return tuple(out)

        zero = (jnp.zeros((HEAD_DIM, TQ), F32), jnp.zeros((1, TQ), F32))
        carry = lax.fori_loop(0, qi * R, lambda j, cr: tile(j, cr, False), (zero, zero))
        for jj in range(R):
            carry = tile(qi * R + jj, carry, True, jj * B)
        dq_t = jnp.concatenate([carry[hh][0] for hh in range(2)], axis=0)
        dq_ref[...] = (dq_t.T * ATT_SCALE).astype(BF16)
        for hh in range(2):
            dcq_ref[hh * SUBLANES:(hh + 1) * SUBLANES, :] = jnp.broadcast_to(carry[hh][1], (SUBLANES, TQ))

        @pl.when(qi == NQ - 1)
        def _():
            dk_ref[...] = dk_acc[...].astype(BF16)
            dv_ref[...] = dv_acc[...].astype(BF16)

        end()

    out = jax.ShapeDtypeStruct((S, D), BF16)
    return _att_call(
        body, name, (NP, NQ), [q_spec, k_spec, v_spec, q_spec, q_spec, row_spec, row_spec, keyb_spec],
        [q_spec, seq_spec, seq_spec, row_spec, keyb_spec],
        [out, out, out, jax.ShapeDtypeStruct((2 * NP * SUBLANES, S), F32), jax.ShapeDtypeStruct((S, 2 * NP * B), F32)],
        [pltpu.VMEM((S, LANES), F32), pltpu.VMEM((S, LANES), F32)],
        [qkv, qkv, qkv, do, o, lse_rows, cq_rows, ck_b], side, ("arbitrary", "arbitrary"))


def _forget_cumsum(f3, bias, name):
    NQ, NH, B = f3.shape

    def body(f_ref, b_ref, cum_ref):
        row = lax.broadcasted_iota(jnp.int32, (B, B), 0)
        col = lax.broadcasted_iota(jnp.int32, (B, B), 1)
        t_incl = jnp.where(row <= col, 1.0, 0.0).astype(BF16)

        def step(b, carry):
            lf, _ = _log_sigmoids(f_ref[b] + b_ref[...])
            cum = _split_dot(lf, t_incl, 3) + carry
            cum_ref[b] = cum
            return jnp.broadcast_to(cum[:, B - 1:B], (NH, B))

        lax.fori_loop(0, NQ, step, jnp.zeros((NH, B), F32))

    return pl.pallas_call(
        body, name=name, in_specs=[VMEM_WHOLE, VMEM_WHOLE], out_specs=VMEM_WHOLE,
        out_shape=jax.ShapeDtypeStruct((NQ, NH, B), F32), compiler_params=_params(),
    )(f3, bias)


def _forget_cumsum_bwd(dcum3, f3, bias, name):
    NQ, NH, B = f3.shape

    def body(d_ref, f_ref, b_ref, df_ref, tot_ref):
        row = lax.broadcasted_iota(jnp.int32, (B, B), 0)
        col = lax.broadcasted_iota(jnp.int32, (B, B), 1)
        t_rev = jnp.where(row >= col, 1.0, 0.0).astype(BF16)

        def step(it, carry):
            run, tot = carry
            b = NQ - 1 - it
            dlf = _split_dot(d_ref[b], t_rev, 3) + run
            f = f_ref[b] + b_ref[...]
            df = dlf * (1.0 / (1.0 + jnp.exp(f)))
            df_ref[b] = df
            return jnp.broadcast_to(dlf[:, 0:1], (NH, B)), tot + df

        _, tot = lax.fori_loop(0, NQ, step, (jnp.zeros((NH, B), F32), jnp.zeros((NH, B), F32)))
        tot_ref[...] = _split_dot(tot, jnp.ones((B, B), BF16), 3)

    return pl.pallas_call(
        body, name=name, in_specs=[VMEM_WHOLE, VMEM_WHOLE, VMEM_WHOLE], out_specs=[VMEM_WHOLE, VMEM_WHOLE],
        out_shape=[jax.ShapeDtypeStruct((NQ, NH, B), F32), jax.ShapeDtypeStruct((NH, B), F32)],
        compiler_params=_params(),
    )(dcum3, f3, bias)


def _silu(x, name):
    def body(x_ref, o_ref):
        v = x_ref[...]
        o_ref[...] = v * (1.0 / (1.0 + jnp.exp(-v)))

    return pl.pallas_call(body, name=name, in_specs=[VMEM_WHOLE], out_specs=VMEM_WHOLE,
                          out_shape=jax.ShapeDtypeStruct(x.shape, F32), compiler_params=_params())(x)


def _sum_leading(x, name):
    N, R, C = x.shape
    T = _pick(R, (256, 128, 64, 32, 16, 8))

    def body(x_ref, o_ref):
        acc = x_ref[0].astype(F32)
        for r in range(1, N):
            acc = acc + x_ref[r].astype(F32)
        o_ref[...] = acc

    return pl.pallas_call(
        body, name=name, grid=(R // T,), in_specs=[pl.BlockSpec((N, T, C), lambda i: (0, i, 0))],
        out_specs=pl.BlockSpec((T, C), lambda i: (i, 0)), out_shape=jax.ShapeDtypeStruct((R, C), F32),
        compiler_params=_params(),
    )(x)


def _adamw(w, g, m, v, name):
    shape = w.shape
    C = shape[-1]
    R = w.size // C
    T = R
    for cand in (512, 256, 128, 64, 32, 16, 8):
        if R % cand == 0 and cand * C * 4 <= (1 << 20):
            T = cand
            break
    spec = pl.BlockSpec((T, C), lambda i: (i, 0))
    c1 = 1.0 / (1.0 - ADAM_B1 ** ADAM_STEP)
    c2 = 1.0 / (1.0 - ADAM_B2 ** ADAM_STEP)

    def body(w_ref, g_ref, m_ref, v_ref, d_ref, nm_ref, nv_ref):
        gv = g_ref[...]
        nm = ADAM_B1 * m_ref[...] + (1.0 - ADAM_B1) * gv
        nv = ADAM_B2 * v_ref[...] + (1.0 - ADAM_B2) * (gv * gv)
        nm_ref[...] = nm
        nv_ref[...] = nv
        d_ref[...] = -ADAM_LR * ((nm * c1) / (jnp.sqrt(nv * c2) + ADAM_EPS) + ADAM_WD * w_ref[...])

    out = jax.ShapeDtypeStruct((R, C), F32)
    d, nm, nv = pl.pallas_call(
        body, name=name, grid=(R // T,), in_specs=[spec] * 4, out_specs=[spec] * 3, out_shape=[out] * 3,
        compiler_params=_params(),
    )(w.reshape(R, C), g.reshape(R, C), m.reshape(R, C), v.reshape(R, C))
    return d.reshape(shape), nm.reshape(shape), nv.reshape(shape)


def _mesh_pos():
    return lax.axis_index("x"), lax.axis_index("y"), lax.axis_index("c")


def _all_gather_small(x2d, name):
    m_per, n = x2d.shape

    def body(x_ref, out_ref, send_sems, recv_sems, local_sem):
        x, y, c = _mesh_pos()
        me, sibling = (x, y, c), (x, y, 1 - c)
        chips = [(1 - x, y), (x, 1 - y), (1 - x, 1 - y)]

        def rows(px, py, pc):
            return out_ref.at[pl.ds((4 * px + 2 * py + pc) * m_per, m_per), :]

        def copy(k, block, to, src=None):
            return pltpu.make_async_remote_copy(
                src_ref=rows(*block) if src is None else src, dst_ref=rows(*block),
                send_sem=send_sems.at[k], recv_sem=recv_sems.at[k], device_id=to, device_id_type=MESH)

        mine = pltpu.make_async_copy(x_ref, rows(*me), local_sem)
        mine.start()
        first = [copy(0, me, sibling, src=x_ref)]
        first += [copy(1 + j, me, (*chip, c), src=x_ref) for j, chip in enumerate(chips)]
        for cp in first:
            cp.start()
        passed = [copy(4 + j, (*chip, c), sibling) for j, chip in enumerate(chips)]
        for j, chip in enumerate(chips):
            copy(1 + j, (*chip, c), me).wait_recv()
            passed[j].start()
        copy(0, sibling, me).wait_recv()
        for j, chip in enumerate(chips):
            copy(4 + j, (*chip, 1 - c), me).wait_recv()
        for cp in first + passed:
            cp.wait_send()
        mine.wait()

    return pl.pallas_call(
        body, name=name, out_shape=jax.ShapeDtypeStruct((N_DEV * m_per, n), x2d.dtype),
        in_specs=[VMEM_WHOLE], out_specs=VMEM_WHOLE,
        scratch_shapes=[pltpu.SemaphoreType.DMA((7,)), pltpu.SemaphoreType.DMA((7,)), pltpu.SemaphoreType.DMA],
        compiler_params=_params(),
    )(x2d)


class _Side:
    def __init__(self, inputs, out_shapes, sems, start, wait):
        self.inputs, self.out_shapes, self.sems, self.start, self.wait = inputs, out_shapes, sems, start, wait


def _join_sides(sides):
    sides = [s for s in sides if s is not None]
    if not sides:
        return None
    bounds_in, bounds_out, bounds_sem = [0], [0], [0]
    for s in sides:
        bounds_in.append(bounds_in[-1] + len(s.inputs))
        bounds_out.append(bounds_out[-1] + len(s.out_shapes))
        bounds_sem.append(bounds_sem[-1] + len(s.sems))

    def each(method):
        def run(ins, outs, sems):
            for t, s in enumerate(sides):
                getattr(s, method)(ins[bounds_in[t]:bounds_in[t + 1]], outs[bounds_out[t]:bounds_out[t + 1]],
                                   sems[bounds_sem[t]:bounds_sem[t + 1]])
        return run

    return _Side([a for s in sides for a in s.inputs], [o for s in sides for o in s.out_shapes],
                 [m for s in sides for m in s.sems], each("start"), each("wait"))


def _side_parts(side, refs, n_in, n_out):
    if side is None:
        return refs[:n_in], refs[n_in:n_in + n_out], refs[n_in + n_out:], None
    si, so, ss = len(side.inputs), len(side.out_shapes), len(side.sems)
    a = n_in + si
    b = a + n_out + so
    ins, side_in = refs[:n_in], refs[n_in:a]
    outs, side_out = refs[a:a + n_out], refs[a + n_out:b]
    scratch, side_sems = refs[b:len(refs) - ss], refs[len(refs) - ss:]
    return ins, outs, scratch, (side_in, side_out, side_sems)


def _side_hooks(side, parts, first, last):
    if side is None:
        return lambda: None, lambda: None

    def begin():
        @pl.when(first)
        def _():
            side.start(*parts)

    def end():
        @pl.when(last)
        def _():
            side.wait(*parts)

    return begin, end


def _run_side(side, name):
    n_in = len(side.inputs)
    n_out = len(side.out_shapes)

    def body(*refs):
        parts = (refs[:n_in], refs[n_in:n_in + n_out], refs[n_in + n_out:])
        side.start(*parts)
        side.wait(*parts)

    return pl.pallas_call(
        body, name=name, out_shape=side.out_shapes, in_specs=[ANY] * n_in, out_specs=[ANY] * n_out,
        scratch_shapes=side.sems, compiler_params=_params(),
    )(*side.inputs)


def _gather_side(ws):
    n = len(ws)

    def copies(ins, outs, sems):
        send_sems, recv_sems, local_sems = sems
        x, y, c = _mesh_pos()
        k_me = 2 * x + y
        chips = [(1 - x, y), (x, 1 - y), (1 - x, 1 - y)]

        def remote(i, j, slot):
            px, py = chips[j]
            return pltpu.make_async_remote_copy(
                src_ref=ins[i], dst_ref=outs[i].at[slot], send_sem=send_sems.at[i, j],
                recv_sem=recv_sems.at[i, j], device_id=(px, py, c), device_id_type=MESH)

        local = [pltpu.make_async_copy(ins[i], outs[i].at[k_me], local_sems.at[i]) for i in range(n)]
        return remote, local, k_me, chips

    def start(ins, outs, sems):
        remote, local, k_me, _ = copies(ins, outs, sems)
        for i in range(n):
            local[i].start()
            for j in range(3):
                remote(i, j, k_me).start()

    def wait(ins, outs, sems):
        remote, local, k_me, chips = copies(ins, outs, sems)
        for i in range(n):
            for j, (px, py) in enumerate(chips):
                remote(i, j, 2 * px + py).wait_recv()
        for i in range(n):
            for j in range(3):
                remote(i, j, k_me).wait_send()
            local[i].wait()

    return _Side(list(ws), [jax.ShapeDtypeStruct((N_CHIPS,) + w.shape, w.dtype) for w in ws],
                 [pltpu.SemaphoreType.DMA((n, 3)), pltpu.SemaphoreType.DMA((n, 3)), pltpu.SemaphoreType.DMA((n,))],
                 start, wait)


def _scatter_side(gs):
    n = len(gs)
    halves = [g.shape[1] // 2 for g in gs]

    def copies(ins, outs, sems):
        send_sems, recv_sems, local_sems = sems
        x, y, c = _mesh_pos()

        def flip(v, bit):
            return 1 - v if bit else v

        def piece(i, px, py, pc):
            return ins[i].at[2 * px + py, pl.ds(pc * halves[i], halves[i])]

        def remote(i, r):
            px, py, pc = flip(x, r & 4), flip(y, r & 2), flip(c, r & 1)
            return pltpu.make_async_remote_copy(
                src_ref=piece(i, px, py, pc), dst_ref=outs[i].at[r], send_sem=send_sems.at[i, r - 1],
                recv_sem=recv_sems.at[i, r - 1], device_id=(px, py, pc), device_id_type=MESH)

        local = [pltpu.make_async_copy(piece(i, x, y, c), outs[i].at[0], local_sems.at[i]) for i in range(n)]
        return remote, local

    def start(ins, outs, sems):
        remote, local = copies(ins, outs, sems)
        for i in range(n):
            local[i].start()
            for r in range(1, N_DEV):
                remote(i, r).start()

    def wait(ins, outs, sems):
        remote, local = copies(ins, outs, sems)
        for i in range(n):
            for r in range(1, N_DEV):
                remote(i, r).wait_recv()
        for i in range(n):
            for r in range(1, N_DEV):
                remote(i, r).wait_send()
            local[i].wait()

    return _Side(list(gs), [jax.ShapeDtypeStruct((N_DEV, h) + g.shape[2:], g.dtype) for g, h in zip(gs, halves)],
                 [pltpu.SemaphoreType.DMA((n, N_DEV - 1)), pltpu.SemaphoreType.DMA((n, N_DEV - 1)),
                  pltpu.SemaphoreType.DMA((n,))], start, wait)


def _swap_side(hs):
    n = len(hs)

    def copies(ins, outs, sems):
        send_sems, recv_sems, local_sems = sems
        x, y, c = _mesh_pos()

        def remote(i, slot):
            return pltpu.make_async_remote_copy(
                src_ref=ins[i], dst_ref=outs[i].at[slot], send_sem=send_sems.at[i], recv_sem=recv_sems.at[i],
                device_id=(x, y, 1 - c), device_id_type=MESH)

        local = [pltpu.make_async_copy(ins[i], outs[i].at[c], local_sems.at[i]) for i in range(n)]
        return remote, local, c

    def start(ins, outs, sems):
        remote, local, c = copies(ins, outs, sems)
        for i in range(n):
            local[i].start()
            remote(i, c).start()

    def wait(ins, outs, sems):
        remote, local, c = copies(ins, outs, sems)
        for i in range(n):
            remote(i, 1 - c).wait_recv()
        for i in range(n):
            remote(i, c).wait_send()
            local[i].wait()

    return _Side(list(hs), [jax.ShapeDtypeStruct((2,) + h.shape, h.dtype) for h in hs],
                 [pltpu.SemaphoreType.DMA((n,)), pltpu.SemaphoreType.DMA((n,)), pltpu.SemaphoreType.DMA((n,))],
                 start, wait)


def _pad_rows(a, rows):
    return jnp.pad(a, ((0, rows - a.shape[0]), (0, 0)))


def kernel(x, c, w_mod, b_mod, g_mix_pre, g_mix_post, w_qkv, w_o, w_fg, b_fg, g_ffn_pre, g_ffn_post, w_ffn_gate, w_ffn_up, w_conv, b_conv, w_ffn_down, loss_target, m_w_mod, m_b_mod, m_g_mix_pre, m_g_mix_post, m_w_qkv, m_w_o, m_w_fg, m_b_fg, m_g_ffn_pre, m_g_ffn_post, m_w_ffn_gate, m_w_ffn_up, m_w_conv, m_b_conv, m_w_ffn_down, v_w_mod, v_b_mod, v_g_mix_pre, v_g_mix_post, v_w_qkv, v_w_o, v_w_fg, v_b_fg, v_g_ffn_pre, v_g_ffn_post, v_w_ffn_gate, v_w_ffn_up, v_w_conv, v_b_conv, v_w_ffn_down):
    xs = x[0]
    target = loss_target[0]
    S, D = xs.shape
    L = w_mod.shape[0]
    LF = w_fg.shape[0]
    MS = w_mod.shape[2]
    QS = w_qkv.shape[2]
    OS = w_o.shape[1]
    FS = w_ffn_gate.shape[2]
    F = N_CHIPS * FS
    NH = D // HEAD_DIM
    B = ATT_BLOCK
    NQ = S // B
    ax, ay, ac = _mesh_pos()
    k_me = 2 * ax + ay
    b_me = 4 * ax + 2 * ay + ac

    conv_rows = -(-(L * 3 * FS) // D)
    conv_rows = -(-conv_rows // SUBLANES) * SUBLANES
    conv_flat = jnp.pad(w_conv.reshape(-1), (0, conv_rows * D - L * 3 * FS)).reshape(conv_rows, D)
    first = jnp.concatenate([_pad_rows(c, SUBLANES), conv_flat], axis=0)
    first_all = _all_gather_small(first, "ag_cond").reshape(N_DEV, SUBLANES + conv_rows, D)
    c_all = first_all[:, 0, :]
    conv_all = first_all[0::2, SUBLANES:, :].reshape(N_CHIPS, -1)[:, :L * 3 * FS]
    w_conv_full = conv_all.reshape(N_CHIPS, L, 3, FS).transpose(1, 2, 0, 3).reshape(L, 3, F)
    c_act = _silu(c_all, "silu_c")

    mod_part = jnp.concatenate(
        [_mm(c_act, w_mod[l], "nn", F32, "mm_mod", tm=N_DEV, tn=MS, tk=D) for l in range(L)], axis=1)
    mod_all = _all_gather_small(mod_part, "ag_mod").reshape(N_CHIPS, 2, N_DEV, L, MS)[:, 0]
    mod_mine = lax.dynamic_index_in_dim(mod_all, b_me, axis=1, keepdims=False)
    mod = mod_mine.transpose(1, 0, 2).reshape(L, N_MOD * D) + b_mod

    shards_b = [w.astype(BF16) for w in (w_qkv, w_o, w_ffn_gate, w_ffn_up, w_ffn_down)]
    w_fg_b = w_fg.astype(BF16)

    def layer_shards(l):
        ws = [w[l] for w in shards_b]
        return ws + [w_fg_b[l // 2]] if l % 2 == 1 else ws

    def relayout(gathered):
        def side_by_side(gth):
            return gth.transpose(1, 0, 2).reshape(D, -1)

        gq, go, gg, gu_, gd = gathered[:5]
        wts = dict(qkv=side_by_side(gq), o=go.reshape(D, D), g=side_by_side(gg), u=side_by_side(gu_),
                   d=gd.reshape(F, D))
        if len(gathered) > 5:
            wts["fg"] = jnp.pad(gathered[5].reshape(D, NH), ((0, 0), (0, LANES - NH)))
        return wts

    W = [None] * L
    W[0] = relayout(_run_side(_gather_side(layer_shards(0)), "ag_weights_first"))

    def vec(a):
        return a.reshape(1, -1)

    saved = []
    xcur = xs
    for l in range(L):
        sh_a, sc_a, gt_a, sh_f, sc_f, gt_f = [vec(mod[l, j * D:(j + 1) * D]) for j in range(N_MOD)]
        is_fox = l % 2 == 1
        jf = l // 2
        h1 = _norm_mod(xcur, vec(g_mix_pre[l]), sc_a, sh_a, "norm_mod")
        qkv = _mm(h1, W[l]["qkv"], "nn", BF16, "mm_qkv")
        next_weights = _gather_side(layer_shards(l + 1)) if l + 1 < L else None
        if is_fox:
            flog = _mm(h1, W[l]["fg"], "nn", F32, "mm_fg")[:, :NH]
            f3 = flog.reshape(NQ, B, NH).transpose(0, 2, 1)
            bias = b_fg[jf].reshape(NH, 1)
            cum3 = _forget_cumsum(f3, bias, "forget_cumsum")
            cum_sn = cum3.transpose(0, 2, 1).reshape(S, NH)
            ck_b = jnp.repeat(cum_sn, B, axis=1)
            cq_rows = jnp.repeat(cum_sn.T, SUBLANES, axis=0)
            (o, stat), gathered = _fox_fwd(qkv, cq_rows, ck_b, "fox_fwd", next_weights)
            extra = (f3, bias, cq_rows, ck_b)
        else:
            (o, stat, extra), gathered = _sb_fwd(qkv, "sb_fwd", next_weights)
        if next_weights is not None:
            W[l + 1] = relayout(gathered)
        p = _mm(o, W[l]["o"], "nn", F32, "mm_o")
        x1 = _post_res(xcur, p, vec(g_mix_post[l]), gt_a, "post_res")
        h2 = _norm_mod(x1, vec(g_ffn_pre[l]), sc_f, sh_f, "norm_mod")
        gate = _mm(h2, W[l]["g"], "nn", F32, "mm_gate")
        up = _mm(h2, W[l]["u"], "nn", F32, "mm_up")
        wc = w_conv_full[l]
        bc = vec(b_conv[l])
        a = _conv_gate(gate, up, wc, bc, "conv_gate")
        yv = _mm(a, W[l]["d"], "nn", F32, "mm_down")
        x2 = _post_res(x1, yv, vec(g_ffn_post[l]), gt_f, "post_res")
        saved.append(dict(x0=xcur, h1=h1, qkv=qkv, o=o, stat=stat, extra=extra, p=p, x1=x1, h2=h2, gate=gate, up=up, a=a,
                          y=yv, mods=(sh_a, sc_a, gt_a, sh_f, sc_f, gt_f), wc=wc, bc=bc))
        xcur = x2

    g, sq = _loss_head(xcur, target, "loss_head")
    loss_part = 0.5 * jnp.sum(sq) / D
    loss = lax.psum(loss_part, ("x", "y", "c"))

    dW_qkv, dW_o, dW_g, dW_u, dW_d = [[None] * L for _ in range(5)]
    dW_fg, db_fg = [None] * LF, [None] * LF
    dmod, dg_mix_pre, dg_mix_post, dg_ffn_pre, dg_ffn_post = [[None] * L for _ in range(5)]
    dw_conv, db_conv = [None] * L, [None] * L

    def cols(dw, width):
        return dw.reshape(dw.shape[0], N_CHIPS, width).transpose(1, 0, 2).astype(BF16)

    def rows(dw, height):
        return dw.reshape(N_CHIPS, height, dw.shape[1]).astype(BF16)

    n_big = 5
    grad_shards = [[None] * n_big for _ in range(L)]
    to_scatter, to_swap = [], []

    def carried():
        sc, sw = list(to_scatter), list(to_swap)
        del to_scatter[:], to_swap[:]
        side = _join_sides([_scatter_side([a for _, _, a in sc]) if sc else None,
                            _swap_side([a for _, _, a in sw]) if sw else None])

        def taken(outs):
            for (lay, i, _), recv in zip(sc, outs[:len(sc)]):
                to_swap.append((lay, i, _sum_leading(recv, f"sum_grad_pieces_{i}")))
            for (lay, i, _), both in zip(sw, outs[len(sc):]):
                grad_shards[lay][i] = both.reshape((-1,) + both.shape[2:])

        return side, taken

    for l in reversed(range(L)):
        sv = saved[l]
        sh_a, sc_a, gt_a, sh_f, sc_f, gt_f = sv["mods"]
        is_fox = l % 2 == 1
        jf = l // 2
        dy, dgp_f, dgt_f = _post_res_bwd(sv["y"], vec(g_ffn_post[l]), gt_f, g, "post_res_bwd")
        da = _mm(dy, W[l]["d"], "nt", F32, "mm_da")
        dW_d[l] = _mm(sv["a"], dy, "tn", F32, "mm_dwd")
        dgate, dup, dwc, dbc = _conv_gate_bwd(sv["gate"], sv["up"], sv["wc"], sv["bc"], da, "conv_gate_bwd")
        dh2 = [_mm(dgate, W[l]["g"], "nt", F32, "mm_dh2g"), _mm(dup, W[l]["u"], "nt", F32, "mm_dh2u")]
        dW_g[l] = _mm(sv["h2"], dgate, "tn", F32, "mm_dwg")
        dW_u[l] = _mm(sv["h2"], dup, "tn", F32, "mm_dwu")
        g, dg_f, dsc_f, dsh_f = _norm_mod_bwd(sv["x1"], vec(g_ffn_pre[l]), sc_f, sh_f, dh2, g, "norm_mod_bwd")
        dp, dgp_a, dgt_a = _post_res_bwd(sv["p"], vec(g_mix_post[l]), gt_a, g, "post_res_bwd")
        do = _mm(dp, W[l]["o"], "nt", F32, "mm_do")
        dW_o[l] = _mm(sv["o"], dp, "tn", F32, "mm_dwo")
        to_scatter += [(l, 1, rows(dW_o[l], OS)), (l, 2, cols(dW_g[l], FS)), (l, 3, cols(dW_u[l], FS)),
                       (l, 4, rows(dW_d[l], FS))]
        side, taken = carried()
        if is_fox:
            f3, bias, cq_rows, ck_b = sv["extra"]
            (dq, dk, dv, dcq_rows, dck_b), side_outs = _fox_bwd(sv["qkv"], do, sv["o"], sv["stat"], cq_rows, ck_b,
                                                                "fox_bwd", side)
            dcum_sn = dcq_rows[::SUBLANES].T + dck_b[:, ::B]
            dcum3 = dcum_sn.reshape(NQ, B, NH).transpose(0, 2, 1)
            df3, df_tot = _forget_cumsum_bwd(dcum3, f3, bias, "forget_cumsum_bwd")
            df = df3.transpose(0, 2, 1).reshape(S, NH)
            dfp = jnp.pad(df, ((0, 0), (0, LANES - NH)))
            dW_fg[jf] = _mm(sv["h1"], dfp, "tn", F32, "mm_dwfg")[:, :NH]
            db_fg[jf] = df_tot[:, 0]
            dh_extra = [_mm(dfp, W[l]["fg"], "nt", F32, "mm_dh1f")]
        else:
            (dq, dk, dv), side_outs = _sb_bwd(sv["qkv"], do, sv["stat"], sv["extra"], "sb_bwd", side)
            dh_extra = []
        taken(side_outs)
        dqkv = jnp.concatenate([dq, dk, dv], axis=1)
        dh1 = _mm(dqkv, W[l]["qkv"], "nt", F32, "mm_dh1")
        dW_qkv[l] = _mm(sv["h1"], dqkv, "tn", F32, "mm_dwqkv")
        to_scatter.append((l, 0, cols(dW_qkv[l], QS)))
        g, dg_a, dsc_a, dsh_a = _norm_mod_bwd(sv["x0"], vec(g_mix_pre[l]), sc_a, sh_a, [dh1] + dh_extra, g,
                                              "norm_mod_bwd")
        dmod[l] = jnp.concatenate([dsh_a, dsc_a, dgt_a, dsh_f, dsc_f, dgt_f], axis=1)[0]
        dg_mix_pre[l], dg_mix_post[l], dg_ffn_pre[l], dg_ffn_post[l] = dg_a[0], dgp_a[0], dg_f[0], dgp_f[0]
        dw_conv[l], db_conv[l] = dwc, dbc[0]
    grad_x = g[None]

    pieces = [jnp.stack(dmod), jnp.stack(dg_mix_pre), jnp.stack(dg_mix_post), jnp.stack(dg_ffn_pre),
              jnp.stack(dg_ffn_post), jnp.stack(db_fg), jnp.stack(dW_fg), jnp.stack(dw_conv), jnp.stack(db_conv)]
    sizes = [pc.size for pc in pieces]
    total = sum(sizes)
    pack_rows = -(-total // (LANES * SUBLANES)) * SUBLANES
    pack = jnp.pad(jnp.concatenate([pc.reshape(-1) for pc in pieces]), (0, pack_rows * LANES - total))
    pack_all = _all_gather_small(pack.reshape(pack_rows, LANES), "ag_small_grads").reshape(N_DEV, pack_rows, LANES)
    small = _sum_leading(pack_all, "sum_small_grads").reshape(-1)
    offs = [0]
    for sz in sizes:
        offs.append(offs[-1] + sz)
    parts = [small[offs[i]:offs[i + 1]].reshape(pieces[i].shape) for i in range(len(pieces))]
    g_b_mod, g_g_mix_pre, g_g_mix_post, g_g_ffn_pre, g_g_ffn_post, g_b_fg, g_w_fg_full, g_w_conv_full, g_b_conv = parts
    g_w_fg = lax.dynamic_slice_in_dim(g_w_fg_full, k_me * OS, OS, axis=1)
    g_w_conv = lax.dynamic_slice_in_dim(g_w_conv_full, k_me * FS, FS, axis=2)
    dmod_all = pack_all.reshape(N_DEV, -1)[:, :L * N_MOD * D].reshape(N_DEV, L, N_CHIPS, MS)
    dmod_cols = lax.dynamic_index_in_dim(dmod_all, k_me, axis=2, keepdims=False).reshape(N_DEV, L * MS)
    g_w_mod = _mm(_pad_rows(c_act, LANES), _pad_rows(dmod_cols, LANES), "tn", F32, "mm_dwmod", tm=D, tn=MS, tk=LANES)
    g_w_mod = g_w_mod.reshape(D, L, MS).transpose(1, 0, 2)

    side, taken = carried()
    taken(_run_side(side, "rs_tail_scatter"))
    side, taken = carried()
    taken(_run_side(side, "rs_tail_swap"))
    g_w_qkv, g_w_o, g_w_gate, g_w_up, g_w_down = [jnp.stack([grad_shards[l][i] for l in range(L)])
                                                  for i in range(n_big)]

    grads = [g_w_mod, g_b_mod, g_g_mix_pre, g_g_mix_post, g_w_qkv, g_w_o, g_w_fg, g_b_fg, g_g_ffn_pre,
             g_g_ffn_post, g_w_gate, g_w_up, g_w_conv, g_b_conv, g_w_down]
    weights = [w_mod, b_mod, g_mix_pre, g_mix_post, w_qkv, w_o, w_fg, b_fg, g_ffn_pre, g_ffn_post, w_ffn_gate,
               w_ffn_up, w_conv, b_conv, w_ffn_down]
    ms = [m_w_mod, m_b_mod, m_g_mix_pre, m_g_mix_post, m_w_qkv, m_w_o, m_w_fg, m_b_fg, m_g_ffn_pre, m_g_ffn_post,
          m_w_ffn_gate, m_w_ffn_up, m_w_conv, m_b_conv, m_w_ffn_down]
    vs = [v_w_mod, v_b_mod, v_g_mix_pre, v_g_mix_post, v_w_qkv, v_w_o, v_w_fg, v_b_fg, v_g_ffn_pre, v_g_ffn_post,
          v_w_ffn_gate, v_w_ffn_up, v_w_conv, v_b_conv, v_w_ffn_down]
    deltas, new_ms, new_vs = [], [], []
    for wv, gv, mv, vv in zip(weights, grads, ms, vs):
        d, nm, nv = _adamw(wv, gv, mv, vv, "adamw")
        deltas.append(d)
        new_ms.append(nm)
        new_vs.append(nv)
    return (loss, grad_x, *grads, *deltas, *new_ms, *new_vs)
```

```python
import functools

import jax
import jax.numpy as jnp
from jax import lax
from jax.experimental import pallas as pl
from jax.experimental.pallas import tpu as pltpu

F32 = jnp.float32
BF16 = jnp.bfloat16
MESH = pl.DeviceIdType.MESH

HEAD_DIM = 64
ATT_BLOCK = 128
SB_Q_TILE = 512
FOX_Q_TILE = 1024
SB_DEAD_LOG = -110.0
LANES = 128
SUBLANES = 8
RMS_EPS = 1e-6
N_MOD = 6
N_CHIPS = 4
N_DEV = 8
ADAM_LR = 0.001
ADAM_B1 = 0.9
ADAM_B2 = 0.999
ADAM_EPS = 1e-08
ADAM_WD = 0.01
ADAM_STEP = 10
VMEM_LIMIT_BYTES = 56 * 1024 * 1024
NEG_BIG = -1e30

ANY = pl.BlockSpec(memory_space=pl.ANY)
VMEM_WHOLE = pl.BlockSpec(memory_space=pltpu.VMEM)


def _params(**kw):
    return pltpu.CompilerParams(vmem_limit_bytes=VMEM_LIMIT_BYTES, **kw)


def _dot(a, b):
    return jnp.dot(a, b, preferred_element_type=F32)


def _dot_nt(a, b):
    return lax.dot_general(a, b, (((1,), (1,)), ((), ())), preferred_element_type=F32)


def _dot_tn(a, b):
    return lax.dot_general(a, b, (((0,), (0,)), ((), ())), preferred_element_type=F32)


def _split_dot(x, t, parts):
    acc = None
    rem = x
    for _ in range(parts):
        piece = rem.astype(BF16)
        rem = rem - piece.astype(F32)
        d = _dot(piece, t)
        acc = d if acc is None else acc + d
    return acc


def _pick(n, prefs):
    for p in prefs:
        if n % p == 0:
            return p
    return n


def _mm(a, b, dims, out_dtype, name, tm=None, tn=None, tk=None):
    if dims == "tn":
        K, M = a.shape
    else:
        M, K = a.shape
    N = b.shape[0] if dims == "nt" else b.shape[1]
    tm = tm or _pick(M, (1024, 512, 256, 128))
    tn = tn or _pick(N, (1536, 1408, 1024, 768, 512, 256, 128))
    tk = tk or _pick(K, (1024, 1408, 512, 256, 128))
    nk = K // tk
    grid = (N // tn, M // tm, nk)
    if dims == "tn":
        a_spec = pl.BlockSpec((tk, tm), lambda j, i, k: (k, i))
    else:
        a_spec = pl.BlockSpec((tm, tk), lambda j, i, k: (i, k))
    if dims == "nt":
        b_spec = pl.BlockSpec((tn, tk), lambda j, i, k: (j, k))
    else:
        b_spec = pl.BlockSpec((tk, tn), lambda j, i, k: (k, j))
    o_spec = pl.BlockSpec((tm, tn), lambda j, i, k: (i, j))

    def body(a_ref, b_ref, o_ref, *scratch):
        x = a_ref[...].astype(BF16)
        y = b_ref[...].astype(BF16)
        if dims == "nn":
            r = _dot(x, y)
        elif dims == "nt":
            r = _dot_nt(x, y)
        else:
            r = _dot_tn(x, y)
        if nk == 1:
            o_ref[...] = r.astype(out_dtype)
        else:
            acc = scratch[0]
            k = pl.program_id(2)

            @pl.when(k == 0)
            def _():
                acc[...] = r

            @pl.when(k > 0)
            def _():
                acc[...] += r

            @pl.when(k == nk - 1)
            def _():
                o_ref[...] = acc[...].astype(out_dtype)

    return pl.pallas_call(
        body,
        name=name,
        grid=grid,
        in_specs=[a_spec, b_spec],
        out_specs=o_spec,
        out_shape=jax.ShapeDtypeStruct((M, N), out_dtype),
        scratch_shapes=[pltpu.VMEM((tm, tn), F32)] if nk > 1 else [],
        compiler_params=_params(dimension_semantics=("parallel", "parallel", "arbitrary")),
    )(a, b)


def _row_tile(S):
    return _pick(S, (256, 128, 64, 32, 16, 8))


def _norm_mod(x, g, sc, sh, name):
    S, D = x.shape
    T = _row_tile(S)
    row = pl.BlockSpec((T, D), lambda i: (i, 0))
    vec = pl.BlockSpec((1, D), lambda i: (0, 0))

    def body(x_ref, g_ref, sc_ref, sh_ref, h_ref):
        xv = x_ref[...]
        r = lax.rsqrt(jnp.mean(xv * xv, axis=-1, keepdims=True) + RMS_EPS)
        n = (xv * r) * g_ref[...]
        h_ref[...] = (n * (1.0 + sc_ref[...]) + sh_ref[...]).astype(BF16)

    return pl.pallas_call(
        body, name=name, grid=(S // T,), in_specs=[row, vec, vec, vec], out_specs=row,
        out_shape=jax.ShapeDtypeStruct((S, D), BF16), compiler_params=_params(),
    )(x, g, sc, sh)


def _norm_mod_bwd(x, g, sc, sh, dhs, gres, name):
    S, D = x.shape
    T = _row_tile(S)
    n_dh = len(dhs)
    row = pl.BlockSpec((T, D), lambda i: (i, 0))
    vec = pl.BlockSpec((1, D), lambda i: (0, 0))

    def body(x_ref, g_ref, sc_ref, sh_ref, *refs):
        dh_refs = refs[:n_dh]
        gres_ref, dx_ref, dg_ref, dsc_ref, dsh_ref = refs[n_dh:]
        xv = x_ref[...]
        r = lax.rsqrt(jnp.mean(xv * xv, axis=-1, keepdims=True) + RMS_EPS)
        xn = xv * r
        n = xn * g_ref[...]
        dh = dh_refs[0][...]
        for extra in dh_refs[1:]:
            dh = dh + extra[...]
        dn = dh * (1.0 + sc_ref[...])
        dxn = dn * g_ref[...]
        dx = r * (dxn - xn * jnp.mean(dxn * xn, axis=-1, keepdims=True))
        dx_ref[...] = gres_ref[...] + dx

        @pl.when(pl.program_id(0) == 0)
        def _():
            dg_ref[...] = jnp.zeros_like(dg_ref)
            dsc_ref[...] = jnp.zeros_like(dsc_ref)
            dsh_ref[...] = jnp.zeros_like(dsh_ref)

        dg_ref[...] += jnp.sum(dn * xn, axis=0, keepdims=True)
        dsc_ref[...] += jnp.sum(dh * n, axis=0, keepdims=True)
        dsh_ref[...] += jnp.sum(dh, axis=0, keepdims=True)

    vshape = jax.ShapeDtypeStruct((1, D), F32)
    return pl.pallas_call(
        body, name=name, grid=(S // T,), in_specs=[row, vec, vec, vec] + [row] * (n_dh + 1),
        out_specs=[row, vec, vec, vec],
        out_shape=[jax.ShapeDtypeStruct((S, D), F32), vshape, vshape, vshape],
        compiler_params=_params(dimension_semantics=("arbitrary",)),
    )(x, g, sc, sh, *dhs, gres)


def _post_res(x, p, gp, gt, name):
    S, D = x.shape
    T = _row_tile(S)
    row = pl.BlockSpec((T, D), lambda i: (i, 0))
    vec = pl.BlockSpec((1, D), lambda i: (0, 0))

    def body(x_ref, p_ref, gp_ref, gt_ref, o_ref):
        pv = p_ref[...]
        r = lax.rsqrt(jnp.mean(pv * pv, axis=-1, keepdims=True) + RMS_EPS)
        o_ref[...] = x_ref[...] + gt_ref[...] * ((pv * r) * gp_ref[...])

    return pl.pallas_call(
        body, name=name, grid=(S // T,), in_specs=[row, row, vec, vec], out_specs=row,
        out_shape=jax.ShapeDtypeStruct((S, D), F32), compiler_params=_params(),
    )(x, p, gp, gt)


def _post_res_bwd(p, gp, gt, g, name):
    S, D = p.shape
    T = _row_tile(S)
    row = pl.BlockSpec((T, D), lambda i: (i, 0))
    vec = pl.BlockSpec((1, D), lambda i: (0, 0))

    def body(p_ref, gp_ref, gt_ref, g_ref, dp_ref, dgp_ref, dgt_ref):
        pv = p_ref[...]
        gv = g_ref[...]
        r = lax.rsqrt(jnp.mean(pv * pv, axis=-1, keepdims=True) + RMS_EPS)
        pn = pv * r
        n2 = pn * gp_ref[...]
        dn2 = gv * gt_ref[...]
        dpn = dn2 * gp_ref[...]
        dp = r * (dpn - pn * jnp.mean(dpn * pn, axis=-1, keepdims=True))
        dp_ref[...] = dp.astype(BF16)

        @pl.when(pl.program_id(0) == 0)
        def _():
            dgp_ref[...] = jnp.zeros_like(dgp_ref)
            dgt_ref[...] = jnp.zeros_like(dgt_ref)

        dgp_ref[...] += jnp.sum(dn2 * pn, axis=0, keepdims=True)
        dgt_ref[...] += jnp.sum(gv * n2, axis=0, keepdims=True)

    vshape = jax.ShapeDtypeStruct((1, D), F32)
    return pl.pallas_call(
        body, name=name, grid=(S // T,), in_specs=[row, vec, vec, row], out_specs=[row, vec, vec],
        out_shape=[jax.ShapeDtypeStruct((S, D), BF16), vshape, vshape],
        compiler_params=_params(dimension_semantics=("arbitrary",)),
    )(p, gp, gt, g)


def _loss_head(y, target, name):
    S, D = y.shape
    T = _row_tile(S)
    row = pl.BlockSpec((T, D), lambda i: (i, 0))
    vec = pl.BlockSpec((1, D), lambda i: (0, 0))

    def body(y_ref, t_ref, dy_ref, sq_ref):
        e = y_ref[...] - t_ref[...]
        dy_ref[...] = e * (1.0 / D)

        @pl.when(pl.program_id(0) == 0)
        def _():
            sq_ref[...] = jnp.zeros_like(sq_ref)

        sq_ref[...] += jnp.sum(e * e, axis=0, keepdims=True)

    return pl.pallas_call(
        body, name=name, grid=(S // T,), in_specs=[row, row], out_specs=[row, vec],
        out_shape=[jax.ShapeDtypeStruct((S, D), F32), jax.ShapeDtypeStruct((1, D), F32)],
        compiler_params=_params(dimension_semantics=("arbitrary",)),
    )(y, target)


def _shift_down(v, k, rows):
    return jnp.where(rows >= k, pltpu.roll(v, k, 0), 0.0)


def _shift_up(v, k, rows, S):
    return jnp.where(rows < S - k, pltpu.roll(v, S - k, 0), 0.0)


def _conv_gate(gate_all, up_all, wc, bc, name):
    S, F = gate_all.shape
    C = LANES
    seq = pl.BlockSpec((S, C), lambda j: (0, j))

    def body(g_ref, u_ref, w_ref, b_ref, a_ref):
        rows = lax.broadcasted_iota(jnp.int32, (S, C), 0)
        gate = g_ref[...]
        w = w_ref[...]
        gc = w[2:3] * gate + w[1:2] * _shift_down(gate, 1, rows) + w[0:1] * _shift_down(gate, 2, rows) + b_ref[...]
        a_ref[...] = (gc * (1.0 / (1.0 + jnp.exp(-gc))) * u_ref[...]).astype(BF16)

    return pl.pallas_call(
        body, name=name, grid=(F // C,),
        in_specs=[seq, seq, pl.BlockSpec((3, C), lambda j: (0, j)), pl.BlockSpec((1, C), lambda j: (0, j))],
        out_specs=seq, out_shape=jax.ShapeDtypeStruct((S, F), BF16), compiler_params=_params(),
    )(gate_all, up_all, wc, bc)


def _conv_gate_bwd(gate_all, up_all, wc, bc, da, name):
    S, F = gate_all.shape
    C = LANES
    seq = pl.BlockSpec((S, C), lambda j: (0, j))

    def body(g_ref, u_ref, w_ref, b_ref, da_ref, dg_ref, du_ref, dw_ref, db_ref):
        rows = lax.broadcasted_iota(jnp.int32, (S, C), 0)
        gate = g_ref[...]
        up = u_ref[...]
        dav = da_ref[...]
        w = w_ref[...]
        g1 = _shift_down(gate, 1, rows)
        g2 = _shift_down(gate, 2, rows)
        gc = w[2:3] * gate + w[1:2] * g1 + w[0:1] * g2 + b_ref[...]
        sg = 1.0 / (1.0 + jnp.exp(-gc))
        du_ref[...] = (dav * (gc * sg)).astype(BF16)
        dgc = dav * up * (sg * (1.0 + gc * (1.0 - sg)))
        db_ref[...] = jnp.sum(dgc, axis=0, keepdims=True)
        dw_ref[0:1, :] = jnp.sum(dgc * g2, axis=0, keepdims=True)
        dw_ref[1:2, :] = jnp.sum(dgc * g1, axis=0, keepdims=True)
        dw_ref[2:3, :] = jnp.sum(dgc * gate, axis=0, keepdims=True)
        dgate = w[2:3] * dgc + w[1:2] * _shift_up(dgc, 1, rows, S) + w[0:1] * _shift_up(dgc, 2, rows, S)
        dg_ref[...] = dgate.astype(BF16)

    return pl.pallas_call(
        body, name=name, grid=(F // C,),
        in_specs=[seq, seq, pl.BlockSpec((3, C), lambda j: (0, j)), pl.BlockSpec((1, C), lambda j: (0, j)), seq],
        out_specs=[seq, seq, pl.BlockSpec((3, C), lambda j: (0, j)), pl.BlockSpec((1, C), lambda j: (0, j))],
        out_shape=[jax.ShapeDtypeStruct((S, F), BF16), jax.ShapeDtypeStruct((S, F), BF16),
                   jax.ShapeDtypeStruct((3, F), F32), jax.ShapeDtypeStruct((1, F), F32)],
        compiler_params=_params(),
    )(gate_all, up_all, wc, bc, da)


ATT_SCALE = HEAD_DIM ** -0.5


def _att_specs(S, D, TQ):
    nb = D // LANES
    q_spec = pl.BlockSpec((TQ, LANES), lambda p, i: (i, p))
    k_spec = pl.BlockSpec((S, LANES), lambda p, i: (0, nb + p))
    v_spec = pl.BlockSpec((S, LANES), lambda p, i: (0, 2 * nb + p))
    stat_spec = pl.BlockSpec((TQ, 2 * ATT_BLOCK), lambda p, i: (i, p))
    seq_spec = pl.BlockSpec((S, LANES), lambda p, i: (0, p))
    ck_spec = pl.BlockSpec((2, S // ATT_BLOCK, SUBLANES, ATT_BLOCK), lambda p, i: (p, 0, 0, 0))
    return q_spec, k_spec, v_spec, stat_spec, seq_spec, ck_spec


def _att_call(body, name, grid, in_specs, out_specs, out_shape, scratch, args, side, semantics):
    n_out = len(out_shape)
    if side is not None:
        in_specs = in_specs + [ANY] * len(side.inputs)
        out_specs = out_specs + [ANY] * len(side.out_shapes)
        out_shape = out_shape + side.out_shapes
        scratch = scratch + side.sems
        args = args + side.inputs
    res = pl.pallas_call(
        body, name=name, grid=grid, in_specs=in_specs, out_specs=out_specs, out_shape=out_shape,
        scratch_shapes=scratch, compiler_params=_params(dimension_semantics=semantics),
    )(*args)
    return res[:n_out], res[n_out:]


def _grid_ends(NP, NQ):
    p, qi = pl.program_id(0), pl.program_id(1)
    return jnp.logical_and(p == 0, qi == 0), jnp.logical_and(p == NP - 1, qi == NQ - 1)


def _head_lanes(hh):
    return slice(hh * HEAD_DIM, (hh + 1) * HEAD_DIM)


def _below(old, lo, new, axis=0):
    if lo == 0:
        return new
    keep = old[:lo] if axis == 0 else old[:, :lo]
    return jnp.concatenate([keep, new], axis=axis)


def _scaled_q(q_ref, hh):
    return (q_ref[:, _head_lanes(hh)].astype(F32) * ATT_SCALE).astype(BF16)


def _tri(cmp):
    B = ATT_BLOCK
    row = lax.broadcasted_iota(jnp.int32, (B, B), 0)
    col = lax.broadcasted_iota(jnp.int32, (B, B), 1)
    half = jnp.concatenate([jnp.where(cmp(row, col), 1.0, 0.0).astype(BF16), jnp.ones((B, B), BF16)], axis=1)
    return jnp.concatenate([half, half], axis=0)


def _hi_lo_dot(x, t):
    hi = x.astype(BF16)
    lo = (x - hi.astype(F32)).astype(BF16)
    return _dot(jnp.concatenate([hi, lo], axis=1), t)


def _key_minus_query(j, qi, TQ):
    row = lax.broadcasted_iota(jnp.int32, (TQ, ATT_BLOCK), 0)
    col = lax.broadcasted_iota(jnp.int32, (TQ, ATT_BLOCK), 1)
    return col - row + (j * ATT_BLOCK - qi * TQ)


def _log_sigmoids(z):
    sp = jnp.log(1.0 + jnp.exp(-jnp.abs(z)))
    lb = jnp.minimum(z, 0.0) - sp
    return lb, lb - z


def _log_sigmoids_fast(z):
    zc = jnp.maximum(z, -80.0)
    lb = -jnp.log(1.0 + jnp.exp(-zc))
    return lb, lb - zc


def _sb_fwd(qkv, name, side=None):
    S, D3 = qkv.shape
    D = D3 // 3
    B, TQ = ATT_BLOCK, SB_Q_TILE
    R = TQ // B
    NP = D // LANES
    NQ = S // TQ
    q_spec, k_spec, v_spec, stat_spec, _, _ = _att_specs(S, D, TQ)

    def body(*refs):
        (q_ref, k_ref, v_ref), (o_ref, lt_ref, first_ref), _, parts = _side_parts(side, refs, 3, 3)
        begin, end = _side_hooks(side, parts, *_grid_ends(NP, NQ))
        begin()
        qi = pl.program_id(1)
        t_suffix = _tri(lambda r, c: r > c)
        qs = [_scaled_q(q_ref, hh) for hh in range(2)]

        def tile(j, carry, masked, lo=0):
            r0 = pl.multiple_of(j * B, B)
            if masked:
                strict = _key_minus_query(j, qi, TQ)[lo:] < 0
            out = []
            for hh in range(2):
                c, acc = carry[hh]
                k = k_ref[pl.ds(r0, B), _head_lanes(hh)]
                v = v_ref[pl.ds(r0, B), _head_lanes(hh)]
                lb, l1 = _log_sigmoids_fast(_dot_nt(qs[hh][lo:], k))
                if masked:
                    l1 = jnp.where(strict, l1, 0.0)
                sums = _hi_lo_dot(l1, t_suffix)
                a = jnp.exp(lb + c[lo:] + sums[:, :B])
                if masked:
                    a = jnp.where(strict, a, 0.0)
                out.append((_below(c, lo, c[lo:] + sums[:, B:]), _below(acc, lo, acc[lo:] + _dot(a.astype(BF16), v))))
            return tuple(out)

        carry = (jnp.zeros((TQ, B), F32), jnp.zeros((TQ, HEAD_DIM), F32))
        carry = (carry, carry)
        for jj in reversed(range(R)):
            carry = tile(qi * R + jj, carry, True, jj * B)

        def alive(cr):
            return jnp.max(jnp.maximum(cr[0][0], cr[1][0])) > SB_DEAD_LOG

        def walk(state):
            j, cr, _ = state
            cr = tile(j, cr, False)
            return j - 1, cr, alive(cr)

        j, carry, _ = lax.while_loop(lambda st: jnp.logical_and(st[0] >= 0, st[2]), walk,
                                     (qi * R - 1, carry, alive(carry)))
        first_ref[pl.program_id(0), qi] = (j + 1).astype(F32)
        for hh in range(2):
            c, acc = carry[hh]
            o_ref[:, _head_lanes(hh)] = acc
            lt_ref[:, hh * B:(hh + 1) * B] = c
        end()

    return _att_call(
        body, name, (NP, NQ), [q_spec, k_spec, v_spec],
        [q_spec, stat_spec, pl.BlockSpec(memory_space=pltpu.SMEM)],
        [jax.ShapeDtypeStruct((S, D), F32), jax.ShapeDtypeStruct((S, 2 * NP * B), F32),
         jax.ShapeDtypeStruct((NP, NQ), F32)], [], [qkv, qkv, qkv], side, ("arbitrary", "arbitrary"))


def _sb_bwd(qkv, do, lt, first, name, side=None):
    S, D3 = qkv.shape
    D = D3 // 3
    B, TQ = ATT_BLOCK, SB_Q_TILE
    R = TQ // B
    NP = D // LANES
    NQ = S // TQ
    q_spec, k_spec, v_spec, stat_spec, seq_spec, _ = _att_specs(S, D, TQ)

    def body(*refs):
        ins, (dq_ref, dk_ref, dv_ref), (dk_acc, dv_acc), parts = _side_parts(side, refs, 6, 3)
        first_ref, q_ref, k_ref, v_ref, do_ref, lt_ref = ins
        begin, end = _side_hooks(side, parts, *_grid_ends(NP, NQ))
        begin()
        qi = pl.program_id(1)
        t_prefix = _tri(lambda r, c: r <= c)
        t_before = _tri(lambda r, c: r < c)

        @pl.when(qi == 0)
        def _():
            dk_acc[...] = jnp.zeros_like(dk_acc)
            dv_acc[...] = jnp.zeros_like(dv_acc)

        qs = [_scaled_q(q_ref, hh) for hh in range(2)]
        dob = [do_ref[:, _head_lanes(hh)].astype(BF16) for hh in range(2)]
        ltot = [lt_ref[:, hh * B:(hh + 1) * B] for hh in range(2)]

        def tile(j, carry, masked, lo=0):
            r0 = pl.multiple_of(j * B, B)
            if masked:
                strict = _key_minus_query(j, qi, TQ)[lo:] < 0
            out = []
            for hh in range(2):
                pre, cu, dq = carry[hh]
                lanes = _head_lanes(hh)
                k = k_ref[pl.ds(r0, B), lanes]
                v = v_ref[pl.ds(r0, B), lanes]
                lb, l1 = _log_sigmoids_fast(_dot_nt(qs[hh][lo:], k))
                if masked:
                    l1 = jnp.where(strict, l1, 0.0)
                sums = _hi_lo_dot(l1, t_prefix)
                a = jnp.exp(lb + (ltot[hh][lo:] - pre[lo:] - sums[:, :B]))
                if masked:
                    a = jnp.where(strict, a, 0.0)
                u = a * _dot_nt(dob[hh][lo:], v)
                usums = _hi_lo_dot(u, t_before)
                dz = u - (u + cu[lo:] + usums[:, :B]) * jnp.exp(lb)
                if masked:
                    dz = jnp.where(strict, dz, 0.0)
                dzb = dz.astype(BF16)
                dk_acc[pl.ds(r0, B), lanes] += _dot_tn(dzb, qs[hh][lo:])
                dv_acc[pl.ds(r0, B), lanes] += _dot_tn(a.astype(BF16), dob[hh][lo:])
                out.append((_below(pre, lo, pre[lo:] + sums[:, B:]), _below(cu, lo, cu[lo:] + usums[:, B:]),
                            _below(dq, lo, dq[lo:] + _dot(dzb, k))))
            return tuple(out)

        zero = (jnp.zeros((TQ, B), F32), jnp.zeros((TQ, B), F32), jnp.zeros((TQ, HEAD_DIM), F32))
        first = jnp.clip(first_ref[pl.program_id(0), qi].astype(jnp.int32), 0, qi * R)
        carry = lax.fori_loop(first, qi * R, lambda j, cr: tile(j, cr, False), (zero, zero))
        for jj in range(R):
            carry = tile(qi * R + jj, carry, True, jj * B)
        for hh in range(2):
            dq_ref[:, _head_lanes(hh)] = (carry[hh][2] * ATT_SCALE).astype(BF16)

        @pl.when(qi == NQ - 1)
        def _():
            dk_ref[...] = dk_acc[...].astype(BF16)
            dv_ref[...] = dv_acc[...].astype(BF16)

        end()

    out = jax.ShapeDtypeStruct((S, D), BF16)
    return _att_call(
        body, name, (NP, NQ), [pl.BlockSpec(memory_space=pltpu.SMEM), q_spec, k_spec, v_spec, q_spec, stat_spec],
        [q_spec, seq_spec, seq_spec], [out, out, out],
        [pltpu.VMEM((S, LANES), F32), pltpu.VMEM((S, LANES), F32)], [first, qkv, qkv, qkv, do, lt], side,
        ("arbitrary", "arbitrary"))


def _fox_specs(S, D, TQ):
    row_spec = pl.BlockSpec((2 * SUBLANES, TQ), lambda p, i: (p, i))
    keyb_spec = pl.BlockSpec((S, 2 * ATT_BLOCK), lambda p, i: (0, p))
    return row_spec, keyb_spec


def _key_gt_query_t(j, qi, TQ):
    key = lax.broadcasted_iota(jnp.int32, (ATT_BLOCK, TQ), 0)
    qry = lax.broadcasted_iota(jnp.int32, (ATT_BLOCK, TQ), 1)
    return key - qry + (j * ATT_BLOCK - qi * TQ) > 0


def _fox_fwd(qkv, cq_rows, ck_b, name, side=None):
    S, D3 = qkv.shape
    D = D3 // 3
    B, TQ = ATT_BLOCK, FOX_Q_TILE
    R = TQ // B
    NP = D // LANES
    NQ = S // TQ
    q_spec, k_spec, v_spec, _, _, _ = _att_specs(S, D, TQ)
    row_spec, keyb_spec = _fox_specs(S, D, TQ)

    def body(*refs):
        (q_ref, k_ref, v_ref, cq_ref, ck_ref), (o_ref, lse_ref), _, parts = _side_parts(side, refs, 5, 2)
        begin, end = _side_hooks(side, parts, *_grid_ends(NP, NQ))
        begin()
        qi = pl.program_id(1)
        qs = [_scaled_q(q_ref, hh) for hh in range(2)]
        cq = [cq_ref[hh * SUBLANES:hh * SUBLANES + 1, :] for hh in range(2)]

        def tile(j, carry, masked, lo=0):
            r0 = pl.multiple_of(j * B, B)
            if masked:
                hidden = _key_gt_query_t(j, qi, TQ)[:, lo:]
            out = []
            for hh in range(2):
                m, lsum, acc = carry[hh]
                k = k_ref[pl.ds(r0, B), _head_lanes(hh)]
                v = v_ref[pl.ds(r0, B), _head_lanes(hh)]
                ck = jnp.tile(ck_ref[pl.ds(r0, B), hh * B:(hh + 1) * B], (1, (TQ - lo) // B))
                s = _dot_nt(k, qs[hh][lo:]) + (cq[hh][:, lo:] - ck)
                if masked:
                    s = jnp.where(hidden, NEG_BIG, s)
                m_new = jnp.maximum(m[:, lo:], jnp.max(s, axis=0, keepdims=True))
                p = jnp.exp(s - m_new)
                alpha = jnp.exp(m[:, lo:] - m_new)
                out.append((_below(m, lo, m_new, 1),
                            _below(lsum, lo, alpha * lsum[:, lo:] + jnp.sum(p, axis=0, keepdims=True), 1),
                            _below(acc, lo, alpha * acc[:, lo:] + _dot_tn(v, p.astype(BF16)), 1)))
            return tuple(out)

        zero = (jnp.full((1, TQ), NEG_BIG, F32), jnp.zeros((1, TQ), F32), jnp.zeros((HEAD_DIM, TQ), F32))
        carry = lax.fori_loop(0, qi * (R // 2), lambda t, cr: tile(2 * t + 1, tile(2 * t, cr, False), False),
                              (zero, zero))
        for jj in range(R):
            carry = tile(qi * R + jj, carry, True, jj * B)
        o_t = jnp.concatenate([carry[hh][2] * (1.0 / carry[hh][1]) for hh in range(2)], axis=0)
        o_ref[...] = o_t.T
        for hh in range(2):
            m, lsum, _ = carry[hh]
            lse_ref[hh * SUBLANES:(hh + 1) * SUBLANES, :] = jnp.broadcast_to(m + jnp.log(lsum), (SUBLANES, TQ))
        end()

    return _att_call(
        body, name, (NP, NQ), [q_spec, k_spec, v_spec, row_spec, keyb_spec], [q_spec, row_spec],
        [jax.ShapeDtypeStruct((S, D), F32), jax.ShapeDtypeStruct((2 * NP * SUBLANES, S), F32)], [],
        [qkv, qkv, qkv, cq_rows, ck_b], side, ("arbitrary", "arbitrary"))


def _fox_bwd(qkv, do, o, lse_rows, cq_rows, ck_b, name, side=None):
    S, D3 = qkv.shape
    D = D3 // 3
    B, TQ = ATT_BLOCK, FOX_Q_TILE
    R = TQ // B
    NP = D // LANES
    NQ = S // TQ
    q_spec, k_spec, v_spec, _, seq_spec, _ = _att_specs(S, D, TQ)
    row_spec, keyb_spec = _fox_specs(S, D, TQ)

    def body(*refs):
        ins, outs, (dk_acc, dv_acc), parts = _side_parts(side, refs, 8, 5)
        q_ref, k_ref, v_ref, do_ref, o_ref, lse_ref, cq_ref, ck_ref = ins
        dq_ref, dk_ref, dv_ref, dcq_ref, dck_ref = outs
        begin, end = _side_hooks(side, parts, *_grid_ends(NP, NQ))
        begin()
        qi = pl.program_id(1)

        @pl.when(qi == 0)
        def _():
            dk_acc[...] = jnp.zeros_like(dk_acc)
            dv_acc[...] = jnp.zeros_like(dv_acc)
            dck_ref[...] = jnp.zeros_like(dck_ref)

        qs = [_scaled_q(q_ref, hh) for hh in range(2)]
        dob = [do_ref[:, _head_lanes(hh)].astype(BF16) for hh in range(2)]
        prod_t = (do_ref[...] * o_ref[...]).T
        delta = [jnp.sum(prod_t[hh * HEAD_DIM:(hh + 1) * HEAD_DIM], axis=0, keepdims=True) for hh in range(2)]
        cq = [cq_ref[hh * SUBLANES:hh * SUBLANES + 1, :] for hh in range(2)]
        lse = [lse_ref[hh * SUBLANES:hh * SUBLANES + 1, :] for hh in range(2)]

        def tile(j, carry, masked, lo=0):
            r0 = pl.multiple_of(j * B, B)
            if masked:
                hidden = _key_gt_query_t(j, qi, TQ)[:, lo:]
            out = []
            for hh in range(2):
                dq, keysum = carry[hh]
                lanes = _head_lanes(hh)
                k = k_ref[pl.ds(r0, B), lanes]
                v = v_ref[pl.ds(r0, B), lanes]
                ck = jnp.tile(ck_ref[pl.ds(r0, B), hh * B:(hh + 1) * B], (1, (TQ - lo) // B))
                p = jnp.exp(_dot_nt(k, qs[hh][lo:]) + (cq[hh][:, lo:] - ck) - lse[hh][:, lo:])
                if masked:
                    p = jnp.where(hidden, 0.0, p)
                ds = p * (_dot_nt(v, dob[hh][lo:]) - delta[hh][:, lo:])
                dsb = ds.astype(BF16)
                dk_acc[pl.ds(r0, B), lanes] += _dot(dsb, qs[hh][lo:])
                dv_acc[pl.ds(r0, B), lanes] += _dot(p.astype(BF16), dob[hh][lo:])
                qsum = jnp.sum(ds, axis=1, keepdims=True)
                dck_ref[pl.ds(r0, B), hh * B:(hh + 1) * B] -= jnp.broadcast_to(qsum, (B, B))
                out.append((_below(dq, lo, dq[:, lo:] + _dot_tn(k, dsb), 1),
                            _below(keysum, lo, keysum[:, lo:] + jnp.sum(ds, axis=0, keepdims=True), 1)))
            return tuple(out)

        zero = (jnp.zeros((HEAD_DIM, TQ), F32), jnp.zeros((1, TQ), F32))
        carry = lax.fori_loop(0, qi * (R // 2), lambda t, cr: tile(2 * t + 1, tile(2 * t, cr, False), False),
                              (zero, zero))
        for jj in range(R):
            carry = tile(qi * R + jj, carry, True, jj * B)
        dq_t = jnp.concatenate([carry[hh][0] for hh in range(2)], axis=0)
        dq_ref[...] = (dq_t.T * ATT_SCALE).astype(BF16)
        for hh in range(2):
            dcq_ref[hh * SUBLANES:(hh + 1) * SUBLANES, :] = jnp.broadcast_to(carry[hh][1], (SUBLANES, TQ))

        @pl.when(qi == NQ - 1)
        def _():
            dk_ref[...] = dk_acc[...].astype(BF16)
            dv_ref[...] = dv_acc[...].astype(BF16)

        end()

    out = jax.ShapeDtypeStruct((S, D), BF16)
    return _att_call(
        body, name, (NP, NQ), [q_spec, k_spec, v_spec, q_spec, q_spec, row_spec, row_spec, keyb_spec],
        [q_spec, seq_spec, seq_spec, row_spec, keyb_spec],
        [out, out, out, jax.ShapeDtypeStruct((2 * NP * SUBLANES, S), F32), jax.ShapeDtypeStruct((S, 2 * NP * B), F32)],
        [pltpu.VMEM((S, LANES), F32), pltpu.VMEM((S, LANES), F32)],
        [qkv, qkv, qkv, do, o, lse_rows, cq_rows, ck_b], side, ("arbitrary", "arbitrary"))


def _forget_cumsum(f3, bias, name):
    NQ, NH, B = f3.shape

    def body(f_ref, b_ref, cum_ref):
        row = lax.broadcasted_iota(jnp.int32, (B, B), 0)
        col = lax.broadcasted_iota(jnp.int32, (B, B), 1)
        t_incl = jnp.where(row <= col, 1.0, 0.0).astype(BF16)

        def step(b, carry):
            lf, _ = _log_sigmoids(f_ref[b] + b_ref[...])
            cum = _split_dot(lf, t_incl, 3) + carry
            cum_ref[b] = cum
            return jnp.broadcast_to(cum[:, B - 1:B], (NH, B))

        lax.fori_loop(0, NQ, step, jnp.zeros((NH, B), F32))

    return pl.pallas_call(
        body, name=name, in_specs=[VMEM_WHOLE, VMEM_WHOLE], out_specs=VMEM_WHOLE,
        out_shape=jax.ShapeDtypeStruct((NQ, NH, B), F32), compiler_params=_params(),
    )(f3, bias)


def _forget_cumsum_bwd(dcum3, f3, bias, name):
    NQ, NH, B = f3.shape

    def body(d_ref, f_ref, b_ref, df_ref, tot_ref):
        row = lax.broadcasted_iota(jnp.int32, (B, B), 0)
        col = lax.broadcasted_iota(jnp.int32, (B, B), 1)
        t_rev = jnp.where(row >= col, 1.0, 0.0).astype(BF16)

        def step(it, carry):
            run, tot = carry
            b = NQ - 1 - it
            dlf = _split_dot(d_ref[b], t_rev, 3) + run
            f = f_ref[b] + b_ref[...]
            df = dlf * (1.0 / (1.0 + jnp.exp(f)))
            df_ref[b] = df
            return jnp.broadcast_to(dlf[:, 0:1], (NH, B)), tot + df

        _, tot = lax.fori_loop(0, NQ, step, (jnp.zeros((NH, B), F32), jnp.zeros((NH, B), F32)))
        tot_ref[...] = _split_dot(tot, jnp.ones((B, B), BF16), 3)

    return pl.pallas_call(
        body, name=name, in_specs=[VMEM_WHOLE, VMEM_WHOLE, VMEM_WHOLE], out_specs=[VMEM_WHOLE, VMEM_WHOLE],
        out_shape=[jax.ShapeDtypeStruct((NQ, NH, B), F32), jax.ShapeDtypeStruct((NH, B), F32)],
        compiler_params=_params(),
    )(dcum3, f3, bias)


def _silu(x, name):
    def body(x_ref, o_ref):
        v = x_ref[...]
        o_ref[...] = v * (1.0 / (1.0 + jnp.exp(-v)))

    return pl.pallas_call(body, name=name, in_specs=[VMEM_WHOLE], out_specs=VMEM_WHOLE,
                          out_shape=jax.ShapeDtypeStruct(x.shape, F32), compiler_params=_params())(x)


def _sum_leading(x, name):
    N, R, C = x.shape
    T = _pick(R, (256, 128, 64, 32, 16, 8))

    def body(x_ref, o_ref):
        acc = x_ref[0].astype(F32)
        for r in range(1, N):
            acc = acc + x_ref[r].astype(F32)
        o_ref[...] = acc

    return pl.pallas_call(
        body, name=name, grid=(R // T,), in_specs=[pl.BlockSpec((N, T, C), lambda i: (0, i, 0))],
        out_specs=pl.BlockSpec((T, C), lambda i: (i, 0)), out_shape=jax.ShapeDtypeStruct((R, C), F32),
        compiler_params=_params(),
    )(x)


def _adamw(w, g, m, v, name):
    shape = w.shape
    C = shape[-1]
    R = w.size // C
    T = R
    for cand in (512, 256, 128, 64, 32, 16, 8):
        if R % cand == 0 and cand * C * 4 <= (1 << 20):
            T = cand
            break
    spec = pl.BlockSpec((T, C), lambda i: (i, 0))
    c1 = 1.0 / (1.0 - ADAM_B1 ** ADAM_STEP)
    c2 = 1.0 / (1.0 - ADAM_B2 ** ADAM_STEP)

    def body(w_ref, g_ref, m_ref, v_ref, d_ref, nm_ref, nv_ref):
        gv = g_ref[...]
        nm = ADAM_B1 * m_ref[...] + (1.0 - ADAM_B1) * gv
        nv = ADAM_B2 * v_ref[...] + (1.0 - ADAM_B2) * (gv * gv)
        nm_ref[...] = nm
        nv_ref[...] = nv
        d_ref[...] = -ADAM_LR * ((nm * c1) / (jnp.sqrt(nv * c2) + ADAM_EPS) + ADAM_WD * w_ref[...])

    out = jax.ShapeDtypeStruct((R, C), F32)
    d, nm, nv = pl.pallas_call(
        body, name=name, grid=(R // T,), in_specs=[spec] * 4, out_specs=[spec] * 3, out_shape=[out] * 3,
        compiler_params=_params(),
    )(w.reshape(R, C), g.reshape(R, C), m.reshape(R, C), v.reshape(R, C))
    return d.reshape(shape), nm.reshape(shape), nv.reshape(shape)


def _mesh_pos():
    return lax.axis_index("x"), lax.axis_index("y"), lax.axis_index("c")


def _all_gather_small(x2d, name):
    m_per, n = x2d.shape

    def body(x_ref, out_ref, send_sems, recv_sems, local_sem):
        x, y, c = _mesh_pos()
        me, sibling = (x, y, c), (x, y, 1 - c)
        chips = [(1 - x, y), (x, 1 - y), (1 - x, 1 - y)]

        def rows(px, py, pc):
            return out_ref.at[pl.ds((4 * px + 2 * py + pc) * m_per, m_per), :]

        def copy(k, block, to, src=None):
            return pltpu.make_async_remote_copy(
                src_ref=rows(*block) if src is None else src, dst_ref=rows(*block),
                send_sem=send_sems.at[k], recv_sem=recv_sems.at[k], device_id=to, device_id_type=MESH)

        mine = pltpu.make_async_copy(x_ref, rows(*me), local_sem)
        mine.start()
        first = [copy(0, me, sibling, src=x_ref)]
        first += [copy(1 + j, me, (*chip, c), src=x_ref) for j, chip in enumerate(chips)]
        for cp in first:
            cp.start()
        passed = [copy(4 + j, (*chip, c), sibling) for j, chip in enumerate(chips)]
        for j, chip in enumerate(chips):
            copy(1 + j, (*chip, c), me).wait_recv()
            passed[j].start()
        copy(0, sibling, me).wait_recv()
        for j, chip in enumerate(chips):
            copy(4 + j, (*chip, 1 - c), me).wait_recv()
        for cp in first + passed:
            cp.wait_send()
        mine.wait()

    return pl.pallas_call(
        body, name=name, out_shape=jax.ShapeDtypeStruct((N_DEV * m_per, n), x2d.dtype),
        in_specs=[VMEM_WHOLE], out_specs=VMEM_WHOLE,
        scratch_shapes=[pltpu.SemaphoreType.DMA((7,)), pltpu.SemaphoreType.DMA((7,)), pltpu.SemaphoreType.DMA],
        compiler_params=_params(),
    )(x2d)


class _Side:
    def __init__(self, inputs, out_shapes, sems, start, wait):
        self.inputs, self.out_shapes, self.sems, self.start, self.wait = inputs, out_shapes, sems, start, wait


def _join_sides(sides):
    sides = [s for s in sides if s is not None]
    if not sides:
        return None
    bounds_in, bounds_out, bounds_sem = [0], [0], [0]
    for s in sides:
        bounds_in.append(bounds_in[-1] + len(s.inputs))
        bounds_out.append(bounds_out[-1] + len(s.out_shapes))
        bounds_sem.append(bounds_sem[-1] + len(s.sems))

    def each(method):
        def run(ins, outs, sems):
            for t, s in enumerate(sides):
                getattr(s, method)(ins[bounds_in[t]:bounds_in[t + 1]], outs[bounds_out[t]:bounds_out[t + 1]],
                                   sems[bounds_sem[t]:bounds_sem[t + 1]])
        return run

    return _Side([a for s in sides for a in s.inputs], [o for s in sides for o in s.out_shapes],
                 [m for s in sides for m in s.sems], each("start"), each("wait"))


def _side_parts(side, refs, n_in, n_out):
    if side is None:
        return refs[:n_in], refs[n_in:n_in + n_out], refs[n_in + n_out:], None
    si, so, ss = len(side.inputs), len(side.out_shapes), len(side.sems)
    a = n_in + si
    b = a + n_out + so
    ins, side_in = refs[:n_in], refs[n_in:a]
    outs, side_out = refs[a:a + n_out], refs[a + n_out:b]
    scratch, side_sems = refs[b:len(refs) - ss], refs[len(refs) - ss:]
    return ins, outs, scratch, (side_in, side_out, side_sems)


def _side_hooks(side, parts, first, last):
    if side is None:
        return lambda: None, lambda: None

    def begin():
        @pl.when(first)
        def _():
            side.start(*parts)

    def end():
        @pl.when(last)
        def _():
            side.wait(*parts)

    return begin, end


def _run_side(side, name):
    n_in = len(side.inputs)
    n_out = len(side.out_shapes)

    def body(*refs):
        parts = (refs[:n_in], refs[n_in:n_in + n_out], refs[n_in + n_out:])
        side.start(*parts)
        side.wait(*parts)

    return pl.pallas_call(
        body, name=name, out_shape=side.out_shapes, in_specs=[ANY] * n_in, out_specs=[ANY] * n_out,
        scratch_shapes=side.sems, compiler_params=_params(),
    )(*side.inputs)


def _gather_side(ws):
    n = len(ws)

    def copies(ins, outs, sems):
        send_sems, recv_sems, local_sems = sems
        x, y, c = _mesh_pos()
        k_me = 2 * x + y
        chips = [(1 - x, y), (x, 1 - y), (1 - x, 1 - y)]

        def remote(i, j, slot):
            px, py = chips[j]
            return pltpu.make_async_remote_copy(
                src_ref=ins[i], dst_ref=outs[i].at[slot], send_sem=send_sems.at[i, j],
                recv_sem=recv_sems.at[i, j], device_id=(px, py, c), device_id_type=MESH)

        local = [pltpu.make_async_copy(ins[i], outs[i].at[k_me], local_sems.at[i]) for i in range(n)]
        return remote, local, k_me, chips

    def start(ins, outs, sems):
        remote, local, k_me, _ = copies(ins, outs, sems)
        for i in range(n):
            local[i].start()
            for j in range(3):
                remote(i, j, k_me).start()

    def wait(ins, outs, sems):
        remote, local, k_me, chips = copies(ins, outs, sems)
        for i in range(n):
            for j, (px, py) in enumerate(chips):
                remote(i, j, 2 * px + py).wait_recv()
        for i in range(n):
            for j in range(3):
                remote(i, j, k_me).wait_send()
            local[i].wait()

    return _Side(list(ws), [jax.ShapeDtypeStruct((N_CHIPS,) + w.shape, w.dtype) for w in ws],
                 [pltpu.SemaphoreType.DMA((n, 3)), pltpu.SemaphoreType.DMA((n, 3)), pltpu.SemaphoreType.DMA((n,))],
                 start, wait)


def _scatter_side(gs):
    n = len(gs)
    halves = [g.shape[1] // 2 for g in gs]

    def copies(ins, outs, sems):
        send_sems, recv_sems, local_sems = sems
        x, y, c = _mesh_pos()

        def flip(v, bit):
            return 1 - v if bit else v

        def piece(i, px, py, pc):
            return ins[i].at[2 * px + py, pl.ds(pc * halves[i], halves[i])]

        def remote(i, r):
            px, py, pc = flip(x, r & 4), flip(y, r & 2), flip(c, r & 1)
            return pltpu.make_async_remote_copy(
                src_ref=piece(i, px, py, pc), dst_ref=outs[i].at[r], send_sem=send_sems.at[i, r - 1],
                recv_sem=recv_sems.at[i, r - 1], device_id=(px, py, pc), device_id_type=MESH)

        local = [pltpu.make_async_copy(piece(i, x, y, c), outs[i].at[0], local_sems.at[i]) for i in range(n)]
        return remote, local

    def start(ins, outs, sems):
        remote, local = copies(ins, outs, sems)
        for i in range(n):
            local[i].start()
            for r in range(1, N_DEV):
                remote(i, r).start()

    def wait(ins, outs, sems):
        remote, local = copies(ins, outs, sems)
        for i in range(n):
            for r in range(1, N_DEV):
                remote(i, r).wait_recv()
        for i in range(n):
            for r in range(1, N_DEV):
                remote(i, r).wait_send()
            local[i].wait()

    return _Side(list(gs), [jax.ShapeDtypeStruct((N_DEV, h) + g.shape[2:], g.dtype) for g, h in zip(gs, halves)],
                 [pltpu.SemaphoreType.DMA((n, N_DEV - 1)), pltpu.SemaphoreType.DMA((n, N_DEV - 1)),
                  pltpu.SemaphoreType.DMA((n,))], start, wait)


def _swap_side(hs):
    n = len(hs)

    def copies(ins, outs, sems):
        send_sems, recv_sems, local_sems = sems
        x, y, c = _mesh_pos()

        def remote(i, slot):
            return pltpu.make_async_remote_copy(
                src_ref=ins[i], dst_ref=outs[i].at[slot], send_sem=send_sems.at[i], recv_sem=recv_sems.at[i],
                device_id=(x, y, 1 - c), device_id_type=MESH)

        local = [pltpu.make_async_copy(ins[i], outs[i].at[c], local_sems.at[i]) for i in range(n)]
        return remote, local, c

    def start(ins, outs, sems):
        remote, local, c = copies(ins, outs, sems)
        for i in range(n):
            local[i].start()
            remote(i, c).start()

    def wait(ins, outs, sems):
        remote, local, c = copies(ins, outs, sems)
        for i in range(n):
            remote(i, 1 - c).wait_recv()
        for i in range(n):
            remote(i, c).wait_send()
            local[i].wait()

    return _Side(list(hs), [jax.ShapeDtypeStruct((2,) + h.shape, h.dtype) for h in hs],
                 [pltpu.SemaphoreType.DMA((n,)), pltpu.SemaphoreType.DMA((n,)), pltpu.SemaphoreType.DMA((n,))],
                 start, wait)


def _pad_rows(a, rows):
    return jnp.pad(a, ((0, rows - a.shape[0]), (0, 0)))


def kernel(x, c, w_mod, b_mod, g_mix_pre, g_mix_post, w_qkv, w_o, w_fg, b_fg, g_ffn_pre, g_ffn_post, w_ffn_gate, w_ffn_up, w_conv, b_conv, w_ffn_down, loss_target, m_w_mod, m_b_mod, m_g_mix_pre, m_g_mix_post, m_w_qkv, m_w_o, m_w_fg, m_b_fg, m_g_ffn_pre, m_g_ffn_post, m_w_ffn_gate, m_w_ffn_up, m_w_conv, m_b_conv, m_w_ffn_down, v_w_mod, v_b_mod, v_g_mix_pre, v_g_mix_post, v_w_qkv, v_w_o, v_w_fg, v_b_fg, v_g_ffn_pre, v_g_ffn_post, v_w_ffn_gate, v_w_ffn_up, v_w_conv, v_b_conv, v_w_ffn_down):
    xs = x[0]
    target = loss_target[0]
    S, D = xs.shape
    L = w_mod.shape[0]
    LF = w_fg.shape[0]
    MS = w_mod.shape[2]
    QS = w_qkv.shape[2]
    OS = w_o.shape[1]
    FS = w_ffn_gate.shape[2]
    F = N_CHIPS * FS
    NH = D // HEAD_DIM
    B = ATT_BLOCK
    NQ = S // B
    ax, ay, ac = _mesh_pos()
    k_me = 2 * ax + ay
    b_me = 4 * ax + 2 * ay + ac

    conv_rows = -(-(L * 3 * FS) // D)
    conv_rows = -(-conv_rows // SUBLANES) * SUBLANES
    conv_flat = jnp.pad(w_conv.reshape(-1), (0, conv_rows * D - L * 3 * FS)).reshape(conv_rows, D)
    first = jnp.concatenate([_pad_rows(c, SUBLANES), conv_flat], axis=0)
    first_all = _all_gather_small(first, "ag_cond").reshape(N_DEV, SUBLANES + conv_rows, D)
    c_all = first_all[:, 0, :]
    conv_all = first_all[0::2, SUBLANES:, :].reshape(N_CHIPS, -1)[:, :L * 3 * FS]
    w_conv_full = conv_all.reshape(N_CHIPS, L, 3, FS).transpose(1, 2, 0, 3).reshape(L, 3, F)
    c_act = _silu(c_all, "silu_c")

    mod_part = jnp.concatenate(
        [_mm(c_act, w_mod[l], "nn", F32, "mm_mod", tm=N_DEV, tn=MS, tk=D) for l in range(L)], axis=1)
    mod_all = _all_gather_small(mod_part, "ag_mod").reshape(N_CHIPS, 2, N_DEV, L, MS)[:, 0]
    mod_mine = lax.dynamic_index_in_dim(mod_all, b_me, axis=1, keepdims=False)
    mod = mod_mine.transpose(1, 0, 2).reshape(L, N_MOD * D) + b_mod

    shards_b = [w.astype(BF16) for w in (w_qkv, w_o, w_ffn_gate, w_ffn_up, w_ffn_down)]
    w_fg_b = w_fg.astype(BF16)

    def layer_shards(l):
        ws = [w[l] for w in shards_b]
        return ws + [w_fg_b[l // 2]] if l % 2 == 1 else ws

    def relayout(gathered):
        def side_by_side(gth):
            return gth.transpose(1, 0, 2).reshape(D, -1)

        gq, go, gg, gu_, gd = gathered[:5]
        wts = dict(qkv=side_by_side(gq), o=go.reshape(D, D), g=side_by_side(gg), u=side_by_side(gu_),
                   d=gd.reshape(F, D))
        if len(gathered) > 5:
            wts["fg"] = jnp.pad(gathered[5].reshape(D, NH), ((0, 0), (0, LANES - NH)))
        return wts

    W = [None] * L
    W[0] = relayout(_run_side(_gather_side(layer_shards(0)), "ag_weights_first"))

    def vec(a):
        return a.reshape(1, -1)

    saved = []
    xcur = xs
    for l in range(L):
        sh_a, sc_a, gt_a, sh_f, sc_f, gt_f = [vec(mod[l, j * D:(j + 1) * D]) for j in range(N_MOD)]
        is_fox = l % 2 == 1
        jf = l // 2
        h1 = _norm_mod(xcur, vec(g_mix_pre[l]), sc_a, sh_a, "norm_mod")
        qkv = _mm(h1, W[l]["qkv"], "nn", BF16, "mm_qkv")
        next_weights = _gather_side(layer_shards(l + 1)) if l + 1 < L else None
        if is_fox:
            flog = _mm(h1, W[l]["fg"], "nn", F32, "mm_fg")[:, :NH]
            f3 = flog.reshape(NQ, B, NH).transpose(0, 2, 1)
            bias = b_fg[jf].reshape(NH, 1)
            cum3 = _forget_cumsum(f3, bias, "forget_cumsum")
            cum_sn = cum3.transpose(0, 2, 1).reshape(S, NH)
            ck_b = jnp.repeat(cum_sn, B, axis=1)
            cq_rows = jnp.repeat(cum_sn.T, SUBLANES, axis=0)
            (o, stat), gathered = _fox_fwd(qkv, cq_rows, ck_b, "fox_fwd", next_weights)
            extra = (f3, bias, cq_rows, ck_b)
        else:
            (o, stat, extra), gathered = _sb_fwd(qkv, "sb_fwd", next_weights)
        if next_weights is not None:
            W[l + 1] = relayout(gathered)
        p = _mm(o, W[l]["o"], "nn", F32, "mm_o")
        x1 = _post_res(xcur, p, vec(g_mix_post[l]), gt_a, "post_res")
        h2 = _norm_mod(x1, vec(g_ffn_pre[l]), sc_f, sh_f, "norm_mod")
        gate = _mm(h2, W[l]["g"], "nn", F32, "mm_gate")
        up = _mm(h2, W[l]["u"], "nn", F32, "mm_up")
        wc = w_conv_full[l]
        bc = vec(b_conv[l])
        a = _conv_gate(gate, up, wc, bc, "conv_gate")
        yv = _mm(a, W[l]["d"], "nn", F32, "mm_down")
        x2 = _post_res(x1, yv, vec(g_ffn_post[l]), gt_f, "post_res")
        saved.append(dict(x0=xcur, h1=h1, qkv=qkv, o=o, stat=stat, extra=extra, p=p, x1=x1, h2=h2, gate=gate, up=up, a=a,
                          y=yv, mods=(sh_a, sc_a, gt_a, sh_f, sc_f, gt_f), wc=wc, bc=bc))
        xcur = x2

    g, sq = _loss_head(xcur, target, "loss_head")
    loss_part = 0.5 * jnp.sum(sq) / D
    loss = lax.psum(loss_part, ("x", "y", "c"))

    dW_qkv, dW_o, dW_g, dW_u, dW_d = [[None] * L for _ in range(5)]
    dW_fg, db_fg = [None] * LF, [None] * LF
    dmod, dg_mix_pre, dg_mix_post, dg_ffn_pre, dg_ffn_post = [[None] * L for _ in range(5)]
    dw_conv, db_conv = [None] * L, [None] * L

    def cols(dw, width):
        return dw.reshape(dw.shape[0], N_CHIPS, width).transpose(1, 0, 2).astype(BF16)

    def rows(dw, height):
        return dw.reshape(N_CHIPS, height, dw.shape[1]).astype(BF16)

    n_big = 5
    grad_shards = [[None] * n_big for _ in range(L)]
    to_scatter, to_swap = [], []

    def carried():
        sc, sw = list(to_scatter), list(to_swap)
        del to_scatter[:], to_swap[:]
        side = _join_sides([_scatter_side([a for _, _, a in sc]) if sc else None,
                            _swap_side([a for _, _, a in sw]) if sw else None])

        def taken(outs):
            for (lay, i, _), recv in zip(sc, outs[:len(sc)]):
                to_swap.append((lay, i, _sum_leading(recv, f"sum_grad_pieces_{i}")))
            for (lay, i, _), both in zip(sw, outs[len(sc):]):
                grad_shards[lay][i] = both.reshape((-1,) + both.shape[2:])

        return side, taken

    for l in reversed(range(L)):
        sv = saved[l]
        sh_a, sc_a, gt_a, sh_f, sc_f, gt_f = sv["mods"]
        is_fox = l % 2 == 1
        jf = l // 2
        dy, dgp_f, dgt_f = _post_res_bwd(sv["y"], vec(g_ffn_post[l]), gt_f, g, "post_res_bwd")
        da = _mm(dy, W[l]["d"], "nt", F32, "mm_da")
        dW_d[l] = _mm(sv["a"], dy, "tn", F32, "mm_dwd")
        dgate, dup, dwc, dbc = _conv_gate_bwd(sv["gate"], sv["up"], sv["wc"], sv["bc"], da, "conv_gate_bwd")
        dh2 = [_mm(dgate, W[l]["g"], "nt", F32, "mm_dh2g"), _mm(dup, W[l]["u"], "nt", F32, "mm_dh2u")]
        dW_g[l] = _mm(sv["h2"], dgate, "tn", F32, "mm_dwg")
        dW_u[l] = _mm(sv["h2"], dup, "tn", F32, "mm_dwu")
        g, dg_f, dsc_f, dsh_f = _norm_mod_bwd(sv["x1"], vec(g_ffn_pre[l]), sc_f, sh_f, dh2, g, "norm_mod_bwd")
        dp, dgp_a, dgt_a = _post_res_bwd(sv["p"], vec(g_mix_post[l]), gt_a, g, "post_res_bwd")
        do = _mm(dp, W[l]["o"], "nt", F32, "mm_do")
        dW_o[l] = _mm(sv["o"], dp, "tn", F32, "mm_dwo")
        to_scatter += [(l, 1, rows(dW_o[l], OS)), (l, 2, cols(dW_g[l], FS)), (l, 3, cols(dW_u[l], FS)),
                       (l, 4, rows(dW_d[l], FS))]
        side, taken = carried()
        if is_fox:
            f3, bias, cq_rows, ck_b = sv["extra"]
            (dq, dk, dv, dcq_rows, dck_b), side_outs = _fox_bwd(sv["qkv"], do, sv["o"], sv["stat"], cq_rows, ck_b,
                                                                "fox_bwd", side)
            dcum_sn = dcq_rows[::SUBLANES].T + dck_b[:, ::B]
            dcum3 = dcum_sn.reshape(NQ, B, NH).transpose(0, 2, 1)
            df3, df_tot = _forget_cumsum_bwd(dcum3, f3, bias, "forget_cumsum_bwd")
            df = df3.transpose(0, 2, 1).reshape(S, NH)
            dfp = jnp.pad(df, ((0, 0), (0, LANES - NH)))
            dW_fg[jf] = _mm(sv["h1"], dfp, "tn", F32, "mm_dwfg")[:, :NH]
            db_fg[jf] = df_tot[:, 0]
            dh_extra = [_mm(dfp, W[l]["fg"], "nt", F32, "mm_dh1f")]
        else:
            (dq, dk, dv), side_outs = _sb_bwd(sv["qkv"], do, sv["stat"], sv["extra"], "sb_bwd", side)
            dh_extra = []
        taken(side_outs)
        dqkv = jnp.concatenate([dq, dk, dv], axis=1)
        dh1 = _mm(dqkv, W[l]["qkv"], "nt", F32, "mm_dh1")
        dW_qkv[l] = _mm(sv["h1"], dqkv, "tn", F32, "mm_dwqkv")
        to_scatter.append((l, 0, cols(dW_qkv[l], QS)))
        g, dg_a, dsc_a, dsh_a = _norm_mod_bwd(sv["x0"], vec(g_mix_pre[l]), sc_a, sh_a, [dh1] + dh_extra, g,
                                              "norm_mod_bwd")
        dmod[l] = jnp.concatenate([dsh_a, dsc_a, dgt_a, dsh_f, dsc_f, dgt_f], axis=1)[0]
        dg_mix_pre[l], dg_mix_post[l], dg_ffn_pre[l], dg_ffn_post[l] = dg_a[0], dgp_a[0], dg_f[0], dgp_f[0]
        dw_conv[l], db_conv[l] = dwc, dbc[0]
    grad_x = g[None]

    pieces = [jnp.stack(dmod), jnp.stack(dg_mix_pre), jnp.stack(dg_mix_post), jnp.stack(dg_ffn_pre),
              jnp.stack(dg_ffn_post), jnp.stack(db_fg), jnp.stack(dW_fg), jnp.stack(dw_conv), jnp.stack(db_conv)]
    sizes = [pc.size for pc in pieces]
    total = sum(sizes)
    pack_rows = -(-total // (LANES * SUBLANES)) * SUBLANES
    pack = jnp.pad(jnp.concatenate([pc.reshape(-1) for pc in pieces]), (0, pack_rows * LANES - total))
    pack_all = _all_gather_small(pack.reshape(pack_rows, LANES), "ag_small_grads").reshape(N_DEV, pack_rows, LANES)
    small = _sum_leading(pack_all, "sum_small_grads").reshape(-1)
    offs = [0]
    for sz in sizes:
        offs.append(offs[-1] + sz)
    parts = [small[offs[i]:offs[i + 1]].reshape(pieces[i].shape) for i in range(len(pieces))]
    g_b_mod, g_g_mix_pre, g_g_mix_post, g_g_ffn_pre, g_g_ffn_post, g_b_fg, g_w_fg_full, g_w_conv_full, g_b_conv = parts
    g_w_fg = lax.dynamic_slice_in_dim(g_w_fg_full, k_me * OS, OS, axis=1)
    g_w_conv = lax.dynamic_slice_in_dim(g_w_conv_full, k_me * FS, FS, axis=2)
    dmod_all = pack_all.reshape(N_DEV, -1)[:, :L * N_MOD * D].reshape(N_DEV, L, N_CHIPS, MS)
    dmod_cols = lax.dynamic_index_in_dim(dmod_all, k_me, axis=2, keepdims=False).reshape(N_DEV, L * MS)
    g_w_mod = _mm(_pad_rows(c_act, LANES), _pad_rows(dmod_cols, LANES), "tn", F32, "mm_dwmod", tm=D, tn=MS, tk=LANES)
    g_w_mod = g_w_mod.reshape(D, L, MS).transpose(1, 0, 2)

    side, taken = carried()
    taken(_run_side(side, "rs_tail_scatter"))
    side, taken = carried()
    taken(_run_side(side, "rs_tail_swap"))
    g_w_qkv, g_w_o, g_w_gate, g_w_up, g_w_down = [jnp.stack([grad_shards[l][i] for l in range(L)])
                                                  for i in range(n_big)]

    grads = [g_w_mod, g_b_mod, g_g_mix_pre, g_g_mix_post, g_w_qkv, g_w_o, g_w_fg, g_b_fg, g_g_ffn_pre,
             g_g_ffn_post, g_w_gate, g_w_up, g_w_conv, g_b_conv, g_w_down]
    weights = [w_mod, b_mod, g_mix_pre, g_mix_post, w_qkv, w_o, w_fg, b_fg, g_ffn_pre, g_ffn_post, w_ffn_gate,
               w_ffn_up, w_conv, b_conv, w_ffn_down]
    ms = [m_w_mod, m_b_mod, m_g_mix_pre, m_g_mix_post, m_w_qkv, m_w_o, m_w_fg, m_b_fg, m_g_ffn_pre, m_g_ffn_post,
          m_w_ffn_gate, m_w_ffn_up, m_w_conv, m_b_conv, m_w_ffn_down]
    vs = [v_w_mod, v_b_mod, v_g_mix_pre, v_g_mix_post, v_w_qkv, v_w_o, v_w_fg, v_b_fg, v_g_ffn_pre, v_g_ffn_post,
          v_w_ffn_gate, v_w_ffn_up, v_w_conv, v_b_conv, v_w_ffn_down]
    deltas, new_ms, new_vs = [], [], []
    for wv, gv, mv, vv in zip(weights, grads, ms, vs):
        d, nm, nv = _adamw(wv, gv, mv, vv, "adamw")
        deltas.append(d)
        new_ms.append(nm)
        new_vs.append(nv)
    return (loss, grad_x, *grads, *deltas, *new_ms, *new_vs)
```

```python
import functools

import jax
import jax.numpy as jnp
from jax import lax
from jax.experimental import pallas as pl
from jax.experimental.pallas import tpu as pltpu

F32 = jnp.float32
BF16 = jnp.bfloat16
MESH = pl.DeviceIdType.MESH

HEAD_DIM = 64
ATT_BLOCK = 128
SB_Q_TILE = 512
FOX_Q_TILE = 1024
SB_DEAD_LOG = -110.0
LANES = 128
SUBLANES = 8
RMS_EPS = 1e-6
N_MOD = 6
N_CHIPS = 4
N_DEV = 8
ADAM_LR = 0.001
ADAM_B1 = 0.9
ADAM_B2 = 0.999
ADAM_EPS = 1e-08
ADAM_WD = 0.01
ADAM_STEP = 10
VMEM_LIMIT_BYTES = 56 * 1024 * 1024
NEG_BIG = -1e30

ANY = pl.BlockSpec(memory_space=pl.ANY)
VMEM_WHOLE = pl.BlockSpec(memory_space=pltpu.VMEM)


def _params(**kw):
    return pltpu.CompilerParams(vmem_limit_bytes=VMEM_LIMIT_BYTES, **kw)


def _dot(a, b):
    return jnp.dot(a, b, preferred_element_type=F32)


def _dot_nt(a, b):
    return lax.dot_general(a, b, (((1,), (1,)), ((), ())), preferred_element_type=F32)


def _dot_tn(a, b):
    return lax.dot_general(a, b, (((0,), (0,)), ((), ())), preferred_element_type=F32)


def _split_dot(x, t, parts):
    acc = None
    rem = x
    for _ in range(parts):
        piece = rem.astype(BF16)
        rem = rem - piece.astype(F32)
        d = _dot(piece, t)
        acc = d if acc is None else acc + d
    return acc


def _pick(n, prefs):
    for p in prefs:
        if n % p == 0:
            return p
    return n


def _mm(a, b, dims, out_dtype, name, tm=None, tn=None, tk=None):
    if dims == "tn":
        K, M = a.shape
    else:
        M, K = a.shape
    N = b.shape[0] if dims == "nt" else b.shape[1]
    tm = tm or _pick(M, (1024, 1408, 512, 256, 128))
    tn = tn or _pick(N, (1536, 1408, 1024, 768, 512, 256, 128))
    tk = tk or _pick(K, (1024, 1408, 512, 256, 128))
    nk = K // tk
    grid = (N // tn, M // tm, nk)
    if dims == "tn":
        a_spec = pl.BlockSpec((tk, tm), lambda j, i, k: (k, i))
    else:
        a_spec = pl.BlockSpec((tm, tk), lambda j, i, k: (i, k))
    if dims == "nt":
        b_spec = pl.BlockSpec((tn, tk), lambda j, i, k: (j, k))
    else:
        b_spec = pl.BlockSpec((tk, tn), lambda j, i, k: (k, j))
    o_spec = pl.BlockSpec((tm, tn), lambda j, i, k: (i, j))

    def body(a_ref, b_ref, o_ref, *scratch):
        x = a_ref[...].astype(BF16)
        y = b_ref[...].astype(BF16)
        if dims == "nn":
            r = _dot(x, y)
        elif dims == "nt":
            r = _dot_nt(x, y)
        else:
            r = _dot_tn(x, y)
        if nk == 1:
            o_ref[...] = r.astype(out_dtype)
        else:
            acc = scratch[0]
            k = pl.program_id(2)

            @pl.when(k == 0)
            def _():
                acc[...] = r

            @pl.when(k > 0)
            def _():
                acc[...] += r

            @pl.when(k == nk - 1)
            def _():
                o_ref[...] = acc[...].astype(out_dtype)

    return pl.pallas_call(
        body,
        name=name,
        grid=grid,
        in_specs=[a_spec, b_spec],
        out_specs=o_spec,
        out_shape=jax.ShapeDtypeStruct((M, N), out_dtype),
        scratch_shapes=[pltpu.VMEM((tm, tn), F32)] if nk > 1 else [],
        compiler_params=_params(dimension_semantics=("parallel", "parallel", "arbitrary")),
    )(a, b)


def _row_tile(S):
    return _pick(S, (512, 256, 128, 64, 32, 16, 8))


def _norm_mod(x, g, sc, sh, name):
    S, D = x.shape
    T = _row_tile(S)
    row = pl.BlockSpec((T, D), lambda i: (i, 0))
    vec = pl.BlockSpec((1, D), lambda i: (0, 0))

    def body(x_ref, g_ref, sc_ref, sh_ref, h_ref):
        xv = x_ref[...]
        r = lax.rsqrt(jnp.mean(xv * xv, axis=-1, keepdims=True) + RMS_EPS)
        n = (xv * r) * g_ref[...]
        h_ref[...] = (n * (1.0 + sc_ref[...]) + sh_ref[...]).astype(BF16)

    return pl.pallas_call(
        body, name=name, grid=(S // T,), in_specs=[row, vec, vec, vec], out_specs=row,
        out_shape=jax.ShapeDtypeStruct((S, D), BF16), compiler_params=_params(),
    )(x, g, sc, sh)


def _norm_mod_bwd(x, g, sc, sh, dhs, gres, name):
    S, D = x.shape
    T = _row_tile(S)
    n_dh = len(dhs)
    row = pl.BlockSpec((T, D), lambda i: (i, 0))
    vec = pl.BlockSpec((1, D), lambda i: (0, 0))

    def body(x_ref, g_ref, sc_ref, sh_ref, *refs):
        dh_refs = refs[:n_dh]
        gres_ref, dx_ref, dg_ref, dsc_ref, dsh_ref = refs[n_dh:]
        xv = x_ref[...]
        r = lax.rsqrt(jnp.mean(xv * xv, axis=-1, keepdims=True) + RMS_EPS)
        xn = xv * r
        n = xn * g_ref[...]
        dh = dh_refs[0][...]
        for extra in dh_refs[1:]:
            dh = dh + extra[...]
        dn = dh * (1.0 + sc_ref[...])
        dxn = dn * g_ref[...]
        dx = r * (dxn - xn * jnp.mean(dxn * xn, axis=-1, keepdims=True))
        dx_ref[...] = gres_ref[...] + dx

        @pl.when(pl.program_id(0) == 0)
        def _():
            dg_ref[...] = jnp.zeros_like(dg_ref)
            dsc_ref[...] = jnp.zeros_like(dsc_ref)
            dsh_ref[...] = jnp.zeros_like(dsh_ref)

        dg_ref[...] += jnp.sum(dn * xn, axis=0, keepdims=True)
        dsc_ref[...] += jnp.sum(dh * n, axis=0, keepdims=True)
        dsh_ref[...] += jnp.sum(dh, axis=0, keepdims=True)

    vshape = jax.ShapeDtypeStruct((1, D), F32)
    return pl.pallas_call(
        body, name=name, grid=(S // T,), in_specs=[row, vec, vec, vec] + [row] * (n_dh + 1),
        out_specs=[row, vec, vec, vec],
        out_shape=[jax.ShapeDtypeStruct((S, D), F32), vshape, vshape, vshape],
        compiler_params=_params(dimension_semantics=("arbitrary",)),
    )(x, g, sc, sh, *dhs, gres)


def _post_res(x, p, gp, gt, name):
    S, D = x.shape
    T = _row_tile(S)
    row = pl.BlockSpec((T, D), lambda i: (i, 0))
    vec = pl.BlockSpec((1, D), lambda i: (0, 0))

    def body(x_ref, p_ref, gp_ref, gt_ref, o_ref):
        pv = p_ref[...]
        r = lax.rsqrt(jnp.mean(pv * pv, axis=-1, keepdims=True) + RMS_EPS)
        o_ref[...] = x_ref[...] + gt_ref[...] * ((pv * r) * gp_ref[...])

    return pl.pallas_call(
        body, name=name, grid=(S // T,), in_specs=[row, row, vec, vec], out_specs=row,
        out_shape=jax.ShapeDtypeStruct((S, D), F32), compiler_params=_params(),
    )(x, p, gp, gt)


def _post_res_bwd(p, gp, gt, g, name):
    S, D = p.shape
    T = _row_tile(S)
    row = pl.BlockSpec((T, D), lambda i: (i, 0))
    vec = pl.BlockSpec((1, D), lambda i: (0, 0))

    def body(p_ref, gp_ref, gt_ref, g_ref, dp_ref, dgp_ref, dgt_ref):
        pv = p_ref[...]
        gv = g_ref[...]
        r = lax.rsqrt(jnp.mean(pv * pv, axis=-1, keepdims=True) + RMS_EPS)
        pn = pv * r
        n2 = pn * gp_ref[...]
        dn2 = gv * gt_ref[...]
        dpn = dn2 * gp_ref[...]
        dp = r * (dpn - pn * jnp.mean(dpn * pn, axis=-1, keepdims=True))
        dp_ref[...] = dp.astype(BF16)

        @pl.when(pl.program_id(0) == 0)
        def _():
            dgp_ref[...] = jnp.zeros_like(dgp_ref)
            dgt_ref[...] = jnp.zeros_like(dgt_ref)

        dgp_ref[...] += jnp.sum(dn2 * pn, axis=0, keepdims=True)
        dgt_ref[...] += jnp.sum(gv * n2, axis=0, keepdims=True)

    vshape = jax.ShapeDtypeStruct((1, D), F32)
    return pl.pallas_call(
        body, name=name, grid=(S // T,), in_specs=[row, vec, vec, row], out_specs=[row, vec, vec],
        out_shape=[jax.ShapeDtypeStruct((S, D), BF16), vshape, vshape],
        compiler_params=_params(dimension_semantics=("arbitrary",)),
    )(p, gp, gt, g)


def _loss_head(y, target, name):
    S, D = y.shape
    T = _row_tile(S)
    row = pl.BlockSpec((T, D), lambda i: (i, 0))
    vec = pl.BlockSpec((1, D), lambda i: (0, 0))

    def body(y_ref, t_ref, dy_ref, sq_ref):
        e = y_ref[...] - t_ref[...]
        dy_ref[...] = e * (1.0 / D)

        @pl.when(pl.program_id(0) == 0)
        def _():
            sq_ref[...] = jnp.zeros_like(sq_ref)

        sq_ref[...] += jnp.sum(e * e, axis=0, keepdims=True)

    return pl.pallas_call(
        body, name=name, grid=(S // T,), in_specs=[row, row], out_specs=[row, vec],
        out_shape=[jax.ShapeDtypeStruct((S, D), F32), jax.ShapeDtypeStruct((1, D), F32)],
        compiler_params=_params(dimension_semantics=("arbitrary",)),
    )(y, target)


def _shift_down(v, k, rows):
    return jnp.where(rows >= k, pltpu.roll(v, k, 0), 0.0)


def _shift_up(v, k, rows, S):
    return jnp.where(rows < S - k, pltpu.roll(v, S - k, 0), 0.0)


def _conv_gate(gate_all, up_all, wc, bc, name):
    S, F = gate_all.shape
    C = LANES
    seq = pl.BlockSpec((S, C), lambda j: (0, j))

    def body(g_ref, u_ref, w_ref, b_ref, a_ref):
        rows = lax.broadcasted_iota(jnp.int32, (S, C), 0)
        gate = g_ref[...]
        w = w_ref[...]
        gc = w[2:3] * gate + w[1:2] * _shift_down(gate, 1, rows) + w[0:1] * _shift_down(gate, 2, rows) + b_ref[...]
        a_ref[...] = (gc * (1.0 / (1.0 + jnp.exp(-gc))) * u_ref[...]).astype(BF16)

    return pl.pallas_call(
        body, name=name, grid=(F // C,),
        in_specs=[seq, seq, pl.BlockSpec((3, C), lambda j: (0, j)), pl.BlockSpec((1, C), lambda j: (0, j))],
        out_specs=seq, out_shape=jax.ShapeDtypeStruct((S, F), BF16), compiler_params=_params(),
    )(gate_all, up_all, wc, bc)


def _conv_gate_bwd(gate_all, up_all, wc, bc, da, name):
    S, F = gate_all.shape
    C = LANES
    seq = pl.BlockSpec((S, C), lambda j: (0, j))

    def body(g_ref, u_ref, w_ref, b_ref, da_ref, dg_ref, du_ref, dw_ref, db_ref):
        rows = lax.broadcasted_iota(jnp.int32, (S, C), 0)
        gate = g_ref[...]
        up = u_ref[...]
        dav = da_ref[...]
        w = w_ref[...]
        g1 = _shift_down(gate, 1, rows)
        g2 = _shift_down(gate, 2, rows)
        gc = w[2:3] * gate + w[1:2] * g1 + w[0:1] * g2 + b_ref[...]
        sg = 1.0 / (1.0 + jnp.exp(-gc))
        du_ref[...] = (dav * (gc * sg)).astype(BF16)
        dgc = dav * up * (sg * (1.0 + gc * (1.0 - sg)))
        db_ref[...] = jnp.sum(dgc, axis=0, keepdims=True)
        dw_ref[0:1, :] = jnp.sum(dgc * g2, axis=0, keepdims=True)
        dw_ref[1:2, :] = jnp.sum(dgc * g1, axis=0, keepdims=True)
        dw_ref[2:3, :] = jnp.sum(dgc * gate, axis=0, keepdims=True)
        dgate = w[2:3] * dgc + w[1:2] * _shift_up(dgc, 1, rows, S) + w[0:1] * _shift_up(dgc, 2, rows, S)
        dg_ref[...] = dgate.astype(BF16)

    return pl.pallas_call(
        body, name=name, grid=(F // C,),
        in_specs=[seq, seq, pl.BlockSpec((3, C), lambda j: (0, j)), pl.BlockSpec((1, C), lambda j: (0, j)), seq],
        out_specs=[seq, seq, pl.BlockSpec((3, C), lambda j: (0, j)), pl.BlockSpec((1, C), lambda j: (0, j))],
        out_shape=[jax.ShapeDtypeStruct((S, F), BF16), jax.ShapeDtypeStruct((S, F), BF16),
                   jax.ShapeDtypeStruct((3, F), F32), jax.ShapeDtypeStruct((1, F), F32)],
        compiler_params=_params(),
    )(gate_all, up_all, wc, bc, da)


ATT_SCALE = HEAD_DIM ** -0.5


def _att_specs(S, D, TQ):
    nb = D // LANES
    q_spec = pl.BlockSpec((TQ, LANES), lambda p, i: (i, p))
    k_spec = pl.BlockSpec((S, LANES), lambda p, i: (0, nb + p))
    v_spec = pl.BlockSpec((S, LANES), lambda p, i: (0, 2 * nb + p))
    stat_spec = pl.BlockSpec((TQ, 2 * ATT_BLOCK), lambda p, i: (i, p))
    seq_spec = pl.BlockSpec((S, LANES), lambda p, i: (0, p))
    ck_spec = pl.BlockSpec((2, S // ATT_BLOCK, SUBLANES, ATT_BLOCK), lambda p, i: (p, 0, 0, 0))
    return q_spec, k_spec, v_spec, stat_spec, seq_spec, ck_spec


def _att_call(body, name, grid, in_specs, out_specs, out_shape, scratch, args, side, semantics):
    n_out = len(out_shape)
    if side is not None:
        in_specs = in_specs + [ANY] * len(side.inputs)
        out_specs = out_specs + [ANY] * len(side.out_shapes)
        out_shape = out_shape + side.out_shapes
        scratch = scratch + side.sems
        args = args + side.inputs
    res = pl.pallas_call(
        body, name=name, grid=grid, in_specs=in_specs, out_specs=out_specs, out_shape=out_shape,
        scratch_shapes=scratch, compiler_params=_params(dimension_semantics=semantics),
    )(*args)
    return res[:n_out], res[n_out:]


def _grid_ends(NP, NQ):
    p, qi = pl.program_id(0), pl.program_id(1)
    return jnp.logical_and(p == 0, qi == 0), jnp.logical_and(p == NP - 1, qi == NQ - 1)


def _head_lanes(hh):
    return slice(hh * HEAD_DIM, (hh + 1) * HEAD_DIM)


def _below(old, lo, new, axis=0):
    if lo == 0:
        return new
    keep = old[:lo] if axis == 0 else old[:, :lo]
    return jnp.concatenate([keep, new], axis=axis)


def _scaled_q(q_ref, hh):
    return (q_ref[:, _head_lanes(hh)].astype(F32) * ATT_SCALE).astype(BF16)


def _tri(cmp):
    B = ATT_BLOCK
    row = lax.broadcasted_iota(jnp.int32, (B, B), 0)
    col = lax.broadcasted_iota(jnp.int32, (B, B), 1)
    half = jnp.concatenate([jnp.where(cmp(row, col), 1.0, 0.0).astype(BF16), jnp.ones((B, B), BF16)], axis=1)
    return jnp.concatenate([half, half], axis=0)


def _hi_lo_dot(x, t):
    hi = x.astype(BF16)
    lo = (x - hi.astype(F32)).astype(BF16)
    return _dot(jnp.concatenate([hi, lo], axis=1), t)


def _key_minus_query(j, qi, TQ):
    row = lax.broadcasted_iota(jnp.int32, (TQ, ATT_BLOCK), 0)
    col = lax.broadcasted_iota(jnp.int32, (TQ, ATT_BLOCK), 1)
    return col - row + (j * ATT_BLOCK - qi * TQ)


def _log_sigmoids(z):
    sp = jnp.log(1.0 + jnp.exp(-jnp.abs(z)))
    lb = jnp.minimum(z, 0.0) - sp
    return lb, lb - z


def _log_sigmoids_fast(z):
    zc = jnp.maximum(z, -80.0)
    lb = -jnp.log(1.0 + jnp.exp(-zc))
    return lb, lb - zc


def _sb_fwd(qkv, name, side=None):
    S, D3 = qkv.shape
    D = D3 // 3
    B, TQ = ATT_BLOCK, SB_Q_TILE
    R = TQ // B
    NP = D // LANES
    NQ = S // TQ
    q_spec, k_spec, v_spec, stat_spec, _, _ = _att_specs(S, D, TQ)

    def body(*refs):
        (q_ref, k_ref, v_ref), (o_ref, lt_ref, first_ref), _, parts = _side_parts(side, refs, 3, 3)
        begin, end = _side_hooks(side, parts, *_grid_ends(NP, NQ))
        begin()
        qi = pl.program_id(1)
        t_suffix = _tri(lambda r, c: r > c)
        qs = [_scaled_q(q_ref, hh) for hh in range(2)]

        def tile(j, carry, masked, lo=0):
            r0 = pl.multiple_of(j * B, B)
            if masked:
                strict = _key_minus_query(j, qi, TQ)[lo:] < 0
            out = []
            for hh in range(2):
                c, acc = carry[hh]
                k = k_ref[pl.ds(r0, B), _head_lanes(hh)]
                v = v_ref[pl.ds(r0, B), _head_lanes(hh)]
                lb, l1 = _log_sigmoids_fast(_dot_nt(qs[hh][lo:], k))
                if masked:
                    l1 = jnp.where(strict, l1, 0.0)
                sums = _hi_lo_dot(l1, t_suffix)
                a = jnp.exp(lb + c[lo:] + sums[:, :B])
                if masked:
                    a = jnp.where(strict, a, 0.0)
                out.append((_below(c, lo, c[lo:] + sums[:, B:]), _below(acc, lo, acc[lo:] + _dot(a.astype(BF16), v))))
            return tuple(out)

        carry = (jnp.zeros((TQ, B), F32), jnp.zeros((TQ, HEAD_DIM), F32))
        carry = (carry, carry)
        for jj in reversed(range(R)):
            carry = tile(qi * R + jj, carry, True, jj * B)

        def alive(cr):
            return jnp.max(jnp.maximum(cr[0][0], cr[1][0])) > SB_DEAD_LOG

        def walk(state):
            j, cr, _ = state
            cr = tile(j, cr, False)
            return j - 1, cr, alive(cr)

        j, carry, _ = lax.while_loop(lambda st: jnp.logical_and(st[0] >= 0, st[2]), walk,
                                     (qi * R - 1, carry, alive(carry)))
        first_ref[pl.program_id(0), qi] = (j + 1).astype(F32)
        for hh in range(2):
            c, acc = carry[hh]
            o_ref[:, _head_lanes(hh)] = acc
            lt_ref[:, hh * B:(hh + 1) * B] = c
        end()

    return _att_call(
        body, name, (NP, NQ), [q_spec, k_spec, v_spec],
        [q_spec, stat_spec, pl.BlockSpec(memory_space=pltpu.SMEM)],
        [jax.ShapeDtypeStruct((S, D), F32), jax.ShapeDtypeStruct((S, 2 * NP * B), F32),
         jax.ShapeDtypeStruct((NP, NQ), F32)], [], [qkv, qkv, qkv], side, ("arbitrary", "arbitrary"))


def _sb_bwd(qkv, do, lt, first, name, side=None):
    S, D3 = qkv.shape
    D = D3 // 3
    B, TQ = ATT_BLOCK, SB_Q_TILE
    R = TQ // B
    NP = D // LANES
    NQ = S // TQ
    q_spec, k_spec, v_spec, stat_spec, seq_spec, _ = _att_specs(S, D, TQ)

    def body(*refs):
        ins, (dq_ref, dk_ref, dv_ref), (dk_acc, dv_acc), parts = _side_parts(side, refs, 6, 3)
        first_ref, q_ref, k_ref, v_ref, do_ref, lt_ref = ins
        begin, end = _side_hooks(side, parts, *_grid_ends(NP, NQ))
        begin()
        qi = pl.program_id(1)
        t_prefix = _tri(lambda r, c: r <= c)
        t_before = _tri(lambda r, c: r < c)

        @pl.when(qi == 0)
        def _():
            dk_acc[...] = jnp.zeros_like(dk_acc)
            dv_acc[...] = jnp.zeros_like(dv_acc)

        qs = [_scaled_q(q_ref, hh) for hh in range(2)]
        dob = [do_ref[:, _head_lanes(hh)].astype(BF16) for hh in range(2)]
        ltot = [lt_ref[:, hh * B:(hh + 1) * B] for hh in range(2)]

        def tile(j, carry, masked, lo=0):
            r0 = pl.multiple_of(j * B, B)
            if masked:
                strict = _key_minus_query(j, qi, TQ)[lo:] < 0
            out = []
            for hh in range(2):
                pre, cu, dq = carry[hh]
                lanes = _head_lanes(hh)
                k = k_ref[pl.ds(r0, B), lanes]
                v = v_ref[pl.ds(r0, B), lanes]
                lb, l1 = _log_sigmoids_fast(_dot_nt(qs[hh][lo:], k))
                if masked:
                    l1 = jnp.where(strict, l1, 0.0)
                sums = _hi_lo_dot(l1, t_prefix)
                a = jnp.exp(lb + (ltot[hh][lo:] - pre[lo:] - sums[:, :B]))
                if masked:
                    a = jnp.where(strict, a, 0.0)
                u = a * _dot_nt(dob[hh][lo:], v)
                usums = _hi_lo_dot(u, t_before)
                dz = u - (u + cu[lo:] + usums[:, :B]) * jnp.exp(lb)
                if masked:
                    dz = jnp.where(strict, dz, 0.0)
                dzb = dz.astype(BF16)
                dk_acc[pl.ds(r0, B), lanes] += _dot_tn(dzb, qs[hh][lo:])
                dv_acc[pl.ds(r0, B), lanes] += _dot_tn(a.astype(BF16), dob[hh][lo:])
                out.append((_below(pre, lo, pre[lo:] + sums[:, B:]), _below(cu, lo, cu[lo:] + usums[:, B:]),
                            _below(dq, lo, dq[lo:] + _dot(dzb, k))))
            return tuple(out)

        zero = (jnp.zeros((TQ, B), F32), jnp.zeros((TQ, B), F32), jnp.zeros((TQ, HEAD_DIM), F32))
        first = jnp.clip(first_ref[pl.program_id(0), qi].astype(jnp.int32), 0, qi * R)
        carry = lax.fori_loop(first, qi * R, lambda j, cr: tile(j, cr, False), (zero, zero))
        for jj in range(R):
            carry = tile(qi * R + jj, carry, True, jj * B)
        for hh in range(2):
            dq_ref[:, _head_lanes(hh)] = (carry[hh][2] * ATT_SCALE).astype(BF16)

        @pl.when(qi == NQ - 1)
        def _():
            dk_ref[...] = dk_acc[...].astype(BF16)
            dv_ref[...] = dv_acc[...].astype(BF16)

        end()

    out = jax.ShapeDtypeStruct((S, D), BF16)
    return _att_call(
        body, name, (NP, NQ), [pl.BlockSpec(memory_space=pltpu.SMEM), q_spec, k_spec, v_spec, q_spec, stat_spec],
        [q_spec, seq_spec, seq_spec], [out, out, out],
        [pltpu.VMEM((S, LANES), F32), pltpu.VMEM((S, LANES), F32)], [first, qkv, qkv, qkv, do, lt], side,
        ("arbitrary", "arbitrary"))


def _fox_specs(S, D, TQ):
    row_spec = pl.BlockSpec((2 * SUBLANES, TQ), lambda p, i: (p, i))
    keyb_spec = pl.BlockSpec((S, 2 * ATT_BLOCK), lambda p, i: (0, p))
    return row_spec, keyb_spec


def _key_gt_query_t(j, qi, TQ):
    key = lax.broadcasted_iota(jnp.int32, (ATT_BLOCK, TQ), 0)
    qry = lax.broadcasted_iota(jnp.int32, (ATT_BLOCK, TQ), 1)
    return key - qry + (j * ATT_BLOCK - qi * TQ) > 0


def _fox_fwd(qkv, cq_rows, ck_b, name, side=None):
    S, D3 = qkv.shape
    D = D3 // 3
    B, TQ = ATT_BLOCK, FOX_Q_TILE
    R = TQ // B
    NP = D // LANES
    NQ = S // TQ
    q_spec, k_spec, v_spec, _, _, _ = _att_specs(S, D, TQ)
    row_spec, keyb_spec = _fox_specs(S, D, TQ)

    def body(*refs):
        (q_ref, k_ref, v_ref, cq_ref, ck_ref), (o_ref, lse_ref), _, parts = _side_parts(side, refs, 5, 2)
        begin, end = _side_hooks(side, parts, *_grid_ends(NP, NQ))
        begin()
        qi = pl.program_id(1)
        qs = [_scaled_q(q_ref, hh) for hh in range(2)]
        cq = [cq_ref[hh * SUBLANES:hh * SUBLANES + 1, :] for hh in range(2)]

        def tile(j, carry, masked, lo=0):
            r0 = pl.multiple_of(j * B, B)
            if masked:
                hidden = _key_gt_query_t(j, qi, TQ)[:, lo:]
            out = []
            for hh in range(2):
                m, lsum, acc = carry[hh]
                k = k_ref[pl.ds(r0, B), _head_lanes(hh)]
                v = v_ref[pl.ds(r0, B), _head_lanes(hh)]
                ck = jnp.tile(ck_ref[pl.ds(r0, B), hh * B:(hh + 1) * B], (1, (TQ - lo) // B))
                s = _dot_nt(k, qs[hh][lo:]) + (cq[hh][:, lo:] - ck)
                if masked:
                    s = jnp.where(hidden, NEG_BIG, s)
                m_new = jnp.maximum(m[:, lo:], jnp.max(s, axis=0, keepdims=True))
                p = jnp.exp(s - m_new)
                alpha = jnp.exp(m[:, lo:] - m_new)
                out.append((_below(m, lo, m_new, 1),
                            _below(lsum, lo, alpha * lsum[:, lo:] + jnp.sum(p, axis=0, keepdims=True), 1),
                            _below(acc, lo, alpha * acc[:, lo:] + _dot_tn(v, p.astype(BF16)), 1)))
            return tuple(out)

        zero = (jnp.full((1, TQ), NEG_BIG, F32), jnp.zeros((1, TQ), F32), jnp.zeros((HEAD_DIM, TQ), F32))
        carry = lax.fori_loop(0, qi * (R // 2), lambda t, cr: tile(2 * t + 1, tile(2 * t, cr, False), False),
                              (zero, zero))
        for jj in range(R):
            carry = tile(qi * R + jj, carry, True, jj * B)
        o_t = jnp.concatenate([carry[hh][2] * (1.0 / carry[hh][1]) for hh in range(2)], axis=0)
        o_ref[...] = o_t.T
        for hh in range(2):
            m, lsum, _ = carry[hh]
            lse_ref[hh * SUBLANES:(hh + 1) * SUBLANES, :] = jnp.broadcast_to(m + jnp.log(lsum), (SUBLANES, TQ))
        end()

    return _att_call(
        body, name, (NP, NQ), [q_spec, k_spec, v_spec, row_spec, keyb_spec], [q_spec, row_spec],
        [jax.ShapeDtypeStruct((S, D), F32), jax.ShapeDtypeStruct((2 * NP * SUBLANES, S), F32)], [],
        [qkv, qkv, qkv, cq_rows, ck_b], side, ("arbitrary", "arbitrary"))


def _fox_bwd(qkv, do, o, lse_rows, cq_rows, ck_b, name, side=None):
    S, D3 = qkv.shape
    D = D3 // 3
    B, TQ = ATT_BLOCK, FOX_Q_TILE
    R = TQ // B
    NP = D // LANES
    NQ = S // TQ
    q_spec, k_spec, v_spec, _, seq_spec, _ = _att_specs(S, D, TQ)
    row_spec, keyb_spec = _fox_specs(S, D, TQ)

    def body(*refs):
        ins, outs, (dk_acc, dv_acc), parts = _side_parts(side, refs, 8, 5)
        q_ref, k_ref, v_ref, do_ref, o_ref, lse_ref, cq_ref, ck_ref = ins
        dq_ref, dk_ref, dv_ref, dcq_ref, dck_ref = outs
        begin, end = _side_hooks(side, parts, *_grid_ends(NP, NQ))
        begin()
        qi = pl.program_id(1)

        @pl.when(qi == 0)
        def _():
            dk_acc[...] = jnp.zeros_like(dk_acc)
            dv_acc[...] = jnp.zeros_like(dv_acc)
            dck_ref[...] = jnp.zeros_like(dck_ref)

        qs = [_scaled_q(q_ref, hh) for hh in range(2)]
        dob = [do_ref[:, _head_lanes(hh)].astype(BF16) for hh in range(2)]
        prod_t = (do_ref[...] * o_ref[...]).T
        delta = [jnp.sum(prod_t[hh * HEAD_DIM:(hh + 1) * HEAD_DIM], axis=0, keepdims=True) for hh in range(2)]
        cq = [cq_ref[hh * SUBLANES:hh * SUBLANES + 1, :] for hh in range(2)]
        lse = [lse_ref[hh * SUBLANES:hh * SUBLANES + 1, :] for hh in range(2)]

        def tile(j, carry, masked, lo=0):
            r0 = pl.multiple_of(j * B, B)
            if masked:
                hidden = _key_gt_query_t(j, qi, TQ)[:, lo:]
            out = []
            for hh in range(2):
                dq, keysum = carry[hh]
                lanes = _head_lanes(hh)
                k = k_ref[pl.ds(r0, B), lanes]
                v = v_ref[pl.ds(r0, B), lanes]
                ck = jnp.tile(ck_ref[pl.ds(r0, B), hh * B:(hh + 1) * B], (1, (TQ - lo) // B))
                p = jnp.exp(_dot_nt(k, qs[hh][lo:]) + (cq[hh][:, lo:] - ck) - lse[hh][:, lo:])
                if masked:
                    p = jnp.where(hidden, 0.0, p)
                ds = p * (_dot_nt(v, dob[hh][lo:]) - delta[hh][:, lo:])
                dsb = ds.astype(BF16)
                dk_acc[pl.ds(r0, B), lanes] += _dot(dsb, qs[hh][lo:])
                dv_acc[pl.ds(r0, B), lanes] += _dot(p.astype(BF16), dob[hh][lo:])
                qsum = jnp.sum(ds, axis=1, keepdims=True)
                dck_ref[pl.ds(r0, B), hh * B:(hh + 1) * B] -= jnp.broadcast_to(qsum, (B, B))
                out.append((_below(dq, lo, dq[:, lo:] + _dot_tn(k, dsb), 1),
                            _below(keysum, lo, keysum[:, lo:] + jnp.sum(ds, axis=0, keepdims=True), 1)))
            return tuple(out)

        zero = (jnp.zeros((HEAD_DIM, TQ), F32), jnp.zeros((1, TQ), F32))
        carry = lax.fori_loop(0, qi * (R // 2), lambda t, cr: tile(2 * t + 1, tile(2 * t, cr, False), False),
                              (zero, zero))
        for jj in range(R):
            carry = tile(qi * R + jj, carry, True, jj * B)
        dq_t = jnp.concatenate([carry[hh][0] for hh in range(2)], axis=0)
        dq_ref[...] = (dq_t.T * ATT_SCALE).astype(BF16)
        for hh in range(2):
            dcq_ref[hh * SUBLANES:(hh + 1) * SUBLANES, :] = jnp.broadcast_to(carry[hh][1], (SUBLANES, TQ))

        @pl.when(qi == NQ - 1)
        def _():
            dk_ref[...] = dk_acc[...].astype(BF16)
            dv_ref[...] = dv_acc[...].astype(BF16)

        end()

    out = jax.ShapeDtypeStruct((S, D), BF16)
    return _att_call(
        body, name, (NP, NQ), [q_spec, k_spec, v_spec, q_spec, q_spec, row_spec, row_spec, keyb_spec],
        [q_spec, seq_spec, seq_spec, row_spec, keyb_spec],
        [out, out, out, jax.ShapeDtypeStruct((2 * NP * SUBLANES, S), F32), jax.ShapeDtypeStruct((S, 2 * NP * B), F32)],
        [pltpu.VMEM((S, LANES), F32), pltpu.VMEM((S, LANES), F32)],
        [qkv, qkv, qkv, do, o, lse_rows, cq_rows, ck_b], side, ("arbitrary", "arbitrary"))


def _forget_cumsum(f3, bias, name):
    NQ, NH, B = f3.shape

    def body(f_ref, b_ref, cum_ref):
        row = lax.broadcasted_iota(jnp.int32, (B, B), 0)
        col = lax.broadcasted_iota(jnp.int32, (B, B), 1)
        t_incl = jnp.where(row <= col, 1.0, 0.0).astype(BF16)

        def step(b, carry):
            lf, _ = _log_sigmoids(f_ref[b] + b_ref[...])
            cum = _split_dot(lf, t_incl, 3) + carry
            cum_ref[b] = cum
            return jnp.broadcast_to(cum[:, B - 1:B], (NH, B))

        lax.fori_loop(0, NQ, step, jnp.zeros((NH, B), F32))

    return pl.pallas_call(
        body, name=name, in_specs=[VMEM_WHOLE, VMEM_WHOLE], out_specs=VMEM_WHOLE,
        out_shape=jax.ShapeDtypeStruct((NQ, NH, B), F32), compiler_params=_params(),
    )(f3, bias)


def _forget_cumsum_bwd(dcum3, f3, bias, name):
    NQ, NH, B = f3.shape

    def body(d_ref, f_ref, b_ref, df_ref, tot_ref):
        row = lax.broadcasted_iota(jnp.int32, (B, B), 0)
        col = lax.broadcasted_iota(jnp.int32, (B, B), 1)
        t_rev = jnp.where(row >= col, 1.0, 0.0).astype(BF16)

        def step(it, carry):
            run, tot = carry
            b = NQ - 1 - it
            dlf = _split_dot(d_ref[b], t_rev, 3) + run
            f = f_ref[b] + b_ref[...]
            df = dlf * (1.0 / (1.0 + jnp.exp(f)))
            df_ref[b] = df
            return jnp.broadcast_to(dlf[:, 0:1], (NH, B)), tot + df

        _, tot = lax.fori_loop(0, NQ, step, (jnp.zeros((NH, B), F32), jnp.zeros((NH, B), F32)))
        tot_ref[...] = _split_dot(tot, jnp.ones((B, B), BF16), 3)

    return pl.pallas_call(
        body, name=name, in_specs=[VMEM_WHOLE, VMEM_WHOLE, VMEM_WHOLE], out_specs=[VMEM_WHOLE, VMEM_WHOLE],
        out_shape=[jax.ShapeDtypeStruct((NQ, NH, B), F32), jax.ShapeDtypeStruct((NH, B), F32)],
        compiler_params=_params(),
    )(dcum3, f3, bias)


def _silu(x, name):
    def body(x_ref, o_ref):
        v = x_ref[...]
        o_ref[...] = v * (1.0 / (1.0 + jnp.exp(-v)))

    return pl.pallas_call(body, name=name, in_specs=[VMEM_WHOLE], out_specs=VMEM_WHOLE,
                          out_shape=jax.ShapeDtypeStruct(x.shape, F32), compiler_params=_params())(x)


def _sum_leading(x, name):
    N, R, C = x.shape
    T = _pick(R, (256, 128, 64, 32, 16, 8))

    def body(x_ref, o_ref):
        acc = x_ref[0].astype(F32)
        for r in range(1, N):
            acc = acc + x_ref[r].astype(F32)
        o_ref[...] = acc

    return pl.pallas_call(
        body, name=name, grid=(R // T,), in_specs=[pl.BlockSpec((N, T, C), lambda i: (0, i, 0))],
        out_specs=pl.BlockSpec((T, C), lambda i: (i, 0)), out_shape=jax.ShapeDtypeStruct((R, C), F32),
        compiler_params=_params(),
    )(x)


def _adamw(w, g, m, v, name):
    shape = w.shape
    C = shape[-1]
    R = w.size // C
    T = R
    for cand in (512, 256, 128, 64, 32, 16, 8):
        if R % cand == 0 and cand * C * 4 <= (1 << 20):
            T = cand
            break
    spec = pl.BlockSpec((T, C), lambda i: (i, 0))
    c1 = 1.0 / (1.0 - ADAM_B1 ** ADAM_STEP)
    c2 = 1.0 / (1.0 - ADAM_B2 ** ADAM_STEP)

    def body(w_ref, g_ref, m_ref, v_ref, d_ref, nm_ref, nv_ref):
        gv = g_ref[...]
        nm = ADAM_B1 * m_ref[...] + (1.0 - ADAM_B1) * gv
        nv = ADAM_B2 * v_ref[...] + (1.0 - ADAM_B2) * (gv * gv)
        nm_ref[...] = nm
        nv_ref[...] = nv
        d_ref[...] = -ADAM_LR * ((nm * c1) / (jnp.sqrt(nv * c2) + ADAM_EPS) + ADAM_WD * w_ref[...])

    out = jax.ShapeDtypeStruct((R, C), F32)
    d, nm, nv = pl.pallas_call(
        body, name=name, grid=(R // T,), in_specs=[spec] * 4, out_specs=[spec] * 3, out_shape=[out] * 3,
        compiler_params=_params(),
    )(w.reshape(R, C), g.reshape(R, C), m.reshape(R, C), v.reshape(R, C))
    return d.reshape(shape), nm.reshape(shape), nv.reshape(shape)


def _mesh_pos():
    return lax.axis_index("x"), lax.axis_index("y"), lax.axis_index("c")


def _all_gather_small(x2d, name):
    m_per, n = x2d.shape

    def body(x_ref, out_ref, send_sems, recv_sems, local_sem):
        x, y, c = _mesh_pos()
        me, sibling = (x, y, c), (x, y, 1 - c)
        chips = [(1 - x, y), (x, 1 - y), (1 - x, 1 - y)]

        def rows(px, py, pc):
            return out_ref.at[pl.ds((4 * px + 2 * py + pc) * m_per, m_per), :]

        def copy(k, block, to, src=None):
            return pltpu.make_async_remote_copy(
                src_ref=rows(*block) if src is None else src, dst_ref=rows(*block),
                send_sem=send_sems.at[k], recv_sem=recv_sems.at[k], device_id=to, device_id_type=MESH)

        mine = pltpu.make_async_copy(x_ref, rows(*me), local_sem)
        mine.start()
        first = [copy(0, me, sibling, src=x_ref)]
        first += [copy(1 + j, me, (*chip, c), src=x_ref) for j, chip in enumerate(chips)]
        for cp in first:
            cp.start()
        passed = [copy(4 + j, (*chip, c), sibling) for j, chip in enumerate(chips)]
        for j, chip in enumerate(chips):
            copy(1 + j, (*chip, c), me).wait_recv()
            passed[j].start()
        copy(0, sibling, me).wait_recv()
        for j, chip in enumerate(chips):
            copy(4 + j, (*chip, 1 - c), me).wait_recv()
        for cp in first + passed:
            cp.wait_send()
        mine.wait()

    return pl.pallas_call(
        body, name=name, out_shape=jax.ShapeDtypeStruct((N_DEV * m_per, n), x2d.dtype),
        in_specs=[VMEM_WHOLE], out_specs=VMEM_WHOLE,
        scratch_shapes=[pltpu.SemaphoreType.DMA((7,)), pltpu.SemaphoreType.DMA((7,)), pltpu.SemaphoreType.DMA],
        compiler_params=_params(),
    )(x2d)


class _Side:
    def __init__(self, inputs, out_shapes, sems, start, wait):
        self.inputs, self.out_shapes, self.sems, self.start, self.wait = inputs, out_shapes, sems, start, wait


def _join_sides(sides):
    sides = [s for s in sides if s is not None]
    if not sides:
        return None
    bounds_in, bounds_out, bounds_sem = [0], [0], [0]
    for s in sides:
        bounds_in.append(bounds_in[-1] + len(s.inputs))
        bounds_out.append(bounds_out[-1] + len(s.out_shapes))
        bounds_sem.append(bounds_sem[-1] + len(s.sems))

    def each(method):
        def run(ins, outs, sems):
            for t, s in enumerate(sides):
                getattr(s, method)(ins[bounds_in[t]:bounds_in[t + 1]], outs[bounds_out[t]:bounds_out[t + 1]],
                                   sems[bounds_sem[t]:bounds_sem[t + 1]])
        return run

    return _Side([a for s in sides for a in s.inputs], [o for s in sides for o in s.out_shapes],
                 [m for s in sides for m in s.sems], each("start"), each("wait"))


def _side_parts(side, refs, n_in, n_out):
    if side is None:
        return refs[:n_in], refs[n_in:n_in + n_out], refs[n_in + n_out:], None
    si, so, ss = len(side.inputs), len(side.out_shapes), len(side.sems)
    a = n_in + si
    b = a + n_out + so
    ins, side_in = refs[:n_in], refs[n_in:a]
    outs, side_out = refs[a:a + n_out], refs[a + n_out:b]
    scratch, side_sems = refs[b:len(refs) - ss], refs[len(refs) - ss:]
    return ins, outs, scratch, (side_in, side_out, side_sems)


def _side_hooks(side, parts, first, last):
    if side is None:
        return lambda: None, lambda: None

    def begin():
        @pl.when(first)
        def _():
            side.start(*parts)

    def end():
        @pl.when(last)
        def _():
            side.wait(*parts)

    return begin, end


def _run_side(side, name):
    n_in = len(side.inputs)
    n_out = len(side.out_shapes)

    def body(*refs):
        parts = (refs[:n_in], refs[n_in:n_in + n_out], refs[n_in + n_out:])
        side.start(*parts)
        side.wait(*parts)

    return pl.pallas_call(
        body, name=name, out_shape=side.out_shapes, in_specs=[ANY] * n_in, out_specs=[ANY] * n_out,
        scratch_shapes=side.sems, compiler_params=_params(),
    )(*side.inputs)


def _gather_side(ws):
    n = len(ws)

    def copies(ins, outs, sems):
        send_sems, recv_sems, local_sems = sems
        x, y, c = _mesh_pos()
        k_me = 2 * x + y
        chips = [(1 - x, y), (x, 1 - y), (1 - x, 1 - y)]

        def remote(i, j, slot):
            px, py = chips[j]
            return pltpu.make_async_remote_copy(
                src_ref=ins[i], dst_ref=outs[i].at[slot], send_sem=send_sems.at[i, j],
                recv_sem=recv_sems.at[i, j], device_id=(px, py, c), device_id_type=MESH)

        local = [pltpu.make_async_copy(ins[i], outs[i].at[k_me], local_sems.at[i]) for i in range(n)]
        return remote, local, k_me, chips

    def start(ins, outs, sems):
        remote, local, k_me, _ = copies(ins, outs, sems)
        for i in range(n):
            local[i].start()
            for j in range(3):
                remote(i, j, k_me).start()

    def wait(ins, outs, sems):
        remote, local, k_me, chips = copies(ins, outs, sems)
        for i in range(n):
            for j, (px, py) in enumerate(chips):
                remote(i, j, 2 * px + py).wait_recv()
        for i in range(n):
            for j in range(3):
                remote(i, j, k_me).wait_send()
            local[i].wait()

    return _Side(list(ws), [jax.ShapeDtypeStruct((N_CHIPS,) + w.shape, w.dtype) for w in ws],
                 [pltpu.SemaphoreType.DMA((n, 3)), pltpu.SemaphoreType.DMA((n, 3)), pltpu.SemaphoreType.DMA((n,))],
                 start, wait)


def _scatter_side(gs, whole_shard=False):
    n = len(gs)
    halves = [g.shape[1] if whole_shard else g.shape[1] // 2 for g in gs]

    def copies(ins, outs, sems):
        send_sems, recv_sems, local_sems = sems
        x, y, c = _mesh_pos()

        def flip(v, bit):
            return 1 - v if bit else v

        def piece(i, px, py, pc):
            if whole_shard:
                return ins[i].at[2 * px + py]
            return ins[i].at[2 * px + py, pl.ds(pc * halves[i], halves[i])]

        me = 4 * x + 2 * y + c

        def remote(i, r, arriving=False):
            px, py, pc = flip(x, r & 4), flip(y, r & 2), flip(c, r & 1)
            slot = 4 * px + 2 * py + pc if arriving else me
            return pltpu.make_async_remote_copy(
                src_ref=piece(i, px, py, pc), dst_ref=outs[i].at[slot], send_sem=send_sems.at[i, r - 1],
                recv_sem=recv_sems.at[i, r - 1], device_id=(px, py, pc), device_id_type=MESH)

        local = [pltpu.make_async_copy(piece(i, x, y, c), outs[i].at[me], local_sems.at[i]) for i in range(n)]
        return remote, local

    def start(ins, outs, sems):
        remote, local = copies(ins, outs, sems)
        for i in range(n):
            local[i].start()
            for r in range(1, N_DEV):
                remote(i, r).start()

    def wait(ins, outs, sems):
        remote, local = copies(ins, outs, sems)
        for i in range(n):
            for r in range(1, N_DEV):
                remote(i, r, arriving=True).wait_recv()
        for i in range(n):
            for r in range(1, N_DEV):
                remote(i, r).wait_send()
            local[i].wait()

    return _Side(list(gs), [jax.ShapeDtypeStruct((N_DEV, h) + g.shape[2:], g.dtype) for g, h in zip(gs, halves)],
                 [pltpu.SemaphoreType.DMA((n, N_DEV - 1)), pltpu.SemaphoreType.DMA((n, N_DEV - 1)),
                  pltpu.SemaphoreType.DMA((n,))], start, wait)


def _swap_side(hs):
    n = len(hs)

    def copies(ins, outs, sems):
        send_sems, recv_sems, local_sems = sems
        x, y, c = _mesh_pos()

        def remote(i, slot):
            return pltpu.make_async_remote_copy(
                src_ref=ins[i], dst_ref=outs[i].at[slot], send_sem=send_sems.at[i], recv_sem=recv_sems.at[i],
                device_id=(x, y, 1 - c), device_id_type=MESH)

        local = [pltpu.make_async_copy(ins[i], outs[i].at[c], local_sems.at[i]) for i in range(n)]
        return remote, local, c

    def start(ins, outs, sems):
        remote, local, c = copies(ins, outs, sems)
        for i in range(n):
            local[i].start()
            remote(i, c).start()

    def wait(ins, outs, sems):
        remote, local, c = copies(ins, outs, sems)
        for i in range(n):
            remote(i, 1 - c).wait_recv()
        for i in range(n):
            remote(i, c).wait_send()
            local[i].wait()

    return _Side(list(hs), [jax.ShapeDtypeStruct((2,) + h.shape, h.dtype) for h in hs],
                 [pltpu.SemaphoreType.DMA((n,)), pltpu.SemaphoreType.DMA((n,)), pltpu.SemaphoreType.DMA((n,))],
                 start, wait)


def _pad_rows(a, rows):
    return jnp.pad(a, ((0, rows - a.shape[0]), (0, 0)))


def kernel(x, c, w_mod, b_mod, g_mix_pre, g_mix_post, w_qkv, w_o, w_fg, b_fg, g_ffn_pre, g_ffn_post, w_ffn_gate, w_ffn_up, w_conv, b_conv, w_ffn_down, loss_target, m_w_mod, m_b_mod, m_g_mix_pre, m_g_mix_post, m_w_qkv, m_w_o, m_w_fg, m_b_fg, m_g_ffn_pre, m_g_ffn_post, m_w_ffn_gate, m_w_ffn_up, m_w_conv, m_b_conv, m_w_ffn_down, v_w_mod, v_b_mod, v_g_mix_pre, v_g_mix_post, v_w_qkv, v_w_o, v_w_fg, v_b_fg, v_g_ffn_pre, v_g_ffn_post, v_w_ffn_gate, v_w_ffn_up, v_w_conv, v_b_conv, v_w_ffn_down):
    xs = x[0]
    target = loss_target[0]
    S, D = xs.shape
    L = w_mod.shape[0]
    LF = w_fg.shape[0]
    MS = w_mod.shape[2]
    QS = w_qkv.shape[2]
    OS = w_o.shape[1]
    FS = w_ffn_gate.shape[2]
    F = N_CHIPS * FS
    NH = D // HEAD_DIM
    B = ATT_BLOCK
    NQ = S // B
    ax, ay, ac = _mesh_pos()
    k_me = 2 * ax + ay
    b_me = 4 * ax + 2 * ay + ac

    conv_rows = -(-(L * 3 * FS) // D)
    conv_rows = -(-conv_rows // SUBLANES) * SUBLANES
    conv_flat = jnp.pad(w_conv.reshape(-1), (0, conv_rows * D - L * 3 * FS)).reshape(conv_rows, D)
    first = jnp.concatenate([_pad_rows(c, SUBLANES), conv_flat], axis=0)
    first_all = _all_gather_small(first, "ag_cond").reshape(N_DEV, SUBLANES + conv_rows, D)
    c_all = first_all[:, 0, :]
    conv_all = first_all[0::2, SUBLANES:, :].reshape(N_CHIPS, -1)[:, :L * 3 * FS]
    w_conv_full = conv_all.reshape(N_CHIPS, L, 3, FS).transpose(1, 2, 0, 3).reshape(L, 3, F)
    c_act = _silu(c_all, "silu_c")

    mod_part = jnp.concatenate(
        [_mm(c_act, w_mod[l], "nn", F32, "mm_mod", tm=N_DEV, tn=MS, tk=D) for l in range(L)], axis=1)
    mod_all = _all_gather_small(mod_part, "ag_mod").reshape(N_CHIPS, 2, N_DEV, L, MS)[:, 0]
    mod_mine = lax.dynamic_index_in_dim(mod_all, b_me, axis=1, keepdims=False)
    mod = mod_mine.transpose(1, 0, 2).reshape(L, N_MOD * D) + b_mod

    wq_b, wo_b, wg_b, wu_b, wd_b = [w.astype(BF16) for w in (w_qkv, w_o, w_ffn_gate, w_ffn_up, w_ffn_down)]
    w_fg_b = w_fg.astype(BF16)

    def mixer_shards(l):
        return [wq_b[l], wo_b[l]] + ([w_fg_b[l // 2]] if l % 2 == 1 else [])

    def mlp_shards(l):
        return [wg_b[l], wu_b[l], wd_b[l]]

    def side_by_side(gth):
        return gth.transpose(1, 0, 2).reshape(D, -1)

    def mixer_weights(gathered):
        wts = dict(qkv=side_by_side(gathered[0]), o=gathered[1].reshape(D, D))
        if len(gathered) > 2:
            wts["fg"] = jnp.pad(gathered[2].reshape(D, NH), ((0, 0), (0, LANES - NH)))
        return wts

    def mlp_weights(gathered):
        return dict(g=side_by_side(gathered[0]), u=side_by_side(gathered[1]), d=gathered[2].reshape(F, D))

    W = [None] * L
    W[0] = mixer_weights(_run_side(_gather_side(mixer_shards(0)), "ag_weights_first"))

    def vec(a):
        return a.reshape(1, -1)

    saved = []
    xcur = xs
    for l in range(L):
        sh_a, sc_a, gt_a, sh_f, sc_f, gt_f = [vec(mod[l, j * D:(j + 1) * D]) for j in range(N_MOD)]
        is_fox = l % 2 == 1
        jf = l // 2
        h1 = _norm_mod(xcur, vec(g_mix_pre[l]), sc_a, sh_a, "norm_mod")
        qkv = _mm(h1, W[l]["qkv"], "nn", BF16, "mm_qkv")
        carried_shards = mlp_shards(l) + (mixer_shards(l + 1) if l + 1 < L else [])
        next_weights = _gather_side(carried_shards)
        if is_fox:
            flog = _mm(h1, W[l]["fg"], "nn", F32, "mm_fg")[:, :NH]
            f3 = flog.reshape(NQ, B, NH).transpose(0, 2, 1)
            bias = b_fg[jf].reshape(NH, 1)
            cum3 = _forget_cumsum(f3, bias, "forget_cumsum")
            cum_sn = cum3.transpose(0, 2, 1).reshape(S, NH)
            ck_b = jnp.repeat(cum_sn, B, axis=1)
            cq_rows = jnp.repeat(cum_sn.T, SUBLANES, axis=0)
            (o, stat), gathered = _fox_fwd(qkv, cq_rows, ck_b, "fox_fwd", next_weights)
            extra = (f3, bias, cq_rows, ck_b)
        else:
            (o, stat, extra), gathered = _sb_fwd(qkv, "sb_fwd", next_weights)
        W[l].update(mlp_weights(gathered[:3]))
        if l + 1 < L:
            W[l + 1] = mixer_weights(gathered[3:])
        p = _mm(o, W[l]["o"], "nn", F32, "mm_o")
        x1 = _post_res(xcur, p, vec(g_mix_post[l]), gt_a, "post_res")
        h2 = _norm_mod(x1, vec(g_ffn_pre[l]), sc_f, sh_f, "norm_mod")
        gate = _mm(h2, W[l]["g"], "nn", F32, "mm_gate")
        up = _mm(h2, W[l]["u"], "nn", F32, "mm_up")
        wc = w_conv_full[l]
        bc = vec(b_conv[l])
        a = _conv_gate(gate, up, wc, bc, "conv_gate")
        yv = _mm(a, W[l]["d"], "nn", F32, "mm_down")
        x2 = _post_res(x1, yv, vec(g_ffn_post[l]), gt_f, "post_res")
        saved.append(dict(x0=xcur, h1=h1, qkv=qkv, o=o, stat=stat, extra=extra, p=p, x1=x1, h2=h2, gate=gate, up=up, a=a,
                          y=yv, mods=(sh_a, sc_a, gt_a, sh_f, sc_f, gt_f), wc=wc, bc=bc))
        xcur = x2

    g, sq = _loss_head(xcur, target, "loss_head")
    loss_part = 0.5 * jnp.sum(sq) / D
    loss = lax.psum(loss_part, ("x", "y", "c"))

    dW_qkv, dW_o, dW_g, dW_u, dW_d = [[None] * L for _ in range(5)]
    dW_fg, db_fg = [None] * LF, [None] * LF
    dmod, dg_mix_pre, dg_mix_post, dg_ffn_pre, dg_ffn_post = [[None] * L for _ in range(5)]
    dw_conv, db_conv = [None] * L, [None] * L

    def cols(dw, width):
        return dw.reshape(dw.shape[0], N_CHIPS, width).transpose(1, 0, 2).astype(BF16)

    def rows(dw, height):
        return dw.reshape(N_CHIPS, height, dw.shape[1]).astype(BF16)

    n_big = 5
    grad_shards = [[None] * n_big for _ in range(L)]
    to_scatter, to_swap = [], []

    def carried():
        sc = [e for e in to_scatter if e[0] > 0]
        sc_whole = [e for e in to_scatter if e[0] == 0]
        sw = list(to_swap)
        del to_scatter[:], to_swap[:]
        side = _join_sides([_scatter_side([a for _, _, a in sc]) if sc else None,
                            _scatter_side([a for _, _, a in sc_whole], whole_shard=True) if sc_whole else None,
                            _swap_side([a for _, _, a in sw]) if sw else None])

        def taken(outs):
            for (lay, i, _), recv in zip(sc, outs[:len(sc)]):
                to_swap.append((lay, i, _sum_leading(recv, f"sum_grad_pieces_{i}")))
            for (lay, i, _), recv in zip(sc_whole, outs[len(sc):len(sc) + len(sc_whole)]):
                grad_shards[lay][i] = _sum_leading(recv, f"sum_grad_shard_{i}")
            for (lay, i, _), both in zip(sw, outs[len(sc) + len(sc_whole):]):
                grad_shards[lay][i] = both.reshape((-1,) + both.shape[2:])

        return side, taken

    for l in reversed(range(L)):
        sv = saved[l]
        sh_a, sc_a, gt_a, sh_f, sc_f, gt_f = sv["mods"]
        is_fox = l % 2 == 1
        jf = l // 2
        dy, dgp_f, dgt_f = _post_res_bwd(sv["y"], vec(g_ffn_post[l]), gt_f, g, "post_res_bwd")
        da = _mm(dy, W[l]["d"], "nt", F32, "mm_da")
        dW_d[l] = _mm(sv["a"], dy, "tn", F32, "mm_dwd")
        dgate, dup, dwc, dbc = _conv_gate_bwd(sv["gate"], sv["up"], sv["wc"], sv["bc"], da, "conv_gate_bwd")
        dh2 = [_mm(dgate, W[l]["g"], "nt", F32, "mm_dh2g"), _mm(dup, W[l]["u"], "nt", F32, "mm_dh2u")]
        dW_g[l] = _mm(sv["h2"], dgate, "tn", F32, "mm_dwg")
        dW_u[l] = _mm(sv["h2"], dup, "tn", F32, "mm_dwu")
        g, dg_f, dsc_f, dsh_f = _norm_mod_bwd(sv["x1"], vec(g_ffn_pre[l]), sc_f, sh_f, dh2, g, "norm_mod_bwd")
        dp, dgp_a, dgt_a = _post_res_bwd(sv["p"], vec(g_mix_post[l]), gt_a, g, "post_res_bwd")
        do = _mm(dp, W[l]["o"], "nt", F32, "mm_do")
        dW_o[l] = _mm(sv["o"], dp, "tn", F32, "mm_dwo")
        to_scatter += [(l, 1, rows(dW_o[l], OS)), (l, 2, cols(dW_g[l], FS)), (l, 3, cols(dW_u[l], FS)),
                       (l, 4, rows(dW_d[l], FS))]
        side, taken = carried()
        if is_fox:
            f3, bias, cq_rows, ck_b = sv["extra"]
            (dq, dk, dv, dcq_rows, dck_b), side_outs = _fox_bwd(sv["qkv"], do, sv["o"], sv["stat"], cq_rows, ck_b,
                                                                "fox_bwd", side)
            dcum_sn = dcq_rows[::SUBLANES].T + dck_b[:, ::B]
            dcum3 = dcum_sn.reshape(NQ, B, NH).transpose(0, 2, 1)
            df3, df_tot = _forget_cumsum_bwd(dcum3, f3, bias, "forget_cumsum_bwd")
            df = df3.transpose(0, 2, 1).reshape(S, NH)
            dfp = jnp.pad(df, ((0, 0), (0, LANES - NH)))
            dW_fg[jf] = _mm(sv["h1"], dfp, "tn", F32, "mm_dwfg")[:, :NH]
            db_fg[jf] = df_tot[:, 0]
            dh_extra = [_mm(dfp, W[l]["fg"], "nt", F32, "mm_dh1f")]
        else:
            (dq, dk, dv), side_outs = _sb_bwd(sv["qkv"], do, sv["stat"], sv["extra"], "sb_bwd", side)
            dh_extra = []
        taken(side_outs)
        dqkv = jnp.concatenate([dq, dk, dv], axis=1)
        dh1 = _mm(dqkv, W[l]["qkv"], "nt", F32, "mm_dh1")
        dW_qkv[l] = _mm(sv["h1"], dqkv, "tn", F32, "mm_dwqkv")
        to_scatter.append((l, 0, cols(dW_qkv[l], QS)))
        g, dg_a, dsc_a, dsh_a = _norm_mod_bwd(sv["x0"], vec(g_mix_pre[l]), sc_a, sh_a, [dh1] + dh_extra, g,
                                              "norm_mod_bwd")
        dmod[l] = jnp.concatenate([dsh_a, dsc_a, dgt_a, dsh_f, dsc_f, dgt_f], axis=1)[0]
        dg_mix_pre[l], dg_mix_post[l], dg_ffn_pre[l], dg_ffn_post[l] = dg_a[0], dgp_a[0], dg_f[0], dgp_f[0]
        dw_conv[l], db_conv[l] = dwc, dbc[0]
    grad_x = g[None]

    pieces = [jnp.stack(dmod), jnp.stack(dg_mix_pre), jnp.stack(dg_mix_post), jnp.stack(dg_ffn_pre),
              jnp.stack(dg_ffn_post), jnp.stack(db_fg), jnp.stack(dW_fg), jnp.stack(dw_conv), jnp.stack(db_conv)]
    sizes = [pc.size for pc in pieces]
    total = sum(sizes)
    pack_rows = -(-total // (LANES * SUBLANES)) * SUBLANES
    pack = jnp.pad(jnp.concatenate([pc.reshape(-1) for pc in pieces]), (0, pack_rows * LANES - total))
    pack_all = _all_gather_small(pack.reshape(pack_rows, LANES), "ag_small_grads").reshape(N_DEV, pack_rows, LANES)
    small = _sum_leading(pack_all, "sum_small_grads").reshape(-1)
    offs = [0]
    for sz in sizes:
        offs.append(offs[-1] + sz)
    parts = [small[offs[i]:offs[i + 1]].reshape(pieces[i].shape) for i in range(len(pieces))]
    g_b_mod, g_g_mix_pre, g_g_mix_post, g_g_ffn_pre, g_g_ffn_post, g_b_fg, g_w_fg_full, g_w_conv_full, g_b_conv = parts
    g_w_fg = lax.dynamic_slice_in_dim(g_w_fg_full, k_me * OS, OS, axis=1)
    g_w_conv = lax.dynamic_slice_in_dim(g_w_conv_full, k_me * FS, FS, axis=2)
    dmod_all = pack_all.reshape(N_DEV, -1)[:, :L * N_MOD * D].reshape(N_DEV, L, N_CHIPS, MS)
    dmod_cols = lax.dynamic_index_in_dim(dmod_all, k_me, axis=2, keepdims=False).reshape(N_DEV, L * MS)
    g_w_mod = _mm(_pad_rows(c_act, LANES), _pad_rows(dmod_cols, LANES), "tn", F32, "mm_dwmod", tm=D, tn=MS, tk=LANES)
    g_w_mod = g_w_mod.reshape(D, L, MS).transpose(1, 0, 2)

    tail = 0
    while to_scatter or to_swap:
        side, taken = carried()
        taken(_run_side(side, f"rs_tail_{tail}"))
        tail += 1
    g_w_qkv, g_w_o, g_w_gate, g_w_up, g_w_down = [jnp.stack([grad_shards[l][i] for l in range(L)])
                                                  for i in range(n_big)]

    grads = [g_w_mod, g_b_mod, g_g_mix_pre, g_g_mix_post, g_w_qkv, g_w_o, g_w_fg, g_b_fg, g_g_ffn_pre,
             g_g_ffn_post, g_w_gate, g_w_up, g_w_conv, g_b_conv, g_w_down]
    weights = [w_mod, b_mod, g_mix_pre, g_mix_post, w_qkv, w_o, w_fg, b_fg, g_ffn_pre, g_ffn_post, w_ffn_gate,
               w_ffn_up, w_conv, b_conv, w_ffn_down]
    ms = [m_w_mod, m_b_mod, m_g_mix_pre, m_g_mix_post, m_w_qkv, m_w_o, m_w_fg, m_b_fg, m_g_ffn_pre, m_g_ffn_post,
          m_w_ffn_gate, m_w_ffn_up, m_w_conv, m_b_conv, m_w_ffn_down]
    vs = [v_w_mod, v_b_mod, v_g_mix_pre, v_g_mix_post, v_w_qkv, v_w_o, v_w_fg, v_b_fg, v_g_ffn_pre, v_g_ffn_post,
          v_w_ffn_gate, v_w_ffn_up, v_w_conv, v_b_conv, v_w_ffn_down]
    deltas, new_ms, new_vs = [], [], []
    for wv, gv, mv, vv in zip(weights, grads, ms, vs):
        d, nm, nv = _adamw(wv, gv, mv, vv, "adamw")
        deltas.append(d)
        new_ms.append(nm)
        new_vs.append(nv)
    return (loss, grad_x, *grads, *deltas, *new_ms, *new_vs)
```

```python
import functools

import jax
import jax.numpy as jnp
from jax import lax
from jax.experimental import pallas as pl
from jax.experimental.pallas import tpu as pltpu

F32 = jnp.float32
BF16 = jnp.bfloat16
MESH = pl.DeviceIdType.MESH

HEAD_DIM = 64
ATT_BLOCK = 128
SB_Q_TILE = 512
FOX_Q_TILE = 1024
SB_DEAD_LOG = -110.0
LANES = 128
SUBLANES = 8
RMS_EPS = 1e-6
N_MOD = 6
N_CHIPS = 4
N_DEV = 8
ADAM_LR = 0.001
ADAM_B1 = 0.9
ADAM_B2 = 0.999
ADAM_EPS = 1e-08
ADAM_WD = 0.01
ADAM_STEP = 10
VMEM_LIMIT_BYTES = 56 * 1024 * 1024
NEG_BIG = -1e30

ANY = pl.BlockSpec(memory_space=pl.ANY)
VMEM_WHOLE = pl.BlockSpec(memory_space=pltpu.VMEM)


def _params(**kw):
    return pltpu.CompilerParams(vmem_limit_bytes=VMEM_LIMIT_BYTES, **kw)


def _dot(a, b):
    return jnp.dot(a, b, preferred_element_type=F32)


def _dot_nt(a, b):
    return lax.dot_general(a, b, (((1,), (1,)), ((), ())), preferred_element_type=F32)


def _dot_tn(a, b):
    return lax.dot_general(a, b, (((0,), (0,)), ((), ())), preferred_element_type=F32)


def _split_dot(x, t, parts):
    acc = None
    rem = x
    for _ in range(parts):
        piece = rem.astype(BF16)
        rem = rem - piece.astype(F32)
        d = _dot(piece, t)
        acc = d if acc is None else acc + d
    return acc


def _pick(n, prefs):
    for p in prefs:
        if n % p == 0:
            return p
    return n


def _mm(a, b, dims, out_dtype, name, tm=None, tn=None, tk=None):
    if dims == "tn":
        K, M = a.shape
    else:
        M, K = a.shape
    N = b.shape[0] if dims == "nt" else b.shape[1]
    tm = tm or _pick(M, (1024, 1408, 512, 256, 128))
    tn = tn or _pick(N, (1536, 1408, 1024, 768, 512, 256, 128))
    tk = tk or _pick(K, (1024, 2816, 1408, 512, 256, 128))
    nk = K // tk
    grid = (N // tn, M // tm, nk)
    if dims == "tn":
        a_spec = pl.BlockSpec((tk, tm), lambda j, i, k: (k, i))
    else:
        a_spec = pl.BlockSpec((tm, tk), lambda j, i, k: (i, k))
    if dims == "nt":
        b_spec = pl.BlockSpec((tn, tk), lambda j, i, k: (j, k))
    else:
        b_spec = pl.BlockSpec((tk, tn), lambda j, i, k: (k, j))
    o_spec = pl.BlockSpec((tm, tn), lambda j, i, k: (i, j))

    def body(a_ref, b_ref, o_ref, *scratch):
        x = a_ref[...].astype(BF16)
        y = b_ref[...].astype(BF16)
        if dims == "nn":
            r = _dot(x, y)
        elif dims == "nt":
            r = _dot_nt(x, y)
        else:
            r = _dot_tn(x, y)
        if nk == 1:
            o_ref[...] = r.astype(out_dtype)
        else:
            acc = scratch[0]
            k = pl.program_id(2)

            @pl.when(k == 0)
            def _():
                acc[...] = r

            @pl.when(k > 0)
            def _():
                acc[...] += r

            @pl.when(k == nk - 1)
            def _():
                o_ref[...] = acc[...].astype(out_dtype)

    return pl.pallas_call(
        body,
        name=name,
        grid=grid,
        in_specs=[a_spec, b_spec],
        out_specs=o_spec,
        out_shape=jax.ShapeDtypeStruct((M, N), out_dtype),
        scratch_shapes=[pltpu.VMEM((tm, tn), F32)] if nk > 1 else [],
        compiler_params=_params(dimension_semantics=("parallel", "parallel", "arbitrary")),
    )(a, b)


def _row_tile(S):
    return _pick(S, (512, 256, 128, 64, 32, 16, 8))


def _norm_mod(x, g, sc, sh, name):
    S, D = x.shape
    T = _row_tile(S)
    row = pl.BlockSpec((T, D), lambda i: (i, 0))
    vec = pl.BlockSpec((1, D), lambda i: (0, 0))

    def body(x_ref, g_ref, sc_ref, sh_ref, h_ref):
        xv = x_ref[...]
        r = lax.rsqrt(jnp.mean(xv * xv, axis=-1, keepdims=True) + RMS_EPS)
        n = (xv * r) * g_ref[...]
        h_ref[...] = (n * (1.0 + sc_ref[...]) + sh_ref[...]).astype(BF16)

    return pl.pallas_call(
        body, name=name, grid=(S // T,), in_specs=[row, vec, vec, vec], out_specs=row,
        out_shape=jax.ShapeDtypeStruct((S, D), BF16), compiler_params=_params(),
    )(x, g, sc, sh)


def _norm_mod_bwd(x, g, sc, sh, dhs, gres, name):
    S, D = x.shape
    T = _row_tile(S)
    n_dh = len(dhs)
    row = pl.BlockSpec((T, D), lambda i: (i, 0))
    vec = pl.BlockSpec((1, D), lambda i: (0, 0))

    def body(x_ref, g_ref, sc_ref, sh_ref, *refs):
        dh_refs = refs[:n_dh]
        gres_ref, dx_ref, dg_ref, dsc_ref, dsh_ref = refs[n_dh:]
        xv = x_ref[...]
        r = lax.rsqrt(jnp.mean(xv * xv, axis=-1, keepdims=True) + RMS_EPS)
        xn = xv * r
        n = xn * g_ref[...]
        dh = dh_refs[0][...]
        for extra in dh_refs[1:]:
            dh = dh + extra[...]
        dn = dh * (1.0 + sc_ref[...])
        dxn = dn * g_ref[...]
        dx = r * (dxn - xn * jnp.mean(dxn * xn, axis=-1, keepdims=True))
        dx_ref[...] = gres_ref[...] + dx

        @pl.when(pl.program_id(0) == 0)
        def _():
            dg_ref[...] = jnp.zeros_like(dg_ref)
            dsc_ref[...] = jnp.zeros_like(dsc_ref)
            dsh_ref[...] = jnp.zeros_like(dsh_ref)

        dg_ref[...] += jnp.sum(dn * xn, axis=0, keepdims=True)
        dsc_ref[...] += jnp.sum(dh * n, axis=0, keepdims=True)
        dsh_ref[...] += jnp.sum(dh, axis=0, keepdims=True)

    vshape = jax.ShapeDtypeStruct((1, D), F32)
    return pl.pallas_call(
        body, name=name, grid=(S // T,), in_specs=[row, vec, vec, vec] + [row] * (n_dh + 1),
        out_specs=[row, vec, vec, vec],
        out_shape=[jax.ShapeDtypeStruct((S, D), F32), vshape, vshape, vshape],
        compiler_params=_params(dimension_semantics=("arbitrary",)),
    )(x, g, sc, sh, *dhs, gres)


def _post_res(x, p, gp, gt, name):
    S, D = x.shape
    T = _row_tile(S)
    row = pl.BlockSpec((T, D), lambda i: (i, 0))
    vec = pl.BlockSpec((1, D), lambda i: (0, 0))

    def body(x_ref, p_ref, gp_ref, gt_ref, o_ref):
        pv = p_ref[...]
        r = lax.rsqrt(jnp.mean(pv * pv, axis=-1, keepdims=True) + RMS_EPS)
        o_ref[...] = x_ref[...] + gt_ref[...] * ((pv * r) * gp_ref[...])

    return pl.pallas_call(
        body, name=name, grid=(S // T,), in_specs=[row, row, vec, vec], out_specs=row,
        out_shape=jax.ShapeDtypeStruct((S, D), F32), compiler_params=_params(),
    )(x, p, gp, gt)


def _post_res_bwd(p, gp, gt, g, name):
    S, D = p.shape
    T = _row_tile(S)
    row = pl.BlockSpec((T, D), lambda i: (i, 0))
    vec = pl.BlockSpec((1, D), lambda i: (0, 0))

    def body(p_ref, gp_ref, gt_ref, g_ref, dp_ref, dgp_ref, dgt_ref):
        pv = p_ref[...]
        gv = g_ref[...]
        r = lax.rsqrt(jnp.mean(pv * pv, axis=-1, keepdims=True) + RMS_EPS)
        pn = pv * r
        n2 = pn * gp_ref[...]
        dn2 = gv * gt_ref[...]
        dpn = dn2 * gp_ref[...]
        dp = r * (dpn - pn * jnp.mean(dpn * pn, axis=-1, keepdims=True))
        dp_ref[...] = dp.astype(BF16)

        @pl.when(pl.program_id(0) == 0)
        def _():
            dgp_ref[...] = jnp.zeros_like(dgp_ref)
            dgt_ref[...] = jnp.zeros_like(dgt_ref)

        dgp_ref[...] += jnp.sum(dn2 * pn, axis=0, keepdims=True)
        dgt_ref[...] += jnp.sum(gv * n2, axis=0, keepdims=True)

    vshape = jax.ShapeDtypeStruct((1, D), F32)
    return pl.pallas_call(
        body, name=name, grid=(S // T,), in_specs=[row, vec, vec, row], out_specs=[row, vec, vec],
        out_shape=[jax.ShapeDtypeStruct((S, D), BF16), vshape, vshape],
        compiler_params=_params(dimension_semantics=("arbitrary",)),
    )(p, gp, gt, g)


def _loss_head(y, target, name):
    S, D = y.shape
    T = _row_tile(S)
    row = pl.BlockSpec((T, D), lambda i: (i, 0))
    vec = pl.BlockSpec((1, D), lambda i: (0, 0))

    def body(y_ref, t_ref, dy_ref, sq_ref):
        e = y_ref[...] - t_ref[...]
        dy_ref[...] = e * (1.0 / D)

        @pl.when(pl.program_id(0) == 0)
        def _():
            sq_ref[...] = jnp.zeros_like(sq_ref)

        sq_ref[...] += jnp.sum(e * e, axis=0, keepdims=True)

    return pl.pallas_call(
        body, name=name, grid=(S // T,), in_specs=[row, row], out_specs=[row, vec],
        out_shape=[jax.ShapeDtypeStruct((S, D), F32), jax.ShapeDtypeStruct((1, D), F32)],
        compiler_params=_params(dimension_semantics=("arbitrary",)),
    )(y, target)


def _shift_down(v, k, rows):
    return jnp.where(rows >= k, pltpu.roll(v, k, 0), 0.0)


def _shift_up(v, k, rows, S):
    return jnp.where(rows < S - k, pltpu.roll(v, S - k, 0), 0.0)


def _conv_gate(gate_all, up_all, wc, bc, name):
    S, F = gate_all.shape
    C = LANES
    seq = pl.BlockSpec((S, C), lambda j: (0, j))

    def body(g_ref, u_ref, w_ref, b_ref, a_ref):
        rows = lax.broadcasted_iota(jnp.int32, (S, C), 0)
        gate = g_ref[...]
        w = w_ref[...]
        gc = w[2:3] * gate + w[1:2] * _shift_down(gate, 1, rows) + w[0:1] * _shift_down(gate, 2, rows) + b_ref[...]
        a_ref[...] = (gc * (1.0 / (1.0 + jnp.exp(-gc))) * u_ref[...]).astype(BF16)

    return pl.pallas_call(
        body, name=name, grid=(F // C,),
        in_specs=[seq, seq, pl.BlockSpec((3, C), lambda j: (0, j)), pl.BlockSpec((1, C), lambda j: (0, j))],
        out_specs=seq, out_shape=jax.ShapeDtypeStruct((S, F), BF16), compiler_params=_params(),
    )(gate_all, up_all, wc, bc)


def _conv_gate_bwd(gate_all, up_all, wc, bc, da, name):
    S, F = gate_all.shape
    C = LANES
    seq = pl.BlockSpec((S, C), lambda j: (0, j))

    def body(g_ref, u_ref, w_ref, b_ref, da_ref, dg_ref, du_ref, dw_ref, db_ref):
        rows = lax.broadcasted_iota(jnp.int32, (S, C), 0)
        gate = g_ref[...]
        up = u_ref[...]
        dav = da_ref[...]
        w = w_ref[...]
        g1 = _shift_down(gate, 1, rows)
        g2 = _shift_down(gate, 2, rows)
        gc = w[2:3] * gate + w[1:2] * g1 + w[0:1] * g2 + b_ref[...]
        sg = 1.0 / (1.0 + jnp.exp(-gc))
        du_ref[...] = (dav * (gc * sg)).astype(BF16)
        dgc = dav * up * (sg * (1.0 + gc * (1.0 - sg)))
        db_ref[...] = jnp.sum(dgc, axis=0, keepdims=True)
        dw_ref[0:1, :] = jnp.sum(dgc * g2, axis=0, keepdims=True)
        dw_ref[1:2, :] = jnp.sum(dgc * g1, axis=0, keepdims=True)
        dw_ref[2:3, :] = jnp.sum(dgc * gate, axis=0, keepdims=True)
        dgate = w[2:3] * dgc + w[1:2] * _shift_up(dgc, 1, rows, S) + w[0:1] * _shift_up(dgc, 2, rows, S)
        dg_ref[...] = dgate.astype(BF16)

    return pl.pallas_call(
        body, name=name, grid=(F // C,),
        in_specs=[seq, seq, pl.BlockSpec((3, C), lambda j: (0, j)), pl.BlockSpec((1, C), lambda j: (0, j)), seq],
        out_specs=[seq, seq, pl.BlockSpec((3, C), lambda j: (0, j)), pl.BlockSpec((1, C), lambda j: (0, j))],
        out_shape=[jax.ShapeDtypeStruct((S, F), BF16), jax.ShapeDtypeStruct((S, F), BF16),
                   jax.ShapeDtypeStruct((3, F), F32), jax.ShapeDtypeStruct((1, F), F32)],
        compiler_params=_params(),
    )(gate_all, up_all, wc, bc, da)


ATT_SCALE = HEAD_DIM ** -0.5


def _att_specs(S, D, TQ):
    nb = D // LANES
    q_spec = pl.BlockSpec((TQ, LANES), lambda p, i: (i, p))
    k_spec = pl.BlockSpec((S, LANES), lambda p, i: (0, nb + p))
    v_spec = pl.BlockSpec((S, LANES), lambda p, i: (0, 2 * nb + p))
    stat_spec = pl.BlockSpec((TQ, 2 * ATT_BLOCK), lambda p, i: (i, p))
    seq_spec = pl.BlockSpec((S, LANES), lambda p, i: (0, p))
    ck_spec = pl.BlockSpec((2, S // ATT_BLOCK, SUBLANES, ATT_BLOCK), lambda p, i: (p, 0, 0, 0))
    return q_spec, k_spec, v_spec, stat_spec, seq_spec, ck_spec


def _att_call(body, name, grid, in_specs, out_specs, out_shape, scratch, args, side, semantics):
    n_out = len(out_shape)
    if side is not None:
        in_specs = in_specs + [ANY] * len(side.inputs)
        out_specs = out_specs + [ANY] * len(side.out_shapes)
        out_shape = out_shape + side.out_shapes
        scratch = scratch + side.sems
        args = args + side.inputs
    res = pl.pallas_call(
        body, name=name, grid=grid, in_specs=in_specs, out_specs=out_specs, out_shape=out_shape,
        scratch_shapes=scratch, compiler_params=_params(dimension_semantics=semantics),
    )(*args)
    return res[:n_out], res[n_out:]


def _grid_ends(NP, NQ):
    p, qi = pl.program_id(0), pl.program_id(1)
    return jnp.logical_and(p == 0, qi == 0), jnp.logical_and(p == NP - 1, qi == NQ - 1)


def _head_lanes(hh):
    return slice(hh * HEAD_DIM, (hh + 1) * HEAD_DIM)


def _below(old, lo, new, axis=0):
    if lo == 0:
        return new
    keep = old[:lo] if axis == 0 else old[:, :lo]
    return jnp.concatenate([keep, new], axis=axis)


def _scaled_q(q_ref, hh):
    return (q_ref[:, _head_lanes(hh)].astype(F32) * ATT_SCALE).astype(BF16)


def _tri(cmp):
    B = ATT_BLOCK
    row = lax.broadcasted_iota(jnp.int32, (B, B), 0)
    col = lax.broadcasted_iota(jnp.int32, (B, B), 1)
    half = jnp.concatenate([jnp.where(cmp(row, col), 1.0, 0.0).astype(BF16), jnp.ones((B, B), BF16)], axis=1)
    return jnp.concatenate([half, half], axis=0)


def _hi_lo_dot(x, t):
    hi = x.astype(BF16)
    lo = (x - hi.astype(F32)).astype(BF16)
    return _dot(jnp.concatenate([hi, lo], axis=1), t)


def _key_minus_query(j, qi, TQ):
    row = lax.broadcasted_iota(jnp.int32, (TQ, ATT_BLOCK), 0)
    col = lax.broadcasted_iota(jnp.int32, (TQ, ATT_BLOCK), 1)
    return col - row + (j * ATT_BLOCK - qi * TQ)


def _log_sigmoids(z):
    sp = jnp.log(1.0 + jnp.exp(-jnp.abs(z)))
    lb = jnp.minimum(z, 0.0) - sp
    return lb, lb - z


def _log_sigmoids_fast(z):
    zc = jnp.maximum(z, -80.0)
    lb = -jnp.log(1.0 + jnp.exp(-zc))
    return lb, lb - zc


def _sb_fwd(qkv, name, side=None):
    S, D3 = qkv.shape
    D = D3 // 3
    B, TQ = ATT_BLOCK, SB_Q_TILE
    R = TQ // B
    NP = D // LANES
    NQ = S // TQ
    q_spec, k_spec, v_spec, stat_spec, _, _ = _att_specs(S, D, TQ)

    def body(*refs):
        (q_ref, k_ref, v_ref), (o_ref, lt_ref, first_ref), _, parts = _side_parts(side, refs, 3, 3)
        begin, end = _side_hooks(side, parts, *_grid_ends(NP, NQ))
        begin()
        qi = pl.program_id(1)
        t_suffix = _tri(lambda r, c: r > c)
        qs = [_scaled_q(q_ref, hh) for hh in range(2)]

        def tile(j, carry, masked, lo=0):
            r0 = pl.multiple_of(j * B, B)
            if masked:
                strict = _key_minus_query(j, qi, TQ)[lo:] < 0
            out = []
            for hh in range(2):
                c, acc = carry[hh]
                k = k_ref[pl.ds(r0, B), _head_lanes(hh)]
                v = v_ref[pl.ds(r0, B), _head_lanes(hh)]
                lb, l1 = _log_sigmoids_fast(_dot_nt(qs[hh][lo:], k))
                if masked:
                    l1 = jnp.where(strict, l1, 0.0)
                sums = _hi_lo_dot(l1, t_suffix)
                a = jnp.exp(lb + c[lo:] + sums[:, :B])
                if masked:
                    a = jnp.where(strict, a, 0.0)
                out.append((_below(c, lo, c[lo:] + sums[:, B:]), _below(acc, lo, acc[lo:] + _dot(a.astype(BF16), v))))
            return tuple(out)

        carry = (jnp.zeros((TQ, B), F32), jnp.zeros((TQ, HEAD_DIM), F32))
        carry = (carry, carry)
        for jj in reversed(range(R)):
            carry = tile(qi * R + jj, carry, True, jj * B)

        def alive(cr):
            return jnp.max(jnp.maximum(cr[0][0], cr[1][0])) > SB_DEAD_LOG

        def walk(state):
            j, cr, _ = state
            cr = tile(j, cr, False)
            return j - 1, cr, alive(cr)

        j, carry, _ = lax.while_loop(lambda st: jnp.logical_and(st[0] >= 0, st[2]), walk,
                                     (qi * R - 1, carry, alive(carry)))
        first_ref[pl.program_id(0), qi] = (j + 1).astype(F32)
        for hh in range(2):
            c, acc = carry[hh]
            o_ref[:, _head_lanes(hh)] = acc
            lt_ref[:, hh * B:(hh + 1) * B] = c
        end()

    return _att_call(
        body, name, (NP, NQ), [q_spec, k_spec, v_spec],
        [q_spec, stat_spec, pl.BlockSpec(memory_space=pltpu.SMEM)],
        [jax.ShapeDtypeStruct((S, D), F32), jax.ShapeDtypeStruct((S, 2 * NP * B), F32),
         jax.ShapeDtypeStruct((NP, NQ), F32)], [], [qkv, qkv, qkv], side, ("arbitrary", "arbitrary"))


def _sb_bwd(qkv, do, lt, first, name, side=None):
    S, D3 = qkv.shape
    D = D3 // 3
    B, TQ = ATT_BLOCK, SB_Q_TILE
    R = TQ // B
    NP = D // LANES
    NQ = S // TQ
    q_spec, k_spec, v_spec, stat_spec, seq_spec, _ = _att_specs(S, D, TQ)

    def body(*refs):
        ins, (dq_ref, dk_ref, dv_ref), (dk_acc, dv_acc), parts = _side_parts(side, refs, 6, 3)
        first_ref, q_ref, k_ref, v_ref, do_ref, lt_ref = ins
        begin, end = _side_hooks(side, parts, *_grid_ends(NP, NQ))
        begin()
        qi = pl.program_id(1)
        t_prefix = _tri(lambda r, c: r <= c)
        t_before = _tri(lambda r, c: r < c)

        @pl.when(qi == 0)
        def _():
            dk_acc[...] = jnp.zeros_like(dk_acc)
            dv_acc[...] = jnp.zeros_like(dv_acc)

        qs = [_scaled_q(q_ref, hh) for hh in range(2)]
        dob = [do_ref[:, _head_lanes(hh)].astype(BF16) for hh in range(2)]
        ltot = [lt_ref[:, hh * B:(hh + 1) * B] for hh in range(2)]

        def tile(j, carry, masked, lo=0):
            r0 = pl.multiple_of(j * B, B)
            if masked:
                strict = _key_minus_query(j, qi, TQ)[lo:] < 0
            out = []
            for hh in range(2):
                pre, cu, dq = carry[hh]
                lanes = _head_lanes(hh)
                k = k_ref[pl.ds(r0, B), lanes]
                v = v_ref[pl.ds(r0, B), lanes]
                lb, l1 = _log_sigmoids_fast(_dot_nt(qs[hh][lo:], k))
                if masked:
                    l1 = jnp.where(strict, l1, 0.0)
                sums = _hi_lo_dot(l1, t_prefix)
                a = jnp.exp(lb + (ltot[hh][lo:] - pre[lo:] - sums[:, :B]))
                if masked:
                    a = jnp.where(strict, a, 0.0)
                u = a * _dot_nt(dob[hh][lo:], v)
                usums = _hi_lo_dot(u, t_before)
                dz = u - (u + cu[lo:] + usums[:, :B]) * jnp.exp(lb)
                if masked:
                    dz = jnp.where(strict, dz, 0.0)
                dzb = dz.astype(BF16)
                dk_acc[pl.ds(r0, B), lanes] += _dot_tn(dzb, qs[hh][lo:])
                dv_acc[pl.ds(r0, B), lanes] += _dot_tn(a.astype(BF16), dob[hh][lo:])
                out.append((_below(pre, lo, pre[lo:] + sums[:, B:]), _below(cu, lo, cu[lo:] + usums[:, B:]),
                            _below(dq, lo, dq[lo:] + _dot(dzb, k))))
            return tuple(out)

        zero = (jnp.zeros((TQ, B), F32), jnp.zeros((TQ, B), F32), jnp.zeros((TQ, HEAD_DIM), F32))
        first = jnp.clip(first_ref[pl.program_id(0), qi].astype(jnp.int32), 0, qi * R)
        carry = lax.fori_loop(first, qi * R, lambda j, cr: tile(j, cr, False), (zero, zero))
        for jj in range(R):
            carry = tile(qi * R + jj, carry, True, jj * B)
        for hh in range(2):
            dq_ref[:, _head_lanes(hh)] = (carry[hh][2] * ATT_SCALE).astype(BF16)

        @pl.when(qi == NQ - 1)
        def _():
            dk_ref[...] = dk_acc[...].astype(BF16)
            dv_ref[...] = dv_acc[...].astype(BF16)

        end()

    out = jax.ShapeDtypeStruct((S, D), BF16)
    return _att_call(
        body, name, (NP, NQ), [pl.BlockSpec(memory_space=pltpu.SMEM), q_spec, k_spec, v_spec, q_spec, stat_spec],
        [q_spec, seq_spec, seq_spec], [out, out, out],
        [pltpu.VMEM((S, LANES), F32), pltpu.VMEM((S, LANES), F32)], [first, qkv, qkv, qkv, do, lt], side,
        ("arbitrary", "arbitrary"))


def _fox_specs(S, D, TQ):
    row_spec = pl.BlockSpec((2 * SUBLANES, TQ), lambda p, i: (p, i))
    keyb_spec = pl.BlockSpec((S, 2 * ATT_BLOCK), lambda p, i: (0, p))
    return row_spec, keyb_spec


def _key_gt_query_t(j, qi, TQ):
    key = lax.broadcasted_iota(jnp.int32, (ATT_BLOCK, TQ), 0)
    qry = lax.broadcasted_iota(jnp.int32, (ATT_BLOCK, TQ), 1)
    return key - qry + (j * ATT_BLOCK - qi * TQ) > 0


def _fox_fwd(qkv, cq_rows, ck_b, name, side=None):
    S, D3 = qkv.shape
    D = D3 // 3
    B, TQ = ATT_BLOCK, FOX_Q_TILE
    R = TQ // B
    NP = D // LANES
    NQ = S // TQ
    q_spec, k_spec, v_spec, _, _, _ = _att_specs(S, D, TQ)
    row_spec, keyb_spec = _fox_specs(S, D, TQ)

    def body(*refs):
        (q_ref, k_ref, v_ref, cq_ref, ck_ref), (o_ref, lse_ref), _, parts = _side_parts(side, refs, 5, 2)
        begin, end = _side_hooks(side, parts, *_grid_ends(NP, NQ))
        begin()
        qi = pl.program_id(1)
        qs = [_scaled_q(q_ref, hh) for hh in range(2)]
        cq = [cq_ref[hh * SUBLANES:hh * SUBLANES + 1, :] for hh in range(2)]

        def tile(j, carry, masked, lo=0):
            r0 = pl.multiple_of(j * B, B)
            if masked:
                hidden = _key_gt_query_t(j, qi, TQ)[:, lo:]
            out = []
            for hh in range(2):
                m, lsum, acc = carry[hh]
                k = k_ref[pl.ds(r0, B), _head_lanes(hh)]
                v = v_ref[pl.ds(r0, B), _head_lanes(hh)]
                ck = jnp.tile(ck_ref[pl.ds(r0, B), hh * B:(hh + 1) * B], (1, (TQ - lo) // B))
                s = _dot_nt(k, qs[hh][lo:]) + (cq[hh][:, lo:] - ck)
                if masked:
                    s = jnp.where(hidden, NEG_BIG, s)
                m_new = jnp.maximum(m[:, lo:], jnp.max(s, axis=0, keepdims=True))
                p = jnp.exp(s - m_new)
                alpha = jnp.exp(m[:, lo:] - m_new)
                out.append((_below(m, lo, m_new, 1),
                            _below(lsum, lo, alpha * lsum[:, lo:] + jnp.sum(p, axis=0, keepdims=True), 1),
                            _below(acc, lo, alpha * acc[:, lo:] + _dot_tn(v, p.astype(BF16)), 1)))
            return tuple(out)

        zero = (jnp.full((1, TQ), NEG_BIG, F32), jnp.zeros((1, TQ), F32), jnp.zeros((HEAD_DIM, TQ), F32))
        carry = lax.fori_loop(0, qi * (R // 2), lambda t, cr: tile(2 * t + 1, tile(2 * t, cr, False), False),
                              (zero, zero))
        for jj in range(R):
            carry = tile(qi * R + jj, carry, True, jj * B)
        o_t = jnp.concatenate([carry[hh][2] * (1.0 / carry[hh][1]) for hh in range(2)], axis=0)
        o_ref[...] = o_t.T
        for hh in range(2):
            m, lsum, _ = carry[hh]
            lse_ref[hh * SUBLANES:(hh + 1) * SUBLANES, :] = jnp.broadcast_to(m + jnp.log(lsum), (SUBLANES, TQ))
        end()

    return _att_call(
        body, name, (NP, NQ), [q_spec, k_spec, v_spec, row_spec, keyb_spec], [q_spec, row_spec],
        [jax.ShapeDtypeStruct((S, D), F32), jax.ShapeDtypeStruct((2 * NP * SUBLANES, S), F32)], [],
        [qkv, qkv, qkv, cq_rows, ck_b], side, ("arbitrary", "arbitrary"))


def _fox_bwd(qkv, do, o, lse_rows, cq_rows, ck_b, name, side=None):
    S, D3 = qkv.shape
    D = D3 // 3
    B, TQ = ATT_BLOCK, FOX_Q_TILE
    R = TQ // B
    NP = D // LANES
    NQ = S // TQ
    q_spec, k_spec, v_spec, _, seq_spec, _ = _att_specs(S, D, TQ)
    row_spec, keyb_spec = _fox_specs(S, D, TQ)

    def body(*refs):
        ins, outs, (dk_acc, dv_acc), parts = _side_parts(side, refs, 8, 5)
        q_ref, k_ref, v_ref, do_ref, o_ref, lse_ref, cq_ref, ck_ref = ins
        dq_ref, dk_ref, dv_ref, dcq_ref, dck_ref = outs
        begin, end = _side_hooks(side, parts, *_grid_ends(NP, NQ))
        begin()
        qi = pl.program_id(1)

        @pl.when(qi == 0)
        def _():
            dk_acc[...] = jnp.zeros_like(dk_acc)
            dv_acc[...] = jnp.zeros_like(dv_acc)
            dck_ref[...] = jnp.zeros_like(dck_ref)

        qs = [_scaled_q(q_ref, hh) for hh in range(2)]
        dob = [do_ref[:, _head_lanes(hh)].astype(BF16) for hh in range(2)]
        prod_t = (do_ref[...] * o_ref[...]).T
        delta = [jnp.sum(prod_t[hh * HEAD_DIM:(hh + 1) * HEAD_DIM], axis=0, keepdims=True) for hh in range(2)]
        cq = [cq_ref[hh * SUBLANES:hh * SUBLANES + 1, :] for hh in range(2)]
        lse = [lse_ref[hh * SUBLANES:hh * SUBLANES + 1, :] for hh in range(2)]

        def tile(j, carry, masked, lo=0):
            r0 = pl.multiple_of(j * B, B)
            if masked:
                hidden = _key_gt_query_t(j, qi, TQ)[:, lo:]
            out = []
            for hh in range(2):
                dq, keysum = carry[hh]
                lanes = _head_lanes(hh)
                k = k_ref[pl.ds(r0, B), lanes]
                v = v_ref[pl.ds(r0, B), lanes]
                ck = jnp.tile(ck_ref[pl.ds(r0, B), hh * B:(hh + 1) * B], (1, (TQ - lo) // B))
                p = jnp.exp(_dot_nt(k, qs[hh][lo:]) + (cq[hh][:, lo:] - ck) - lse[hh][:, lo:])
                if masked:
                    p = jnp.where(hidden, 0.0, p)
                ds = p * (_dot_nt(v, dob[hh][lo:]) - delta[hh][:, lo:])
                dsb = ds.astype(BF16)
                dk_acc[pl.ds(r0, B), lanes] += _dot(dsb, qs[hh][lo:])
                dv_acc[pl.ds(r0, B), lanes] += _dot(p.astype(BF16), dob[hh][lo:])
                qsum = jnp.sum(ds, axis=1, keepdims=True)
                dck_ref[pl.ds(r0, B), hh * B:(hh + 1) * B] -= jnp.broadcast_to(qsum, (B, B))
                out.append((_below(dq, lo, dq[:, lo:] + _dot_tn(k, dsb), 1),
                            _below(keysum, lo, keysum[:, lo:] + jnp.sum(ds, axis=0, keepdims=True), 1)))
            return tuple(out)

        zero = (jnp.zeros((HEAD_DIM, TQ), F32), jnp.zeros((1, TQ), F32))
        carry = lax.fori_loop(0, qi * (R // 2), lambda t, cr: tile(2 * t + 1, tile(2 * t, cr, False), False),
                              (zero, zero))
        for jj in range(R):
            carry = tile(qi * R + jj, carry, True, jj * B)
        dq_t = jnp.concatenate([carry[hh][0] for hh in range(2)], axis=0)
        dq_ref[...] = (dq_t.T * ATT_SCALE).astype(BF16)
        for hh in range(2):
            dcq_ref[hh * SUBLANES:(hh + 1) * SUBLANES, :] = jnp.broadcast_to(carry[hh][1], (SUBLANES, TQ))

        @pl.when(qi == NQ - 1)
        def _():
            dk_ref[...] = dk_acc[...].astype(BF16)
            dv_ref[...] = dv_acc[...].astype(BF16)

        end()

    out = jax.ShapeDtypeStruct((S, D), BF16)
    return _att_call(
        body, name, (NP, NQ), [q_spec, k_spec, v_spec, q_spec, q_spec, row_spec, row_spec, keyb_spec],
        [q_spec, seq_spec, seq_spec, row_spec, keyb_spec],
        [out, out, out, jax.ShapeDtypeStruct((2 * NP * SUBLANES, S), F32), jax.ShapeDtypeStruct((S, 2 * NP * B), F32)],
        [pltpu.VMEM((S, LANES), F32), pltpu.VMEM((S, LANES), F32)],
        [qkv, qkv, qkv, do, o, lse_rows, cq_rows, ck_b], side, ("arbitrary", "arbitrary"))


def _forget_cumsum(f3, bias, name):
    NQ, NH, B = f3.shape

    def body(f_ref, b_ref, cum_ref):
        row = lax.broadcasted_iota(jnp.int32, (B, B), 0)
        col = lax.broadcasted_iota(jnp.int32, (B, B), 1)
        t_incl = jnp.where(row <= col, 1.0, 0.0).astype(BF16)

        def step(b, carry):
            lf, _ = _log_sigmoids(f_ref[b] + b_ref[...])
            cum = _split_dot(lf, t_incl, 3) + carry
            cum_ref[b] = cum
            return jnp.broadcast_to(cum[:, B - 1:B], (NH, B))

        lax.fori_loop(0, NQ, step, jnp.zeros((NH, B), F32))

    return pl.pallas_call(
        body, name=name, in_specs=[VMEM_WHOLE, VMEM_WHOLE], out_specs=VMEM_WHOLE,
        out_shape=jax.ShapeDtypeStruct((NQ, NH, B), F32), compiler_params=_params(),
    )(f3, bias)


def _forget_cumsum_bwd(dcum3, f3, bias, name):
    NQ, NH, B = f3.shape

    def body(d_ref, f_ref, b_ref, df_ref, tot_ref):
        row = lax.broadcasted_iota(jnp.int32, (B, B), 0)
        col = lax.broadcasted_iota(jnp.int32, (B, B), 1)
        t_rev = jnp.where(row >= col, 1.0, 0.0).astype(BF16)

        def step(it, carry):
            run, tot = carry
            b = NQ - 1 - it
            dlf = _split_dot(d_ref[b], t_rev, 3) + run
            f = f_ref[b] + b_ref[...]
            df = dlf * (1.0 / (1.0 + jnp.exp(f)))
            df_ref[b] = df
            return jnp.broadcast_to(dlf[:, 0:1], (NH, B)), tot + df

        _, tot = lax.fori_loop(0, NQ, step, (jnp.zeros((NH, B), F32), jnp.zeros((NH, B), F32)))
        tot_ref[...] = _split_dot(tot, jnp.ones((B, B), BF16), 3)

    return pl.pallas_call(
        body, name=name, in_specs=[VMEM_WHOLE, VMEM_WHOLE, VMEM_WHOLE], out_specs=[VMEM_WHOLE, VMEM_WHOLE],
        out_shape=[jax.ShapeDtypeStruct((NQ, NH, B), F32), jax.ShapeDtypeStruct((NH, B), F32)],
        compiler_params=_params(),
    )(dcum3, f3, bias)


def _silu(x, name):
    def body(x_ref, o_ref):
        v = x_ref[...]
        o_ref[...] = v * (1.0 / (1.0 + jnp.exp(-v)))

    return pl.pallas_call(body, name=name, in_specs=[VMEM_WHOLE], out_specs=VMEM_WHOLE,
                          out_shape=jax.ShapeDtypeStruct(x.shape, F32), compiler_params=_params())(x)


def _sum_leading(x, name):
    N, R, C = x.shape
    T = _pick(R, (256, 128, 64, 32, 16, 8))

    def body(x_ref, o_ref):
        acc = x_ref[0].astype(F32)
        for r in range(1, N):
            acc = acc + x_ref[r].astype(F32)
        o_ref[...] = acc

    return pl.pallas_call(
        body, name=name, grid=(R // T,), in_specs=[pl.BlockSpec((N, T, C), lambda i: (0, i, 0))],
        out_specs=pl.BlockSpec((T, C), lambda i: (i, 0)), out_shape=jax.ShapeDtypeStruct((R, C), F32),
        compiler_params=_params(),
    )(x)


def _adamw(w, g, m, v, name):
    shape = w.shape
    C = shape[-1]
    R = w.size // C
    T = R
    for cand in (512, 256, 128, 64, 32, 16, 8):
        if R % cand == 0 and cand * C * 4 <= (1 << 20):
            T = cand
            break
    spec = pl.BlockSpec((T, C), lambda i: (i, 0))
    c1 = 1.0 / (1.0 - ADAM_B1 ** ADAM_STEP)
    c2 = 1.0 / (1.0 - ADAM_B2 ** ADAM_STEP)

    def body(w_ref, g_ref, m_ref, v_ref, d_ref, nm_ref, nv_ref):
        gv = g_ref[...]
        nm = ADAM_B1 * m_ref[...] + (1.0 - ADAM_B1) * gv
        nv = ADAM_B2 * v_ref[...] + (1.0 - ADAM_B2) * (gv * gv)
        nm_ref[...] = nm
        nv_ref[...] = nv
        d_ref[...] = -ADAM_LR * ((nm * c1) / (jnp.sqrt(nv * c2) + ADAM_EPS) + ADAM_WD * w_ref[...])

    out = jax.ShapeDtypeStruct((R, C), F32)
    d, nm, nv = pl.pallas_call(
        body, name=name, grid=(R // T,), in_specs=[spec] * 4, out_specs=[spec] * 3, out_shape=[out] * 3,
        compiler_params=_params(),
    )(w.reshape(R, C), g.reshape(R, C), m.reshape(R, C), v.reshape(R, C))
    return d.reshape(shape), nm.reshape(shape), nv.reshape(shape)


def _mesh_pos():
    return lax.axis_index("x"), lax.axis_index("y"), lax.axis_index("c")


def _all_gather_small(x2d, name):
    m_per, n = x2d.shape

    def body(x_ref, out_ref, send_sems, recv_sems, local_sem):
        x, y, c = _mesh_pos()
        me, sibling = (x, y, c), (x, y, 1 - c)
        chips = [(1 - x, y), (x, 1 - y), (1 - x, 1 - y)]

        def rows(px, py, pc):
            return out_ref.at[pl.ds((4 * px + 2 * py + pc) * m_per, m_per), :]

        def copy(k, block, to, src=None):
            return pltpu.make_async_remote_copy(
                src_ref=rows(*block) if src is None else src, dst_ref=rows(*block),
                send_sem=send_sems.at[k], recv_sem=recv_sems.at[k], device_id=to, device_id_type=MESH)

        mine = pltpu.make_async_copy(x_ref, rows(*me), local_sem)
        mine.start()
        first = [copy(0, me, sibling, src=x_ref)]
        first += [copy(1 + j, me, (*chip, c), src=x_ref) for j, chip in enumerate(chips)]
        for cp in first:
            cp.start()
        passed = [copy(4 + j, (*chip, c), sibling) for j, chip in enumerate(chips)]
        for j, chip in enumerate(chips):
            copy(1 + j, (*chip, c), me).wait_recv()
            passed[j].start()
        copy(0, sibling, me).wait_recv()
        for j, chip in enumerate(chips):
            copy(4 + j, (*chip, 1 - c), me).wait_recv()
        for cp in first + passed:
            cp.wait_send()
        mine.wait()

    return pl.pallas_call(
        body, name=name, out_shape=jax.ShapeDtypeStruct((N_DEV * m_per, n), x2d.dtype),
        in_specs=[VMEM_WHOLE], out_specs=VMEM_WHOLE,
        scratch_shapes=[pltpu.SemaphoreType.DMA((7,)), pltpu.SemaphoreType.DMA((7,)), pltpu.SemaphoreType.DMA],
        compiler_params=_params(),
    )(x2d)


class _Side:
    def __init__(self, inputs, out_shapes, sems, start, wait):
        self.inputs, self.out_shapes, self.sems, self.start, self.wait = inputs, out_shapes, sems, start, wait


def _join_sides(sides):
    sides = [s for s in sides if s is not None]
    if not sides:
        return None
    bounds_in, bounds_out, bounds_sem = [0], [0], [0]
    for s in sides:
        bounds_in.append(bounds_in[-1] + len(s.inputs))
        bounds_out.append(bounds_out[-1] + len(s.out_shapes))
        bounds_sem.append(bounds_sem[-1] + len(s.sems))

    def each(method):
        def run(ins, outs, sems):
            for t, s in enumerate(sides):
                getattr(s, method)(ins[bounds_in[t]:bounds_in[t + 1]], outs[bounds_out[t]:bounds_out[t + 1]],
                                   sems[bounds_sem[t]:bounds_sem[t + 1]])
        return run

    return _Side([a for s in sides for a in s.inputs], [o for s in sides for o in s.out_shapes],
                 [m for s in sides for m in s.sems], each("start"), each("wait"))


def _side_parts(side, refs, n_in, n_out):
    if side is None:
        return refs[:n_in], refs[n_in:n_in + n_out], refs[n_in + n_out:], None
    si, so, ss = len(side.inputs), len(side.out_shapes), len(side.sems)
    a = n_in + si
    b = a + n_out + so
    ins, side_in = refs[:n_in], refs[n_in:a]
    outs, side_out = refs[a:a + n_out], refs[a + n_out:b]
    scratch, side_sems = refs[b:len(refs) - ss], refs[len(refs) - ss:]
    return ins, outs, scratch, (side_in, side_out, side_sems)


def _side_hooks(side, parts, first, last):
    if side is None:
        return lambda: None, lambda: None

    def begin():
        @pl.when(first)
        def _():
            side.start(*parts)

    def end():
        @pl.when(last)
        def _():
            side.wait(*parts)

    return begin, end


def _run_side(side, name):
    n_in = len(side.inputs)
    n_out = len(side.out_shapes)

    def body(*refs):
        parts = (refs[:n_in], refs[n_in:n_in + n_out], refs[n_in + n_out:])
        side.start(*parts)
        side.wait(*parts)

    return pl.pallas_call(
        body, name=name, out_shape=side.out_shapes, in_specs=[ANY] * n_in, out_specs=[ANY] * n_out,
        scratch_shapes=side.sems, compiler_params=_params(),
    )(*side.inputs)


def _gather_side(ws):
    n = len(ws)

    def copies(ins, outs, sems):
        send_sems, recv_sems, local_sems = sems
        x, y, c = _mesh_pos()
        k_me = 2 * x + y
        chips = [(1 - x, y), (x, 1 - y), (1 - x, 1 - y)]

        def remote(i, j, slot):
            px, py = chips[j]
            return pltpu.make_async_remote_copy(
                src_ref=ins[i], dst_ref=outs[i].at[slot], send_sem=send_sems.at[i, j],
                recv_sem=recv_sems.at[i, j], device_id=(px, py, c), device_id_type=MESH)

        local = [pltpu.make_async_copy(ins[i], outs[i].at[k_me], local_sems.at[i]) for i in range(n)]
        return remote, local, k_me, chips

    def start(ins, outs, sems):
        remote, local, k_me, _ = copies(ins, outs, sems)
        for i in range(n):
            local[i].start()
            for j in range(3):
                remote(i, j, k_me).start()

    def wait(ins, outs, sems):
        remote, local, k_me, chips = copies(ins, outs, sems)
        for i in range(n):
            for j, (px, py) in enumerate(chips):
                remote(i, j, 2 * px + py).wait_recv()
        for i in range(n):
            for j in range(3):
                remote(i, j, k_me).wait_send()
            local[i].wait()

    return _Side(list(ws), [jax.ShapeDtypeStruct((N_CHIPS,) + w.shape, w.dtype) for w in ws],
                 [pltpu.SemaphoreType.DMA((n, 3)), pltpu.SemaphoreType.DMA((n, 3)), pltpu.SemaphoreType.DMA((n,))],
                 start, wait)


def _scatter_side(gs, whole_shard=False):
    n = len(gs)
    halves = [g.shape[1] if whole_shard else g.shape[1] // 2 for g in gs]

    def copies(ins, outs, sems):
        send_sems, recv_sems, local_sems = sems
        x, y, c = _mesh_pos()

        def flip(v, bit):
            return 1 - v if bit else v

        def piece(i, px, py, pc):
            if whole_shard:
                return ins[i].at[2 * px + py]
            return ins[i].at[2 * px + py, pl.ds(pc * halves[i], halves[i])]

        me = 4 * x + 2 * y + c

        def remote(i, r, arriving=False):
            px, py, pc = flip(x, r & 4), flip(y, r & 2), flip(c, r & 1)
            slot = 4 * px + 2 * py + pc if arriving else me
            return pltpu.make_async_remote_copy(
                src_ref=piece(i, px, py, pc), dst_ref=outs[i].at[slot], send_sem=send_sems.at[i, r - 1],
                recv_sem=recv_sems.at[i, r - 1], device_id=(px, py, pc), device_id_type=MESH)

        local = [pltpu.make_async_copy(piece(i, x, y, c), outs[i].at[me], local_sems.at[i]) for i in range(n)]
        return remote, local

    def start(ins, outs, sems):
        remote, local = copies(ins, outs, sems)
        for i in range(n):
            local[i].start()
            for r in range(1, N_DEV):
                remote(i, r).start()

    def wait(ins, outs, sems):
        remote, local = copies(ins, outs, sems)
        for i in range(n):
            for r in range(1, N_DEV):
                remote(i, r, arriving=True).wait_recv()
        for i in range(n):
            for r in range(1, N_DEV):
                remote(i, r).wait_send()
            local[i].wait()

    return _Side(list(gs), [jax.ShapeDtypeStruct((N_DEV, h) + g.shape[2:], g.dtype) for g, h in zip(gs, halves)],
                 [pltpu.SemaphoreType.DMA((n, N_DEV - 1)), pltpu.SemaphoreType.DMA((n, N_DEV - 1)),
                  pltpu.SemaphoreType.DMA((n,))], start, wait)


def _swap_side(hs):
    n = len(hs)

    def copies(ins, outs, sems):
        send_sems, recv_sems, local_sems = sems
        x, y, c = _mesh_pos()

        def remote(i, slot):
            return pltpu.make_async_remote_copy(
                src_ref=ins[i], dst_ref=outs[i].at[slot], send_sem=send_sems.at[i], recv_sem=recv_sems.at[i],
                device_id=(x, y, 1 - c), device_id_type=MESH)

        local = [pltpu.make_async_copy(ins[i], outs[i].at[c], local_sems.at[i]) for i in range(n)]
        return remote, local, c

    def start(ins, outs, sems):
        remote, local, c = copies(ins, outs, sems)
        for i in range(n):
            local[i].start()
            remote(i, c).start()

    def wait(ins, outs, sems):
        remote, local, c = copies(ins, outs, sems)
        for i in range(n):
            remote(i, 1 - c).wait_recv()
        for i in range(n):
            remote(i, c).wait_send()
            local[i].wait()

    return _Side(list(hs), [jax.ShapeDtypeStruct((2,) + h.shape, h.dtype) for h in hs],
                 [pltpu.SemaphoreType.DMA((n,)), pltpu.SemaphoreType.DMA((n,)), pltpu.SemaphoreType.DMA((n,))],
                 start, wait)


def _pad_rows(a, rows):
    return jnp.pad(a, ((0, rows - a.shape[0]), (0, 0)))


def kernel(x, c, w_mod, b_mod, g_mix_pre, g_mix_post, w_qkv, w_o, w_fg, b_fg, g_ffn_pre, g_ffn_post, w_ffn_gate, w_ffn_up, w_conv, b_conv, w_ffn_down, loss_target, m_w_mod, m_b_mod, m_g_mix_pre, m_g_mix_post, m_w_qkv, m_w_o, m_w_fg, m_b_fg, m_g_ffn_pre, m_g_ffn_post, m_w_ffn_gate, m_w_ffn_up, m_w_conv, m_b_conv, m_w_ffn_down, v_w_mod, v_b_mod, v_g_mix_pre, v_g_mix_post, v_w_qkv, v_w_o, v_w_fg, v_b_fg, v_g_ffn_pre, v_g_ffn_post, v_w_ffn_gate, v_w_ffn_up, v_w_conv, v_b_conv, v_w_ffn_down):
    xs = x[0]
    target = loss_target[0]
    S, D = xs.shape
    L = w_mod.shape[0]
    LF = w_fg.shape[0]
    MS = w_mod.shape[2]
    QS = w_qkv.shape[2]
    OS = w_o.shape[1]
    FS = w_ffn_gate.shape[2]
    F = N_CHIPS * FS
    NH = D // HEAD_DIM
    B = ATT_BLOCK
    NQ = S // B
    ax, ay, ac = _mesh_pos()
    k_me = 2 * ax + ay
    b_me = 4 * ax + 2 * ay + ac

    conv_rows = -(-(L * 3 * FS) // D)
    conv_rows = -(-conv_rows // SUBLANES) * SUBLANES
    conv_flat = jnp.pad(w_conv.reshape(-1), (0, conv_rows * D - L * 3 * FS)).reshape(conv_rows, D)
    first = jnp.concatenate([_pad_rows(c, SUBLANES), conv_flat], axis=0)
    first_all = _all_gather_small(first, "ag_cond").reshape(N_DEV, SUBLANES + conv_rows, D)
    c_all = first_all[:, 0, :]
    conv_all = first_all[0::2, SUBLANES:, :].reshape(N_CHIPS, -1)[:, :L * 3 * FS]
    w_conv_full = conv_all.reshape(N_CHIPS, L, 3, FS).transpose(1, 2, 0, 3).reshape(L, 3, F)
    c_act = _silu(c_all, "silu_c")

    mod_part = jnp.concatenate(
        [_mm(c_act, w_mod[l], "nn", F32, "mm_mod", tm=N_DEV, tn=MS, tk=D) for l in range(L)], axis=1)
    mod_all = _all_gather_small(mod_part, "ag_mod").reshape(N_CHIPS, 2, N_DEV, L, MS)[:, 0]
    mod_mine = lax.dynamic_index_in_dim(mod_all, b_me, axis=1, keepdims=False)
    mod = mod_mine.transpose(1, 0, 2).reshape(L, N_MOD * D) + b_mod

    wq_b, wo_b, wg_b, wu_b, wd_b = [w.astype(BF16) for w in (w_qkv, w_o, w_ffn_gate, w_ffn_up, w_ffn_down)]
    w_fg_b = w_fg.astype(BF16)

    def mixer_shards(l):
        return [wq_b[l], wo_b[l]] + ([w_fg_b[l // 2]] if l % 2 == 1 else [])

    def mlp_shards(l):
        return [wg_b[l], wu_b[l], wd_b[l]]

    def side_by_side(gth):
        return gth.transpose(1, 0, 2).reshape(D, -1)

    def mixer_weights(gathered):
        wts = dict(qkv=side_by_side(gathered[0]), o=gathered[1].reshape(D, D))
        if len(gathered) > 2:
            wts["fg"] = jnp.pad(gathered[2].reshape(D, NH), ((0, 0), (0, LANES - NH)))
        return wts

    def mlp_weights(gathered):
        return dict(g=side_by_side(gathered[0]), u=side_by_side(gathered[1]), d=gathered[2].reshape(F, D))

    W = [None] * L
    W[0] = mixer_weights(_run_side(_gather_side(mixer_shards(0)), "ag_weights_first"))

    def vec(a):
        return a.reshape(1, -1)

    saved = []
    xcur = xs
    for l in range(L):
        sh_a, sc_a, gt_a, sh_f, sc_f, gt_f = [vec(mod[l, j * D:(j + 1) * D]) for j in range(N_MOD)]
        is_fox = l % 2 == 1
        jf = l // 2
        h1 = _norm_mod(xcur, vec(g_mix_pre[l]), sc_a, sh_a, "norm_mod")
        qkv = _mm(h1, W[l]["qkv"], "nn", BF16, "mm_qkv")
        carried_shards = mlp_shards(l) + (mixer_shards(l + 1) if l + 1 < L else [])
        next_weights = _gather_side(carried_shards)
        if is_fox:
            flog = _mm(h1, W[l]["fg"], "nn", F32, "mm_fg")[:, :NH]
            f3 = flog.reshape(NQ, B, NH).transpose(0, 2, 1)
            bias = b_fg[jf].reshape(NH, 1)
            cum3 = _forget_cumsum(f3, bias, "forget_cumsum")
            cum_sn = cum3.transpose(0, 2, 1).reshape(S, NH)
            ck_b = jnp.repeat(cum_sn, B, axis=1)
            cq_rows = jnp.repeat(cum_sn.T, SUBLANES, axis=0)
            (o, stat), gathered = _fox_fwd(qkv, cq_rows, ck_b, "fox_fwd", next_weights)
            extra = (f3, bias, cq_rows, ck_b)
        else:
            (o, stat, extra), gathered = _sb_fwd(qkv, "sb_fwd", next_weights)
        W[l].update(mlp_weights(gathered[:3]))
        if l + 1 < L:
            W[l + 1] = mixer_weights(gathered[3:])
        p = _mm(o, W[l]["o"], "nn", F32, "mm_o")
        x1 = _post_res(xcur, p, vec(g_mix_post[l]), gt_a, "post_res")
        h2 = _norm_mod(x1, vec(g_ffn_pre[l]), sc_f, sh_f, "norm_mod")
        gate = _mm(h2, W[l]["g"], "nn", F32, "mm_gate")
        up = _mm(h2, W[l]["u"], "nn", F32, "mm_up")
        wc = w_conv_full[l]
        bc = vec(b_conv[l])
        a = _conv_gate(gate, up, wc, bc, "conv_gate")
        yv = _mm(a, W[l]["d"], "nn", F32, "mm_down")
        x2 = _post_res(x1, yv, vec(g_ffn_post[l]), gt_f, "post_res")
        saved.append(dict(x0=xcur, h1=h1, qkv=qkv, o=o, stat=stat, extra=extra, p=p, x1=x1, h2=h2, gate=gate, up=up, a=a,
                          y=yv, mods=(sh_a, sc_a, gt_a, sh_f, sc_f, gt_f), wc=wc, bc=bc))
        xcur = x2

    g, sq = _loss_head(xcur, target, "loss_head")
    loss_part = 0.5 * jnp.sum(sq) / D
    loss = lax.psum(loss_part, ("x", "y", "c"))

    dW_qkv, dW_o, dW_g, dW_u, dW_d = [[None] * L for _ in range(5)]
    dW_fg, db_fg = [None] * LF, [None] * LF
    dmod, dg_mix_pre, dg_mix_post, dg_ffn_pre, dg_ffn_post = [[None] * L for _ in range(5)]
    dw_conv, db_conv = [None] * L, [None] * L

    def cols(dw, width):
        return dw.reshape(dw.shape[0], N_CHIPS, width).transpose(1, 0, 2).astype(BF16)

    def rows(dw, height):
        return dw.reshape(N_CHIPS, height, dw.shape[1]).astype(BF16)

    n_big = 5
    grad_shards = [[None] * n_big for _ in range(L)]
    to_scatter, to_swap = [], []

    def carried():
        last = lambda e: e[0] == 0 or (e[0] == 1 and e[1] == 0)
        sc = [e for e in to_scatter if not last(e)]
        sc_whole = [e for e in to_scatter if last(e)]
        sw = list(to_swap)
        del to_scatter[:], to_swap[:]
        side = _join_sides([_scatter_side([a for _, _, a in sc]) if sc else None,
                            _scatter_side([a for _, _, a in sc_whole], whole_shard=True) if sc_whole else None,
                            _swap_side([a for _, _, a in sw]) if sw else None])

        def taken(outs):
            for (lay, i, _), recv in zip(sc, outs[:len(sc)]):
                to_swap.append((lay, i, _sum_leading(recv, f"sum_grad_pieces_{i}")))
            for (lay, i, _), recv in zip(sc_whole, outs[len(sc):len(sc) + len(sc_whole)]):
                grad_shards[lay][i] = _sum_leading(recv, f"sum_grad_shard_{i}")
            for (lay, i, _), both in zip(sw, outs[len(sc) + len(sc_whole):]):
                grad_shards[lay][i] = both.reshape((-1,) + both.shape[2:])

        return side, taken

    for l in reversed(range(L)):
        sv = saved[l]
        sh_a, sc_a, gt_a, sh_f, sc_f, gt_f = sv["mods"]
        is_fox = l % 2 == 1
        jf = l // 2
        dy, dgp_f, dgt_f = _post_res_bwd(sv["y"], vec(g_ffn_post[l]), gt_f, g, "post_res_bwd")
        da = _mm(dy, W[l]["d"], "nt", F32, "mm_da")
        dW_d[l] = _mm(sv["a"], dy, "tn", F32, "mm_dwd")
        dgate, dup, dwc, dbc = _conv_gate_bwd(sv["gate"], sv["up"], sv["wc"], sv["bc"], da, "conv_gate_bwd")
        dh2 = [_mm(dgate, W[l]["g"], "nt", F32, "mm_dh2g"), _mm(dup, W[l]["u"], "nt", F32, "mm_dh2u")]
        dW_g[l] = _mm(sv["h2"], dgate, "tn", F32, "mm_dwg")
        dW_u[l] = _mm(sv["h2"], dup, "tn", F32, "mm_dwu")
        g, dg_f, dsc_f, dsh_f = _norm_mod_bwd(sv["x1"], vec(g_ffn_pre[l]), sc_f, sh_f, dh2, g, "norm_mod_bwd")
        dp, dgp_a, dgt_a = _post_res_bwd(sv["p"], vec(g_mix_post[l]), gt_a, g, "post_res_bwd")
        do = _mm(dp, W[l]["o"], "nt", F32, "mm_do")
        dW_o[l] = _mm(sv["o"], dp, "tn", F32, "mm_dwo")
        to_scatter += [(l, 1, rows(dW_o[l], OS)), (l, 2, cols(dW_g[l], FS)), (l, 3, cols(dW_u[l], FS)),
                       (l, 4, rows(dW_d[l], FS))]
        side, taken = carried()
        if is_fox:
            f3, bias, cq_rows, ck_b = sv["extra"]
            (dq, dk, dv, dcq_rows, dck_b), side_outs = _fox_bwd(sv["qkv"], do, sv["o"], sv["stat"], cq_rows, ck_b,
                                                                "fox_bwd", side)
            dcum_sn = dcq_rows[::SUBLANES].T + dck_b[:, ::B]
            dcum3 = dcum_sn.reshape(NQ, B, NH).transpose(0, 2, 1)
            df3, df_tot = _forget_cumsum_bwd(dcum3, f3, bias, "forget_cumsum_bwd")
            df = df3.transpose(0, 2, 1).reshape(S, NH)
            dfp = jnp.pad(df, ((0, 0), (0, LANES - NH)))
            dW_fg[jf] = _mm(sv["h1"], dfp, "tn", F32, "mm_dwfg")[:, :NH]
            db_fg[jf] = df_tot[:, 0]
            dh_extra = [_mm(dfp, W[l]["fg"], "nt", F32, "mm_dh1f")]
        else:
            (dq, dk, dv), side_outs = _sb_bwd(sv["qkv"], do, sv["stat"], sv["extra"], "sb_bwd", side)
            dh_extra = []
        taken(side_outs)
        dqkv = jnp.concatenate([dq, dk, dv], axis=1)
        dh1 = _mm(dqkv, W[l]["qkv"], "nt", F32, "mm_dh1")
        dW_qkv[l] = _mm(sv["h1"], dqkv, "tn", F32, "mm_dwqkv")
        to_scatter.append((l, 0, cols(dW_qkv[l], QS)))
        g, dg_a, dsc_a, dsh_a = _norm_mod_bwd(sv["x0"], vec(g_mix_pre[l]), sc_a, sh_a, [dh1] + dh_extra, g,
                                              "norm_mod_bwd")
        dmod[l] = jnp.concatenate([dsh_a, dsc_a, dgt_a, dsh_f, dsc_f, dgt_f], axis=1)[0]
        dg_mix_pre[l], dg_mix_post[l], dg_ffn_pre[l], dg_ffn_post[l] = dg_a[0], dgp_a[0], dg_f[0], dgp_f[0]
        dw_conv[l], db_conv[l] = dwc, dbc[0]
    grad_x = g[None]

    pieces = [jnp.stack(dmod), jnp.stack(dg_mix_pre), jnp.stack(dg_mix_post), jnp.stack(dg_ffn_pre),
              jnp.stack(dg_ffn_post), jnp.stack(db_fg), jnp.stack(dW_fg), jnp.stack(dw_conv), jnp.stack(db_conv)]
    sizes = [pc.size for pc in pieces]
    total = sum(sizes)
    pack_rows = -(-total // (LANES * SUBLANES)) * SUBLANES
    pack = jnp.pad(jnp.concatenate([pc.reshape(-1) for pc in pieces]), (0, pack_rows * LANES - total))
    pack_all = _all_gather_small(pack.reshape(pack_rows, LANES), "ag_small_grads").reshape(N_DEV, pack_rows, LANES)
    small = _sum_leading(pack_all, "sum_small_grads").reshape(-1)
    offs = [0]
    for sz in sizes:
        offs.append(offs[-1] + sz)
    parts = [small[offs[i]:offs[i + 1]].reshape(pieces[i].shape) for i in range(len(pieces))]
    g_b_mod, g_g_mix_pre, g_g_mix_post, g_g_ffn_pre, g_g_ffn_post, g_b_fg, g_w_fg_full, g_w_conv_full, g_b_conv = parts
    g_w_fg = lax.dynamic_slice_in_dim(g_w_fg_full, k_me * OS, OS, axis=1)
    g_w_conv = lax.dynamic_slice_in_dim(g_w_conv_full, k_me * FS, FS, axis=2)
    dmod_all = pack_all.reshape(N_DEV, -1)[:, :L * N_MOD * D].reshape(N_DEV, L, N_CHIPS, MS)
    dmod_cols = lax.dynamic_index_in_dim(dmod_all, k_me, axis=2, keepdims=False).reshape(N_DEV, L * MS)
    g_w_mod = _mm(_pad_rows(c_act, LANES), _pad_rows(dmod_cols, LANES), "tn", F32, "mm_dwmod", tm=D, tn=MS, tk=LANES)
    g_w_mod = g_w_mod.reshape(D, L, MS).transpose(1, 0, 2)

    tail = 0
    while to_scatter or to_swap:
        side, taken = carried()
        taken(_run_side(side, f"rs_tail_{tail}"))
        tail += 1
    g_w_qkv, g_w_o, g_w_gate, g_w_up, g_w_down = [jnp.stack([grad_shards[l][i] for l in range(L)])
                                                  for i in range(n_big)]

    grads = [g_w_mod, g_b_mod, g_g_mix_pre, g_g_mix_post, g_w_qkv, g_w_o, g_w_fg, g_b_fg, g_g_ffn_pre,
             g_g_ffn_post, g_w_gate, g_w_up, g_w_conv, g_b_conv, g_w_down]
    weights = [w_mod, b_mod, g_mix_pre, g_mix_post, w_qkv, w_o, w_fg, b_fg, g_ffn_pre, g_ffn_post, w_ffn_gate,
               w_ffn_up, w_conv, b_conv, w_ffn_down]
    ms = [m_w_mod, m_b_mod, m_g_mix_pre, m_g_mix_post, m_w_qkv, m_w_o, m_w_fg, m_b_fg, m_g_ffn_pre, m_g_ffn_post,
          m_w_ffn_gate, m_w_ffn_up, m_w_conv, m_b_conv, m_w_ffn_down]
    vs = [v_w_mod, v_b_mod, v_g_mix_pre, v_g_mix_post, v_w_qkv, v_w_o, v_w_fg, v_b_fg, v_g_ffn_pre, v_g_ffn_post,
          v_w_ffn_gate, v_w_ffn_up, v_w_conv, v_b_conv, v_w_ffn_down]
    deltas, new_ms, new_vs = [], [], []
    for wv, gv, mv, vv in zip(weights, grads, ms, vs):
        d, nm, nv = _adamw(wv, gv, mv, vv, "adamw")
        deltas.append(d)
        new_ms.append(nm)
        new_vs.append(nv)
    return (loss, grad_x, *grads, *deltas, *new_ms, *new_vs)
```

```python
import functools

import jax
import jax.numpy as jnp
from jax import lax
from jax.experimental import pallas as pl
from jax.experimental.pallas import tpu as pltpu

F32 = jnp.float32
BF16 = jnp.bfloat16
MESH = pl.DeviceIdType.MESH

HEAD_DIM = 64
ATT_BLOCK = 128
SB_Q_TILE = 512
FOX_Q_TILE = 1024
SB_DEAD_LOG = -110.0
LANES = 128
SUBLANES = 8
RMS_EPS = 1e-6
N_MOD = 6
N_CHIPS = 4
N_DEV = 8
ADAM_LR = 0.001
ADAM_B1 = 0.9
ADAM_B2 = 0.999
ADAM_EPS = 1e-08
ADAM_WD = 0.01
ADAM_STEP = 10
VMEM_LIMIT_BYTES = 56 * 1024 * 1024
NEG_BIG = -1e30

ANY = pl.BlockSpec(memory_space=pl.ANY)
VMEM_WHOLE = pl.BlockSpec(memory_space=pltpu.VMEM)


def _params(**kw):
    return pltpu.CompilerParams(vmem_limit_bytes=VMEM_LIMIT_BYTES, **kw)


def _dot(a, b):
    return jnp.dot(a, b, preferred_element_type=F32)


def _dot_nt(a, b):
    return lax.dot_general(a, b, (((1,), (1,)), ((), ())), preferred_element_type=F32)


def _dot_tn(a, b):
    return lax.dot_general(a, b, (((0,), (0,)), ((), ())), preferred_element_type=F32)


def _split_dot(x, t, parts):
    acc = None
    rem = x
    for _ in range(parts):
        piece = rem.astype(BF16)
        rem = rem - piece.astype(F32)
        d = _dot(piece, t)
        acc = d if acc is None else acc + d
    return acc


def _pick(n, prefs):
    for p in prefs:
        if n % p == 0:
            return p
    return n


def _mm(a, b, dims, out_dtype, name, tm=None, tn=None, tk=None):
    if dims == "tn":
        K, M = a.shape
    else:
        M, K = a.shape
    N = b.shape[0] if dims == "nt" else b.shape[1]
    tm = tm or _pick(M, (1024, 1408, 512, 256, 128))
    tn = tn or _pick(N, (1536, 1408, 1024, 768, 512, 256, 128))
    tk = tk or _pick(K, (1024, 2816, 1408, 512, 256, 128))
    nk = K // tk
    grid = (N // tn, M // tm, nk)
    if dims == "tn":
        a_spec = pl.BlockSpec((tk, tm), lambda j, i, k: (k, i))
    else:
        a_spec = pl.BlockSpec((tm, tk), lambda j, i, k: (i, k))
    if dims == "nt":
        b_spec = pl.BlockSpec((tn, tk), lambda j, i, k: (j, k))
    else:
        b_spec = pl.BlockSpec((tk, tn), lambda j, i, k: (k, j))
    o_spec = pl.BlockSpec((tm, tn), lambda j, i, k: (i, j))

    def body(a_ref, b_ref, o_ref, *scratch):
        x = a_ref[...].astype(BF16)
        y = b_ref[...].astype(BF16)
        if dims == "nn":
            r = _dot(x, y)
        elif dims == "nt":
            r = _dot_nt(x, y)
        else:
            r = _dot_tn(x, y)
        if nk == 1:
            o_ref[...] = r.astype(out_dtype)
        else:
            acc = scratch[0]
            k = pl.program_id(2)

            @pl.when(k == 0)
            def _():
                acc[...] = r

            @pl.when(k > 0)
            def _():
                acc[...] += r

            @pl.when(k == nk - 1)
            def _():
                o_ref[...] = acc[...].astype(out_dtype)

    return pl.pallas_call(
        body,
        name=name,
        grid=grid,
        in_specs=[a_spec, b_spec],
        out_specs=o_spec,
        out_shape=jax.ShapeDtypeStruct((M, N), out_dtype),
        scratch_shapes=[pltpu.VMEM((tm, tn), F32)] if nk > 1 else [],
        compiler_params=_params(dimension_semantics=("parallel", "parallel", "arbitrary")),
    )(a, b)


def _row_tile(S):
    return _pick(S, (512, 256, 128, 64, 32, 16, 8))


def _norm_mod(x, g, sc, sh, name):
    S, D = x.shape
    T = _row_tile(S)
    row = pl.BlockSpec((T, D), lambda i: (i, 0))
    vec = pl.BlockSpec((1, D), lambda i: (0, 0))

    def body(x_ref, g_ref, sc_ref, sh_ref, h_ref):
        xv = x_ref[...]
        r = lax.rsqrt(jnp.mean(xv * xv, axis=-1, keepdims=True) + RMS_EPS)
        n = (xv * r) * g_ref[...]
        h_ref[...] = (n * (1.0 + sc_ref[...]) + sh_ref[...]).astype(BF16)

    return pl.pallas_call(
        body, name=name, grid=(S // T,), in_specs=[row, vec, vec, vec], out_specs=row,
        out_shape=jax.ShapeDtypeStruct((S, D), BF16), compiler_params=_params(),
    )(x, g, sc, sh)


def _norm_mod_bwd(x, g, sc, sh, dhs, gres, name):
    S, D = x.shape
    T = _row_tile(S)
    n_dh = len(dhs)
    row = pl.BlockSpec((T, D), lambda i: (i, 0))
    vec = pl.BlockSpec((1, D), lambda i: (0, 0))

    def body(x_ref, g_ref, sc_ref, sh_ref, *refs):
        dh_refs = refs[:n_dh]
        gres_ref, dx_ref, dg_ref, dsc_ref, dsh_ref = refs[n_dh:]
        xv = x_ref[...]
        r = lax.rsqrt(jnp.mean(xv * xv, axis=-1, keepdims=True) + RMS_EPS)
        xn = xv * r
        n = xn * g_ref[...]
        dh = dh_refs[0][...]
        for extra in dh_refs[1:]:
            dh = dh + extra[...]
        dn = dh * (1.0 + sc_ref[...])
        dxn = dn * g_ref[...]
        dx = r * (dxn - xn * jnp.mean(dxn * xn, axis=-1, keepdims=True))
        dx_ref[...] = gres_ref[...] + dx

        @pl.when(pl.program_id(0) == 0)
        def _():
            dg_ref[...] = jnp.zeros_like(dg_ref)
            dsc_ref[...] = jnp.zeros_like(dsc_ref)
            dsh_ref[...] = jnp.zeros_like(dsh_ref)

        dg_ref[...] += jnp.sum(dn * xn, axis=0, keepdims=True)
        dsc_ref[...] += jnp.sum(dh * n, axis=0, keepdims=True)
        dsh_ref[...] += jnp.sum(dh, axis=0, keepdims=True)

    vshape = jax.ShapeDtypeStruct((1, D), F32)
    return pl.pallas_call(
        body, name=name, grid=(S // T,), in_specs=[row, vec, vec, vec] + [row] * (n_dh + 1),
        out_specs=[row, vec, vec, vec],
        out_shape=[jax.ShapeDtypeStruct((S, D), F32), vshape, vshape, vshape],
        compiler_params=_params(dimension_semantics=("arbitrary",)),
    )(x, g, sc, sh, *dhs, gres)


def _post_res(x, p, gp, gt, name):
    S, D = x.shape
    T = _row_tile(S)
    row = pl.BlockSpec((T, D), lambda i: (i, 0))
    vec = pl.BlockSpec((1, D), lambda i: (0, 0))

    def body(x_ref, p_ref, gp_ref, gt_ref, o_ref):
        pv = p_ref[...]
        r = lax.rsqrt(jnp.mean(pv * pv, axis=-1, keepdims=True) + RMS_EPS)
        o_ref[...] = x_ref[...] + gt_ref[...] * ((pv * r) * gp_ref[...])

    return pl.pallas_call(
        body, name=name, grid=(S // T,), in_specs=[row, row, vec, vec], out_specs=row,
        out_shape=jax.ShapeDtypeStruct((S, D), F32), compiler_params=_params(),
    )(x, p, gp, gt)


def _post_res_norm_mod(x, p, gp, gt, g, sc, sh, name):
    S, D = x.shape
    T = _row_tile(S)
    row = pl.BlockSpec((T, D), lambda i: (i, 0))
    vec = pl.BlockSpec((1, D), lambda i: (0, 0))

    def body(x_ref, p_ref, gp_ref, gt_ref, g_ref, sc_ref, sh_ref, x1_ref, h_ref):
        pv = p_ref[...]
        r = lax.rsqrt(jnp.mean(pv * pv, axis=-1, keepdims=True) + RMS_EPS)
        x1 = x_ref[...] + gt_ref[...] * ((pv * r) * gp_ref[...])
        x1_ref[...] = x1
        r1 = lax.rsqrt(jnp.mean(x1 * x1, axis=-1, keepdims=True) + RMS_EPS)
        n = (x1 * r1) * g_ref[...]
        h_ref[...] = (n * (1.0 + sc_ref[...]) + sh_ref[...]).astype(BF16)

    return pl.pallas_call(
        body, name=name, grid=(S // T,), in_specs=[row, row, vec, vec, vec, vec, vec], out_specs=[row, row],
        out_shape=[jax.ShapeDtypeStruct((S, D), F32), jax.ShapeDtypeStruct((S, D), BF16)],
        compiler_params=_params(),
    )(x, p, gp, gt, g, sc, sh)


def _post_res_bwd(p, gp, gt, g, name):
    S, D = p.shape
    T = _row_tile(S)
    row = pl.BlockSpec((T, D), lambda i: (i, 0))
    vec = pl.BlockSpec((1, D), lambda i: (0, 0))

    def body(p_ref, gp_ref, gt_ref, g_ref, dp_ref, dgp_ref, dgt_ref):
        pv = p_ref[...]
        gv = g_ref[...]
        r = lax.rsqrt(jnp.mean(pv * pv, axis=-1, keepdims=True) + RMS_EPS)
        pn = pv * r
        n2 = pn * gp_ref[...]
        dn2 = gv * gt_ref[...]
        dpn = dn2 * gp_ref[...]
        dp = r * (dpn - pn * jnp.mean(dpn * pn, axis=-1, keepdims=True))
        dp_ref[...] = dp.astype(BF16)

        @pl.when(pl.program_id(0) == 0)
        def _():
            dgp_ref[...] = jnp.zeros_like(dgp_ref)
            dgt_ref[...] = jnp.zeros_like(dgt_ref)

        dgp_ref[...] += jnp.sum(dn2 * pn, axis=0, keepdims=True)
        dgt_ref[...] += jnp.sum(gv * n2, axis=0, keepdims=True)

    vshape = jax.ShapeDtypeStruct((1, D), F32)
    return pl.pallas_call(
        body, name=name, grid=(S // T,), in_specs=[row, vec, vec, row], out_specs=[row, vec, vec],
        out_shape=[jax.ShapeDtypeStruct((S, D), BF16), vshape, vshape],
        compiler_params=_params(dimension_semantics=("arbitrary",)),
    )(p, gp, gt, g)


def _loss_head(y, target, name):
    S, D = y.shape
    T = _row_tile(S)
    row = pl.BlockSpec((T, D), lambda i: (i, 0))
    vec = pl.BlockSpec((1, D), lambda i: (0, 0))

    def body(y_ref, t_ref, dy_ref, sq_ref):
        e = y_ref[...] - t_ref[...]
        dy_ref[...] = e * (1.0 / D)

        @pl.when(pl.program_id(0) == 0)
        def _():
            sq_ref[...] = jnp.zeros_like(sq_ref)

        sq_ref[...] += jnp.sum(e * e, axis=0, keepdims=True)

    return pl.pallas_call(
        body, name=name, grid=(S // T,), in_specs=[row, row], out_specs=[row, vec],
        out_shape=[jax.ShapeDtypeStruct((S, D), F32), jax.ShapeDtypeStruct((1, D), F32)],
        compiler_params=_params(dimension_semantics=("arbitrary",)),
    )(y, target)


def _shift_down(v, k, rows):
    return jnp.where(rows >= k, pltpu.roll(v, k, 0), 0.0)


def _shift_up(v, k, rows, S):
    return jnp.where(rows < S - k, pltpu.roll(v, S - k, 0), 0.0)


def _conv_gate(gate_all, up_all, wc, bc, name):
    S, F = gate_all.shape
    C = LANES
    seq = pl.BlockSpec((S, C), lambda j: (0, j))

    def body(g_ref, u_ref, w_ref, b_ref, a_ref):
        rows = lax.broadcasted_iota(jnp.int32, (S, C), 0)
        gate = g_ref[...]
        w = w_ref[...]
        gc = w[2:3] * gate + w[1:2] * _shift_down(gate, 1, rows) + w[0:1] * _shift_down(gate, 2, rows) + b_ref[...]
        a_ref[...] = (gc * (1.0 / (1.0 + jnp.exp(-gc))) * u_ref[...]).astype(BF16)

    return pl.pallas_call(
        body, name=name, grid=(F // C,),
        in_specs=[seq, seq, pl.BlockSpec((3, C), lambda j: (0, j)), pl.BlockSpec((1, C), lambda j: (0, j))],
        out_specs=seq, out_shape=jax.ShapeDtypeStruct((S, F), BF16), compiler_params=_params(),
    )(gate_all, up_all, wc, bc)


def _conv_gate_bwd(gate_all, up_all, wc, bc, da, name):
    S, F = gate_all.shape
    C = LANES
    seq = pl.BlockSpec((S, C), lambda j: (0, j))

    def body(g_ref, u_ref, w_ref, b_ref, da_ref, dg_ref, du_ref, dw_ref, db_ref):
        rows = lax.broadcasted_iota(jnp.int32, (S, C), 0)
        gate = g_ref[...]
        up = u_ref[...]
        dav = da_ref[...]
        w = w_ref[...]
        g1 = _shift_down(gate, 1, rows)
        g2 = _shift_down(gate, 2, rows)
        gc = w[2:3] * gate + w[1:2] * g1 + w[0:1] * g2 + b_ref[...]
        sg = 1.0 / (1.0 + jnp.exp(-gc))
        du_ref[...] = (dav * (gc * sg)).astype(BF16)
        dgc = dav * up * (sg * (1.0 + gc * (1.0 - sg)))
        db_ref[...] = jnp.sum(dgc, axis=0, keepdims=True)
        dw_ref[0:1, :] = jnp.sum(dgc * g2, axis=0, keepdims=True)
        dw_ref[1:2, :] = jnp.sum(dgc * g1, axis=0, keepdims=True)
        dw_ref[2:3, :] = jnp.sum(dgc * gate, axis=0, keepdims=True)
        dgate = w[2:3] * dgc + w[1:2] * _shift_up(dgc, 1, rows, S) + w[0:1] * _shift_up(dgc, 2, rows, S)
        dg_ref[...] = dgate.astype(BF16)

    return pl.pallas_call(
        body, name=name, grid=(F // C,),
        in_specs=[seq, seq, pl.BlockSpec((3, C), lambda j: (0, j)), pl.BlockSpec((1, C), lambda j: (0, j)), seq],
        out_specs=[seq, seq, pl.BlockSpec((3, C), lambda j: (0, j)), pl.BlockSpec((1, C), lambda j: (0, j))],
        out_shape=[jax.ShapeDtypeStruct((S, F), BF16), jax.ShapeDtypeStruct((S, F), BF16),
                   jax.ShapeDtypeStruct((3, F), F32), jax.ShapeDtypeStruct((1, F), F32)],
        compiler_params=_params(),
    )(gate_all, up_all, wc, bc, da)


ATT_SCALE = HEAD_DIM ** -0.5


def _att_specs(S, D, TQ):
    nb = D // LANES
    q_spec = pl.BlockSpec((TQ, LANES), lambda p, i: (i, p))
    k_spec = pl.BlockSpec((S, LANES), lambda p, i: (0, nb + p))
    v_spec = pl.BlockSpec((S, LANES), lambda p, i: (0, 2 * nb + p))
    stat_spec = pl.BlockSpec((TQ, 2 * ATT_BLOCK), lambda p, i: (i, p))
    seq_spec = pl.BlockSpec((S, LANES), lambda p, i: (0, p))
    ck_spec = pl.BlockSpec((2, S // ATT_BLOCK, SUBLANES, ATT_BLOCK), lambda p, i: (p, 0, 0, 0))
    return q_spec, k_spec, v_spec, stat_spec, seq_spec, ck_spec


def _att_call(body, name, grid, in_specs, out_specs, out_shape, scratch, args, side, semantics):
    n_out = len(out_shape)
    if side is not None:
        in_specs = in_specs + [ANY] * len(side.inputs)
        out_specs = out_specs + [ANY] * len(side.out_shapes)
        out_shape = out_shape + side.out_shapes
        scratch = scratch + side.sems
        args = args + side.inputs
    res = pl.pallas_call(
        body, name=name, grid=grid, in_specs=in_specs, out_specs=out_specs, out_shape=out_shape,
        scratch_shapes=scratch, compiler_params=_params(dimension_semantics=semantics),
    )(*args)
    return res[:n_out], res[n_out:]


def _grid_ends(NP, NQ):
    p, qi = pl.program_id(0), pl.program_id(1)
    return jnp.logical_and(p == 0, qi == 0), jnp.logical_and(p == NP - 1, qi == NQ - 1)


def _head_lanes(hh):
    return slice(hh * HEAD_DIM, (hh + 1) * HEAD_DIM)


def _below(old, lo, new, axis=0):
    if lo == 0:
        return new
    keep = old[:lo] if axis == 0 else old[:, :lo]
    return jnp.concatenate([keep, new], axis=axis)


def _scaled_q(q_ref, hh):
    return (q_ref[:, _head_lanes(hh)].astype(F32) * ATT_SCALE).astype(BF16)


def _tri(cmp):
    B = ATT_BLOCK
    row = lax.broadcasted_iota(jnp.int32, (B, B), 0)
    col = lax.broadcasted_iota(jnp.int32, (B, B), 1)
    half = jnp.concatenate([jnp.where(cmp(row, col), 1.0, 0.0).astype(BF16), jnp.ones((B, B), BF16)], axis=1)
    return jnp.concatenate([half, half], axis=0)


def _hi_lo_dot(x, t):
    hi = x.astype(BF16)
    lo = (x - hi.astype(F32)).astype(BF16)
    return _dot(jnp.concatenate([hi, lo], axis=1), t)


def _key_minus_query(j, qi, TQ):
    row = lax.broadcasted_iota(jnp.int32, (TQ, ATT_BLOCK), 0)
    col = lax.broadcasted_iota(jnp.int32, (TQ, ATT_BLOCK), 1)
    return col - row + (j * ATT_BLOCK - qi * TQ)


def _log_sigmoids(z):
    sp = jnp.log(1.0 + jnp.exp(-jnp.abs(z)))
    lb = jnp.minimum(z, 0.0) - sp
    return lb, lb - z


def _log_sigmoids_fast(z):
    zc = jnp.maximum(z, -80.0)
    lb = -jnp.log(1.0 + jnp.exp(-zc))
    return lb, lb - zc


def _sb_fwd(qkv, name, side=None):
    S, D3 = qkv.shape
    D = D3 // 3
    B, TQ = ATT_BLOCK, SB_Q_TILE
    R = TQ // B
    NP = D // LANES
    NQ = S // TQ
    q_spec, k_spec, v_spec, stat_spec, _, _ = _att_specs(S, D, TQ)

    def body(*refs):
        (q_ref, k_ref, v_ref), (o_ref, lt_ref, first_ref), _, parts = _side_parts(side, refs, 3, 3)
        begin, end = _side_hooks(side, parts, *_grid_ends(NP, NQ))
        begin()
        qi = pl.program_id(1)
        t_suffix = _tri(lambda r, c: r > c)
        qs = [_scaled_q(q_ref, hh) for hh in range(2)]

        def tile(j, carry, masked, lo=0):
            r0 = pl.multiple_of(j * B, B)
            if masked:
                strict = _key_minus_query(j, qi, TQ)[lo:] < 0
            out = []
            for hh in range(2):
                c, acc = carry[hh]
                k = k_ref[pl.ds(r0, B), _head_lanes(hh)]
                v = v_ref[pl.ds(r0, B), _head_lanes(hh)]
                lb, l1 = _log_sigmoids_fast(_dot_nt(qs[hh][lo:], k))
                if masked:
                    l1 = jnp.where(strict, l1, 0.0)
                sums = _hi_lo_dot(l1, t_suffix)
                a = jnp.exp(lb + c[lo:] + sums[:, :B])
                if masked:
                    a = jnp.where(strict, a, 0.0)
                out.append((_below(c, lo, c[lo:] + sums[:, B:]), _below(acc, lo, acc[lo:] + _dot(a.astype(BF16), v))))
            return tuple(out)

        carry = (jnp.zeros((TQ, B), F32), jnp.zeros((TQ, HEAD_DIM), F32))
        carry = (carry, carry)
        for jj in reversed(range(R)):
            carry = tile(qi * R + jj, carry, True, jj * B)

        def alive(cr):
            return jnp.max(jnp.maximum(cr[0][0], cr[1][0])) > SB_DEAD_LOG

        def walk(state):
            j, cr, _ = state
            cr = tile(j, cr, False)
            return j - 1, cr, alive(cr)

        j, carry, _ = lax.while_loop(lambda st: jnp.logical_and(st[0] >= 0, st[2]), walk,
                                     (qi * R - 1, carry, alive(carry)))
        first_ref[pl.program_id(0), qi] = (j + 1).astype(F32)
        for hh in range(2):
            c, acc = carry[hh]
            o_ref[:, _head_lanes(hh)] = acc
            lt_ref[:, hh * B:(hh + 1) * B] = c
        end()

    return _att_call(
        body, name, (NP, NQ), [q_spec, k_spec, v_spec],
        [q_spec, stat_spec, pl.BlockSpec(memory_space=pltpu.SMEM)],
        [jax.ShapeDtypeStruct((S, D), F32), jax.ShapeDtypeStruct((S, 2 * NP * B), F32),
         jax.ShapeDtypeStruct((NP, NQ), F32)], [], [qkv, qkv, qkv], side, ("arbitrary", "arbitrary"))


def _sb_bwd(qkv, do, lt, first, name, side=None):
    S, D3 = qkv.shape
    D = D3 // 3
    B, TQ = ATT_BLOCK, SB_Q_TILE
    R = TQ // B
    NP = D // LANES
    NQ = S // TQ
    q_spec, k_spec, v_spec, stat_spec, seq_spec, _ = _att_specs(S, D, TQ)

    def body(*refs):
        ins, (dq_ref, dk_ref, dv_ref), (dk_acc, dv_acc), parts = _side_parts(side, refs, 6, 3)
        first_ref, q_ref, k_ref, v_ref, do_ref, lt_ref = ins
        begin, end = _side_hooks(side, parts, *_grid_ends(NP, NQ))
        begin()
        qi = pl.program_id(1)
        t_prefix = _tri(lambda r, c: r <= c)
        t_before = _tri(lambda r, c: r < c)

        @pl.when(qi == 0)
        def _():
            dk_acc[...] = jnp.zeros_like(dk_acc)
            dv_acc[...] = jnp.zeros_like(dv_acc)

        qs = [_scaled_q(q_ref, hh) for hh in range(2)]
        dob = [do_ref[:, _head_lanes(hh)].astype(BF16) for hh in range(2)]
        ltot = [lt_ref[:, hh * B:(hh + 1) * B] for hh in range(2)]

        def tile(j, carry, masked, lo=0):
            r0 = pl.multiple_of(j * B, B)
            if masked:
                strict = _key_minus_query(j, qi, TQ)[lo:] < 0
            out = []
            for hh in range(2):
                pre, cu, dq = carry[hh]
                lanes = _head_lanes(hh)
                k = k_ref[pl.ds(r0, B), lanes]
                v = v_ref[pl.ds(r0, B), lanes]
                lb, l1 = _log_sigmoids_fast(_dot_nt(qs[hh][lo:], k))
                if masked:
                    l1 = jnp.where(strict, l1, 0.0)
                sums = _hi_lo_dot(l1, t_prefix)
                a = jnp.exp(lb + (ltot[hh][lo:] - pre[lo:] - sums[:, :B]))
                if masked:
                    a = jnp.where(strict, a, 0.0)
                u = a * _dot_nt(dob[hh][lo:], v)
                usums = _hi_lo_dot(u, t_before)
                dz = u - (u + cu[lo:] + usums[:, :B]) * jnp.exp(lb)
                if masked:
                    dz = jnp.where(strict, dz, 0.0)
                dzb = dz.astype(BF16)
                dk_acc[pl.ds(r0, B), lanes] += _dot_tn(dzb, qs[hh][lo:])
                dv_acc[pl.ds(r0, B), lanes] += _dot_tn(a.astype(BF16), dob[hh][lo:])
                out.append((_below(pre, lo, pre[lo:] + sums[:, B:]), _below(cu, lo, cu[lo:] + usums[:, B:]),
                            _below(dq, lo, dq[lo:] + _dot(dzb, k))))
            return tuple(out)

        zero = (jnp.zeros((TQ, B), F32), jnp.zeros((TQ, B), F32), jnp.zeros((TQ, HEAD_DIM), F32))
        first = jnp.clip(first_ref[pl.program_id(0), qi].astype(jnp.int32), 0, qi * R)
        carry = lax.fori_loop(first, qi * R, lambda j, cr: tile(j, cr, False), (zero, zero))
        for jj in range(R):
            carry = tile(qi * R + jj, carry, True, jj * B)
        for hh in range(2):
            dq_ref[:, _head_lanes(hh)] = (carry[hh][2] * ATT_SCALE).astype(BF16)

        @pl.when(qi == NQ - 1)
        def _():
            dk_ref[...] = dk_acc[...].astype(BF16)
            dv_ref[...] = dv_acc[...].astype(BF16)

        end()

    out = jax.ShapeDtypeStruct((S, D), BF16)
    return _att_call(
        body, name, (NP, NQ), [pl.BlockSpec(memory_space=pltpu.SMEM), q_spec, k_spec, v_spec, q_spec, stat_spec],
        [q_spec, seq_spec, seq_spec], [out, out, out],
        [pltpu.VMEM((S, LANES), F32), pltpu.VMEM((S, LANES), F32)], [first, qkv, qkv, qkv, do, lt], side,
        ("arbitrary", "arbitrary"))


def _fox_specs(S, D, TQ):
    row_spec = pl.BlockSpec((2 * SUBLANES, TQ), lambda p, i: (p, i))
    keyb_spec = pl.BlockSpec((S, 2 * ATT_BLOCK), lambda p, i: (0, p))
    return row_spec, keyb_spec


def _key_gt_query_t(j, qi, TQ):
    key = lax.broadcasted_iota(jnp.int32, (ATT_BLOCK, TQ), 0)
    qry = lax.broadcasted_iota(jnp.int32, (ATT_BLOCK, TQ), 1)
    return key - qry + (j * ATT_BLOCK - qi * TQ) > 0


def _fox_fwd(qkv, cq_rows, ck_b, name, side=None):
    S, D3 = qkv.shape
    D = D3 // 3
    B, TQ = ATT_BLOCK, FOX_Q_TILE
    R = TQ // B
    NP = D // LANES
    NQ = S // TQ
    q_spec, k_spec, v_spec, _, _, _ = _att_specs(S, D, TQ)
    row_spec, keyb_spec = _fox_specs(S, D, TQ)

    def body(*refs):
        (q_ref, k_ref, v_ref, cq_ref, ck_ref), (o_ref, lse_ref), _, parts = _side_parts(side, refs, 5, 2)
        begin, end = _side_hooks(side, parts, *_grid_ends(NP, NQ))
        begin()
        qi = pl.program_id(1)
        qs = [_scaled_q(q_ref, hh) for hh in range(2)]
        cq = [cq_ref[hh * SUBLANES:hh * SUBLANES + 1, :] for hh in range(2)]

        def tile(j, carry, masked, lo=0):
            r0 = pl.multiple_of(j * B, B)
            if masked:
                hidden = _key_gt_query_t(j, qi, TQ)[:, lo:]
            out = []
            for hh in range(2):
                m, lsum, acc = carry[hh]
                k = k_ref[pl.ds(r0, B), _head_lanes(hh)]
                v = v_ref[pl.ds(r0, B), _head_lanes(hh)]
                ck = jnp.tile(ck_ref[pl.ds(r0, B), hh * B:(hh + 1) * B], (1, (TQ - lo) // B))
                s = _dot_nt(k, qs[hh][lo:]) + (cq[hh][:, lo:] - ck)
                if masked:
                    s = jnp.where(hidden, NEG_BIG, s)
                m_new = jnp.maximum(m[:, lo:], jnp.max(s, axis=0, keepdims=True))
                p = jnp.exp(s - m_new)
                alpha = jnp.exp(m[:, lo:] - m_new)
                out.append((_below(m, lo, m_new, 1),
                            _below(lsum, lo, alpha * lsum[:, lo:] + jnp.sum(p, axis=0, keepdims=True), 1),
                            _below(acc, lo, alpha * acc[:, lo:] + _dot_tn(v, p.astype(BF16)), 1)))
            return tuple(out)

        zero = (jnp.full((1, TQ), NEG_BIG, F32), jnp.zeros((1, TQ), F32), jnp.zeros((HEAD_DIM, TQ), F32))
        carry = lax.fori_loop(0, qi * (R // 2), lambda t, cr: tile(2 * t + 1, tile(2 * t, cr, False), False),
                              (zero, zero))
        for jj in range(R):
            carry = tile(qi * R + jj, carry, True, jj * B)
        o_t = jnp.concatenate([carry[hh][2] * (1.0 / carry[hh][1]) for hh in range(2)], axis=0)
        o_ref[...] = o_t.T
        for hh in range(2):
            m, lsum, _ = carry[hh]
            lse_ref[hh * SUBLANES:(hh + 1) * SUBLANES, :] = jnp.broadcast_to(m + jnp.log(lsum), (SUBLANES, TQ))
        end()

    return _att_call(
        body, name, (NP, NQ), [q_spec, k_spec, v_spec, row_spec, keyb_spec], [q_spec, row_spec],
        [jax.ShapeDtypeStruct((S, D), F32), jax.ShapeDtypeStruct((2 * NP * SUBLANES, S), F32)], [],
        [qkv, qkv, qkv, cq_rows, ck_b], side, ("arbitrary", "arbitrary"))


def _fox_bwd(qkv, do, o, lse_rows, cq_rows, ck_b, name, side=None):
    S, D3 = qkv.shape
    D = D3 // 3
    B, TQ = ATT_BLOCK, FOX_Q_TILE
    R = TQ // B
    NP = D // LANES
    NQ = S // TQ
    q_spec, k_spec, v_spec, _, seq_spec, _ = _att_specs(S, D, TQ)
    row_spec, keyb_spec = _fox_specs(S, D, TQ)

    def body(*refs):
        ins, outs, (dk_acc, dv_acc), parts = _side_parts(side, refs, 8, 5)
        q_ref, k_ref, v_ref, do_ref, o_ref, lse_ref, cq_ref, ck_ref = ins
        dq_ref, dk_ref, dv_ref, dcq_ref, dck_ref = outs
        begin, end = _side_hooks(side, parts, *_grid_ends(NP, NQ))
        begin()
        qi = pl.program_id(1)

        @pl.when(qi == 0)
        def _():
            dk_acc[...] = jnp.zeros_like(dk_acc)
            dv_acc[...] = jnp.zeros_like(dv_acc)
            dck_ref[...] = jnp.zeros_like(dck_ref)

        qs = [_scaled_q(q_ref, hh) for hh in range(2)]
        dob = [do_ref[:, _head_lanes(hh)].astype(BF16) for hh in range(2)]
        prod_t = (do_ref[...] * o_ref[...]).T
        delta = [jnp.sum(prod_t[hh * HEAD_DIM:(hh + 1) * HEAD_DIM], axis=0, keepdims=True) for hh in range(2)]
        cq = [cq_ref[hh * SUBLANES:hh * SUBLANES + 1, :] for hh in range(2)]
        lse = [lse_ref[hh * SUBLANES:hh * SUBLANES + 1, :] for hh in range(2)]

        def tile(j, carry, masked, lo=0):
            r0 = pl.multiple_of(j * B, B)
            if masked:
                hidden = _key_gt_query_t(j, qi, TQ)[:, lo:]
            out = []
            for hh in range(2):
                dq, keysum = carry[hh]
                lanes = _head_lanes(hh)
                k = k_ref[pl.ds(r0, B), lanes]
                v = v_ref[pl.ds(r0, B), lanes]
                ck = jnp.tile(ck_ref[pl.ds(r0, B), hh * B:(hh + 1) * B], (1, (TQ - lo) // B))
                p = jnp.exp(_dot_nt(k, qs[hh][lo:]) + (cq[hh][:, lo:] - ck) - lse[hh][:, lo:])
                if masked:
                    p = jnp.where(hidden, 0.0, p)
                ds = p * (_dot_nt(v, dob[hh][lo:]) - delta[hh][:, lo:])
                dsb = ds.astype(BF16)
                dk_acc[pl.ds(r0, B), lanes] += _dot(dsb, qs[hh][lo:])
                dv_acc[pl.ds(r0, B), lanes] += _dot(p.astype(BF16), dob[hh][lo:])
                qsum = jnp.sum(ds, axis=1, keepdims=True)
                dck_ref[pl.ds(r0, B), hh * B:(hh + 1) * B] -= jnp.broadcast_to(qsum, (B, B))
                out.append((_below(dq, lo, dq[:, lo:] + _dot_tn(k, dsb), 1),
                            _below(keysum, lo, keysum[:, lo:] + jnp.sum(ds, axis=0, keepdims=True), 1)))
            return tuple(out)

        zero = (jnp.zeros((HEAD_DIM, TQ), F32), jnp.zeros((1, TQ), F32))
        carry = lax.fori_loop(0, qi * (R // 2), lambda t, cr: tile(2 * t + 1, tile(2 * t, cr, False), False),
                              (zero, zero))
        for jj in range(R):
            carry = tile(qi * R + jj, carry, True, jj * B)
        dq_t = jnp.concatenate([carry[hh][0] for hh in range(2)], axis=0)
        dq_ref[...] = (dq_t.T * ATT_SCALE).astype(BF16)
        for hh in range(2):
            dcq_ref[hh * SUBLANES:(hh + 1) * SUBLANES, :] = jnp.broadcast_to(carry[hh][1], (SUBLANES, TQ))

        @pl.when(qi == NQ - 1)
        def _():
            dk_ref[...] = dk_acc[...].astype(BF16)
            dv_ref[...] = dv_acc[...].astype(BF16)

        end()

    out = jax.ShapeDtypeStruct((S, D), BF16)
    return _att_call(
        body, name, (NP, NQ), [q_spec, k_spec, v_spec, q_spec, q_spec, row_spec, row_spec, keyb_spec],
        [q_spec, seq_spec, seq_spec, row_spec, keyb_spec],
        [out, out, out, jax.ShapeDtypeStruct((2 * NP * SUBLANES, S), F32), jax.ShapeDtypeStruct((S, 2 * NP * B), F32)],
        [pltpu.VMEM((S, LANES), F32), pltpu.VMEM((S, LANES), F32)],
        [qkv, qkv, qkv, do, o, lse_rows, cq_rows, ck_b], side, ("arbitrary", "arbitrary"))


def _forget_cumsum(f3, bias, name):
    NQ, NH, B = f3.shape

    def body(f_ref, b_ref, cum_ref):
        row = lax.broadcasted_iota(jnp.int32, (B, B), 0)
        col = lax.broadcasted_iota(jnp.int32, (B, B), 1)
        t_incl = jnp.where(row <= col, 1.0, 0.0).astype(BF16)

        def step(b, carry):
            lf, _ = _log_sigmoids(f_ref[b] + b_ref[...])
            cum = _split_dot(lf, t_incl, 3) + carry
            cum_ref[b] = cum
            return jnp.broadcast_to(cum[:, B - 1:B], (NH, B))

        lax.fori_loop(0, NQ, step, jnp.zeros((NH, B), F32))

    return pl.pallas_call(
        body, name=name, in_specs=[VMEM_WHOLE, VMEM_WHOLE], out_specs=VMEM_WHOLE,
        out_shape=jax.ShapeDtypeStruct((NQ, NH, B), F32), compiler_params=_params(),
    )(f3, bias)


def _forget_cumsum_bwd(dcum3, f3, bias, name):
    NQ, NH, B = f3.shape

    def body(d_ref, f_ref, b_ref, df_ref, tot_ref):
        row = lax.broadcasted_iota(jnp.int32, (B, B), 0)
        col = lax.broadcasted_iota(jnp.int32, (B, B), 1)
        t_rev = jnp.where(row >= col, 1.0, 0.0).astype(BF16)

        def step(it, carry):
            run, tot = carry
            b = NQ - 1 - it
            dlf = _split_dot(d_ref[b], t_rev, 3) + run
            f = f_ref[b] + b_ref[...]
            df = dlf * (1.0 / (1.0 + jnp.exp(f)))
            df_ref[b] = df
            return jnp.broadcast_to(dlf[:, 0:1], (NH, B)), tot + df

        _, tot = lax.fori_loop(0, NQ, step, (jnp.zeros((NH, B), F32), jnp.zeros((NH, B), F32)))
        tot_ref[...] = _split_dot(tot, jnp.ones((B, B), BF16), 3)

    return pl.pallas_call(
        body, name=name, in_specs=[VMEM_WHOLE, VMEM_WHOLE, VMEM_WHOLE], out_specs=[VMEM_WHOLE, VMEM_WHOLE],
        out_shape=[jax.ShapeDtypeStruct((NQ, NH, B), F32), jax.ShapeDtypeStruct((NH, B), F32)],
        compiler_params=_params(),
    )(dcum3, f3, bias)


def _silu(x, name):
    def body(x_ref, o_ref):
        v = x_ref[...]
        o_ref[...] = v * (1.0 / (1.0 + jnp.exp(-v)))

    return pl.pallas_call(body, name=name, in_specs=[VMEM_WHOLE], out_specs=VMEM_WHOLE,
                          out_shape=jax.ShapeDtypeStruct(x.shape, F32), compiler_params=_params())(x)


def _sum_leading(x, name):
    N, R, C = x.shape
    T = _pick(R, (256, 128, 64, 32, 16, 8))

    def body(x_ref, o_ref):
        acc = x_ref[0].astype(F32)
        for r in range(1, N):
            acc = acc + x_ref[r].astype(F32)
        o_ref[...] = acc

    return pl.pallas_call(
        body, name=name, grid=(R // T,), in_specs=[pl.BlockSpec((N, T, C), lambda i: (0, i, 0))],
        out_specs=pl.BlockSpec((T, C), lambda i: (i, 0)), out_shape=jax.ShapeDtypeStruct((R, C), F32),
        compiler_params=_params(),
    )(x)


def _adamw(w, g, m, v, name):
    shape = w.shape
    C = shape[-1]
    R = w.size // C
    T = R
    for cand in (512, 256, 128, 64, 32, 16, 8):
        if R % cand == 0 and cand * C * 4 <= (1 << 20):
            T = cand
            break
    spec = pl.BlockSpec((T, C), lambda i: (i, 0))
    c1 = 1.0 / (1.0 - ADAM_B1 ** ADAM_STEP)
    c2 = 1.0 / (1.0 - ADAM_B2 ** ADAM_STEP)

    def body(w_ref, g_ref, m_ref, v_ref, d_ref, nm_ref, nv_ref):
        gv = g_ref[...]
        nm = ADAM_B1 * m_ref[...] + (1.0 - ADAM_B1) * gv
        nv = ADAM_B2 * v_ref[...] + (1.0 - ADAM_B2) * (gv * gv)
        nm_ref[...] = nm
        nv_ref[...] = nv
        d_ref[...] = -ADAM_LR * ((nm * c1) / (jnp.sqrt(nv * c2) + ADAM_EPS) + ADAM_WD * w_ref[...])

    out = jax.ShapeDtypeStruct((R, C), F32)
    d, nm, nv = pl.pallas_call(
        body, name=name, grid=(R // T,), in_specs=[spec] * 4, out_specs=[spec] * 3, out_shape=[out] * 3,
        compiler_params=_params(),
    )(w.reshape(R, C), g.reshape(R, C), m.reshape(R, C), v.reshape(R, C))
    return d.reshape(shape), nm.reshape(shape), nv.reshape(shape)


def _mesh_pos():
    return lax.axis_index("x"), lax.axis_index("y"), lax.axis_index("c")


def _all_gather_small(x2d, name):
    m_per, n = x2d.shape

    def body(x_ref, out_ref, send_sems, recv_sems, local_sem):
        x, y, c = _mesh_pos()
        me, sibling = (x, y, c), (x, y, 1 - c)
        chips = [(1 - x, y), (x, 1 - y), (1 - x, 1 - y)]

        def rows(px, py, pc):
            return out_ref.at[pl.ds((4 * px + 2 * py + pc) * m_per, m_per), :]

        def copy(k, block, to, src=None):
            return pltpu.make_async_remote_copy(
                src_ref=rows(*block) if src is None else src, dst_ref=rows(*block),
                send_sem=send_sems.at[k], recv_sem=recv_sems.at[k], device_id=to, device_id_type=MESH)

        mine = pltpu.make_async_copy(x_ref, rows(*me), local_sem)
        mine.start()
        first = [copy(0, me, sibling, src=x_ref)]
        first += [copy(1 + j, me, (*chip, c), src=x_ref) for j, chip in enumerate(chips)]
        for cp in first:
            cp.start()
        passed = [copy(4 + j, (*chip, c), sibling) for j, chip in enumerate(chips)]
        for j, chip in enumerate(chips):
            copy(1 + j, (*chip, c), me).wait_recv()
            passed[j].start()
        copy(0, sibling, me).wait_recv()
        for j, chip in enumerate(chips):
            copy(4 + j, (*chip, 1 - c), me).wait_recv()
        for cp in first + passed:
            cp.wait_send()
        mine.wait()

    return pl.pallas_call(
        body, name=name, out_shape=jax.ShapeDtypeStruct((N_DEV * m_per, n), x2d.dtype),
        in_specs=[VMEM_WHOLE], out_specs=VMEM_WHOLE,
        scratch_shapes=[pltpu.SemaphoreType.DMA((7,)), pltpu.SemaphoreType.DMA((7,)), pltpu.SemaphoreType.DMA],
        compiler_params=_params(),
    )(x2d)


class _Side:
    def __init__(self, inputs, out_shapes, sems, start, wait):
        self.inputs, self.out_shapes, self.sems, self.start, self.wait = inputs, out_shapes, sems, start, wait


def _join_sides(sides):
    sides = [s for s in sides if s is not None]
    if not sides:
        return None
    bounds_in, bounds_out, bounds_sem = [0], [0], [0]
    for s in sides:
        bounds_in.append(bounds_in[-1] + len(s.inputs))
        bounds_out.append(bounds_out[-1] + len(s.out_shapes))
        bounds_sem.append(bounds_sem[-1] + len(s.sems))

    def each(method):
        def run(ins, outs, sems):
            for t, s in enumerate(sides):
                getattr(s, method)(ins[bounds_in[t]:bounds_in[t + 1]], outs[bounds_out[t]:bounds_out[t + 1]],
                                   sems[bounds_sem[t]:bounds_sem[t + 1]])
        return run

    return _Side([a for s in sides for a in s.inputs], [o for s in sides for o in s.out_shapes],
                 [m for s in sides for m in s.sems], each("start"), each("wait"))


def _side_parts(side, refs, n_in, n_out):
    if side is None:
        return refs[:n_in], refs[n_in:n_in + n_out], refs[n_in + n_out:], None
    si, so, ss = len(side.inputs), len(side.out_shapes), len(side.sems)
    a = n_in + si
    b = a + n_out + so
    ins, side_in = refs[:n_in], refs[n_in:a]
    outs, side_out = refs[a:a + n_out], refs[a + n_out:b]
    scratch, side_sems = refs[b:len(refs) - ss], refs[len(refs) - ss:]
    return ins, outs, scratch, (side_in, side_out, side_sems)


def _side_hooks(side, parts, first, last):
    if side is None:
        return lambda: None, lambda: None

    def begin():
        @pl.when(first)
        def _():
            side.start(*parts)

    def end():
        @pl.when(last)
        def _():
            side.wait(*parts)

    return begin, end


def _run_side(side, name):
    n_in = len(side.inputs)
    n_out = len(side.out_shapes)

    def body(*refs):
        parts = (refs[:n_in], refs[n_in:n_in + n_out], refs[n_in + n_out:])
        side.start(*parts)
        side.wait(*parts)

    return pl.pallas_call(
        body, name=name, out_shape=side.out_shapes, in_specs=[ANY] * n_in, out_specs=[ANY] * n_out,
        scratch_shapes=side.sems, compiler_params=_params(),
    )(*side.inputs)


def _gather_side(ws):
    n = len(ws)

    def copies(ins, outs, sems):
        send_sems, recv_sems, local_sems = sems
        x, y, c = _mesh_pos()
        k_me = 2 * x + y
        chips = [(1 - x, y), (x, 1 - y), (1 - x, 1 - y)]

        def remote(i, j, slot):
            px, py = chips[j]
            return pltpu.make_async_remote_copy(
                src_ref=ins[i], dst_ref=outs[i].at[slot], send_sem=send_sems.at[i, j],
                recv_sem=recv_sems.at[i, j], device_id=(px, py, c), device_id_type=MESH)

        local = [pltpu.make_async_copy(ins[i], outs[i].at[k_me], local_sems.at[i]) for i in range(n)]
        return remote, local, k_me, chips

    def start(ins, outs, sems):
        remote, local, k_me, _ = copies(ins, outs, sems)
        for i in range(n):
            local[i].start()
            for j in range(3):
                remote(i, j, k_me).start()

    def wait(ins, outs, sems):
        remote, local, k_me, chips = copies(ins, outs, sems)
        for i in range(n):
            for j, (px, py) in enumerate(chips):
                remote(i, j, 2 * px + py).wait_recv()
        for i in range(n):
            for j in range(3):
                remote(i, j, k_me).wait_send()
            local[i].wait()

    return _Side(list(ws), [jax.ShapeDtypeStruct((N_CHIPS,) + w.shape, w.dtype) for w in ws],
                 [pltpu.SemaphoreType.DMA((n, 3)), pltpu.SemaphoreType.DMA((n, 3)), pltpu.SemaphoreType.DMA((n,))],
                 start, wait)


def _scatter_side(gs, whole_shard=False):
    n = len(gs)
    halves = [g.shape[1] if whole_shard else g.shape[1] // 2 for g in gs]

    def copies(ins, outs, sems):
        send_sems, recv_sems, local_sems = sems
        x, y, c = _mesh_pos()

        def flip(v, bit):
            return 1 - v if bit else v

        def piece(i, px, py, pc):
            if whole_shard:
                return ins[i].at[2 * px + py]
            return ins[i].at[2 * px + py, pl.ds(pc * halves[i], halves[i])]

        me = 4 * x + 2 * y + c

        def remote(i, r, arriving=False):
            px, py, pc = flip(x, r & 4), flip(y, r & 2), flip(c, r & 1)
            slot = 4 * px + 2 * py + pc if arriving else me
            return pltpu.make_async_remote_copy(
                src_ref=piece(i, px, py, pc), dst_ref=outs[i].at[slot], send_sem=send_sems.at[i, r - 1],
                recv_sem=recv_sems.at[i, r - 1], device_id=(px, py, pc), device_id_type=MESH)

        local = [pltpu.make_async_copy(piece(i, x, y, c), outs[i].at[me], local_sems.at[i]) for i in range(n)]
        return remote, local

    def start(ins, outs, sems):
        remote, local = copies(ins, outs, sems)
        for i in range(n):
            local[i].start()
            for r in range(1, N_DEV):
                remote(i, r).start()

    def wait(ins, outs, sems):
        remote, local = copies(ins, outs, sems)
        for i in range(n):
            for r in range(1, N_DEV):
                remote(i, r, arriving=True).wait_recv()
        for i in range(n):
            for r in range(1, N_DEV):
                remote(i, r).wait_send()
            local[i].wait()

    return _Side(list(gs), [jax.ShapeDtypeStruct((N_DEV, h) + g.shape[2:], g.dtype) for g, h in zip(gs, halves)],
                 [pltpu.SemaphoreType.DMA((n, N_DEV - 1)), pltpu.SemaphoreType.DMA((n, N_DEV - 1)),
                  pltpu.SemaphoreType.DMA((n,))], start, wait)


def _swap_side(hs):
    n = len(hs)

    def copies(ins, outs, sems):
        send_sems, recv_sems, local_sems = sems
        x, y, c = _mesh_pos()

        def remote(i, slot):
            return pltpu.make_async_remote_copy(
                src_ref=ins[i], dst_ref=outs[i].at[slot], send_sem=send_sems.at[i], recv_sem=recv_sems.at[i],
                device_id=(x, y, 1 - c), device_id_type=MESH)

        local = [pltpu.make_async_copy(ins[i], outs[i].at[c], local_sems.at[i]) for i in range(n)]
        return remote, local, c

    def start(ins, outs, sems):
        remote, local, c = copies(ins, outs, sems)
        for i in range(n):
            local[i].start()
            remote(i, c).start()

    def wait(ins, outs, sems):
        remote, local, c = copies(ins, outs, sems)
        for i in range(n):
            remote(i, 1 - c).wait_recv()
        for i in range(n):
            remote(i, c).wait_send()
            local[i].wait()

    return _Side(list(hs), [jax.ShapeDtypeStruct((2,) + h.shape, h.dtype) for h in hs],
                 [pltpu.SemaphoreType.DMA((n,)), pltpu.SemaphoreType.DMA((n,)), pltpu.SemaphoreType.DMA((n,))],
                 start, wait)


def _pad_rows(a, rows):
    return jnp.pad(a, ((0, rows - a.shape[0]), (0, 0)))


def kernel(x, c, w_mod, b_mod, g_mix_pre, g_mix_post, w_qkv, w_o, w_fg, b_fg, g_ffn_pre, g_ffn_post, w_ffn_gate, w_ffn_up, w_conv, b_conv, w_ffn_down, loss_target, m_w_mod, m_b_mod, m_g_mix_pre, m_g_mix_post, m_w_qkv, m_w_o, m_w_fg, m_b_fg, m_g_ffn_pre, m_g_ffn_post, m_w_ffn_gate, m_w_ffn_up, m_w_conv, m_b_conv, m_w_ffn_down, v_w_mod, v_b_mod, v_g_mix_pre, v_g_mix_post, v_w_qkv, v_w_o, v_w_fg, v_b_fg, v_g_ffn_pre, v_g_ffn_post, v_w_ffn_gate, v_w_ffn_up, v_w_conv, v_b_conv, v_w_ffn_down):
    xs = x[0]
    target = loss_target[0]
    S, D = xs.shape
    L = w_mod.shape[0]
    LF = w_fg.shape[0]
    MS = w_mod.shape[2]
    QS = w_qkv.shape[2]
    OS = w_o.shape[1]
    FS = w_ffn_gate.shape[2]
    F = N_CHIPS * FS
    NH = D // HEAD_DIM
    B = ATT_BLOCK
    NQ = S // B
    ax, ay, ac = _mesh_pos()
    k_me = 2 * ax + ay
    b_me = 4 * ax + 2 * ay + ac

    conv_rows = -(-(L * 3 * FS) // D)
    conv_rows = -(-conv_rows // SUBLANES) * SUBLANES
    conv_flat = jnp.pad(w_conv.reshape(-1), (0, conv_rows * D - L * 3 * FS)).reshape(conv_rows, D)
    first = jnp.concatenate([_pad_rows(c, SUBLANES), conv_flat], axis=0)
    first_all = _all_gather_small(first, "ag_cond").reshape(N_DEV, SUBLANES + conv_rows, D)
    c_all = first_all[:, 0, :]
    conv_all = first_all[0::2, SUBLANES:, :].reshape(N_CHIPS, -1)[:, :L * 3 * FS]
    w_conv_full = conv_all.reshape(N_CHIPS, L, 3, FS).transpose(1, 2, 0, 3).reshape(L, 3, F)
    c_act = _silu(c_all, "silu_c")

    mod_part = jnp.concatenate(
        [_mm(c_act, w_mod[l], "nn", F32, "mm_mod", tm=N_DEV, tn=MS, tk=D) for l in range(L)], axis=1)
    mod_all = _all_gather_small(mod_part, "ag_mod").reshape(N_CHIPS, 2, N_DEV, L, MS)[:, 0]
    mod_mine = lax.dynamic_index_in_dim(mod_all, b_me, axis=1, keepdims=False)
    mod = mod_mine.transpose(1, 0, 2).reshape(L, N_MOD * D) + b_mod

    wq_b, wo_b, wg_b, wu_b, wd_b = [w.astype(BF16) for w in (w_qkv, w_o, w_ffn_gate, w_ffn_up, w_ffn_down)]
    w_fg_b = w_fg.astype(BF16)

    def mixer_shards(l):
        return [wq_b[l], wo_b[l]] + ([w_fg_b[l // 2]] if l % 2 == 1 else [])

    def mlp_shards(l):
        return [wg_b[l], wu_b[l], wd_b[l]]

    def side_by_side(gth):
        return gth.transpose(1, 0, 2).reshape(D, -1)

    def mixer_weights(gathered):
        wts = dict(qkv=side_by_side(gathered[0]), o=gathered[1].reshape(D, D))
        if len(gathered) > 2:
            wts["fg"] = jnp.pad(gathered[2].reshape(D, NH), ((0, 0), (0, LANES - NH)))
        return wts

    def mlp_weights(gathered):
        return dict(g=side_by_side(gathered[0]), u=side_by_side(gathered[1]), d=gathered[2].reshape(F, D))

    W = [None] * L
    W[0] = mixer_weights(_run_side(_gather_side(mixer_shards(0)), "ag_weights_first"))

    def vec(a):
        return a.reshape(1, -1)

    saved = []
    xcur = xs
    for l in range(L):
        sh_a, sc_a, gt_a, sh_f, sc_f, gt_f = [vec(mod[l, j * D:(j + 1) * D]) for j in range(N_MOD)]
        is_fox = l % 2 == 1
        jf = l // 2
        h1 = _norm_mod(xcur, vec(g_mix_pre[l]), sc_a, sh_a, "norm_mod")
        qkv = _mm(h1, W[l]["qkv"], "nn", BF16, "mm_qkv")
        carried_shards = mlp_shards(l) + (mixer_shards(l + 1) if l + 1 < L else [])
        next_weights = _gather_side(carried_shards)
        if is_fox:
            flog = _mm(h1, W[l]["fg"], "nn", F32, "mm_fg")[:, :NH]
            f3 = flog.reshape(NQ, B, NH).transpose(0, 2, 1)
            bias = b_fg[jf].reshape(NH, 1)
            cum3 = _forget_cumsum(f3, bias, "forget_cumsum")
            cum_sn = cum3.transpose(0, 2, 1).reshape(S, NH)
            ck_b = jnp.repeat(cum_sn, B, axis=1)
            cq_rows = jnp.repeat(cum_sn.T, SUBLANES, axis=0)
            (o, stat), gathered = _fox_fwd(qkv, cq_rows, ck_b, "fox_fwd", next_weights)
            extra = (f3, bias, cq_rows, ck_b)
        else:
            (o, stat, extra), gathered = _sb_fwd(qkv, "sb_fwd", next_weights)
        W[l].update(mlp_weights(gathered[:3]))
        if l + 1 < L:
            W[l + 1] = mixer_weights(gathered[3:])
        p = _mm(o, W[l]["o"], "nn", F32, "mm_o")
        x1, h2 = _post_res_norm_mod(xcur, p, vec(g_mix_post[l]), gt_a, vec(g_ffn_pre[l]), sc_f, sh_f,
                                    "post_res_norm_mod")
        gate = _mm(h2, W[l]["g"], "nn", F32, "mm_gate")
        up = _mm(h2, W[l]["u"], "nn", F32, "mm_up")
        wc = w_conv_full[l]
        bc = vec(b_conv[l])
        a = _conv_gate(gate, up, wc, bc, "conv_gate")
        yv = _mm(a, W[l]["d"], "nn", F32, "mm_down")
        x2 = _post_res(x1, yv, vec(g_ffn_post[l]), gt_f, "post_res")
        saved.append(dict(x0=xcur, h1=h1, qkv=qkv, o=o, stat=stat, extra=extra, p=p, x1=x1, h2=h2, gate=gate, up=up, a=a,
                          y=yv, mods=(sh_a, sc_a, gt_a, sh_f, sc_f, gt_f), wc=wc, bc=bc))
        xcur = x2

    g, sq = _loss_head(xcur, target, "loss_head")
    loss_part = 0.5 * jnp.sum(sq) / D
    loss = lax.psum(loss_part, ("x", "y", "c"))

    dW_qkv, dW_o, dW_g, dW_u, dW_d = [[None] * L for _ in range(5)]
    dW_fg, db_fg = [None] * LF, [None] * LF
    dmod, dg_mix_pre, dg_mix_post, dg_ffn_pre, dg_ffn_post = [[None] * L for _ in range(5)]
    dw_conv, db_conv = [None] * L, [None] * L

    def cols(dw, width):
        return dw.reshape(dw.shape[0], N_CHIPS, width).transpose(1, 0, 2).astype(BF16)

    def rows(dw, height):
        return dw.reshape(N_CHIPS, height, dw.shape[1]).astype(BF16)

    n_big = 5
    grad_shards = [[None] * n_big for _ in range(L)]
    to_scatter, to_swap = [], []

    def carried():
        last = lambda e: e[0] == 0 or (e[0] == 1 and e[1] == 0)
        sc = [e for e in to_scatter if not last(e)]
        sc_whole = [e for e in to_scatter if last(e)]
        sw = list(to_swap)
        del to_scatter[:], to_swap[:]
        side = _join_sides([_scatter_side([a for _, _, a in sc]) if sc else None,
                            _scatter_side([a for _, _, a in sc_whole], whole_shard=True) if sc_whole else None,
                            _swap_side([a for _, _, a in sw]) if sw else None])

        def taken(outs):
            for (lay, i, _), recv in zip(sc, outs[:len(sc)]):
                to_swap.append((lay, i, _sum_leading(recv, f"sum_grad_pieces_{i}")))
            for (lay, i, _), recv in zip(sc_whole, outs[len(sc):len(sc) + len(sc_whole)]):
                grad_shards[lay][i] = _sum_leading(recv, f"sum_grad_shard_{i}")
            for (lay, i, _), both in zip(sw, outs[len(sc) + len(sc_whole):]):
                grad_shards[lay][i] = both.reshape((-1,) + both.shape[2:])

        return side, taken

    for l in reversed(range(L)):
        sv = saved[l]
        sh_a, sc_a, gt_a, sh_f, sc_f, gt_f = sv["mods"]
        is_fox = l % 2 == 1
        jf = l // 2
        dy, dgp_f, dgt_f = _post_res_bwd(sv["y"], vec(g_ffn_post[l]), gt_f, g, "post_res_bwd")
        da = _mm(dy, W[l]["d"], "nt", F32, "mm_da")
        dW_d[l] = _mm(sv["a"], dy, "tn", F32, "mm_dwd")
        dgate, dup, dwc, dbc = _conv_gate_bwd(sv["gate"], sv["up"], sv["wc"], sv["bc"], da, "conv_gate_bwd")
        dh2 = [_mm(dgate, W[l]["g"], "nt", F32, "mm_dh2g"), _mm(dup, W[l]["u"], "nt", F32, "mm_dh2u")]
        dW_g[l] = _mm(sv["h2"], dgate, "tn", F32, "mm_dwg")
        dW_u[l] = _mm(sv["h2"], dup, "tn", F32, "mm_dwu")
        g, dg_f, dsc_f, dsh_f = _norm_mod_bwd(sv["x1"], vec(g_ffn_pre[l]), sc_f, sh_f, dh2, g, "norm_mod_bwd")
        dp, dgp_a, dgt_a = _post_res_bwd(sv["p"], vec(g_mix_post[l]), gt_a, g, "post_res_bwd")
        do = _mm(dp, W[l]["o"], "nt", F32, "mm_do")
        dW_o[l] = _mm(sv["o"], dp, "tn", F32, "mm_dwo")
        to_scatter += [(l, 1, rows(dW_o[l], OS)), (l, 2, cols(dW_g[l], FS)), (l, 3, cols(dW_u[l], FS)),
                       (l, 4, rows(dW_d[l], FS))]
        side, taken = carried()
        if is_fox:
            f3, bias, cq_rows, ck_b = sv["extra"]
            (dq, dk, dv, dcq_rows, dck_b), side_outs = _fox_bwd(sv["qkv"], do, sv["o"], sv["stat"], cq_rows, ck_b,
                                                                "fox_bwd", side)
            dcum_sn = dcq_rows[::SUBLANES].T + dck_b[:, ::B]
            dcum3 = dcum_sn.reshape(NQ, B, NH).transpose(0, 2, 1)
            df3, df_tot = _forget_cumsum_bwd(dcum3, f3, bias, "forget_cumsum_bwd")
            df = df3.transpose(0, 2, 1).reshape(S, NH)
            dfp = jnp.pad(df, ((0, 0), (0, LANES - NH)))
            dW_fg[jf] = _mm(sv["h1"], dfp, "tn", F32, "mm_dwfg")[:, :NH]
            db_fg[jf] = df_tot[:, 0]
            dh_extra = [_mm(dfp, W[l]["fg"], "nt", F32, "mm_dh1f")]
        else:
            (dq, dk, dv), side_outs = _sb_bwd(sv["qkv"], do, sv["stat"], sv["extra"], "sb_bwd", side)
            dh_extra = []
        taken(side_outs)
        dqkv = jnp.concatenate([dq, dk, dv], axis=1)
        dh1 = _mm(dqkv, W[l]["qkv"], "nt", F32, "mm_dh1")
        dW_qkv[l] = _mm(sv["h1"], dqkv, "tn", F32, "mm_dwqkv")
        to_scatter.append((l, 0, cols(dW_qkv[l], QS)))
        g, dg_a, dsc_a, dsh_a = _norm_mod_bwd(sv["x0"], vec(g_mix_pre[l]), sc_a, sh_a, [dh1] + dh_extra, g,
                                              "norm_mod_bwd")
        dmod[l] = jnp.concatenate([dsh_a, dsc_a, dgt_a, dsh_f, dsc_f, dgt_f], axis=1)[0]
        dg_mix_pre[l], dg_mix_post[l], dg_ffn_pre[l], dg_ffn_post[l] = dg_a[0], dgp_a[0], dg_f[0], dgp_f[0]
        dw_conv[l], db_conv[l] = dwc, dbc[0]
    grad_x = g[None]

    pieces = [jnp.stack(dmod), jnp.stack(dg_mix_pre), jnp.stack(dg_mix_post), jnp.stack(dg_ffn_pre),
              jnp.stack(dg_ffn_post), jnp.stack(db_fg), jnp.stack(dW_fg), jnp.stack(dw_conv), jnp.stack(db_conv)]
    sizes = [pc.size for pc in pieces]
    total = sum(sizes)
    pack_rows = -(-total // (LANES * SUBLANES)) * SUBLANES
    pack = jnp.pad(jnp.concatenate([pc.reshape(-1) for pc in pieces]), (0, pack_rows * LANES - total))
    pack_all = _all_gather_small(pack.reshape(pack_rows, LANES), "ag_small_grads").reshape(N_DEV, pack_rows, LANES)
    small = _sum_leading(pack_all, "sum_small_grads").reshape(-1)
    offs = [0]
    for sz in sizes:
        offs.append(offs[-1] + sz)
    parts = [small[offs[i]:offs[i + 1]].reshape(pieces[i].shape) for i in range(len(pieces))]
    g_b_mod, g_g_mix_pre, g_g_mix_post, g_g_ffn_pre, g_g_ffn_post, g_b_fg, g_w_fg_full, g_w_conv_full, g_b_conv = parts
    g_w_fg = lax.dynamic_slice_in_dim(g_w_fg_full, k_me * OS, OS, axis=1)
    g_w_conv = lax.dynamic_slice_in_dim(g_w_conv_full, k_me * FS, FS, axis=2)
    dmod_all = pack_all.reshape(N_DEV, -1)[:, :L * N_MOD * D].reshape(N_DEV, L, N_CHIPS, MS)
    dmod_cols = lax.dynamic_index_in_dim(dmod_all, k_me, axis=2, keepdims=False).reshape(N_DEV, L * MS)
    g_w_mod = _mm(_pad_rows(c_act, LANES), _pad_rows(dmod_cols, LANES), "tn", F32, "mm_dwmod", tm=D, tn=MS, tk=LANES)
    g_w_mod = g_w_mod.reshape(D, L, MS).transpose(1, 0, 2)

    tail = 0
    while to_scatter or to_swap:
        side, taken = carried()
        taken(_run_side(side, f"rs_tail_{tail}"))
        tail += 1
    g_w_qkv, g_w_o, g_w_gate, g_w_up, g_w_down = [jnp.stack([grad_shards[l][i] for l in range(L)])
                                                  for i in range(n_big)]

    grads = [g_w_mod, g_b_mod, g_g_mix_pre, g_g_mix_post, g_w_qkv, g_w_o, g_w_fg, g_b_fg, g_g_ffn_pre,
             g_g_ffn_post, g_w_gate, g_w_up, g_w_conv, g_b_conv, g_w_down]
    weights = [w_mod, b_mod, g_mix_pre, g_mix_post, w_qkv, w_o, w_fg, b_fg, g_ffn_pre, g_ffn_post, w_ffn_gate,
               w_ffn_up, w_conv, b_conv, w_ffn_down]
    ms = [m_w_mod, m_b_mod, m_g_mix_pre, m_g_mix_post, m_w_qkv, m_w_o, m_w_fg, m_b_fg, m_g_ffn_pre, m_g_ffn_post,
          m_w_ffn_gate, m_w_ffn_up, m_w_conv, m_b_conv, m_w_ffn_down]
    vs = [v_w_mod, v_b_mod, v_g_mix_pre, v_g_mix_post, v_w_qkv, v_w_o, v_w_fg, v_b_fg, v_g_ffn_pre, v_g_ffn_post,
          v_w_ffn_gate, v_w_ffn_up, v_w_conv, v_b_conv, v_w_ffn_down]
    deltas, new_ms, new_vs = [], [], []
    for wv, gv, mv, vv in zip(weights, grads, ms, vs):
        d, nm, nv = _adamw(wv, gv, mv, vv, "adamw")
        deltas.append(d)
        new_ms.append(nm)
        new_vs.append(nv)
    return (loss, grad_x, *grads, *deltas, *new_ms, *new_vs)
```
